```python
import math
import jax
import jax.numpy as jnp
from jax import lax
import numpy as np

D_MODEL = 1024
BATCH = 16
SEQ = 4096
DEPTH = 1

CHUNK = 64
N_META = 16
N_HEADS = 8
HEAD_DIM = 64
V_DIM = 2 * HEAD_DIM
QK_W = N_HEADS * 2 * HEAD_DIM
ATTN_W = N_HEADS * V_DIM
Q_BLOCK = 128
CONV_C = D_MODEL
CONV_K = 31
N_EXPERTS = 32
TOP_K = 4
D_EXPERT = D_MODEL
SWIGLU_ALPHA = 1.702
SWIGLU_LIMIT = 7.0
EXPERT_BLOCK = 256
EPS = 1e-5
IN_W = 2 * QK_W + ATTN_W + 2 * CONV_C + 2 * D_MODEL
SPLITS = [QK_W, 2 * QK_W, 2 * QK_W + ATTN_W, 2 * QK_W + ATTN_W + CONV_C,
          2 * QK_W + ATTN_W + 2 * CONV_C, 2 * QK_W + ATTN_W + 2 * CONV_C + D_MODEL]

kernel_name = "hybrid_diffattn_conformer_conv_moe_stream"


def _rmsnorm(x, g):
    xf = x.astype(jnp.float32)
    y = xf * lax.rsqrt(jnp.mean(xf * xf, axis=-1, keepdims=True) + EPS)
    return (y * g.astype(jnp.float32)).astype(x.dtype)


def _layernorm(x, g, b):
    xf = x.astype(jnp.float32)
    mu = jnp.mean(xf, axis=-1, keepdims=True)
    var = jnp.mean(jnp.square(xf - mu), axis=-1, keepdims=True)
    y = (xf - mu) * lax.rsqrt(var + EPS)
    return (y * g.astype(jnp.float32) + b.astype(jnp.float32)).astype(x.dtype)


def _alibi_slopes():
    return jnp.asarray([2.0 ** (-8.0 * (h + 1) / N_HEADS) for h in range(N_HEADS)],
                       dtype=jnp.float32)


def _chunk_ids(length):
    pos = jnp.arange(length, dtype=jnp.int32)
    return jnp.where(pos < N_META, 0, (pos - N_META) // CHUNK + 1)


def _visible_end(i, length):
    if i < N_META:
        end = N_META
    else:
        end = N_META + ((i - N_META) // CHUNK + 1) * CHUNK
    return min(end, length)


def _diff_attention(q, k, v, lam):
    length = q.shape[3]
    pos = jnp.arange(length, dtype=jnp.int32)
    chunk = _chunk_ids(length)
    slopes = _alibi_slopes()
    scale = HEAD_DIM ** -0.5
    outs = []
    for s0 in range(0, length, Q_BLOCK):
        s1 = min(s0 + Q_BLOCK, length)
        kend = _visible_end(s1 - 1, length)
        s = jnp.einsum("bhmqd,bhmkd->bhmqk", q[:, :, :, s0:s1],
                       k[:, :, :, :kend]).astype(jnp.float32) * scale
        dist = jnp.abs(pos[s0:s1, None] - pos[None, :kend]).astype(jnp.float32)
        bias = -slopes[:, None, None] * dist
        visible = chunk[None, :kend] <= chunk[s0:s1, None]
        s = jnp.where(visible, s + bias[None, :, None], -jnp.inf)
        p = jax.nn.softmax(s, axis=-1)
        a = p[:, :, 0] - lam * p[:, :, 1]
        outs.append(jnp.einsum("bhqk,bhkd->bhqd", a.astype(v.dtype), v[:, :, :kend]))
    return jnp.concatenate(outs, axis=2)


def _conformer_conv(ca, cb, w_dw, b_dw, ln_g, ln_b, w_pw):
    u = ca * jax.nn.sigmoid(cb)
    y = lax.conv_general_dilated(u, w_dw[:, None, :].astype(u.dtype), (1,),
                                 [(CONV_K - 1, 0)],
                                 dimension_numbers=("NWC", "WIO", "NWC"),
                                 feature_group_count=CONV_C)
    y = jax.nn.silu(_layernorm(y + b_dw, ln_g, ln_b))
    return y @ w_pw


def _moe(h, w_router, b_router, w_gu, b_gu, w_dn, b_dn):
    n, d = h.shape
    logits = (h @ w_router + b_router).astype(jnp.float32)
    top_logit, top_idx = lax.top_k(logits, TOP_K)
    gates = jax.nn.softmax(top_logit, axis=-1)
    n_assign = n * TOP_K
    e_flat = top_idx.reshape(n_assign).astype(jnp.int32)
    tok_flat = jnp.arange(n_assign, dtype=jnp.int32) // TOP_K
    g_flat = gates.reshape(n_assign)
    order = jnp.argsort(e_flat)
    e_sorted = e_flat[order]
    counts = jnp.zeros((N_EXPERTS,), jnp.int32).at[e_flat].add(1)
    padded = (counts + EXPERT_BLOCK - 1) // EXPERT_BLOCK * EXPERT_BLOCK
    pad_end = jnp.cumsum(padded)
    pad_start = pad_end - padded
    grp_start = jnp.cumsum(counts) - counts
    dest = pad_start[e_sorted] + (jnp.arange(n_assign, dtype=jnp.int32) - grp_start[e_sorted])
    n_blocks = -(-(n_assign + N_EXPERTS * (EXPERT_BLOCK - 1)) // EXPERT_BLOCK)
    n_slots = n_blocks * EXPERT_BLOCK
    slot_tok = jnp.full((n_slots,), n, jnp.int32).at[dest].set(tok_flat[order])
    slot_gate = jnp.zeros((n_slots,), jnp.float32).at[dest].set(g_flat[order])
    block_start = jnp.arange(n_blocks, dtype=jnp.int32) * EXPERT_BLOCK
    block_exp = jnp.minimum(jnp.searchsorted(pad_end, block_start, side="right"),
                            N_EXPERTS - 1)
    h_pad = jnp.concatenate([h, jnp.zeros((1, d), h.dtype)], axis=0)

    def expert_block(args):
        tok, e = args
        xb = h_pad[tok]
        gu = xb @ w_gu[e] + b_gu[e]
        glu = jnp.minimum(gu[:, :D_EXPERT], SWIGLU_LIMIT)
        lin = jnp.clip(gu[:, D_EXPERT:], -SWIGLU_LIMIT, SWIGLU_LIMIT)
        act = glu * jax.nn.sigmoid(SWIGLU_ALPHA * glu) * (lin + 1.0)
        return act @ w_dn[e] + b_dn[e]

    ys = lax.map(expert_block, (slot_tok.reshape(n_blocks, EXPERT_BLOCK), block_exp))
    ys = ys.reshape(n_slots, d) * slot_gate[:, None].astype(h.dtype)
    return jnp.zeros((n + 1, d), h.dtype).at[slot_tok].add(ys)[:n]


def setup_inputs(seed: int = 0) -> dict:
    key = jax.random.key(seed)
    ks = jax.random.split(key, 24)

    def nrm(k, shape, s):
        return jax.random.normal(k, shape, jnp.float32) * s

    return {
        "x": nrm(ks[0], (BATCH, SEQ, D_MODEL), 1.0),
        "meta_tokens": nrm(ks[1], (N_META, D_MODEL), 1.0),
        "norm_mix": 1.0 + nrm(ks[2], (DEPTH, D_MODEL), 0.05),
        "w_in": nrm(ks[3], (DEPTH, D_MODEL, IN_W), D_MODEL ** -0.5),
        "b_in": nrm(ks[4], (DEPTH, IN_W), 0.02),
        "lam_params": nrm(ks[5], (DEPTH, 4, HEAD_DIM), 0.1),
        "subln_gain": 1.0 + nrm(ks[6], (DEPTH, V_DIM), 0.05),
        "w_attn_proj": nrm(ks[7], (DEPTH, ATTN_W, D_MODEL), ATTN_W ** -0.5),
        "conv_dw_w": nrm(ks[8], (DEPTH, CONV_K, CONV_C), CONV_K ** -0.5),
        "conv_dw_b": nrm(ks[9], (DEPTH, CONV_C), 0.02),
        "conv_ln_g": 1.0 + nrm(ks[10], (DEPTH, CONV_C), 0.05),
        "conv_ln_b": nrm(ks[11], (DEPTH, CONV_C), 0.02),
        "w_conv_proj": nrm(ks[12], (DEPTH, CONV_C, D_MODEL), CONV_C ** -0.5),
        "w_out": nrm(ks[13], (DEPTH, D_MODEL, D_MODEL), D_MODEL ** -0.5),
        "norm_ffn": 1.0 + nrm(ks[14], (DEPTH, D_MODEL), 0.05),
        "w_router": nrm(ks[15], (DEPTH, D_MODEL, N_EXPERTS), D_MODEL ** -0.5),
        "b_router": nrm(ks[16], (DEPTH, N_EXPERTS), 0.01),
        "w_gate_up": nrm(ks[17], (DEPTH, N_EXPERTS, D_MODEL, 2 * D_EXPERT), D_MODEL ** -0.5),
        "b_gate_up": nrm(ks[18], (DEPTH, N_EXPERTS, 2 * D_EXPERT), 0.02),
        "w_down": nrm(ks[19], (DEPTH, N_EXPERTS, D_EXPERT, D_MODEL), D_EXPERT ** -0.5),
        "b_down": nrm(ks[20], (DEPTH, N_EXPERTS, D_MODEL), 0.02),
        "norm_final": 1.0 + nrm(ks[21], (D_MODEL,), 0.05),
    }


def reference(x, meta_tokens, norm_mix, w_in, b_in, lam_params, subln_gain, w_attn_proj,
              conv_dw_w, conv_dw_b, conv_ln_g, conv_ln_b, w_conv_proj, w_out, norm_ffn,
              w_router, b_router, w_gate_up, b_gate_up, w_down, b_down, norm_final):
    bsz = x.shape[0]
    meta = jnp.broadcast_to(meta_tokens[None].astype(x.dtype), (bsz, N_META, D_MODEL))
    h = jnp.concatenate([meta, x], axis=1)
    length = h.shape[1]
    for layer in range(DEPTH):
        xn = _rmsnorm(h, norm_mix[layer])
        proj = xn @ w_in[layer] + b_in[layer]
        q, k, v, ca, cb, ga, gc = jnp.split(proj, SPLITS, axis=-1)
        q = q.reshape(bsz, length, N_HEADS, 2, HEAD_DIM).transpose(0, 2, 3, 1, 4)
        k = k.reshape(bsz, length, N_HEADS, 2, HEAD_DIM).transpose(0, 2, 3, 1, 4)
        v = v.reshape(bsz, length, N_HEADS, V_DIM).transpose(0, 2, 1, 3)
        lp = lam_params[layer].astype(jnp.float32)
        lam_init = 0.8 - 0.6 * math.exp(-0.3 * layer)
        lam = jnp.exp(jnp.sum(lp[0] * lp[1])) - jnp.exp(jnp.sum(lp[2] * lp[3])) + lam_init
        o = _diff_attention(q, k, v, lam)
        o = _rmsnorm(o, subln_gain[layer]) * (1.0 - lam_init)
        o = o.transpose(0, 2, 1, 3).reshape(bsz, length, ATTN_W)
        y_attn = o @ w_attn_proj[layer]
        y_conv = _conformer_conv(ca, cb, conv_dw_w[layer], conv_dw_b[layer],
                                 conv_ln_g[layer], conv_ln_b[layer], w_conv_proj[layer])
        mixed = jax.nn.sigmoid(ga) * y_attn + jax.nn.sigmoid(gc) * y_conv
        h = h + mixed @ w_out[layer]
        hn = _rmsnorm(h, norm_ffn[layer]).reshape(bsz * length, D_MODEL)
        moe = _moe(hn, w_router[layer], b_router[layer], w_gate_up[layer],
                   b_gate_up[layer], w_down[layer], b_down[layer])
        h = h + moe.reshape(bsz, length, D_MODEL)
    return _rmsnorm(h, norm_final)[:, N_META:]
```

```python
import functools

import jax
import jax.numpy as jnp
from jax import lax
from jax.experimental import pallas as pl
from jax.experimental.pallas import tpu as pltpu

CHUNK = 64
N_META = 16
N_HEADS = 8
HEAD_DIM = 64
V_DIM = 2 * HEAD_DIM
CONV_K = 31
N_EXPERTS = 32
TOP_K = 4
SWIGLU_ALPHA = 1.702
SWIGLU_LIMIT = 7.0
EXPERT_BLOCK = 256
EPS = 1e-5
NEG = -1e30

V7X_VMEM_LIMIT = 56 * 1024 * 1024
META_PAD = 128
HALO = 32
ROUTE_ROWS = 8

BF16 = jnp.bfloat16
F32 = jnp.float32


def _sigmoid(x):
    return 1.0 / (1.0 + jnp.exp(-x))


def _inproj_kernel(x_ref, g_ref, w_ref, b_ref, q_ref, k_ref, v_ref, u_ref, sga_ref, sgc_ref):
    d = x_ref.shape[1]
    x = x_ref[...]
    xn = x * lax.rsqrt(jnp.mean(x * x, axis=-1, keepdims=True) + EPS) * g_ref[...]
    xb = xn.astype(BF16)

    def proj(c):
        return (jnp.dot(xb, w_ref[:, c * d:(c + 1) * d], preferred_element_type=F32)
                + b_ref[:, c * d:(c + 1) * d])

    q_ref[...] = (proj(0) * (HEAD_DIM ** -0.5)).astype(BF16)
    k_ref[...] = proj(1).astype(BF16)
    v_ref[...] = proj(2).astype(BF16)
    u_ref[...] = (proj(3) * _sigmoid(proj(4))).astype(BF16)
    sga_ref[...] = _sigmoid(proj(5)).astype(BF16)
    sgc_ref[...] = _sigmoid(proj(6)).astype(BF16)


def _inproj(x2d, g, w_bf, b, tm):
    n, d = x2d.shape
    in_w = w_bf.shape[1]
    out = jax.ShapeDtypeStruct((n, d), BF16)
    row = pl.BlockSpec((tm, d), lambda i: (i, 0))
    return pl.pallas_call(
        _inproj_kernel,
        grid=(n // tm,),
        in_specs=[row,
                  pl.BlockSpec((1, d), lambda i: (0, 0)),
                  pl.BlockSpec((d, in_w), lambda i: (0, 0)),
                  pl.BlockSpec((1, in_w), lambda i: (0, 0))],
        out_specs=[row] * 6,
        out_shape=[out] * 6,
        compiler_params=pltpu.CompilerParams(
            dimension_semantics=("arbitrary",), vmem_limit_bytes=V7X_VMEM_LIMIT),
        name="inproj",
    )(x2d, g, w_bf, b)


def _attn_kernel(lam_ref, q_ref, k_ref, v_ref, km_ref, vm_ref, boff_ref, bdiag_ref, bmeta_ref,
                 gain_ref, o_ref, m_s, l_s, acc_s, *, tq, lam_init):
    h = pl.program_id(1)
    qi = pl.program_id(2)
    slope = jnp.exp2(-(h + 1).astype(F32))
    lp = lam_ref[...]
    lam = (jnp.exp(jnp.sum(lp[0:1] * lp[1:2], axis=-1, keepdims=True))
           - jnp.exp(jnp.sum(lp[2:3] * lp[3:4], axis=-1, keepdims=True)) + lam_init)

    q = q_ref[...]
    lane = lax.broadcasted_iota(jnp.int32, q.shape, 1)
    zero = jnp.zeros_like(q)
    qs = (jnp.where(lane < HEAD_DIM, q, zero), jnp.where(lane >= HEAD_DIM, q, zero))

    def scores(qm, kk):
        return lax.dot_general(qm, kk, (((1,), (1,)), ((), ())), preferred_element_type=F32)

    def update(idx, s, vv, first):
        smax = jnp.max(s, axis=-1, keepdims=True)
        if first:
            m_new = smax
        else:
            m_new = jnp.maximum(m_s[idx], smax)
        p = jnp.exp(s - m_new)
        ps = jnp.sum(p, axis=-1, keepdims=True)
        pv = jnp.dot(p.astype(BF16), vv, preferred_element_type=F32)
        if first:
            l_s[idx] = ps
            acc_s[idx] = pv
        else:
            alpha = jnp.exp(m_s[idx] - m_new)
            l_s[idx] = alpha * l_s[idx] + ps
            acc_s[idx] = alpha * acc_s[idx] + pv
        m_s[idx] = m_new

    row0 = (qi * tq).astype(F32)
    km = km_ref[...]
    vm = vm_ref[...]
    for idx in range(2):
        s = scores(qs[idx], km) + (bmeta_ref[0] - slope * row0)
        update(idx, s, vm, True)

    def body(j, carry):
        kk = k_ref[pl.ds(pl.multiple_of(j * tq, tq), tq), :]
        vv = v_ref[pl.ds(pl.multiple_of(j * tq, tq), tq), :]
        shift = slope * ((qi - j) * tq).astype(F32)
        for idx in range(2):
            s = scores(qs[idx], kk) + (boff_ref[0] - shift)
            update(idx, s, vv, False)
        return carry

    lax.fori_loop(0, qi, body, 0)

    kk = k_ref[pl.ds(pl.multiple_of(qi * tq, tq), tq), :]
    vv = v_ref[pl.ds(pl.multiple_of(qi * tq, tq), tq), :]
    for idx in range(2):
        s = scores(qs[idx], kk) + bdiag_ref[0]
        update(idx, s, vv, False)

    o = acc_s[0] / l_s[0] - lam * (acc_s[1] / l_s[1])
    o = o * lax.rsqrt(jnp.mean(o * o, axis=-1, keepdims=True) + EPS) * gain_ref[...]
    o_ref[...] = (o * (1.0 - lam_init)).astype(BF16)


def _attention(lam_params, q, k, v, k_meta, v_meta, gain, bsz, seq, tq, lam_init):
    n, w = q.shape
    nq = seq // tq
    slopes = jnp.asarray([2.0 ** (-8.0 * (h + 1) / N_HEADS) for h in range(N_HEADS)], F32)
    r = jnp.arange(tq, dtype=jnp.int32)
    diff = (r[:, None] - r[None, :])
    boff = -slopes[:, None, None] * diff.astype(F32)[None]
    vis = (r[None, :] // CHUNK) <= (r[:, None] // CHUNK)
    bdiag = jnp.where(vis[None], -slopes[:, None, None] * jnp.abs(diff).astype(F32)[None], NEG)
    mcol = jnp.arange(META_PAD, dtype=jnp.int32)
    dmeta = (N_META + r[:, None] - mcol[None, :]).astype(F32)
    bmeta = jnp.where((mcol < N_META)[None, None, :], -slopes[:, None, None] * dmeta[None], NEG)

    pad = ((0, META_PAD - N_META), (0, 0))
    km = jnp.pad(k_meta, pad)
    vm = jnp.pad(v_meta, pad)

    kern = functools.partial(_attn_kernel, tq=tq, lam_init=lam_init)
    return pl.pallas_call(
        kern,
        grid=(bsz, N_HEADS, nq),
        in_specs=[
            pl.BlockSpec((4, HEAD_DIM), lambda b, h, i: (0, 0)),
            pl.BlockSpec((tq, V_DIM), lambda b, h, i: (b * nq + i, h)),
            pl.BlockSpec((seq, V_DIM), lambda b, h, i: (b, h)),
            pl.BlockSpec((seq, V_DIM), lambda b, h, i: (b, h)),
            pl.BlockSpec((META_PAD, V_DIM), lambda b, h, i: (0, h)),
            pl.BlockSpec((META_PAD, V_DIM), lambda b, h, i: (0, h)),
            pl.BlockSpec((1, tq, tq), lambda b, h, i: (h, 0, 0)),
            pl.BlockSpec((1, tq, tq), lambda b, h, i: (h, 0, 0)),
            pl.BlockSpec((1, tq, META_PAD), lambda b, h, i: (h, 0, 0)),
            pl.BlockSpec((1, V_DIM), lambda b, h, i: (0, 0)),
        ],
        out_specs=pl.BlockSpec((tq, V_DIM), lambda b, h, i: (b * nq + i, h)),
        out_shape=jax.ShapeDtypeStruct((n, w), BF16),
        scratch_shapes=[pltpu.VMEM((2, tq, 1), F32), pltpu.VMEM((2, tq, 1), F32),
                        pltpu.VMEM((2, tq, V_DIM), F32)],
        compiler_params=pltpu.CompilerParams(
            dimension_semantics=("arbitrary", "arbitrary", "arbitrary"),
            vmem_limit_bytes=V7X_VMEM_LIMIT),
        name="diff_attention",
    )(lam_params, q, k, v, km, vm, boff, bdiag, bmeta, gain)


def _mixer_kernel(x_ref, o_ref, u_ref, uh_ref, mh_ref, sga_ref, sgc_ref,
                  wap_ref, wpw_ref, wout_ref, dww_ref, dwb_ref, lng_ref, lnb_ref,
                  nffn_ref, wr_ref, br_ref, tri_ref,
                  h_ref, hn_ref, exp_ref, rank_ref, gate_ref, cnt_ref,
                  ext_s, y_s, cnt_s, *, tm, tiles_per_seq):
    i = pl.program_id(0)
    d = x_ref.shape[1]

    @pl.when(i == 0)
    def _():
        cnt_s[...] = jnp.zeros_like(cnt_s)

    first = (i % tiles_per_seq) == 0
    halo = jnp.where(first, mh_ref[...], uh_ref[...])
    ext_s[0:HALO, :] = halo.astype(F32)
    ext_s[HALO:HALO + tm, :] = u_ref[...].astype(F32)
    off = HALO - (CONV_K - 1)
    for c in range(d // 128):
        cs = slice(c * 128, (c + 1) * 128)
        acc = jnp.zeros((tm, 128), F32)
        for t in range(CONV_K):
            acc = acc + dww_ref[t:t + 1, cs] * ext_s[off + t:off + t + tm, cs]
        y_s[:, cs] = acc
    y = y_s[...] + dwb_ref[...]
    mu = jnp.mean(y, axis=-1, keepdims=True)
    yc = y - mu
    var = jnp.mean(yc * yc, axis=-1, keepdims=True)
    y = yc * lax.rsqrt(var + EPS) * lng_ref[...] + lnb_ref[...]
    y = y * _sigmoid(y)
    y_conv = jnp.dot(y.astype(BF16), wpw_ref[...], preferred_element_type=F32)

    y_attn = jnp.dot(o_ref[...], wap_ref[...], preferred_element_type=F32)
    mixed = sga_ref[...].astype(F32) * y_attn + sgc_ref[...].astype(F32) * y_conv
    h = x_ref[...] + jnp.dot(mixed.astype(BF16), wout_ref[...], preferred_element_type=F32)
    h_ref[...] = h
    hn = h * lax.rsqrt(jnp.mean(h * h, axis=-1, keepdims=True) + EPS) * nffn_ref[...]
    hn_ref[...] = hn

    logits = lax.dot_general(wr_ref[...], hn, (((1,), (1,)), ((), ())),
                             precision=lax.Precision.HIGHEST,
                             preferred_element_type=F32) + br_ref[...]
    eio = lax.broadcasted_iota(jnp.int32, logits.shape, 0)
    sels, tops, idxs = [], [], []
    cur = logits
    for _ in range(TOP_K):
        m = jnp.max(cur, axis=0, keepdims=True)
        idx = jnp.min(jnp.where(cur == m, eio, N_EXPERTS), axis=0, keepdims=True)
        sel = eio == idx
        sels.append(sel)
        tops.append(m)
        idxs.append(idx)
        cur = jnp.where(sel, -jnp.inf, cur)
    exps = [jnp.exp(t - tops[0]) for t in tops]
    denom = exps[0] + exps[1] + exps[2] + exps[3]
    hot = jnp.zeros(logits.shape, F32)
    for sel in sels:
        hot = hot + jnp.where(sel, 1.0, 0.0)
    before = jnp.dot(hot.astype(BF16), tri_ref[...], preferred_element_type=F32) + cnt_s[...]
    ranks = [jnp.sum(jnp.where(sel, before, 0.0), axis=0, keepdims=True).astype(jnp.int32)
             for sel in sels]
    zi = jnp.zeros((ROUTE_ROWS - TOP_K, tm), jnp.int32)
    exp_ref[0] = jnp.concatenate(idxs + [zi], axis=0)
    rank_ref[0] = jnp.concatenate(ranks + [zi], axis=0)
    gate_ref[0] = jnp.concatenate([e / denom for e in exps] + [zi.astype(F32)], axis=0)
    cnt_s[...] = cnt_s[...] + jnp.sum(hot, axis=1, keepdims=True)
    cnt_ref[...] = jnp.broadcast_to(cnt_s[...], cnt_ref.shape)


def _mixer(x2d, o, u, sga, sgc, meta_halo, wap, wpw, wout, dww, dwb, lng, lnb, nffn, wr_t, br,
           seq, tm):
    n, d = x2d.shape
    nt = n // tm
    hb = tm // HALO
    tri = (jnp.arange(tm)[:, None] < jnp.arange(tm)[None, :]).astype(BF16)
    row = pl.BlockSpec((tm, d), lambda i: (i, 0))
    route = pl.BlockSpec((1, ROUTE_ROWS, tm), lambda i: (i, 0, 0))

    def full(shp):
        return pl.BlockSpec(shp, lambda i: tuple(0 for _ in shp))

    kern = functools.partial(_mixer_kernel, tm=tm, tiles_per_seq=seq // tm)
    return pl.pallas_call(
        kern,
        grid=(nt,),
        in_specs=[row, row, row,
                  pl.BlockSpec((HALO, d), lambda i: (jnp.maximum(i * hb - 1, 0), 0)),
                  full((HALO, d)), row, row,
                  full((d, d)), full((d, d)), full((d, d)),
                  full((CONV_K, d)), full((1, d)), full((1, d)), full((1, d)), full((1, d)),
                  full((N_EXPERTS, d)), full((N_EXPERTS, 1)), full((tm, tm))],
        out_specs=[row, row, route, route, route, full((N_EXPERTS, 128))],
        out_shape=[jax.ShapeDtypeStruct((n, d), F32),
                   jax.ShapeDtypeStruct((n, d), F32),
                   jax.ShapeDtypeStruct((nt, ROUTE_ROWS, tm), jnp.int32),
                   jax.ShapeDtypeStruct((nt, ROUTE_ROWS, tm), jnp.int32),
                   jax.ShapeDtypeStruct((nt, ROUTE_ROWS, tm), F32),
                   jax.ShapeDtypeStruct((N_EXPERTS, 128), F32)],
        scratch_shapes=[pltpu.VMEM((HALO + tm, d), F32), pltpu.VMEM((tm, d), F32),
                        pltpu.VMEM((N_EXPERTS, 1), F32)],
        compiler_params=pltpu.CompilerParams(
            dimension_semantics=("arbitrary",), vmem_limit_bytes=V7X_VMEM_LIMIT),
        name="mixer_router",
    )(x2d, o, u, u, meta_halo, sga, sgc, wap, wpw, wout, dww, dwb, lng, lnb, nffn, wr_t, br, tri)


def _dispatch_kernel(dest_hbm, hn_ref, xs_hbm, dsm_s, sem_idx, sem_rows, *, tm):
    i = pl.program_id(0)
    idx_cp = pltpu.make_async_copy(dest_hbm.at[i], dsm_s, sem_idx)
    idx_cp.start()
    idx_cp.wait()

    def row_copy(r, dst):
        return pltpu.make_async_copy(hn_ref.at[pl.ds(r, 1), :], xs_hbm.at[pl.ds(dst, 1), :],
                                     sem_rows)

    def issue(r, c):
        for kk in range(TOP_K):
            row_copy(r, dsm_s[kk, r]).start()
        return c
    lax.fori_loop(0, tm, issue, 0)

    def drain(r, c):
        row_copy(0, 0).wait()
        return c
    lax.fori_loop(0, TOP_K * tm, drain, 0)


def _dispatch(dest, hn, n_slots, tm):
    n, d = hn.shape
    return pl.pallas_call(
        functools.partial(_dispatch_kernel, tm=tm),
        grid=(n // tm,),
        in_specs=[pl.BlockSpec(memory_space=pl.ANY),
                  pl.BlockSpec((tm, d), lambda i: (i, 0))],
        out_specs=pl.BlockSpec(memory_space=pl.ANY),
        out_shape=jax.ShapeDtypeStruct((n_slots, d), F32),
        scratch_shapes=[pltpu.SMEM((ROUTE_ROWS, tm), jnp.int32),
                        pltpu.SemaphoreType.DMA(()), pltpu.SemaphoreType.DMA(())],
        compiler_params=pltpu.CompilerParams(
            dimension_semantics=("arbitrary",), vmem_limit_bytes=V7X_VMEM_LIMIT,
            has_side_effects=True),
        name="dispatch",
    )(dest, hn)


def _expert_kernel(bexp_ref, bvalid_ref, xs_ref, wgu_ref, bgu_ref, wdn_ref, bdn_ref, ys_ref):
    i = pl.program_id(0)
    valid = bvalid_ref[i]
    f = wdn_ref.shape[1]

    @pl.when(valid > 0)
    def _():
        rows = lax.broadcasted_iota(jnp.int32, xs_ref.shape, 0)
        xb = jnp.where(rows < valid, xs_ref[...], 0.0).astype(BF16)
        gu = jnp.dot(xb, wgu_ref[0], preferred_element_type=F32) + bgu_ref[0]
        glu = jnp.minimum(gu[:, :f], SWIGLU_LIMIT)
        lin = jnp.clip(gu[:, f:], -SWIGLU_LIMIT, SWIGLU_LIMIT)
        act = glu * _sigmoid(SWIGLU_ALPHA * glu) * (lin + 1.0)
        ys_ref[...] = jnp.dot(act.astype(BF16), wdn_ref[0], preferred_element_type=F32) + bdn_ref[0]

    @pl.when(valid <= 0)
    def _():
        ys_ref[...] = jnp.zeros_like(ys_ref)


def _experts(block_exp, block_valid, xs, wgu, bgu, wdn, bdn):
    n_slots, d = xs.shape
    nb = n_slots // EXPERT_BLOCK
    f2 = wgu.shape[2]
    f = wdn.shape[1]
    grid_spec = pltpu.PrefetchScalarGridSpec(
        num_scalar_prefetch=2,
        grid=(nb,),
        in_specs=[pl.BlockSpec((EXPERT_BLOCK, d), lambda i, be, bv: (i, 0)),
                  pl.BlockSpec((1, d, f2), lambda i, be, bv: (be[i], 0, 0)),
                  pl.BlockSpec((1, 1, f2), lambda i, be, bv: (be[i], 0, 0)),
                  pl.BlockSpec((1, f, d), lambda i, be, bv: (be[i], 0, 0)),
                  pl.BlockSpec((1, 1, d), lambda i, be, bv: (be[i], 0, 0))],
        out_specs=pl.BlockSpec((EXPERT_BLOCK, d), lambda i, be, bv: (i, 0)),
    )
    return pl.pallas_call(
        _expert_kernel,
        grid_spec=grid_spec,
        out_shape=jax.ShapeDtypeStruct((n_slots, d), F32),
        compiler_params=pltpu.CompilerParams(
            dimension_semantics=("arbitrary",), vmem_limit_bytes=V7X_VMEM_LIMIT),
        name="experts",
    )(block_exp, block_valid, xs, wgu, bgu, wdn, bdn)


def _combine_kernel(dest_hbm, h_ref, gate_ref, g_ref, ys_hbm, out_ref, dsm_s, ybuf, sem_idx, sem_rows,
                    *, tm):
    i = pl.program_id(0)
    idx_cp = pltpu.make_async_copy(dest_hbm.at[i], dsm_s, sem_idx)
    idx_cp.start()
    idx_cp.wait()

    def row_copy(kk, r, src):
        return pltpu.make_async_copy(ys_hbm.at[pl.ds(src, 1), :], ybuf.at[kk, pl.ds(r, 1), :],
                                     sem_rows)

    def issue(r, c):
        for kk in range(TOP_K):
            row_copy(kk, r, dsm_s[kk, r]).start()
        return c
    lax.fori_loop(0, tm, issue, 0)

    def drain(r, c):
        row_copy(0, 0, 0).wait()
        return c
    lax.fori_loop(0, TOP_K * tm, drain, 0)

    gates = jnp.transpose(gate_ref[0])
    acc = h_ref[...]
    for kk in range(TOP_K):
        acc = acc + gates[:, kk:kk + 1] * ybuf[kk]
    out_ref[...] = acc * lax.rsqrt(jnp.mean(acc * acc, axis=-1, keepdims=True) + EPS) * g_ref[...]


def _combine(dest, h, gates, g_final, ys, tm):
    n, d = h.shape
    return pl.pallas_call(
        functools.partial(_combine_kernel, tm=tm),
        grid=(n // tm,),
        in_specs=[pl.BlockSpec(memory_space=pl.ANY),
                  pl.BlockSpec((tm, d), lambda i: (i, 0)),
                  pl.BlockSpec((1, ROUTE_ROWS, tm), lambda i: (i, 0, 0)),
                  pl.BlockSpec((1, d), lambda i: (0, 0)),
                  pl.BlockSpec(memory_space=pl.ANY)],
        out_specs=pl.BlockSpec((tm, d), lambda i: (i, 0)),
        out_shape=jax.ShapeDtypeStruct((n, d), F32),
        scratch_shapes=[pltpu.SMEM((ROUTE_ROWS, tm), jnp.int32),
                        pltpu.VMEM((TOP_K, tm, d), F32),
                        pltpu.SemaphoreType.DMA(()), pltpu.SemaphoreType.DMA(())],
        compiler_params=pltpu.CompilerParams(
            dimension_semantics=("arbitrary",), vmem_limit_bytes=V7X_VMEM_LIMIT),
        name="combine",
    )(dest, h, gates, g_final, ys)


def kernel(x, meta_tokens, norm_mix, w_in, b_in, lam_params, subln_gain, w_attn_proj, conv_dw_w,
           conv_dw_b, conv_ln_g, conv_ln_b, w_conv_proj, w_out, norm_ffn, w_router, b_router,
           w_gate_up, b_gate_up, w_down, b_down, norm_final):
    bsz, seq, d = x.shape
    assert norm_mix.shape[0] == 1, "single-layer block"
    n = bsz * seq
    tm_proj = min(512, seq)
    tq = min(512, seq)
    tm = min(256, seq)
    lam_init = 0.8 - 0.6 * 1.0

    x2d = x.reshape(n, d)
    w_in_bf = w_in[0].astype(BF16)
    g_mix = norm_mix[0][None]
    b_in2 = b_in[0][None]
    q, k, v, u, sga, sgc = _inproj(x2d, g_mix, w_in_bf, b_in2, tm_proj)
    _, k_m, v_m, u_m, _, _ = _inproj(meta_tokens.astype(x.dtype), g_mix, w_in_bf, b_in2, N_META)

    o = _attention(lam_params[0], q, k, v, k_m, v_m, subln_gain[0][None], bsz, seq, tq, lam_init)

    meta_halo = jnp.concatenate([jnp.zeros((HALO - N_META, d), BF16), u_m], axis=0)
    h, hn, eidx, rank, gates, counts = _mixer(
        x2d, o, u, sga, sgc, meta_halo,
        w_attn_proj[0].astype(BF16), w_conv_proj[0].astype(BF16), w_out[0].astype(BF16),
        conv_dw_w[0], conv_dw_b[0][None], conv_ln_g[0][None], conv_ln_b[0][None],
        norm_ffn[0][None], w_router[0].T, b_router[0][:, None], seq, tm)

    cnt = counts[:, 0].astype(jnp.int32)
    padded = (cnt + EXPERT_BLOCK - 1) // EXPERT_BLOCK * EXPERT_BLOCK
    pad_end = jnp.cumsum(padded)
    pad_start = pad_end - padded
    n_blocks = -(-(n * TOP_K + N_EXPERTS * (EXPERT_BLOCK - 1)) // EXPERT_BLOCK)
    n_slots = n_blocks * EXPERT_BLOCK
    dest = pad_start[eidx] + rank
    block_start = jnp.arange(n_blocks, dtype=jnp.int32) * EXPERT_BLOCK
    block_exp = jnp.minimum(jnp.searchsorted(pad_end, block_start, side="right"),
                            N_EXPERTS - 1).astype(jnp.int32)
    block_valid = jnp.clip(pad_start[block_exp] + cnt[block_exp] - block_start, 0,
                           EXPERT_BLOCK).astype(jnp.int32)

    xs = _dispatch(dest, hn, n_slots, tm)
    ys = _experts(block_exp, block_valid, xs,
                  w_gate_up[0].astype(BF16), b_gate_up[0][:, None, :],
                  w_down[0].astype(BF16), b_down[0][:, None, :])
    out = _combine(dest, h, gates, norm_final[None], ys, tm)
    return out.reshape(bsz, seq, d)
```

```python
import functools

import jax
import jax.numpy as jnp
from jax import lax
from jax.experimental import pallas as pl
from jax.experimental.pallas import tpu as pltpu

CHUNK = 64
N_META = 16
N_HEADS = 8
HEAD_DIM = 64
V_DIM = 2 * HEAD_DIM
CONV_K = 31
N_EXPERTS = 32
TOP_K = 4
SWIGLU_ALPHA = 1.702
SWIGLU_LIMIT = 7.0
EXPERT_BLOCK = 256
EPS = 1e-5
NEG = -1e30
LOG2E = 1.4426950408889634

V7X_VMEM_LIMIT = 56 * 1024 * 1024
META_PAD = 128
HALO = 32
ROUTE_ROWS = 8

BF16 = jnp.bfloat16
F32 = jnp.float32


def _sigmoid(x):
    return 1.0 / (1.0 + jnp.exp(-x))


def _inproj_kernel(x_ref, g_ref, w_ref, b_ref, wvt_ref, bvt_ref,
                   q_ref, k_ref, vt_ref, u_ref, sga_ref, sgc_ref):
    d = x_ref.shape[1]
    x = x_ref[...]
    xn = x * lax.rsqrt(jnp.mean(x * x, axis=-1, keepdims=True) + EPS) * g_ref[...]
    xb = xn.astype(BF16)

    def proj(c):
        return (jnp.dot(xb, w_ref[:, c * d:(c + 1) * d], preferred_element_type=F32)
                + b_ref[:, c * d:(c + 1) * d])

    q_ref[...] = (proj(0) * (HEAD_DIM ** -0.5 * LOG2E)).astype(BF16)
    k_ref[...] = proj(1).astype(BF16)
    vt = lax.dot_general(wvt_ref[...], xb, (((1,), (1,)), ((), ())), preferred_element_type=F32)
    vt_ref[...] = (vt + bvt_ref[...]).astype(BF16)
    u_ref[...] = (proj(2) * _sigmoid(proj(3))).astype(BF16)
    sga_ref[...] = _sigmoid(proj(4)).astype(BF16)
    sgc_ref[...] = _sigmoid(proj(5)).astype(BF16)


def _inproj(x2d, g, w_bf, b, wvt_bf, bvt, tm):
    n, d = x2d.shape
    in_w = w_bf.shape[1]
    out = jax.ShapeDtypeStruct((n, d), BF16)
    row = pl.BlockSpec((tm, d), lambda i: (i, 0))
    col = pl.BlockSpec((d, tm), lambda i: (0, i))
    return pl.pallas_call(
        _inproj_kernel,
        grid=(n // tm,),
        in_specs=[row,
                  pl.BlockSpec((1, d), lambda i: (0, 0)),
                  pl.BlockSpec((d, in_w), lambda i: (0, 0)),
                  pl.BlockSpec((1, in_w), lambda i: (0, 0)),
                  pl.BlockSpec((d, d), lambda i: (0, 0)),
                  pl.BlockSpec((d, 1), lambda i: (0, 0))],
        out_specs=[row, row, col, row, row, row],
        out_shape=[out, out, jax.ShapeDtypeStruct((d, n), BF16), out, out, out],
        compiler_params=pltpu.CompilerParams(
            dimension_semantics=("arbitrary",), vmem_limit_bytes=V7X_VMEM_LIMIT),
        name="inproj",
    )(x2d, g, w_bf, b, wvt_bf, bvt)


def _attn_kernel(lam_ref, q_ref, k_ref, vt_ref, km_ref, vmt_ref, boff_ref, bdiag_ref, bmeta_ref,
                 gain_ref, o_ref, m_s, l_s, acc_s, *, tq, lam_init):
    h = pl.program_id(1)
    qi = pl.program_id(2)
    slope = jnp.exp2(-(h + 1).astype(F32))
    lp = lam_ref[...]
    lam = (jnp.exp(jnp.sum(lp[0:1] * lp[1:2], axis=-1, keepdims=True))
           - jnp.exp(jnp.sum(lp[2:3] * lp[3:4], axis=-1, keepdims=True)) + lam_init)

    q = q_ref[...]
    lane = lax.broadcasted_iota(jnp.int32, q.shape, 1)
    zero = jnp.zeros_like(q)
    qs = (jnp.where(lane < HEAD_DIM, q, zero), jnp.where(lane >= HEAD_DIM, q, zero))

    def scores_t(kk, qm):
        return lax.dot_general(kk, qm, (((1,), (1,)), ((), ())), preferred_element_type=F32)

    def update(idx, s, vt, first, shift=None):
        smax = jnp.max(s, axis=0, keepdims=True)
        if shift is not None:
            smax = smax - shift
        if first:
            m_new = smax
        else:
            m_new = jnp.maximum(m_s[idx], smax)
        p = jnp.exp2(s - (m_new if shift is None else m_new + shift))
        ps = jnp.sum(p, axis=0, keepdims=True)
        pv = jnp.dot(vt, p.astype(BF16), preferred_element_type=F32)
        if first:
            l_s[idx] = ps
            acc_s[idx] = pv
        else:
            alpha = jnp.exp2(m_s[idx] - m_new)
            l_s[idx] = alpha * l_s[idx] + ps
            acc_s[idx] = alpha * acc_s[idx] + pv
        m_s[idx] = m_new

    row0 = (qi * tq).astype(F32) * LOG2E
    km = km_ref[...]
    vmt = vmt_ref[...]
    for idx in range(2):
        s = scores_t(km, qs[idx]) + bmeta_ref[0]
        update(idx, s, vmt, True, slope * row0)

    def body(j, carry):
        start = pl.multiple_of(j * tq, tq)
        kk = k_ref[pl.ds(start, tq), :]
        vt = vt_ref[:, pl.ds(start, tq)]
        shift = slope * (((qi - j) * tq).astype(F32) * LOG2E)
        for idx in range(2):
            s = scores_t(kk, qs[idx]) + boff_ref[0]
            update(idx, s, vt, False, shift)
        return carry

    lax.fori_loop(0, qi, body, 0)

    start = pl.multiple_of(qi * tq, tq)
    kk = k_ref[pl.ds(start, tq), :]
    vt = vt_ref[:, pl.ds(start, tq)]
    for idx in range(2):
        s = scores_t(kk, qs[idx]) + bdiag_ref[0]
        update(idx, s, vt, False)

    o = acc_s[0] / l_s[0] - lam * (acc_s[1] / l_s[1])
    o = o * lax.rsqrt(jnp.mean(o * o, axis=0, keepdims=True) + EPS) * gain_ref[...]
    o_ref[...] = jnp.transpose(o * (1.0 - lam_init)).astype(BF16)


def _attention(lam_params, q, k, vt, k_meta, vt_meta, gain, bsz, seq, tq, lam_init):
    n, w = q.shape
    nq = seq // tq
    slopes = jnp.asarray([2.0 ** (-8.0 * (h + 1) / N_HEADS) for h in range(N_HEADS)], F32)
    r = jnp.arange(tq, dtype=jnp.int32)
    diff = r[None, :] - r[:, None]
    slopes = slopes * LOG2E
    boff = -slopes[:, None, None] * diff.astype(F32)[None]
    vis = (r[:, None] // CHUNK) <= (r[None, :] // CHUNK)
    bdiag = jnp.where(vis[None], -slopes[:, None, None] * jnp.abs(diff).astype(F32)[None], NEG)
    mrow = jnp.arange(META_PAD, dtype=jnp.int32)
    dmeta = (N_META + r[None, :] - mrow[:, None]).astype(F32)
    bmeta = jnp.where((mrow < N_META)[None, :, None], -slopes[:, None, None] * dmeta[None], NEG)

    km = jnp.pad(k_meta, ((0, META_PAD - N_META), (0, 0)))
    vmt = jnp.pad(vt_meta, ((0, 0), (0, META_PAD - N_META)))

    kern = functools.partial(_attn_kernel, tq=tq, lam_init=lam_init)
    return pl.pallas_call(
        kern,
        grid=(bsz, N_HEADS, nq),
        in_specs=[
            pl.BlockSpec((4, HEAD_DIM), lambda b, h, i: (0, 0)),
            pl.BlockSpec((tq, V_DIM), lambda b, h, i: (b * nq + i, h)),
            pl.BlockSpec((seq, V_DIM), lambda b, h, i: (b, h)),
            pl.BlockSpec((V_DIM, seq), lambda b, h, i: (h, b)),
            pl.BlockSpec((META_PAD, V_DIM), lambda b, h, i: (0, h)),
            pl.BlockSpec((V_DIM, META_PAD), lambda b, h, i: (h, 0)),
            pl.BlockSpec((1, tq, tq), lambda b, h, i: (h, 0, 0)),
            pl.BlockSpec((1, tq, tq), lambda b, h, i: (h, 0, 0)),
            pl.BlockSpec((1, META_PAD, tq), lambda b, h, i: (h, 0, 0)),
            pl.BlockSpec((V_DIM, 1), lambda b, h, i: (0, 0)),
        ],
        out_specs=pl.BlockSpec((tq, V_DIM), lambda b, h, i: (b * nq + i, h)),
        out_shape=jax.ShapeDtypeStruct((n, w), BF16),
        scratch_shapes=[pltpu.VMEM((2, 1, tq), F32), pltpu.VMEM((2, 1, tq), F32),
                        pltpu.VMEM((2, V_DIM, tq), F32)],
        compiler_params=pltpu.CompilerParams(
            dimension_semantics=("arbitrary", "arbitrary", "arbitrary"),
            vmem_limit_bytes=V7X_VMEM_LIMIT),
        name="diff_attention",
    )(lam_params, q, k, vt, km, vmt, boff, bdiag, bmeta, gain)


def _mixer_kernel(x_ref, o_ref, u_ref, uh_ref, mh_ref, sga_ref, sgc_ref,
                  wap_ref, wpw_ref, wout_ref, dww_ref, dwb_ref, lng_ref, lnb_ref,
                  nffn_ref, wr_ref, br_ref, tri_ref,
                  h_ref, hn_ref, exp_ref, rank_ref, gate_ref, cnt_ref,
                  ext_s, y_s, cnt_s, *, tm, tiles_per_seq):
    i = pl.program_id(0)
    d = x_ref.shape[1]

    @pl.when(i == 0)
    def _():
        cnt_s[...] = jnp.zeros_like(cnt_s)

    first = (i % tiles_per_seq) == 0
    halo = jnp.where(first, mh_ref[...], uh_ref[...])
    ext_s[0:HALO, :] = halo.astype(F32)
    ext_s[HALO:HALO + tm, :] = u_ref[...].astype(F32)
    off = HALO - (CONV_K - 1)
    for c in range(d // 128):
        cs = slice(c * 128, (c + 1) * 128)
        acc = jnp.zeros((tm, 128), F32)
        for t in range(CONV_K):
            acc = acc + dww_ref[t:t + 1, cs] * ext_s[off + t:off + t + tm, cs]
        y_s[:, cs] = acc
    y = y_s[...] + dwb_ref[...]
    mu = jnp.mean(y, axis=-1, keepdims=True)
    yc = y - mu
    var = jnp.mean(yc * yc, axis=-1, keepdims=True)
    y = yc * lax.rsqrt(var + EPS) * lng_ref[...] + lnb_ref[...]
    y = y * _sigmoid(y)
    y_conv = jnp.dot(y.astype(BF16), wpw_ref[...], preferred_element_type=F32)

    y_attn = jnp.dot(o_ref[...], wap_ref[...], preferred_element_type=F32)
    mixed = sga_ref[...].astype(F32) * y_attn + sgc_ref[...].astype(F32) * y_conv
    h = x_ref[...] + jnp.dot(mixed.astype(BF16), wout_ref[...], preferred_element_type=F32)
    h_ref[...] = h
    hn = h * lax.rsqrt(jnp.mean(h * h, axis=-1, keepdims=True) + EPS) * nffn_ref[...]
    hn_ref[...] = hn

    logits = lax.dot_general(wr_ref[...], hn, (((1,), (1,)), ((), ())),
                             precision=lax.Precision.HIGHEST,
                             preferred_element_type=F32) + br_ref[...]
    eio = lax.broadcasted_iota(jnp.int32, logits.shape, 0)
    sels, tops, idxs = [], [], []
    cur = logits
    for _ in range(TOP_K):
        m = jnp.max(cur, axis=0, keepdims=True)
        idx = jnp.min(jnp.where(cur == m, eio, N_EXPERTS), axis=0, keepdims=True)
        sel = eio == idx
        sels.append(sel)
        tops.append(m)
        idxs.append(idx)
        cur = jnp.where(sel, -jnp.inf, cur)
    exps = [jnp.exp(t - tops[0]) for t in tops]
    denom = exps[0] + exps[1] + exps[2] + exps[3]
    hot = jnp.zeros(logits.shape, F32)
    for sel in sels:
        hot = hot + jnp.where(sel, 1.0, 0.0)
    before = jnp.dot(hot.astype(BF16), tri_ref[...], preferred_element_type=F32) + cnt_s[...]
    ranks = [jnp.sum(jnp.where(sel, before, 0.0), axis=0, keepdims=True).astype(jnp.int32)
             for sel in sels]
    zi = jnp.zeros((ROUTE_ROWS - TOP_K, tm), jnp.int32)
    exp_ref[0] = jnp.concatenate(idxs + [zi], axis=0)
    rank_ref[0] = jnp.concatenate(ranks + [zi], axis=0)
    gate_ref[0] = jnp.concatenate([e / denom for e in exps] + [zi.astype(F32)], axis=0)
    cnt_s[...] = cnt_s[...] + jnp.sum(hot, axis=1, keepdims=True)
    cnt_ref[...] = jnp.broadcast_to(cnt_s[...], cnt_ref.shape)


def _mixer(x2d, o, u, sga, sgc, meta_halo, wap, wpw, wout, dww, dwb, lng, lnb, nffn, wr_t, br,
           seq, tm):
    n, d = x2d.shape
    nt = n // tm
    hb = tm // HALO
    tri = (jnp.arange(tm)[:, None] < jnp.arange(tm)[None, :]).astype(BF16)
    row = pl.BlockSpec((tm, d), lambda i: (i, 0))
    route = pl.BlockSpec((1, ROUTE_ROWS, tm), lambda i: (i, 0, 0))

    def full(shp):
        return pl.BlockSpec(shp, lambda i: tuple(0 for _ in shp))

    kern = functools.partial(_mixer_kernel, tm=tm, tiles_per_seq=seq // tm)
    return pl.pallas_call(
        kern,
        grid=(nt,),
        in_specs=[row, row, row,
                  pl.BlockSpec((HALO, d), lambda i: (jnp.maximum(i * hb - 1, 0), 0)),
                  full((HALO, d)), row, row,
                  full((d, d)), full((d, d)), full((d, d)),
                  full((CONV_K, d)), full((1, d)), full((1, d)), full((1, d)), full((1, d)),
                  full((N_EXPERTS, d)), full((N_EXPERTS, 1)), full((tm, tm))],
        out_specs=[row, row, route, route, route, full((N_EXPERTS, 128))],
        out_shape=[jax.ShapeDtypeStruct((n, d), F32),
                   jax.ShapeDtypeStruct((n, d), F32),
                   jax.ShapeDtypeStruct((nt, ROUTE_ROWS, tm), jnp.int32),
                   jax.ShapeDtypeStruct((nt, ROUTE_ROWS, tm), jnp.int32),
                   jax.ShapeDtypeStruct((nt, ROUTE_ROWS, tm), F32),
                   jax.ShapeDtypeStruct((N_EXPERTS, 128), F32)],
        scratch_shapes=[pltpu.VMEM((HALO + tm, d), F32), pltpu.VMEM((tm, d), F32),
                        pltpu.VMEM((N_EXPERTS, 1), F32)],
        compiler_params=pltpu.CompilerParams(
            dimension_semantics=("arbitrary",), vmem_limit_bytes=V7X_VMEM_LIMIT),
        name="mixer_router",
    )(x2d, o, u, u, meta_halo, sga, sgc, wap, wpw, wout, dww, dwb, lng, lnb, nffn, wr_t, br, tri)


def _dispatch_kernel(dest_hbm, hn_ref, xs_hbm, dsm_s, sem_idx, sem_rows, *, tm):
    i = pl.program_id(0)
    idx_cp = pltpu.make_async_copy(dest_hbm.at[i], dsm_s, sem_idx)
    idx_cp.start()
    idx_cp.wait()

    def row_copy(r, dst):
        return pltpu.make_async_copy(hn_ref.at[pl.ds(r, 1), :], xs_hbm.at[pl.ds(dst, 1), :],
                                     sem_rows)

    def issue(r, c):
        for kk in range(TOP_K):
            row_copy(r, dsm_s[kk, r]).start()
        return c
    lax.fori_loop(0, tm, issue, 0)

    def drain(r, c):
        row_copy(0, 0).wait()
        return c
    lax.fori_loop(0, TOP_K * tm, drain, 0)


def _dispatch(dest, hn, n_slots, tm):
    n, d = hn.shape
    return pl.pallas_call(
        functools.partial(_dispatch_kernel, tm=tm),
        grid=(n // tm,),
        in_specs=[pl.BlockSpec(memory_space=pl.ANY),
                  pl.BlockSpec((tm, d), lambda i: (i, 0))],
        out_specs=pl.BlockSpec(memory_space=pl.ANY),
        out_shape=jax.ShapeDtypeStruct((n_slots, d), F32),
        scratch_shapes=[pltpu.SMEM((ROUTE_ROWS, tm), jnp.int32),
                        pltpu.SemaphoreType.DMA(()), pltpu.SemaphoreType.DMA(())],
        compiler_params=pltpu.CompilerParams(
            dimension_semantics=("arbitrary",), vmem_limit_bytes=V7X_VMEM_LIMIT,
            has_side_effects=True),
        name="dispatch",
    )(dest, hn)


def _expert_kernel(bexp_ref, bvalid_ref, xs_ref, wgu_ref, bgu_ref, wdn_ref, bdn_ref, ys_ref):
    i = pl.program_id(0)
    valid = bvalid_ref[i]
    f = wdn_ref.shape[1]

    @pl.when(valid > 0)
    def _():
        rows = lax.broadcasted_iota(jnp.int32, xs_ref.shape, 0)
        xb = jnp.where(rows < valid, xs_ref[...], 0.0).astype(BF16)
        gu = jnp.dot(xb, wgu_ref[0], preferred_element_type=F32) + bgu_ref[0]
        glu = jnp.minimum(gu[:, :f], SWIGLU_LIMIT)
        lin = jnp.clip(gu[:, f:], -SWIGLU_LIMIT, SWIGLU_LIMIT)
        act = glu * _sigmoid(SWIGLU_ALPHA * glu) * (lin + 1.0)
        ys_ref[...] = jnp.dot(act.astype(BF16), wdn_ref[0], preferred_element_type=F32) + bdn_ref[0]

    @pl.when(valid <= 0)
    def _():
        ys_ref[...] = jnp.zeros_like(ys_ref)


def _experts(block_exp, block_valid, xs, wgu, bgu, wdn, bdn):
    n_slots, d = xs.shape
    nb = n_slots // EXPERT_BLOCK
    f2 = wgu.shape[2]
    f = wdn.shape[1]
    grid_spec = pltpu.PrefetchScalarGridSpec(
        num_scalar_prefetch=2,
        grid=(nb,),
        in_specs=[pl.BlockSpec((EXPERT_BLOCK, d), lambda i, be, bv: (i, 0)),
                  pl.BlockSpec((1, d, f2), lambda i, be, bv: (be[i], 0, 0)),
                  pl.BlockSpec((1, 1, f2), lambda i, be, bv: (be[i], 0, 0)),
                  pl.BlockSpec((1, f, d), lambda i, be, bv: (be[i], 0, 0)),
                  pl.BlockSpec((1, 1, d), lambda i, be, bv: (be[i], 0, 0))],
        out_specs=pl.BlockSpec((EXPERT_BLOCK, d), lambda i, be, bv: (i, 0)),
    )
    return pl.pallas_call(
        _expert_kernel,
        grid_spec=grid_spec,
        out_shape=jax.ShapeDtypeStruct((n_slots, d), F32),
        compiler_params=pltpu.CompilerParams(
            dimension_semantics=("arbitrary",), vmem_limit_bytes=V7X_VMEM_LIMIT),
        name="experts",
    )(block_exp, block_valid, xs, wgu, bgu, wdn, bdn)


def _combine_kernel(dest_hbm, h_ref, gate_ref, g_ref, ys_hbm, out_ref, dsm_s, ybuf, sem_idx, sem_rows,
                    *, tm):
    i = pl.program_id(0)
    idx_cp = pltpu.make_async_copy(dest_hbm.at[i], dsm_s, sem_idx)
    idx_cp.start()
    idx_cp.wait()

    def row_copy(kk, r, src):
        return pltpu.make_async_copy(ys_hbm.at[pl.ds(src, 1), :], ybuf.at[kk, pl.ds(r, 1), :],
                                     sem_rows)

    def issue(r, c):
        for kk in range(TOP_K):
            row_copy(kk, r, dsm_s[kk, r]).start()
        return c
    lax.fori_loop(0, tm, issue, 0)

    def drain(r, c):
        row_copy(0, 0, 0).wait()
        return c
    lax.fori_loop(0, TOP_K * tm, drain, 0)

    gates = jnp.transpose(gate_ref[0])
    acc = h_ref[...]
    for kk in range(TOP_K):
        acc = acc + gates[:, kk:kk + 1] * ybuf[kk]
    out_ref[...] = acc * lax.rsqrt(jnp.mean(acc * acc, axis=-1, keepdims=True) + EPS) * g_ref[...]


def _combine(dest, h, gates, g_final, ys, tm):
    n, d = h.shape
    return pl.pallas_call(
        functools.partial(_combine_kernel, tm=tm),
        grid=(n // tm,),
        in_specs=[pl.BlockSpec(memory_space=pl.ANY),
                  pl.BlockSpec((tm, d), lambda i: (i, 0)),
                  pl.BlockSpec((1, ROUTE_ROWS, tm), lambda i: (i, 0, 0)),
                  pl.BlockSpec((1, d), lambda i: (0, 0)),
                  pl.BlockSpec(memory_space=pl.ANY)],
        out_specs=pl.BlockSpec((tm, d), lambda i: (i, 0)),
        out_shape=jax.ShapeDtypeStruct((n, d), F32),
        scratch_shapes=[pltpu.SMEM((ROUTE_ROWS, tm), jnp.int32),
                        pltpu.VMEM((TOP_K, tm, d), F32),
                        pltpu.SemaphoreType.DMA(()), pltpu.SemaphoreType.DMA(())],
        compiler_params=pltpu.CompilerParams(
            dimension_semantics=("arbitrary",), vmem_limit_bytes=V7X_VMEM_LIMIT),
        name="combine",
    )(dest, h, gates, g_final, ys)


def kernel(x, meta_tokens, norm_mix, w_in, b_in, lam_params, subln_gain, w_attn_proj, conv_dw_w,
           conv_dw_b, conv_ln_g, conv_ln_b, w_conv_proj, w_out, norm_ffn, w_router, b_router,
           w_gate_up, b_gate_up, w_down, b_down, norm_final):
    bsz, seq, d = x.shape
    assert norm_mix.shape[0] == 1, "single-layer block"
    n = bsz * seq
    tm_proj = min(512, seq)
    tq = min(512, seq)
    tm = min(256, seq)
    lam_init = 0.8 - 0.6 * 1.0

    x2d = x.reshape(n, d)
    w_rest = jnp.concatenate([w_in[0][:, :2 * d], w_in[0][:, 3 * d:]], axis=1).astype(BF16)
    b_rest = jnp.concatenate([b_in[0][:2 * d], b_in[0][3 * d:]])[None]
    w_vt = w_in[0][:, 2 * d:3 * d].T.astype(BF16)
    b_vt = b_in[0][2 * d:3 * d][:, None]
    g_mix = norm_mix[0][None]
    q, k, vt, u, sga, sgc = _inproj(x2d, g_mix, w_rest, b_rest, w_vt, b_vt, tm_proj)
    _, k_m, vt_m, u_m, _, _ = _inproj(meta_tokens.astype(x.dtype), g_mix, w_rest, b_rest, w_vt, b_vt,
                                      N_META)

    o = _attention(lam_params[0], q, k, vt, k_m, vt_m, subln_gain[0][:, None], bsz, seq, tq,
                   lam_init)

    meta_halo = jnp.concatenate([jnp.zeros((HALO - N_META, d), BF16), u_m], axis=0)
    h, hn, eidx, rank, gates, counts = _mixer(
        x2d, o, u, sga, sgc, meta_halo,
        w_attn_proj[0].astype(BF16), w_conv_proj[0].astype(BF16), w_out[0].astype(BF16),
        conv_dw_w[0], conv_dw_b[0][None], conv_ln_g[0][None], conv_ln_b[0][None],
        norm_ffn[0][None], w_router[0].T, b_router[0][:, None], seq, tm)

    cnt = counts[:, 0].astype(jnp.int32)
    padded = (cnt + EXPERT_BLOCK - 1) // EXPERT_BLOCK * EXPERT_BLOCK
    pad_end = jnp.cumsum(padded)
    pad_start = pad_end - padded
    n_blocks = -(-(n * TOP_K + N_EXPERTS * (EXPERT_BLOCK - 1)) // EXPERT_BLOCK)
    n_slots = n_blocks * EXPERT_BLOCK
    dest = pad_start[eidx] + rank
    block_start = jnp.arange(n_blocks, dtype=jnp.int32) * EXPERT_BLOCK
    block_exp = jnp.minimum(jnp.searchsorted(pad_end, block_start, side="right"),
                            N_EXPERTS - 1).astype(jnp.int32)
    block_valid = jnp.clip(pad_start[block_exp] + cnt[block_exp] - block_start, 0,
                           EXPERT_BLOCK).astype(jnp.int32)

    xs = _dispatch(dest, hn, n_slots, tm)
    ys = _experts(block_exp, block_valid, xs,
                  w_gate_up[0].astype(BF16), b_gate_up[0][:, None, :],
                  w_down[0].astype(BF16), b_down[0][:, None, :])
    out = _combine(dest, h, gates, norm_final[None], ys, tm)
    return out.reshape(bsz, seq, d)
```

```python
import functools

import jax
import jax.numpy as jnp
from jax import lax
from jax.experimental import pallas as pl
from jax.experimental.pallas import tpu as pltpu

CHUNK = 64
N_META = 16
N_HEADS = 8
HEAD_DIM = 64
V_DIM = 2 * HEAD_DIM
CONV_K = 31
N_EXPERTS = 32
TOP_K = 4
SWIGLU_ALPHA = 1.702
SWIGLU_LIMIT = 7.0
EXPERT_BLOCK = 256
EPS = 1e-5
NEG = -1e30
LOG2E = 1.4426950408889634

V7X_VMEM_LIMIT = 56 * 1024 * 1024
META_PAD = 128
HALO = 32
ROUTE_ROWS = 8

BF16 = jnp.bfloat16
F32 = jnp.float32


def _sigmoid(x):
    return 1.0 / (1.0 + jnp.exp(-x))


def _inproj_kernel(x_ref, g_ref, w_ref, b_ref, wvt_ref, bvt_ref,
                   q_ref, k_ref, vt_ref, u_ref, sga_ref, sgc_ref):
    d = x_ref.shape[1]
    x = x_ref[...]
    xn = x * lax.rsqrt(jnp.mean(x * x, axis=-1, keepdims=True) + EPS) * g_ref[...]
    xb = xn.astype(BF16)

    def proj(c):
        return (jnp.dot(xb, w_ref[:, c * d:(c + 1) * d], preferred_element_type=F32)
                + b_ref[:, c * d:(c + 1) * d])

    q_ref[...] = (proj(0) * (HEAD_DIM ** -0.5 * LOG2E)).astype(BF16)
    k_ref[...] = proj(1).astype(BF16)
    vt = lax.dot_general(wvt_ref[...], xb, (((1,), (1,)), ((), ())), preferred_element_type=F32)
    vt_ref[...] = (vt + bvt_ref[...]).astype(BF16)
    u_ref[...] = (proj(2) * _sigmoid(proj(3))).astype(BF16)
    sga_ref[...] = _sigmoid(proj(4)).astype(BF16)
    sgc_ref[...] = _sigmoid(proj(5)).astype(BF16)


def _inproj(x2d, g, w_bf, b, wvt_bf, bvt, tm):
    n, d = x2d.shape
    in_w = w_bf.shape[1]
    out = jax.ShapeDtypeStruct((n, d), BF16)
    row = pl.BlockSpec((tm, d), lambda i: (i, 0))
    col = pl.BlockSpec((d, tm), lambda i: (0, i))
    return pl.pallas_call(
        _inproj_kernel,
        grid=(n // tm,),
        in_specs=[row,
                  pl.BlockSpec((1, d), lambda i: (0, 0)),
                  pl.BlockSpec((d, in_w), lambda i: (0, 0)),
                  pl.BlockSpec((1, in_w), lambda i: (0, 0)),
                  pl.BlockSpec((d, d), lambda i: (0, 0)),
                  pl.BlockSpec((d, 1), lambda i: (0, 0))],
        out_specs=[row, row, col, row, row, row],
        out_shape=[out, out, jax.ShapeDtypeStruct((d, n), BF16), out, out, out],
        compiler_params=pltpu.CompilerParams(
            dimension_semantics=("arbitrary",), vmem_limit_bytes=V7X_VMEM_LIMIT),
        name="inproj",
    )(x2d, g, w_bf, b, wvt_bf, bvt)


def _attn_kernel(lam_ref, q_ref, k_ref, vt_ref, km_ref, vmt_ref, boff_ref, bdiag_ref, bmeta_ref,
                 gain_ref, o_ref, m_s, l_s, acc_s, *, tq, lam_init):
    h = pl.program_id(1)
    qi = pl.program_id(2)
    slope = jnp.exp2(-(h + 1).astype(F32))
    lp = lam_ref[...]
    lam = (jnp.exp(jnp.sum(lp[0:1] * lp[1:2], axis=-1, keepdims=True))
           - jnp.exp(jnp.sum(lp[2:3] * lp[3:4], axis=-1, keepdims=True)) + lam_init)

    q = q_ref[...]
    lane = lax.broadcasted_iota(jnp.int32, q.shape, 1)
    zero = jnp.zeros_like(q)
    qs = (jnp.where(lane < HEAD_DIM, q, zero), jnp.where(lane >= HEAD_DIM, q, zero))

    def scores_t(kk, qm):
        return lax.dot_general(kk, qm, (((1,), (1,)), ((), ())), preferred_element_type=F32)

    def update(idx, s, vt, first, shift=None):
        smax = jnp.max(s, axis=0, keepdims=True)
        if shift is not None:
            smax = smax - shift
        if first:
            m_new = smax
        else:
            m_new = jnp.maximum(m_s[idx], smax)
        p = jnp.exp2(s - (m_new if shift is None else m_new + shift))
        ps = jnp.sum(p, axis=0, keepdims=True)
        pv = jnp.dot(vt, p.astype(BF16), preferred_element_type=F32)
        if first:
            l_s[idx] = ps
            acc_s[idx] = pv
        else:
            alpha = jnp.exp2(m_s[idx] - m_new)
            l_s[idx] = alpha * l_s[idx] + ps
            acc_s[idx] = alpha * acc_s[idx] + pv
        m_s[idx] = m_new

    row0 = (qi * tq).astype(F32) * LOG2E
    km = km_ref[...]
    vmt = vmt_ref[...]
    for idx in range(2):
        s = scores_t(km, qs[idx]) + bmeta_ref[0]
        update(idx, s, vmt, True, slope * row0)

    def body(j, carry):
        start = pl.multiple_of(j * tq, tq)
        kk = k_ref[pl.ds(start, tq), :]
        vt = vt_ref[:, pl.ds(start, tq)]
        shift = slope * (((qi - j) * tq).astype(F32) * LOG2E)
        for idx in range(2):
            s = scores_t(kk, qs[idx]) + boff_ref[0]
            update(idx, s, vt, False, shift)
        return carry

    lax.fori_loop(0, qi, body, 0)

    start = pl.multiple_of(qi * tq, tq)
    kk = k_ref[pl.ds(start, tq), :]
    vt = vt_ref[:, pl.ds(start, tq)]
    for idx in range(2):
        s = scores_t(kk, qs[idx]) + bdiag_ref[0]
        update(idx, s, vt, False)

    o = acc_s[0] / l_s[0] - lam * (acc_s[1] / l_s[1])
    o = o * lax.rsqrt(jnp.mean(o * o, axis=0, keepdims=True) + EPS) * gain_ref[...]
    o_ref[...] = jnp.transpose(o * (1.0 - lam_init)).astype(BF16)


def _attention(lam_params, q, k, vt, k_meta, vt_meta, gain, bsz, seq, tq, lam_init):
    n, w = q.shape
    nq = seq // tq
    slopes = jnp.asarray([2.0 ** (-8.0 * (h + 1) / N_HEADS) for h in range(N_HEADS)], F32)
    r = jnp.arange(tq, dtype=jnp.int32)
    diff = r[None, :] - r[:, None]
    slopes = slopes * LOG2E
    boff = -slopes[:, None, None] * diff.astype(F32)[None]
    vis = (r[:, None] // CHUNK) <= (r[None, :] // CHUNK)
    bdiag = jnp.where(vis[None], -slopes[:, None, None] * jnp.abs(diff).astype(F32)[None], NEG)
    mrow = jnp.arange(META_PAD, dtype=jnp.int32)
    dmeta = (N_META + r[None, :] - mrow[:, None]).astype(F32)
    bmeta = jnp.where((mrow < N_META)[None, :, None], -slopes[:, None, None] * dmeta[None], NEG)

    km = jnp.pad(k_meta, ((0, META_PAD - N_META), (0, 0)))
    vmt = jnp.pad(vt_meta, ((0, 0), (0, META_PAD - N_META)))

    kern = functools.partial(_attn_kernel, tq=tq, lam_init=lam_init)
    return pl.pallas_call(
        kern,
        grid=(bsz, N_HEADS, nq),
        in_specs=[
            pl.BlockSpec((4, HEAD_DIM), lambda b, h, i: (0, 0)),
            pl.BlockSpec((tq, V_DIM), lambda b, h, i: (b * nq + i, h)),
            pl.BlockSpec((seq, V_DIM), lambda b, h, i: (b, h)),
            pl.BlockSpec((V_DIM, seq), lambda b, h, i: (h, b)),
            pl.BlockSpec((META_PAD, V_DIM), lambda b, h, i: (0, h)),
            pl.BlockSpec((V_DIM, META_PAD), lambda b, h, i: (h, 0)),
            pl.BlockSpec((1, tq, tq), lambda b, h, i: (h, 0, 0)),
            pl.BlockSpec((1, tq, tq), lambda b, h, i: (h, 0, 0)),
            pl.BlockSpec((1, META_PAD, tq), lambda b, h, i: (h, 0, 0)),
            pl.BlockSpec((V_DIM, 1), lambda b, h, i: (0, 0)),
        ],
        out_specs=pl.BlockSpec((tq, V_DIM), lambda b, h, i: (b * nq + i, h)),
        out_shape=jax.ShapeDtypeStruct((n, w), BF16),
        scratch_shapes=[pltpu.VMEM((2, 1, tq), F32), pltpu.VMEM((2, 1, tq), F32),
                        pltpu.VMEM((2, V_DIM, tq), F32)],
        compiler_params=pltpu.CompilerParams(
            dimension_semantics=("arbitrary", "arbitrary", "arbitrary"),
            vmem_limit_bytes=V7X_VMEM_LIMIT),
        name="diff_attention",
    )(lam_params, q, k, vt, km, vmt, boff, bdiag, bmeta, gain)


def _mixer_kernel(x_ref, o_ref, u_ref, uh_ref, mh_ref, sga_ref, sgc_ref,
                  wap_ref, wpw_ref, wout_ref, dww_ref, dwb_ref, lng_ref, lnb_ref,
                  nffn_ref, wr_ref, br_ref, tri_ref,
                  h_ref, hn_ref, exp_ref, rank_ref, gate_ref, cnt_ref,
                  ext_s, y_s, cnt_s, *, tm, tiles_per_seq):
    i = pl.program_id(0)
    d = x_ref.shape[1]

    @pl.when(i == 0)
    def _():
        cnt_s[...] = jnp.zeros_like(cnt_s)

    first = (i % tiles_per_seq) == 0
    halo = jnp.where(first, mh_ref[...], uh_ref[...])
    ext_s[0:HALO, :] = halo.astype(F32)
    ext_s[HALO:HALO + tm, :] = u_ref[...].astype(F32)
    off = HALO - (CONV_K - 1)
    for c in range(d // 128):
        cs = slice(c * 128, (c + 1) * 128)
        acc = jnp.zeros((tm, 128), F32)
        for t in range(CONV_K):
            acc = acc + dww_ref[t:t + 1, cs] * ext_s[off + t:off + t + tm, cs]
        y_s[:, cs] = acc
    y = y_s[...] + dwb_ref[...]
    mu = jnp.mean(y, axis=-1, keepdims=True)
    yc = y - mu
    var = jnp.mean(yc * yc, axis=-1, keepdims=True)
    y = yc * lax.rsqrt(var + EPS) * lng_ref[...] + lnb_ref[...]
    y = y * _sigmoid(y)
    y_conv = jnp.dot(y.astype(BF16), wpw_ref[...], preferred_element_type=F32)

    y_attn = jnp.dot(o_ref[...], wap_ref[...], preferred_element_type=F32)
    mixed = sga_ref[...].astype(F32) * y_attn + sgc_ref[...].astype(F32) * y_conv
    h = x_ref[...] + jnp.dot(mixed.astype(BF16), wout_ref[...], preferred_element_type=F32)
    h_ref[...] = h
    hn = h * lax.rsqrt(jnp.mean(h * h, axis=-1, keepdims=True) + EPS) * nffn_ref[...]
    hn_ref[...] = hn

    logits = lax.dot_general(wr_ref[...], hn, (((1,), (1,)), ((), ())),
                             precision=lax.Precision.HIGHEST,
                             preferred_element_type=F32) + br_ref[...]
    eio = lax.broadcasted_iota(jnp.int32, logits.shape, 0)
    sels, tops, idxs = [], [], []
    cur = logits
    for _ in range(TOP_K):
        m = jnp.max(cur, axis=0, keepdims=True)
        idx = jnp.min(jnp.where(cur == m, eio, N_EXPERTS), axis=0, keepdims=True)
        sel = eio == idx
        sels.append(sel)
        tops.append(m)
        idxs.append(idx)
        cur = jnp.where(sel, -jnp.inf, cur)
    exps = [jnp.exp(t - tops[0]) for t in tops]
    denom = exps[0] + exps[1] + exps[2] + exps[3]
    hot = jnp.zeros(logits.shape, F32)
    for sel in sels:
        hot = hot + jnp.where(sel, 1.0, 0.0)
    before = jnp.dot(hot.astype(BF16), tri_ref[...], preferred_element_type=F32) + cnt_s[...]
    ranks = [jnp.sum(jnp.where(sel, before, 0.0), axis=0, keepdims=True).astype(jnp.int32)
             for sel in sels]
    zi = jnp.zeros((ROUTE_ROWS - TOP_K, tm), jnp.int32)
    exp_ref[0] = jnp.concatenate(idxs + [zi], axis=0)
    rank_ref[0] = jnp.concatenate(ranks + [zi], axis=0)
    gate_ref[0] = jnp.concatenate([e / denom for e in exps] + [zi.astype(F32)], axis=0)
    cnt_s[...] = cnt_s[...] + jnp.sum(hot, axis=1, keepdims=True)
    cnt_ref[...] = jnp.broadcast_to(cnt_s[...], cnt_ref.shape)


def _mixer(x2d, o, u, sga, sgc, meta_halo, wap, wpw, wout, dww, dwb, lng, lnb, nffn, wr_t, br,
           seq, tm):
    n, d = x2d.shape
    nt = n // tm
    hb = tm // HALO
    tri = (jnp.arange(tm)[:, None] < jnp.arange(tm)[None, :]).astype(BF16)
    row = pl.BlockSpec((tm, d), lambda i: (i, 0))
    route = pl.BlockSpec((1, ROUTE_ROWS, tm), lambda i: (i, 0, 0))

    def full(shp):
        return pl.BlockSpec(shp, lambda i: tuple(0 for _ in shp))

    kern = functools.partial(_mixer_kernel, tm=tm, tiles_per_seq=seq // tm)
    return pl.pallas_call(
        kern,
        grid=(nt,),
        in_specs=[row, row, row,
                  pl.BlockSpec((HALO, d), lambda i: (jnp.maximum(i * hb - 1, 0), 0)),
                  full((HALO, d)), row, row,
                  full((d, d)), full((d, d)), full((d, d)),
                  full((CONV_K, d)), full((1, d)), full((1, d)), full((1, d)), full((1, d)),
                  full((N_EXPERTS, d)), full((N_EXPERTS, 1)), full((tm, tm))],
        out_specs=[row, row, route, route, route, full((N_EXPERTS, 128))],
        out_shape=[jax.ShapeDtypeStruct((n, d), F32),
                   jax.ShapeDtypeStruct((n, d), F32),
                   jax.ShapeDtypeStruct((nt, ROUTE_ROWS, tm), jnp.int32),
                   jax.ShapeDtypeStruct((nt, ROUTE_ROWS, tm), jnp.int32),
                   jax.ShapeDtypeStruct((nt, ROUTE_ROWS, tm), F32),
                   jax.ShapeDtypeStruct((N_EXPERTS, 128), F32)],
        scratch_shapes=[pltpu.VMEM((HALO + tm, d), F32), pltpu.VMEM((tm, d), F32),
                        pltpu.VMEM((N_EXPERTS, 1), F32)],
        compiler_params=pltpu.CompilerParams(
            dimension_semantics=("arbitrary",), vmem_limit_bytes=V7X_VMEM_LIMIT),
        name="mixer_router",
    )(x2d, o, u, u, meta_halo, sga, sgc, wap, wpw, wout, dww, dwb, lng, lnb, nffn, wr_t, br, tri)


def _prefetch_routes(dest_hbm, dsm_s, sem_idx):
    i = pl.program_id(0)
    slot = i % 2

    def idx_copy(tile, s):
        return pltpu.make_async_copy(dest_hbm.at[tile], dsm_s.at[s], sem_idx.at[s])

    @pl.when(i == 0)
    def _():
        idx_copy(0, 0).start()

    idx_copy(i, slot).wait()

    @pl.when(i + 1 < pl.num_programs(0))
    def _():
        idx_copy(i + 1, 1 - slot).start()

    return slot


def _dispatch_kernel(dest_hbm, hn_ref, xs_hbm, dsm_s, sem_idx, sem_rows, *, tm):
    slot = _prefetch_routes(dest_hbm, dsm_s, sem_idx)

    def issue(r, c):
        for kk in range(TOP_K):
            pltpu.make_async_copy(hn_ref.at[pl.ds(r, 1), :],
                                  xs_hbm.at[pl.ds(dsm_s[slot, kk, r], 1), :], sem_rows).start()
        return c
    lax.fori_loop(0, tm, issue, 0, unroll=8)

    for kk in range(TOP_K):
        pltpu.make_async_copy(hn_ref, xs_hbm.at[pl.ds(0, tm), :], sem_rows).wait()


def _dispatch(dest, hn, n_slots, tm):
    n, d = hn.shape
    return pl.pallas_call(
        functools.partial(_dispatch_kernel, tm=tm),
        grid=(n // tm,),
        in_specs=[pl.BlockSpec(memory_space=pl.ANY),
                  pl.BlockSpec((tm, d), lambda i: (i, 0))],
        out_specs=pl.BlockSpec(memory_space=pl.ANY),
        out_shape=jax.ShapeDtypeStruct((n_slots, d), F32),
        scratch_shapes=[pltpu.SMEM((2, ROUTE_ROWS, tm), jnp.int32),
                        pltpu.SemaphoreType.DMA((2,)), pltpu.SemaphoreType.DMA(())],
        compiler_params=pltpu.CompilerParams(
            dimension_semantics=("arbitrary",), vmem_limit_bytes=V7X_VMEM_LIMIT),
        name="dispatch",
    )(dest, hn)


def _expert_kernel(bexp_ref, bvalid_ref, xs_ref, wgu_ref, bgu_ref, wdn_ref, bdn_ref, ys_ref):
    i = pl.program_id(0)
    valid = bvalid_ref[i]
    f = wdn_ref.shape[1]

    @pl.when(valid > 0)
    def _():
        rows = lax.broadcasted_iota(jnp.int32, xs_ref.shape, 0)
        xb = jnp.where(rows < valid, xs_ref[...], 0.0).astype(BF16)
        gu = jnp.dot(xb, wgu_ref[0], preferred_element_type=F32) + bgu_ref[0]
        glu = jnp.minimum(gu[:, :f], SWIGLU_LIMIT)
        lin = jnp.clip(gu[:, f:], -SWIGLU_LIMIT, SWIGLU_LIMIT)
        act = glu * _sigmoid(SWIGLU_ALPHA * glu) * (lin + 1.0)
        ys_ref[...] = jnp.dot(act.astype(BF16), wdn_ref[0], preferred_element_type=F32) + bdn_ref[0]

    @pl.when(valid <= 0)
    def _():
        ys_ref[...] = jnp.zeros_like(ys_ref)


def _experts(block_exp, block_valid, xs, wgu, bgu, wdn, bdn):
    n_slots, d = xs.shape
    nb = n_slots // EXPERT_BLOCK
    f2 = wgu.shape[2]
    f = wdn.shape[1]
    grid_spec = pltpu.PrefetchScalarGridSpec(
        num_scalar_prefetch=2,
        grid=(nb,),
        in_specs=[pl.BlockSpec((EXPERT_BLOCK, d), lambda i, be, bv: (i, 0)),
                  pl.BlockSpec((1, d, f2), lambda i, be, bv: (be[i], 0, 0)),
                  pl.BlockSpec((1, 1, f2), lambda i, be, bv: (be[i], 0, 0)),
                  pl.BlockSpec((1, f, d), lambda i, be, bv: (be[i], 0, 0)),
                  pl.BlockSpec((1, 1, d), lambda i, be, bv: (be[i], 0, 0))],
        out_specs=pl.BlockSpec((EXPERT_BLOCK, d), lambda i, be, bv: (i, 0)),
    )
    return pl.pallas_call(
        _expert_kernel,
        grid_spec=grid_spec,
        out_shape=jax.ShapeDtypeStruct((n_slots, d), F32),
        compiler_params=pltpu.CompilerParams(
            dimension_semantics=("arbitrary",), vmem_limit_bytes=V7X_VMEM_LIMIT),
        name="experts",
    )(block_exp, block_valid, xs, wgu, bgu, wdn, bdn)


def _combine_kernel(dest_hbm, h_ref, gate_ref, g_ref, ys_hbm, out_ref, dsm_s, ybuf, sem_idx, sem_rows,
                    *, tm):
    i = pl.program_id(0)
    nsteps = pl.num_programs(0)
    slot = i % 2

    def idx_copy(tile, s):
        return pltpu.make_async_copy(dest_hbm.at[tile], dsm_s.at[s], sem_idx.at[s])

    def issue_rows(s):
        def issue(r, c):
            for kk in range(TOP_K):
                pltpu.make_async_copy(ys_hbm.at[pl.ds(dsm_s[s, kk, r], 1), :],
                                      ybuf.at[s, kk, pl.ds(r, 1), :], sem_rows.at[s]).start()
            return c
        lax.fori_loop(0, tm, issue, 0, unroll=8)

    @pl.when(i == 0)
    def _():
        idx_copy(0, 0).start()
        idx_copy(0, 0).wait()
        issue_rows(0)

        @pl.when(nsteps > 1)
        def _():
            idx_copy(1, 1).start()

    @pl.when(i + 1 < nsteps)
    def _():
        idx_copy(i + 1, 1 - slot).wait()
        issue_rows(1 - slot)

        @pl.when(i + 2 < nsteps)
        def _():
            idx_copy(i + 2, slot).start()

    for kk in range(TOP_K):
        pltpu.make_async_copy(ys_hbm.at[pl.ds(0, tm), :], ybuf.at[slot, kk], sem_rows.at[slot]).wait()

    gates = jnp.transpose(gate_ref[0])
    acc = h_ref[...]
    for kk in range(TOP_K):
        acc = acc + gates[:, kk:kk + 1] * ybuf[slot, kk]
    out_ref[...] = acc * lax.rsqrt(jnp.mean(acc * acc, axis=-1, keepdims=True) + EPS) * g_ref[...]


def _combine(dest, h, gates, g_final, ys, tm):
    n, d = h.shape
    return pl.pallas_call(
        functools.partial(_combine_kernel, tm=tm),
        grid=(n // tm,),
        in_specs=[pl.BlockSpec(memory_space=pl.ANY),
                  pl.BlockSpec((tm, d), lambda i: (i, 0)),
                  pl.BlockSpec((1, ROUTE_ROWS, tm), lambda i: (i, 0, 0)),
                  pl.BlockSpec((1, d), lambda i: (0, 0)),
                  pl.BlockSpec(memory_space=pl.ANY)],
        out_specs=pl.BlockSpec((tm, d), lambda i: (i, 0)),
        out_shape=jax.ShapeDtypeStruct((n, d), F32),
        scratch_shapes=[pltpu.SMEM((2, ROUTE_ROWS, tm), jnp.int32),
                        pltpu.VMEM((2, TOP_K, tm, d), F32),
                        pltpu.SemaphoreType.DMA((2,)), pltpu.SemaphoreType.DMA((2,))],
        compiler_params=pltpu.CompilerParams(
            dimension_semantics=("arbitrary",), vmem_limit_bytes=V7X_VMEM_LIMIT),
        name="combine",
    )(dest, h, gates, g_final, ys)


def kernel(x, meta_tokens, norm_mix, w_in, b_in, lam_params, subln_gain, w_attn_proj, conv_dw_w,
           conv_dw_b, conv_ln_g, conv_ln_b, w_conv_proj, w_out, norm_ffn, w_router, b_router,
           w_gate_up, b_gate_up, w_down, b_down, norm_final):
    bsz, seq, d = x.shape
    assert norm_mix.shape[0] == 1, "single-layer block"
    n = bsz * seq
    tm_proj = min(512, seq)
    tq = min(512, seq)
    tm = min(256, seq)
    lam_init = 0.8 - 0.6 * 1.0

    x2d = x.reshape(n, d)
    w_rest = jnp.concatenate([w_in[0][:, :2 * d], w_in[0][:, 3 * d:]], axis=1).astype(BF16)
    b_rest = jnp.concatenate([b_in[0][:2 * d], b_in[0][3 * d:]])[None]
    w_vt = w_in[0][:, 2 * d:3 * d].T.astype(BF16)
    b_vt = b_in[0][2 * d:3 * d][:, None]
    g_mix = norm_mix[0][None]
    q, k, vt, u, sga, sgc = _inproj(x2d, g_mix, w_rest, b_rest, w_vt, b_vt, tm_proj)
    _, k_m, vt_m, u_m, _, _ = _inproj(meta_tokens.astype(x.dtype), g_mix, w_rest, b_rest, w_vt, b_vt,
                                      N_META)

    o = _attention(lam_params[0], q, k, vt, k_m, vt_m, subln_gain[0][:, None], bsz, seq, tq,
                   lam_init)

    meta_halo = jnp.concatenate([jnp.zeros((HALO - N_META, d), BF16), u_m], axis=0)
    h, hn, eidx, rank, gates, counts = _mixer(
        x2d, o, u, sga, sgc, meta_halo,
        w_attn_proj[0].astype(BF16), w_conv_proj[0].astype(BF16), w_out[0].astype(BF16),
        conv_dw_w[0], conv_dw_b[0][None], conv_ln_g[0][None], conv_ln_b[0][None],
        norm_ffn[0][None], w_router[0].T, b_router[0][:, None], seq, tm)

    cnt = counts[:, 0].astype(jnp.int32)
    padded = (cnt + EXPERT_BLOCK - 1) // EXPERT_BLOCK * EXPERT_BLOCK
    pad_end = jnp.cumsum(padded)
    pad_start = pad_end - padded
    n_blocks = -(-(n * TOP_K + N_EXPERTS * (EXPERT_BLOCK - 1)) // EXPERT_BLOCK)
    n_slots = n_blocks * EXPERT_BLOCK
    dest = pad_start[eidx] + rank
    block_start = jnp.arange(n_blocks, dtype=jnp.int32) * EXPERT_BLOCK
    block_exp = jnp.minimum(jnp.searchsorted(pad_end, block_start, side="right"),
                            N_EXPERTS - 1).astype(jnp.int32)
    block_valid = jnp.clip(pad_start[block_exp] + cnt[block_exp] - block_start, 0,
                           EXPERT_BLOCK).astype(jnp.int32)

    xs = _dispatch(dest, hn, n_slots, tm)
    ys = _experts(block_exp, block_valid, xs,
                  w_gate_up[0].astype(BF16), b_gate_up[0][:, None, :],
                  w_down[0].astype(BF16), b_down[0][:, None, :])
    out = _combine(dest, h, gates, norm_final[None], ys, tm)
    return out.reshape(bsz, seq, d)
```

```python
import functools

import jax
import jax.numpy as jnp
from jax import lax
from jax.experimental import pallas as pl
from jax.experimental.pallas import tpu as pltpu

CHUNK = 64
N_META = 16
N_HEADS = 8
HEAD_DIM = 64
V_DIM = 2 * HEAD_DIM
CONV_K = 31
N_EXPERTS = 32
TOP_K = 4
SWIGLU_ALPHA = 1.702
SWIGLU_LIMIT = 7.0
EXPERT_BLOCK = 256
EPS = 1e-5
NEG = -1e30
LOG2E = 1.4426950408889634

V7X_VMEM_LIMIT = 56 * 1024 * 1024
META_PAD = 128
HALO = 32
ROUTE_ROWS = 8

BF16 = jnp.bfloat16
F32 = jnp.float32


def _sigmoid(x):
    return 1.0 / (1.0 + jnp.exp(-x))


def _inproj_kernel(x_ref, g_ref, w_ref, b_ref, wvt_ref, bvt_ref,
                   q_ref, k_ref, vt_ref, u_ref, sga_ref, sgc_ref):
    d = x_ref.shape[1]
    x = x_ref[...]
    xn = x * lax.rsqrt(jnp.mean(x * x, axis=-1, keepdims=True) + EPS) * g_ref[...]
    xb = xn.astype(BF16)

    def proj(c):
        return (jnp.dot(xb, w_ref[:, c * d:(c + 1) * d], preferred_element_type=F32)
                + b_ref[:, c * d:(c + 1) * d])

    q_ref[...] = (proj(0) * (HEAD_DIM ** -0.5 * LOG2E)).astype(BF16)
    k_ref[...] = proj(1).astype(BF16)
    vt = lax.dot_general(wvt_ref[...], xb, (((1,), (1,)), ((), ())), preferred_element_type=F32)
    vt_ref[...] = (vt + bvt_ref[...]).astype(BF16)
    u_ref[...] = (proj(2) * _sigmoid(proj(3))).astype(BF16)
    sga_ref[...] = _sigmoid(proj(4)).astype(BF16)
    sgc_ref[...] = _sigmoid(proj(5)).astype(BF16)


def _inproj(x2d, g, w_bf, b, wvt_bf, bvt, tm):
    n, d = x2d.shape
    in_w = w_bf.shape[1]
    out = jax.ShapeDtypeStruct((n, d), BF16)
    row = pl.BlockSpec((tm, d), lambda i: (i, 0))
    col = pl.BlockSpec((d, tm), lambda i: (0, i))
    return pl.pallas_call(
        _inproj_kernel,
        grid=(n // tm,),
        in_specs=[row,
                  pl.BlockSpec((1, d), lambda i: (0, 0)),
                  pl.BlockSpec((d, in_w), lambda i: (0, 0)),
                  pl.BlockSpec((1, in_w), lambda i: (0, 0)),
                  pl.BlockSpec((d, d), lambda i: (0, 0)),
                  pl.BlockSpec((d, 1), lambda i: (0, 0))],
        out_specs=[row, row, col, row, row, row],
        out_shape=[out, out, jax.ShapeDtypeStruct((d, n), BF16), out, out, out],
        compiler_params=pltpu.CompilerParams(
            dimension_semantics=("arbitrary",), vmem_limit_bytes=V7X_VMEM_LIMIT),
        name="inproj",
    )(x2d, g, w_bf, b, wvt_bf, bvt)


def _attn_kernel(lam_ref, q_ref, k_ref, vt_ref, km_ref, vmt_ref, bias_ref, bmeta_ref,
                 gain_ref, o_ref, m_s, l_s, acc_s, s0, s1, mx0, mx1, *, tq, lam_init):
    h = pl.program_id(1)
    qi = pl.program_id(2)
    slope = jnp.exp2(-(h + 1).astype(F32))
    lp = lam_ref[...]
    lam = (jnp.exp(jnp.sum(lp[0:1] * lp[1:2], axis=-1, keepdims=True))
           - jnp.exp(jnp.sum(lp[2:3] * lp[3:4], axis=-1, keepdims=True)) + lam_init)

    q = q_ref[...]
    lane = lax.broadcasted_iota(jnp.int32, q.shape, 1)
    zero = jnp.zeros_like(q)
    qs = (jnp.where(lane < HEAD_DIM, q, zero), jnp.where(lane >= HEAD_DIM, q, zero))

    def scores_t(kk, qm):
        return lax.dot_general(kk, qm, (((1,), (1,)), ((), ())), preferred_element_type=F32)

    shift0 = slope * ((qi * tq).astype(F32) * LOG2E)
    km = km_ref[...]
    vmt = vmt_ref[...]
    for idx in range(2):
        s = scores_t(km, qs[idx]) + bmeta_ref[0]
        m_new = jnp.max(s, axis=0, keepdims=True) - shift0
        p = jnp.exp2(s - (m_new + shift0))
        l_s[idx] = jnp.sum(p, axis=0, keepdims=True)
        acc_s[idx] = jnp.dot(vmt, p.astype(BF16), preferred_element_type=F32)
        m_s[idx] = m_new

    tk = tq // 2
    n_off = 2 * qi

    def tile_shift(t):
        off = (qi * tq - t * tk).astype(F32) * LOG2E
        return jnp.where(t < n_off, slope * off, 0.0)

    def stage_a(t, s_buf, mx_buf):
        kk = k_ref[pl.ds(pl.multiple_of(t * tk, tk), tk), :]
        bias = bias_ref[0, jnp.where(t < n_off, 0, t - n_off + 1)]
        for idx in range(2):
            s = scores_t(kk, qs[idx]) + bias
            s_buf[idx] = s
            mx_buf[idx] = jnp.max(s, axis=0, keepdims=True)

    def stage_b(t, s_buf, mx_buf):
        vt = vt_ref[:, pl.ds(pl.multiple_of(t * tk, tk), tk)]
        shift = tile_shift(t)
        for idx in range(2):
            m_old = m_s[idx]
            m_new = jnp.maximum(m_old, mx_buf[idx] - shift)
            p = jnp.exp2(s_buf[idx] - (m_new + shift))
            ps = jnp.sum(p, axis=0, keepdims=True)
            pv = jnp.dot(vt, p.astype(BF16), preferred_element_type=F32)
            alpha = jnp.exp2(m_old - m_new)
            l_s[idx] = alpha * l_s[idx] + ps
            acc_s[idx] = alpha * acc_s[idx] + pv
            m_s[idx] = m_new

    stage_a(0, s0, mx0)

    def body(i, carry):
        stage_a(2 * i + 1, s1, mx1)
        stage_b(2 * i, s0, mx0)
        stage_a(2 * i + 2, s0, mx0)
        stage_b(2 * i + 1, s1, mx1)
        return carry

    lax.fori_loop(0, qi, body, 0)
    stage_a(n_off + 1, s1, mx1)
    stage_b(n_off, s0, mx0)
    stage_b(n_off + 1, s1, mx1)

    o = acc_s[0] / l_s[0] - lam * (acc_s[1] / l_s[1])
    o = o * lax.rsqrt(jnp.mean(o * o, axis=0, keepdims=True) + EPS) * gain_ref[...]
    o_ref[...] = jnp.transpose(o * (1.0 - lam_init)).astype(BF16)


def _attention(lam_params, q, k, vt, k_meta, vt_meta, gain, bsz, seq, tq, lam_init):
    n, w = q.shape
    nq = seq // tq
    slopes = jnp.asarray([2.0 ** (-8.0 * (h + 1) / N_HEADS) for h in range(N_HEADS)], F32)
    tk = tq // 2
    r = jnp.arange(tq, dtype=jnp.int32)
    diff = r[None, :] - r[:, None]
    slopes = slopes * LOG2E
    boff = -slopes[:, None, None] * diff.astype(F32)[None, :tk]
    vis = (r[:, None] // CHUNK) <= (r[None, :] // CHUNK)
    bdiag = jnp.where(vis[None], -slopes[:, None, None] * jnp.abs(diff).astype(F32)[None], NEG)
    bias_all = jnp.stack([boff, bdiag[:, :tk], bdiag[:, tk:]], axis=1)
    mrow = jnp.arange(META_PAD, dtype=jnp.int32)
    dmeta = (N_META + r[None, :] - mrow[:, None]).astype(F32)
    bmeta = jnp.where((mrow < N_META)[None, :, None], -slopes[:, None, None] * dmeta[None], NEG)

    km = jnp.pad(k_meta, ((0, META_PAD - N_META), (0, 0)))
    vmt = jnp.pad(vt_meta, ((0, 0), (0, META_PAD - N_META)))

    kern = functools.partial(_attn_kernel, tq=tq, lam_init=lam_init)
    return pl.pallas_call(
        kern,
        grid=(bsz, N_HEADS, nq),
        in_specs=[
            pl.BlockSpec((4, HEAD_DIM), lambda b, h, i: (0, 0)),
            pl.BlockSpec((tq, V_DIM), lambda b, h, i: (b * nq + i, h)),
            pl.BlockSpec((seq, V_DIM), lambda b, h, i: (b, h)),
            pl.BlockSpec((V_DIM, seq), lambda b, h, i: (h, b)),
            pl.BlockSpec((META_PAD, V_DIM), lambda b, h, i: (0, h)),
            pl.BlockSpec((V_DIM, META_PAD), lambda b, h, i: (h, 0)),
            pl.BlockSpec((1, 3, tk, tq), lambda b, h, i: (h, 0, 0, 0)),
            pl.BlockSpec((1, META_PAD, tq), lambda b, h, i: (h, 0, 0)),
            pl.BlockSpec((V_DIM, 1), lambda b, h, i: (0, 0)),
        ],
        out_specs=pl.BlockSpec((tq, V_DIM), lambda b, h, i: (b * nq + i, h)),
        out_shape=jax.ShapeDtypeStruct((n, w), BF16),
        scratch_shapes=[pltpu.VMEM((2, 1, tq), F32), pltpu.VMEM((2, 1, tq), F32),
                        pltpu.VMEM((2, V_DIM, tq), F32),
                        pltpu.VMEM((2, tk, tq), F32), pltpu.VMEM((2, tk, tq), F32),
                        pltpu.VMEM((2, 1, tq), F32), pltpu.VMEM((2, 1, tq), F32)],
        compiler_params=pltpu.CompilerParams(
            dimension_semantics=("arbitrary", "arbitrary", "arbitrary"),
            vmem_limit_bytes=V7X_VMEM_LIMIT),
        name="diff_attention",
    )(lam_params, q, k, vt, km, vmt, bias_all, bmeta, gain)


def _mixer_kernel(x_ref, o_ref, u_ref, uh_ref, mh_ref, sga_ref, sgc_ref,
                  wap_ref, wpw_ref, wout_ref, dww_ref, dwb_ref, lng_ref, lnb_ref,
                  nffn_ref, wr_ref, br_ref, tri_ref,
                  h_ref, hn_ref, exp_ref, rank_ref, gate_ref, cnt_ref,
                  ext_s, y_s, cnt_s, *, tm, tiles_per_seq):
    i = pl.program_id(0)
    d = x_ref.shape[1]

    @pl.when(i == 0)
    def _():
        cnt_s[...] = jnp.zeros_like(cnt_s)

    first = (i % tiles_per_seq) == 0
    halo = jnp.where(first, mh_ref[...], uh_ref[...])
    ext_s[0:HALO, :] = halo.astype(F32)
    ext_s[HALO:HALO + tm, :] = u_ref[...].astype(F32)
    off = HALO - (CONV_K - 1)
    for c in range(d // 128):
        cs = slice(c * 128, (c + 1) * 128)
        acc = jnp.zeros((tm, 128), F32)
        for t in range(CONV_K):
            acc = acc + dww_ref[t:t + 1, cs] * ext_s[off + t:off + t + tm, cs]
        y_s[:, cs] = acc
    y = y_s[...] + dwb_ref[...]
    mu = jnp.mean(y, axis=-1, keepdims=True)
    yc = y - mu
    var = jnp.mean(yc * yc, axis=-1, keepdims=True)
    y = yc * lax.rsqrt(var + EPS) * lng_ref[...] + lnb_ref[...]
    y = y * _sigmoid(y)
    y_conv = jnp.dot(y.astype(BF16), wpw_ref[...], preferred_element_type=F32)

    y_attn = jnp.dot(o_ref[...], wap_ref[...], preferred_element_type=F32)
    mixed = sga_ref[...].astype(F32) * y_attn + sgc_ref[...].astype(F32) * y_conv
    h = x_ref[...] + jnp.dot(mixed.astype(BF16), wout_ref[...], preferred_element_type=F32)
    h_ref[...] = h
    hn = h * lax.rsqrt(jnp.mean(h * h, axis=-1, keepdims=True) + EPS) * nffn_ref[...]
    hn_ref[...] = hn

    logits = lax.dot_general(wr_ref[...], hn, (((1,), (1,)), ((), ())),
                             precision=lax.Precision.HIGHEST,
                             preferred_element_type=F32) + br_ref[...]
    eio = lax.broadcasted_iota(jnp.int32, logits.shape, 0)
    sels, tops, idxs = [], [], []
    cur = logits
    for _ in range(TOP_K):
        m = jnp.max(cur, axis=0, keepdims=True)
        idx = jnp.min(jnp.where(cur == m, eio, N_EXPERTS), axis=0, keepdims=True)
        sel = eio == idx
        sels.append(sel)
        tops.append(m)
        idxs.append(idx)
        cur = jnp.where(sel, -jnp.inf, cur)
    exps = [jnp.exp(t - tops[0]) for t in tops]
    denom = exps[0] + exps[1] + exps[2] + exps[3]
    hot = jnp.zeros(logits.shape, F32)
    for sel in sels:
        hot = hot + jnp.where(sel, 1.0, 0.0)
    before = jnp.dot(hot.astype(BF16), tri_ref[...], preferred_element_type=F32) + cnt_s[...]
    ranks = [jnp.sum(jnp.where(sel, before, 0.0), axis=0, keepdims=True).astype(jnp.int32)
             for sel in sels]
    zi = jnp.zeros((ROUTE_ROWS - TOP_K, tm), jnp.int32)
    exp_ref[0] = jnp.concatenate(idxs + [zi], axis=0)
    rank_ref[0] = jnp.concatenate(ranks + [zi], axis=0)
    gate_ref[0] = jnp.concatenate([e / denom for e in exps] + [zi.astype(F32)], axis=0)
    cnt_s[...] = cnt_s[...] + jnp.sum(hot, axis=1, keepdims=True)
    cnt_ref[...] = jnp.broadcast_to(cnt_s[...], cnt_ref.shape)


def _mixer(x2d, o, u, sga, sgc, meta_halo, wap, wpw, wout, dww, dwb, lng, lnb, nffn, wr_t, br,
           seq, tm):
    n, d = x2d.shape
    nt = n // tm
    hb = tm // HALO
    tri = (jnp.arange(tm)[:, None] < jnp.arange(tm)[None, :]).astype(BF16)
    row = pl.BlockSpec((tm, d), lambda i: (i, 0))
    route = pl.BlockSpec((1, ROUTE_ROWS, tm), lambda i: (i, 0, 0))

    def full(shp):
        return pl.BlockSpec(shp, lambda i: tuple(0 for _ in shp))

    kern = functools.partial(_mixer_kernel, tm=tm, tiles_per_seq=seq // tm)
    return pl.pallas_call(
        kern,
        grid=(nt,),
        in_specs=[row, row, row,
                  pl.BlockSpec((HALO, d), lambda i: (jnp.maximum(i * hb - 1, 0), 0)),
                  full((HALO, d)), row, row,
                  full((d, d)), full((d, d)), full((d, d)),
                  full((CONV_K, d)), full((1, d)), full((1, d)), full((1, d)), full((1, d)),
                  full((N_EXPERTS, d)), full((N_EXPERTS, 1)), full((tm, tm))],
        out_specs=[row, row, route, route, route, full((N_EXPERTS, 128))],
        out_shape=[jax.ShapeDtypeStruct((n, d), F32),
                   jax.ShapeDtypeStruct((n, d), F32),
                   jax.ShapeDtypeStruct((nt, ROUTE_ROWS, tm), jnp.int32),
                   jax.ShapeDtypeStruct((nt, ROUTE_ROWS, tm), jnp.int32),
                   jax.ShapeDtypeStruct((nt, ROUTE_ROWS, tm), F32),
                   jax.ShapeDtypeStruct((N_EXPERTS, 128), F32)],
        scratch_shapes=[pltpu.VMEM((HALO + tm, d), F32), pltpu.VMEM((tm, d), F32),
                        pltpu.VMEM((N_EXPERTS, 1), F32)],
        compiler_params=pltpu.CompilerParams(
            dimension_semantics=("arbitrary",), vmem_limit_bytes=V7X_VMEM_LIMIT),
        name="mixer_router",
    )(x2d, o, u, u, meta_halo, sga, sgc, wap, wpw, wout, dww, dwb, lng, lnb, nffn, wr_t, br, tri)


def _prefetch_routes(dest_hbm, dsm_s, sem_idx):
    i = pl.program_id(0)
    slot = i % 2

    def idx_copy(tile, s):
        return pltpu.make_async_copy(dest_hbm.at[tile], dsm_s.at[s], sem_idx.at[s])

    @pl.when(i == 0)
    def _():
        idx_copy(0, 0).start()

    idx_copy(i, slot).wait()

    @pl.when(i + 1 < pl.num_programs(0))
    def _():
        idx_copy(i + 1, 1 - slot).start()

    return slot


def _dispatch_kernel(dest_hbm, hn_ref, xs_hbm, dsm_s, sem_idx, sem_rows, *, tm):
    slot = _prefetch_routes(dest_hbm, dsm_s, sem_idx)

    def issue(r, c):
        for kk in range(TOP_K):
            pltpu.make_async_copy(hn_ref.at[pl.ds(r, 1), :],
                                  xs_hbm.at[pl.ds(dsm_s[slot, kk, r], 1), :], sem_rows).start()
        return c
    lax.fori_loop(0, tm, issue, 0, unroll=8)

    for kk in range(TOP_K):
        pltpu.make_async_copy(hn_ref, xs_hbm.at[pl.ds(0, tm), :], sem_rows).wait()


def _dispatch(dest, hn, n_slots, tm):
    n, d = hn.shape
    return pl.pallas_call(
        functools.partial(_dispatch_kernel, tm=tm),
        grid=(n // tm,),
        in_specs=[pl.BlockSpec(memory_space=pl.ANY),
                  pl.BlockSpec((tm, d), lambda i: (i, 0))],
        out_specs=pl.BlockSpec(memory_space=pl.ANY),
        out_shape=jax.ShapeDtypeStruct((n_slots, d), F32),
        scratch_shapes=[pltpu.SMEM((2, ROUTE_ROWS, tm), jnp.int32),
                        pltpu.SemaphoreType.DMA((2,)), pltpu.SemaphoreType.DMA(())],
        compiler_params=pltpu.CompilerParams(
            dimension_semantics=("arbitrary",), vmem_limit_bytes=V7X_VMEM_LIMIT),
        name="dispatch",
    )(dest, hn)


def _expert_kernel(bexp_ref, bvalid_ref, xs_ref, wgu_ref, bgu_ref, wdn_ref, bdn_ref, ys_ref):
    i = pl.program_id(0)
    valid = bvalid_ref[i]
    f = wdn_ref.shape[1]

    @pl.when(valid > 0)
    def _():
        rows = lax.broadcasted_iota(jnp.int32, xs_ref.shape, 0)
        xb = jnp.where(rows < valid, xs_ref[...], 0.0).astype(BF16)
        gu = jnp.dot(xb, wgu_ref[0], preferred_element_type=F32) + bgu_ref[0]
        glu = jnp.minimum(gu[:, :f], SWIGLU_LIMIT)
        lin = jnp.clip(gu[:, f:], -SWIGLU_LIMIT, SWIGLU_LIMIT)
        act = glu * _sigmoid(SWIGLU_ALPHA * glu) * (lin + 1.0)
        ys_ref[...] = jnp.dot(act.astype(BF16), wdn_ref[0], preferred_element_type=F32) + bdn_ref[0]

    @pl.when(valid <= 0)
    def _():
        ys_ref[...] = jnp.zeros_like(ys_ref)


def _experts(block_exp, block_valid, xs, wgu, bgu, wdn, bdn):
    n_slots, d = xs.shape
    nb = n_slots // EXPERT_BLOCK
    f2 = wgu.shape[2]
    f = wdn.shape[1]
    grid_spec = pltpu.PrefetchScalarGridSpec(
        num_scalar_prefetch=2,
        grid=(nb,),
        in_specs=[pl.BlockSpec((EXPERT_BLOCK, d), lambda i, be, bv: (i, 0)),
                  pl.BlockSpec((1, d, f2), lambda i, be, bv: (be[i], 0, 0)),
                  pl.BlockSpec((1, 1, f2), lambda i, be, bv: (be[i], 0, 0)),
                  pl.BlockSpec((1, f, d), lambda i, be, bv: (be[i], 0, 0)),
                  pl.BlockSpec((1, 1, d), lambda i, be, bv: (be[i], 0, 0))],
        out_specs=pl.BlockSpec((EXPERT_BLOCK, d), lambda i, be, bv: (i, 0)),
    )
    return pl.pallas_call(
        _expert_kernel,
        grid_spec=grid_spec,
        out_shape=jax.ShapeDtypeStruct((n_slots, d), F32),
        compiler_params=pltpu.CompilerParams(
            dimension_semantics=("arbitrary",), vmem_limit_bytes=V7X_VMEM_LIMIT),
        name="experts",
    )(block_exp, block_valid, xs, wgu, bgu, wdn, bdn)


def _combine_kernel(dest_hbm, h_ref, gate_ref, g_ref, ys_hbm, out_ref, dsm_s, ybuf, sem_idx, sem_rows,
                    *, tm):
    i = pl.program_id(0)
    nsteps = pl.num_programs(0)
    slot = i % 2

    def idx_copy(tile, s):
        return pltpu.make_async_copy(dest_hbm.at[tile], dsm_s.at[s], sem_idx.at[s])

    def issue_rows(s):
        def issue(r, c):
            for kk in range(TOP_K):
                pltpu.make_async_copy(ys_hbm.at[pl.ds(dsm_s[s, kk, r], 1), :],
                                      ybuf.at[s, kk, pl.ds(r, 1), :], sem_rows.at[s]).start()
            return c
        lax.fori_loop(0, tm, issue, 0, unroll=8)

    @pl.when(i == 0)
    def _():
        idx_copy(0, 0).start()
        idx_copy(0, 0).wait()
        issue_rows(0)

        @pl.when(nsteps > 1)
        def _():
            idx_copy(1, 1).start()

    @pl.when(i + 1 < nsteps)
    def _():
        idx_copy(i + 1, 1 - slot).wait()
        issue_rows(1 - slot)

        @pl.when(i + 2 < nsteps)
        def _():
            idx_copy(i + 2, slot).start()

    for kk in range(TOP_K):
        pltpu.make_async_copy(ys_hbm.at[pl.ds(0, tm), :], ybuf.at[slot, kk], sem_rows.at[slot]).wait()

    gates = jnp.transpose(gate_ref[0])
    acc = h_ref[...]
    for kk in range(TOP_K):
        acc = acc + gates[:, kk:kk + 1] * ybuf[slot, kk]
    out_ref[...] = acc * lax.rsqrt(jnp.mean(acc * acc, axis=-1, keepdims=True) + EPS) * g_ref[...]


def _combine(dest, h, gates, g_final, ys, tm):
    n, d = h.shape
    return pl.pallas_call(
        functools.partial(_combine_kernel, tm=tm),
        grid=(n // tm,),
        in_specs=[pl.BlockSpec(memory_space=pl.ANY),
                  pl.BlockSpec((tm, d), lambda i: (i, 0)),
                  pl.BlockSpec((1, ROUTE_ROWS, tm), lambda i: (i, 0, 0)),
                  pl.BlockSpec((1, d), lambda i: (0, 0)),
                  pl.BlockSpec(memory_space=pl.ANY)],
        out_specs=pl.BlockSpec((tm, d), lambda i: (i, 0)),
        out_shape=jax.ShapeDtypeStruct((n, d), F32),
        scratch_shapes=[pltpu.SMEM((2, ROUTE_ROWS, tm), jnp.int32),
                        pltpu.VMEM((2, TOP_K, tm, d), F32),
                        pltpu.SemaphoreType.DMA((2,)), pltpu.SemaphoreType.DMA((2,))],
        compiler_params=pltpu.CompilerParams(
            dimension_semantics=("arbitrary",), vmem_limit_bytes=V7X_VMEM_LIMIT),
        name="combine",
    )(dest, h, gates, g_final, ys)


def kernel(x, meta_tokens, norm_mix, w_in, b_in, lam_params, subln_gain, w_attn_proj, conv_dw_w,
           conv_dw_b, conv_ln_g, conv_ln_b, w_conv_proj, w_out, norm_ffn, w_router, b_router,
           w_gate_up, b_gate_up, w_down, b_down, norm_final):
    bsz, seq, d = x.shape
    assert norm_mix.shape[0] == 1, "single-layer block"
    n = bsz * seq
    tm_proj = min(512, seq)
    tq = min(512, seq)
    tm = min(256, seq)
    lam_init = 0.8 - 0.6 * 1.0

    x2d = x.reshape(n, d)
    w_rest = jnp.concatenate([w_in[0][:, :2 * d], w_in[0][:, 3 * d:]], axis=1).astype(BF16)
    b_rest = jnp.concatenate([b_in[0][:2 * d], b_in[0][3 * d:]])[None]
    w_vt = w_in[0][:, 2 * d:3 * d].T.astype(BF16)
    b_vt = b_in[0][2 * d:3 * d][:, None]
    g_mix = norm_mix[0][None]
    q, k, vt, u, sga, sgc = _inproj(x2d, g_mix, w_rest, b_rest, w_vt, b_vt, tm_proj)
    _, k_m, vt_m, u_m, _, _ = _inproj(meta_tokens.astype(x.dtype), g_mix, w_rest, b_rest, w_vt, b_vt,
                                      N_META)

    o = _attention(lam_params[0], q, k, vt, k_m, vt_m, subln_gain[0][:, None], bsz, seq, tq,
                   lam_init)

    meta_halo = jnp.concatenate([jnp.zeros((HALO - N_META, d), BF16), u_m], axis=0)
    h, hn, eidx, rank, gates, counts = _mixer(
        x2d, o, u, sga, sgc, meta_halo,
        w_attn_proj[0].astype(BF16), w_conv_proj[0].astype(BF16), w_out[0].astype(BF16),
        conv_dw_w[0], conv_dw_b[0][None], conv_ln_g[0][None], conv_ln_b[0][None],
        norm_ffn[0][None], w_router[0].T, b_router[0][:, None], seq, tm)

    cnt = counts[:, 0].astype(jnp.int32)
    padded = (cnt + EXPERT_BLOCK - 1) // EXPERT_BLOCK * EXPERT_BLOCK
    pad_end = jnp.cumsum(padded)
    pad_start = pad_end - padded
    n_blocks = -(-(n * TOP_K + N_EXPERTS * (EXPERT_BLOCK - 1)) // EXPERT_BLOCK)
    n_slots = n_blocks * EXPERT_BLOCK
    dest = pad_start[eidx] + rank
    block_start = jnp.arange(n_blocks, dtype=jnp.int32) * EXPERT_BLOCK
    block_exp = jnp.minimum(jnp.searchsorted(pad_end, block_start, side="right"),
                            N_EXPERTS - 1).astype(jnp.int32)
    block_valid = jnp.clip(pad_start[block_exp] + cnt[block_exp] - block_start, 0,
                           EXPERT_BLOCK).astype(jnp.int32)

    xs = _dispatch(dest, hn, n_slots, tm)
    ys = _experts(block_exp, block_valid, xs,
                  w_gate_up[0].astype(BF16), b_gate_up[0][:, None, :],
                  w_down[0].astype(BF16), b_down[0][:, None, :])
    out = _combine(dest, h, gates, norm_final[None], ys, tm)
    return out.reshape(bsz, seq, d)
```

```python
import functools

import jax
import jax.numpy as jnp
from jax import lax
from jax.experimental import pallas as pl
from jax.experimental.pallas import tpu as pltpu

CHUNK = 64
N_META = 16
N_HEADS = 8
HEAD_DIM = 64
V_DIM = 2 * HEAD_DIM
CONV_K = 31
N_EXPERTS = 32
TOP_K = 4
SWIGLU_ALPHA = 1.702
SWIGLU_LIMIT = 7.0
EXPERT_BLOCK = 256
EPS = 1e-5
NEG = -1e30
LOG2E = 1.4426950408889634

V7X_VMEM_LIMIT = 56 * 1024 * 1024
SUBLANES = 8
META_PAD = 128
HALO = 32
ROUTE_ROWS = 8

BF16 = jnp.bfloat16
F32 = jnp.float32


def _sigmoid(x):
    return 1.0 / (1.0 + jnp.exp(-x))


def _inproj_kernel(x_ref, g_ref, w_ref, b_ref, wvt_ref, bvt_ref,
                   q_ref, k_ref, vt_ref, u_ref, sga_ref, sgc_ref):
    d = x_ref.shape[1]
    x = x_ref[...]
    xn = x * lax.rsqrt(jnp.mean(x * x, axis=-1, keepdims=True) + EPS) * g_ref[...]
    xb = xn.astype(BF16)

    def proj(c):
        return (jnp.dot(xb, w_ref[:, c * d:(c + 1) * d], preferred_element_type=F32)
                + b_ref[:, c * d:(c + 1) * d])

    q_ref[...] = (proj(0) * (HEAD_DIM ** -0.5 * LOG2E)).astype(BF16)
    k_ref[...] = proj(1).astype(BF16)
    vt = lax.dot_general(wvt_ref[...], xb, (((1,), (1,)), ((), ())), preferred_element_type=F32)
    vt_ref[...] = (vt + bvt_ref[...]).astype(BF16)
    u_ref[...] = (proj(2) * _sigmoid(proj(3))).astype(BF16)
    sga_ref[...] = _sigmoid(proj(4)).astype(BF16)
    sgc_ref[...] = _sigmoid(proj(5)).astype(BF16)


def _inproj(x2d, g, w_bf, b, wvt_bf, bvt, tm):
    n, d = x2d.shape
    in_w = w_bf.shape[1]
    out = jax.ShapeDtypeStruct((n, d), BF16)
    row = pl.BlockSpec((tm, d), lambda i: (i, 0))
    col = pl.BlockSpec((d, tm), lambda i: (0, i))
    return pl.pallas_call(
        _inproj_kernel,
        grid=(n // tm,),
        in_specs=[row,
                  pl.BlockSpec((1, d), lambda i: (0, 0)),
                  pl.BlockSpec((d, in_w), lambda i: (0, 0)),
                  pl.BlockSpec((1, in_w), lambda i: (0, 0)),
                  pl.BlockSpec((d, d), lambda i: (0, 0)),
                  pl.BlockSpec((d, 1), lambda i: (0, 0))],
        out_specs=[row, row, col, row, row, row],
        out_shape=[out, out, jax.ShapeDtypeStruct((d, n), BF16), out, out, out],
        compiler_params=pltpu.CompilerParams(
            dimension_semantics=("arbitrary",), vmem_limit_bytes=V7X_VMEM_LIMIT),
        name="inproj",
    )(x2d, g, w_bf, b, wvt_bf, bvt)


def _attn_kernel(lam_ref, q_ref, k_ref, vt_ref, km_ref, vmt_ref, bias_ref, bmeta_ref,
                 gain_ref, o_ref, m_s, l_s, acc_s, s0, s1, mx0, mx1, *, tq, lam_init):
    h = pl.program_id(1)
    qi = pl.program_id(2)
    slope = jnp.exp2(-(h + 1).astype(F32))
    lp = lam_ref[...]
    lam = (jnp.exp(jnp.sum(lp[0:1] * lp[1:2], axis=-1, keepdims=True))
           - jnp.exp(jnp.sum(lp[2:3] * lp[3:4], axis=-1, keepdims=True)) + lam_init)

    q = q_ref[...]
    lane = lax.broadcasted_iota(jnp.int32, q.shape, 1)
    zero = jnp.zeros_like(q)
    qs = (jnp.where(lane < HEAD_DIM, q, zero), jnp.where(lane >= HEAD_DIM, q, zero))

    def scores_t(kk, qm):
        return lax.dot_general(kk, qm, (((1,), (1,)), ((), ())), preferred_element_type=F32)

    shift0 = slope * ((qi * tq).astype(F32) * LOG2E)
    km = km_ref[...]
    vmt = vmt_ref[...]
    for idx in range(2):
        s = scores_t(km, qs[idx]) + bmeta_ref[0]
        m_new = jnp.max(s, axis=0, keepdims=True) - shift0
        p = jnp.exp2(s - (m_new + shift0))
        l_s[idx] = jnp.sum(p, axis=0, keepdims=True)
        acc_s[idx] = jnp.dot(vmt, p.astype(BF16), preferred_element_type=F32)
        m_s[idx] = m_new

    tk = tq // 2
    n_off = 2 * qi

    def tile_shift(t):
        off = (qi * tq - t * tk).astype(F32) * LOG2E
        return jnp.where(t < n_off, slope * off, 0.0)

    def stage_a(t, s_buf, mx_buf):
        kk = k_ref[pl.ds(pl.multiple_of(t * tk, tk), tk), :]
        bias = bias_ref[0, jnp.where(t < n_off, 0, t - n_off + 1)]
        for idx in range(2):
            s = scores_t(kk, qs[idx]) + bias
            s_buf[idx] = s
            mx_buf[idx] = jnp.max(s, axis=0, keepdims=True)

    def stage_b(t, s_buf, mx_buf):
        vt = vt_ref[:, pl.ds(pl.multiple_of(t * tk, tk), tk)]
        shift = tile_shift(t)
        for idx in range(2):
            m_old = m_s[idx]
            m_new = jnp.maximum(m_old, mx_buf[idx] - shift)
            p = jnp.exp2(s_buf[idx] - (m_new + shift))
            ps = jnp.sum(p, axis=0, keepdims=True)
            pv = jnp.dot(vt, p.astype(BF16), preferred_element_type=F32)
            alpha = jnp.exp2(m_old - m_new)
            l_s[idx] = alpha * l_s[idx] + ps
            acc_s[idx] = alpha * acc_s[idx] + pv
            m_s[idx] = m_new

    stage_a(0, s0, mx0)

    def body(i, carry):
        stage_a(2 * i + 1, s1, mx1)
        stage_b(2 * i, s0, mx0)
        stage_a(2 * i + 2, s0, mx0)
        stage_b(2 * i + 1, s1, mx1)
        return carry

    lax.fori_loop(0, qi, body, 0)
    stage_a(n_off + 1, s1, mx1)
    stage_b(n_off, s0, mx0)
    stage_b(n_off + 1, s1, mx1)

    o = acc_s[0] / l_s[0] - lam * (acc_s[1] / l_s[1])
    o = o * lax.rsqrt(jnp.mean(o * o, axis=0, keepdims=True) + EPS) * gain_ref[...]
    o_ref[...] = jnp.transpose(o * (1.0 - lam_init)).astype(BF16)


def _attention(lam_params, q, k, vt, k_meta, vt_meta, gain, bsz, seq, tq, lam_init):
    n, w = q.shape
    nq = seq // tq
    slopes = jnp.asarray([2.0 ** (-8.0 * (h + 1) / N_HEADS) for h in range(N_HEADS)], F32)
    tk = tq // 2
    r = jnp.arange(tq, dtype=jnp.int32)
    diff = r[None, :] - r[:, None]
    slopes = slopes * LOG2E
    boff = -slopes[:, None, None] * diff.astype(F32)[None, :tk]
    vis = (r[:, None] // CHUNK) <= (r[None, :] // CHUNK)
    bdiag = jnp.where(vis[None], -slopes[:, None, None] * jnp.abs(diff).astype(F32)[None], NEG)
    bias_all = jnp.stack([boff, bdiag[:, :tk], bdiag[:, tk:]], axis=1)
    mrow = jnp.arange(META_PAD, dtype=jnp.int32)
    dmeta = (N_META + r[None, :] - mrow[:, None]).astype(F32)
    bmeta = jnp.where((mrow < N_META)[None, :, None], -slopes[:, None, None] * dmeta[None], NEG)

    km = jnp.pad(k_meta, ((0, META_PAD - N_META), (0, 0)))
    vmt = jnp.pad(vt_meta, ((0, 0), (0, META_PAD - N_META)))

    kern = functools.partial(_attn_kernel, tq=tq, lam_init=lam_init)
    return pl.pallas_call(
        kern,
        grid=(bsz, N_HEADS, nq),
        in_specs=[
            pl.BlockSpec((4, HEAD_DIM), lambda b, h, i: (0, 0)),
            pl.BlockSpec((tq, V_DIM), lambda b, h, i: (b * nq + i, h)),
            pl.BlockSpec((seq, V_DIM), lambda b, h, i: (b, h)),
            pl.BlockSpec((V_DIM, seq), lambda b, h, i: (h, b)),
            pl.BlockSpec((META_PAD, V_DIM), lambda b, h, i: (0, h)),
            pl.BlockSpec((V_DIM, META_PAD), lambda b, h, i: (h, 0)),
            pl.BlockSpec((1, 3, tk, tq), lambda b, h, i: (h, 0, 0, 0)),
            pl.BlockSpec((1, META_PAD, tq), lambda b, h, i: (h, 0, 0)),
            pl.BlockSpec((V_DIM, 1), lambda b, h, i: (0, 0)),
        ],
        out_specs=pl.BlockSpec((tq, V_DIM), lambda b, h, i: (b * nq + i, h)),
        out_shape=jax.ShapeDtypeStruct((n, w), BF16),
        scratch_shapes=[pltpu.VMEM((2, 1, tq), F32), pltpu.VMEM((2, 1, tq), F32),
                        pltpu.VMEM((2, V_DIM, tq), F32),
                        pltpu.VMEM((2, tk, tq), F32), pltpu.VMEM((2, tk, tq), F32),
                        pltpu.VMEM((2, 1, tq), F32), pltpu.VMEM((2, 1, tq), F32)],
        compiler_params=pltpu.CompilerParams(
            dimension_semantics=("arbitrary", "arbitrary", "arbitrary"),
            vmem_limit_bytes=V7X_VMEM_LIMIT),
        name="diff_attention",
    )(lam_params, q, k, vt, km, vmt, bias_all, bmeta, gain)


def _mixer_kernel(x_ref, o_ref, u_ref, uh_ref, mh_ref, sga_ref, sgc_ref,
                  wap_ref, wpw_ref, wout_ref, dww_ref, dwb_ref, lng_ref, lnb_ref,
                  nffn_ref, wr_ref, br_ref, tri_ref,
                  h_ref, hn_ref, exp_ref, rank_ref, gate_ref, cnt_ref,
                  ext_s, y_s, cnt_s, *, tm, tiles_per_seq):
    i = pl.program_id(0)
    d = x_ref.shape[1]

    @pl.when(i == 0)
    def _():
        cnt_s[...] = jnp.zeros_like(cnt_s)

    first = (i % tiles_per_seq) == 0
    halo = jnp.where(first, mh_ref[...], uh_ref[...])
    ext_s[0, 0:HALO, :] = halo.astype(F32)
    ext_s[0, HALO:HALO + tm, :] = u_ref[...].astype(F32)
    span = tm + HALO - SUBLANES
    for b in range(1, SUBLANES):
        ext_s[b, 0:span, :] = ext_s[0, b:b + span, :]
    off = HALO - (CONV_K - 1)
    for c in range(d // 128):
        cs = slice(c * 128, (c + 1) * 128)
        acc = jnp.zeros((tm, 128), F32)
        for t in range(CONV_K):
            a, b = divmod(off + t, SUBLANES)
            acc = acc + dww_ref[t:t + 1, cs] * ext_s[b, a * SUBLANES:a * SUBLANES + tm, cs]
        y_s[:, cs] = acc
    y = y_s[...] + dwb_ref[...]
    mu = jnp.mean(y, axis=-1, keepdims=True)
    yc = y - mu
    var = jnp.mean(yc * yc, axis=-1, keepdims=True)
    y = yc * lax.rsqrt(var + EPS) * lng_ref[...] + lnb_ref[...]
    y = y * _sigmoid(y)
    y_conv = jnp.dot(y.astype(BF16), wpw_ref[...], preferred_element_type=F32)

    y_attn = jnp.dot(o_ref[...], wap_ref[...], preferred_element_type=F32)
    mixed = sga_ref[...].astype(F32) * y_attn + sgc_ref[...].astype(F32) * y_conv
    h = x_ref[...] + jnp.dot(mixed.astype(BF16), wout_ref[...], preferred_element_type=F32)
    h_ref[...] = h
    hn = h * lax.rsqrt(jnp.mean(h * h, axis=-1, keepdims=True) + EPS) * nffn_ref[...]
    hn_ref[...] = hn

    logits = lax.dot_general(wr_ref[...], hn, (((1,), (1,)), ((), ())),
                             precision=lax.Precision.HIGHEST,
                             preferred_element_type=F32) + br_ref[...]
    eio = lax.broadcasted_iota(jnp.int32, logits.shape, 0)
    sels, tops, idxs = [], [], []
    cur = logits
    for _ in range(TOP_K):
        m = jnp.max(cur, axis=0, keepdims=True)
        idx = jnp.min(jnp.where(cur == m, eio, N_EXPERTS), axis=0, keepdims=True)
        sel = eio == idx
        sels.append(sel)
        tops.append(m)
        idxs.append(idx)
        cur = jnp.where(sel, -jnp.inf, cur)
    exps = [jnp.exp(t - tops[0]) for t in tops]
    denom = exps[0] + exps[1] + exps[2] + exps[3]
    hot = jnp.zeros(logits.shape, F32)
    for sel in sels:
        hot = hot + jnp.where(sel, 1.0, 0.0)
    before = jnp.dot(hot.astype(BF16), tri_ref[...], preferred_element_type=F32) + cnt_s[...]
    ranks = [jnp.sum(jnp.where(sel, before, 0.0), axis=0, keepdims=True).astype(jnp.int32)
             for sel in sels]
    zi = jnp.zeros((ROUTE_ROWS - TOP_K, tm), jnp.int32)
    exp_ref[0] = jnp.concatenate(idxs + [zi], axis=0)
    rank_ref[0] = jnp.concatenate(ranks + [zi], axis=0)
    gate_ref[0] = jnp.concatenate([e / denom for e in exps] + [zi.astype(F32)], axis=0)
    cnt_s[...] = cnt_s[...] + jnp.sum(hot, axis=1, keepdims=True)
    cnt_ref[...] = jnp.broadcast_to(cnt_s[...], cnt_ref.shape)


def _mixer(x2d, o, u, sga, sgc, meta_halo, wap, wpw, wout, dww, dwb, lng, lnb, nffn, wr_t, br,
           seq, tm):
    n, d = x2d.shape
    nt = n // tm
    hb = tm // HALO
    tri = (jnp.arange(tm)[:, None] < jnp.arange(tm)[None, :]).astype(BF16)
    row = pl.BlockSpec((tm, d), lambda i: (i, 0))
    route = pl.BlockSpec((1, ROUTE_ROWS, tm), lambda i: (i, 0, 0))

    def full(shp):
        return pl.BlockSpec(shp, lambda i: tuple(0 for _ in shp))

    kern = functools.partial(_mixer_kernel, tm=tm, tiles_per_seq=seq // tm)
    return pl.pallas_call(
        kern,
        grid=(nt,),
        in_specs=[row, row, row,
                  pl.BlockSpec((HALO, d), lambda i: (jnp.maximum(i * hb - 1, 0), 0)),
                  full((HALO, d)), row, row,
                  full((d, d)), full((d, d)), full((d, d)),
                  full((CONV_K, d)), full((1, d)), full((1, d)), full((1, d)), full((1, d)),
                  full((N_EXPERTS, d)), full((N_EXPERTS, 1)), full((tm, tm))],
        out_specs=[row, row, route, route, route, full((N_EXPERTS, 128))],
        out_shape=[jax.ShapeDtypeStruct((n, d), F32),
                   jax.ShapeDtypeStruct((n, d), F32),
                   jax.ShapeDtypeStruct((nt, ROUTE_ROWS, tm), jnp.int32),
                   jax.ShapeDtypeStruct((nt, ROUTE_ROWS, tm), jnp.int32),
                   jax.ShapeDtypeStruct((nt, ROUTE_ROWS, tm), F32),
                   jax.ShapeDtypeStruct((N_EXPERTS, 128), F32)],
        scratch_shapes=[pltpu.VMEM((SUBLANES, HALO + tm, d), F32), pltpu.VMEM((tm, d), F32),
                        pltpu.VMEM((N_EXPERTS, 1), F32)],
        compiler_params=pltpu.CompilerParams(
            dimension_semantics=("arbitrary",), vmem_limit_bytes=V7X_VMEM_LIMIT),
        name="mixer_router",
    )(x2d, o, u, u, meta_halo, sga, sgc, wap, wpw, wout, dww, dwb, lng, lnb, nffn, wr_t, br, tri)


def _prefetch_routes(dest_hbm, dsm_s, sem_idx):
    i = pl.program_id(0)
    slot = i % 2

    def idx_copy(tile, s):
        return pltpu.make_async_copy(dest_hbm.at[tile], dsm_s.at[s], sem_idx.at[s])

    @pl.when(i == 0)
    def _():
        idx_copy(0, 0).start()

    idx_copy(i, slot).wait()

    @pl.when(i + 1 < pl.num_programs(0))
    def _():
        idx_copy(i + 1, 1 - slot).start()

    return slot


def _dispatch_kernel(dest_hbm, hn_ref, xs_hbm, dsm_s, sem_idx, sem_rows, *, tm):
    slot = _prefetch_routes(dest_hbm, dsm_s, sem_idx)

    def issue(r, c):
        for kk in range(TOP_K):
            pltpu.make_async_copy(hn_ref.at[pl.ds(r, 1), :],
                                  xs_hbm.at[pl.ds(dsm_s[slot, kk, r], 1), :], sem_rows).start()
        return c
    lax.fori_loop(0, tm, issue, 0, unroll=8)

    for kk in range(TOP_K):
        pltpu.make_async_copy(hn_ref, xs_hbm.at[pl.ds(0, tm), :], sem_rows).wait()


def _dispatch(dest, hn, n_slots, tm):
    n, d = hn.shape
    return pl.pallas_call(
        functools.partial(_dispatch_kernel, tm=tm),
        grid=(n // tm,),
        in_specs=[pl.BlockSpec(memory_space=pl.ANY),
                  pl.BlockSpec((tm, d), lambda i: (i, 0))],
        out_specs=pl.BlockSpec(memory_space=pl.ANY),
        out_shape=jax.ShapeDtypeStruct((n_slots, d), F32),
        scratch_shapes=[pltpu.SMEM((2, ROUTE_ROWS, tm), jnp.int32),
                        pltpu.SemaphoreType.DMA((2,)), pltpu.SemaphoreType.DMA(())],
        compiler_params=pltpu.CompilerParams(
            dimension_semantics=("arbitrary",), vmem_limit_bytes=V7X_VMEM_LIMIT),
        name="dispatch",
    )(dest, hn)


def _expert_kernel(bexp_ref, bvalid_ref, xs_ref, wgu_ref, bgu_ref, wdn_ref, bdn_ref, ys_ref,
                   wgu_bf, wdn_bf):
    i = pl.program_id(0)
    valid = bvalid_ref[i]
    f = wdn_ref.shape[1]

    @pl.when((i == 0) | (bexp_ref[i] != bexp_ref[jnp.maximum(i - 1, 0)]))
    def _():
        wgu_bf[...] = wgu_ref[0].astype(BF16)
        wdn_bf[...] = wdn_ref[0].astype(BF16)

    @pl.when(valid > 0)
    def _():
        rows = lax.broadcasted_iota(jnp.int32, xs_ref.shape, 0)
        xb = jnp.where(rows < valid, xs_ref[...], 0.0).astype(BF16)
        gu = jnp.dot(xb, wgu_bf[...], preferred_element_type=F32) + bgu_ref[0]
        glu = jnp.minimum(gu[:, :f], SWIGLU_LIMIT)
        lin = jnp.clip(gu[:, f:], -SWIGLU_LIMIT, SWIGLU_LIMIT)
        act = glu * _sigmoid(SWIGLU_ALPHA * glu) * (lin + 1.0)
        ys_ref[...] = jnp.dot(act.astype(BF16), wdn_bf[...], preferred_element_type=F32) + bdn_ref[0]

    @pl.when(valid <= 0)
    def _():
        ys_ref[...] = jnp.zeros_like(ys_ref)


def _experts(block_exp, block_valid, xs, wgu, bgu, wdn, bdn):
    n_slots, d = xs.shape
    nb = n_slots // EXPERT_BLOCK
    f2 = wgu.shape[2]
    f = wdn.shape[1]
    grid_spec = pltpu.PrefetchScalarGridSpec(
        num_scalar_prefetch=2,
        grid=(nb,),
        in_specs=[pl.BlockSpec((EXPERT_BLOCK, d), lambda i, be, bv: (i, 0)),
                  pl.BlockSpec((1, d, f2), lambda i, be, bv: (be[i], 0, 0)),
                  pl.BlockSpec((1, 1, f2), lambda i, be, bv: (be[i], 0, 0)),
                  pl.BlockSpec((1, f, d), lambda i, be, bv: (be[i], 0, 0)),
                  pl.BlockSpec((1, 1, d), lambda i, be, bv: (be[i], 0, 0))],
        out_specs=pl.BlockSpec((EXPERT_BLOCK, d), lambda i, be, bv: (i, 0)),
        scratch_shapes=[pltpu.VMEM((d, f2), BF16), pltpu.VMEM((f, d), BF16)],
    )
    return pl.pallas_call(
        _expert_kernel,
        grid_spec=grid_spec,
        out_shape=jax.ShapeDtypeStruct((n_slots, d), F32),
        compiler_params=pltpu.CompilerParams(
            dimension_semantics=("arbitrary",), vmem_limit_bytes=V7X_VMEM_LIMIT),
        name="experts",
    )(block_exp, block_valid, xs, wgu, bgu, wdn, bdn)


def _combine_kernel(dest_hbm, h_ref, gate_ref, g_ref, ys_hbm, out_ref, dsm_s, ybuf, sem_idx, sem_rows,
                    *, tm):
    i = pl.program_id(0)
    nsteps = pl.num_programs(0)
    slot = i % 2

    def idx_copy(tile, s):
        return pltpu.make_async_copy(dest_hbm.at[tile], dsm_s.at[s], sem_idx.at[s])

    def issue_rows(s):
        def issue(r, c):
            for kk in range(TOP_K):
                pltpu.make_async_copy(ys_hbm.at[pl.ds(dsm_s[s, kk, r], 1), :],
                                      ybuf.at[s, kk, pl.ds(r, 1), :], sem_rows.at[s]).start()
            return c
        lax.fori_loop(0, tm, issue, 0, unroll=8)

    @pl.when(i == 0)
    def _():
        idx_copy(0, 0).start()
        idx_copy(0, 0).wait()
        issue_rows(0)

        @pl.when(nsteps > 1)
        def _():
            idx_copy(1, 1).start()

    @pl.when(i + 1 < nsteps)
    def _():
        idx_copy(i + 1, 1 - slot).wait()
        issue_rows(1 - slot)

        @pl.when(i + 2 < nsteps)
        def _():
            idx_copy(i + 2, slot).start()

    for kk in range(TOP_K):
        pltpu.make_async_copy(ys_hbm.at[pl.ds(0, tm), :], ybuf.at[slot, kk], sem_rows.at[slot]).wait()

    gates = jnp.transpose(gate_ref[0])
    acc = h_ref[...]
    for kk in range(TOP_K):
        acc = acc + gates[:, kk:kk + 1] * ybuf[slot, kk]
    out_ref[...] = acc * lax.rsqrt(jnp.mean(acc * acc, axis=-1, keepdims=True) + EPS) * g_ref[...]


def _combine(dest, h, gates, g_final, ys, tm):
    n, d = h.shape
    return pl.pallas_call(
        functools.partial(_combine_kernel, tm=tm),
        grid=(n // tm,),
        in_specs=[pl.BlockSpec(memory_space=pl.ANY),
                  pl.BlockSpec((tm, d), lambda i: (i, 0)),
                  pl.BlockSpec((1, ROUTE_ROWS, tm), lambda i: (i, 0, 0)),
                  pl.BlockSpec((1, d), lambda i: (0, 0)),
                  pl.BlockSpec(memory_space=pl.ANY)],
        out_specs=pl.BlockSpec((tm, d), lambda i: (i, 0)),
        out_shape=jax.ShapeDtypeStruct((n, d), F32),
        scratch_shapes=[pltpu.SMEM((2, ROUTE_ROWS, tm), jnp.int32),
                        pltpu.VMEM((2, TOP_K, tm, d), F32),
                        pltpu.SemaphoreType.DMA((2,)), pltpu.SemaphoreType.DMA((2,))],
        compiler_params=pltpu.CompilerParams(
            dimension_semantics=("arbitrary",), vmem_limit_bytes=V7X_VMEM_LIMIT),
        name="combine",
    )(dest, h, gates, g_final, ys)


def kernel(x, meta_tokens, norm_mix, w_in, b_in, lam_params, subln_gain, w_attn_proj, conv_dw_w,
           conv_dw_b, conv_ln_g, conv_ln_b, w_conv_proj, w_out, norm_ffn, w_router, b_router,
           w_gate_up, b_gate_up, w_down, b_down, norm_final):
    bsz, seq, d = x.shape
    assert norm_mix.shape[0] == 1, "single-layer block"
    n = bsz * seq
    tm_proj = min(512, seq)
    tq = min(512, seq)
    tm = min(256, seq)
    lam_init = 0.8 - 0.6 * 1.0

    x2d = x.reshape(n, d)
    w_rest = jnp.concatenate([w_in[0][:, :2 * d], w_in[0][:, 3 * d:]], axis=1).astype(BF16)
    b_rest = jnp.concatenate([b_in[0][:2 * d], b_in[0][3 * d:]])[None]
    w_vt = w_in[0][:, 2 * d:3 * d].T.astype(BF16)
    b_vt = b_in[0][2 * d:3 * d][:, None]
    g_mix = norm_mix[0][None]
    q, k, vt, u, sga, sgc = _inproj(x2d, g_mix, w_rest, b_rest, w_vt, b_vt, tm_proj)
    _, k_m, vt_m, u_m, _, _ = _inproj(meta_tokens.astype(x.dtype), g_mix, w_rest, b_rest, w_vt, b_vt,
                                      N_META)

    o = _attention(lam_params[0], q, k, vt, k_m, vt_m, subln_gain[0][:, None], bsz, seq, tq,
                   lam_init)

    meta_halo = jnp.concatenate([jnp.zeros((HALO - N_META, d), BF16), u_m], axis=0)
    h, hn, eidx, rank, gates, counts = _mixer(
        x2d, o, u, sga, sgc, meta_halo,
        w_attn_proj[0].astype(BF16), w_conv_proj[0].astype(BF16), w_out[0].astype(BF16),
        conv_dw_w[0], conv_dw_b[0][None], conv_ln_g[0][None], conv_ln_b[0][None],
        norm_ffn[0][None], w_router[0].T, b_router[0][:, None], seq, tm)

    cnt = counts[:, 0].astype(jnp.int32)
    padded = (cnt + EXPERT_BLOCK - 1) // EXPERT_BLOCK * EXPERT_BLOCK
    pad_end = jnp.cumsum(padded)
    pad_start = pad_end - padded
    n_blocks = -(-(n * TOP_K + N_EXPERTS * (EXPERT_BLOCK - 1)) // EXPERT_BLOCK)
    n_slots = n_blocks * EXPERT_BLOCK
    dest = pad_start[eidx] + rank
    block_start = jnp.arange(n_blocks, dtype=jnp.int32) * EXPERT_BLOCK
    block_exp = jnp.minimum(jnp.sum(block_start[:, None] >= pad_end[None, :], axis=1),
                            N_EXPERTS - 1).astype(jnp.int32)
    block_valid = jnp.clip(pad_start[block_exp] + cnt[block_exp] - block_start, 0,
                           EXPERT_BLOCK).astype(jnp.int32)

    xs = _dispatch(dest, hn, n_slots, tm)
    ys = _experts(block_exp, block_valid, xs, w_gate_up[0], b_gate_up[0][:, None, :],
                  w_down[0], b_down[0][:, None, :])
    out = _combine(dest, h, gates, norm_final[None], ys, tm)
    return out.reshape(bsz, seq, d)
```

```python
import functools

import jax
import jax.numpy as jnp
from jax import lax
from jax.experimental import pallas as pl
from jax.experimental.pallas import tpu as pltpu

CHUNK = 64
N_META = 16
N_HEADS = 8
HEAD_DIM = 64
V_DIM = 2 * HEAD_DIM
CONV_K = 31
N_EXPERTS = 32
TOP_K = 4
SWIGLU_ALPHA = 1.702
SWIGLU_LIMIT = 7.0
EXPERT_BLOCK = 256
EPS = 1e-5
NEG = -1e30
LOG2E = 1.4426950408889634

V7X_VMEM_LIMIT = 56 * 1024 * 1024
SUBLANES = 8
LANES = 128
META_PAD = 128
HALO = 32
ROUTE_ROWS = 8

BF16 = jnp.bfloat16
F32 = jnp.float32


def _sigmoid(x):
    return 1.0 / (1.0 + jnp.exp(-x))


def _load_token_tiles(ref, rows):
    return jnp.concatenate([ref[pl.ds(s, rows, stride=SUBLANES), :] for s in range(SUBLANES)],
                           axis=-1)


def _store_token_tiles(ref, x):
    rows = x.shape[0]
    for s in range(SUBLANES):
        ref[pl.ds(s, rows, stride=SUBLANES), :] = x[:, s * LANES:(s + 1) * LANES]


def _inproj_kernel(x_ref, g_ref, w_ref, b_ref, wvt_ref, bvt_ref,
                   q_ref, k_ref, vt_ref, u_ref, sga_ref, sgc_ref):
    d = x_ref.shape[1]
    x = x_ref[...]
    xn = x * lax.rsqrt(jnp.mean(x * x, axis=-1, keepdims=True) + EPS) * g_ref[...]
    xb = xn.astype(BF16)

    def proj(c):
        return (jnp.dot(xb, w_ref[:, c * d:(c + 1) * d], preferred_element_type=F32)
                + b_ref[:, c * d:(c + 1) * d])

    q_ref[...] = (proj(0) * (HEAD_DIM ** -0.5 * LOG2E)).astype(BF16)
    k_ref[...] = proj(1).astype(BF16)
    vt = lax.dot_general(wvt_ref[...], xb, (((1,), (1,)), ((), ())), preferred_element_type=F32)
    vt_ref[...] = (vt + bvt_ref[...]).astype(BF16)
    u_ref[...] = (proj(2) * _sigmoid(proj(3))).astype(BF16)
    sga_ref[...] = _sigmoid(proj(4)).astype(BF16)
    sgc_ref[...] = _sigmoid(proj(5)).astype(BF16)


def _inproj(x2d, g, w_bf, b, wvt_bf, bvt, tm):
    n, d = x2d.shape
    in_w = w_bf.shape[1]
    out = jax.ShapeDtypeStruct((n, d), BF16)
    row = pl.BlockSpec((tm, d), lambda i: (i, 0))
    col = pl.BlockSpec((d, tm), lambda i: (0, i))
    return pl.pallas_call(
        _inproj_kernel,
        grid=(n // tm,),
        in_specs=[row,
                  pl.BlockSpec((1, d), lambda i: (0, 0)),
                  pl.BlockSpec((d, in_w), lambda i: (0, 0)),
                  pl.BlockSpec((1, in_w), lambda i: (0, 0)),
                  pl.BlockSpec((d, d), lambda i: (0, 0)),
                  pl.BlockSpec((d, 1), lambda i: (0, 0))],
        out_specs=[row, row, col, row, row, row],
        out_shape=[out, out, jax.ShapeDtypeStruct((d, n), BF16), out, out, out],
        compiler_params=pltpu.CompilerParams(
            dimension_semantics=("arbitrary",), vmem_limit_bytes=V7X_VMEM_LIMIT),
        name="inproj",
    )(x2d, g, w_bf, b, wvt_bf, bvt)


def _attn_kernel(lam_ref, q_ref, k_ref, vt_ref, km_ref, vmt_ref, bias_ref, bmeta_ref,
                 gain_ref, o_ref, m_s, l_s, acc_s, s0, s1, mx0, mx1, *, tq, lam_init):
    h = pl.program_id(1)
    qi = pl.program_id(2)
    slope = jnp.exp2(-(h + 1).astype(F32))
    lp = lam_ref[...]
    lam = (jnp.exp(jnp.sum(lp[0:1] * lp[1:2], axis=-1, keepdims=True))
           - jnp.exp(jnp.sum(lp[2:3] * lp[3:4], axis=-1, keepdims=True)) + lam_init)

    q = q_ref[...]
    lane = lax.broadcasted_iota(jnp.int32, q.shape, 1)
    zero = jnp.zeros_like(q)
    qs = (jnp.where(lane < HEAD_DIM, q, zero), jnp.where(lane >= HEAD_DIM, q, zero))

    def scores_t(kk, qm):
        return lax.dot_general(kk, qm, (((1,), (1,)), ((), ())), preferred_element_type=F32)

    shift0 = slope * ((qi * tq).astype(F32) * LOG2E)
    km = km_ref[...]
    vmt = vmt_ref[...]
    for idx in range(2):
        s = scores_t(km, qs[idx]) + bmeta_ref[0]
        m_new = jnp.max(s, axis=0, keepdims=True) - shift0
        p = jnp.exp2(s - (m_new + shift0))
        l_s[idx] = jnp.sum(p, axis=0, keepdims=True)
        acc_s[idx] = jnp.dot(vmt, p.astype(BF16), preferred_element_type=F32)
        m_s[idx] = m_new

    tk = tq // 2
    n_off = 2 * qi

    def tile_shift(t):
        off = (qi * tq - t * tk).astype(F32) * LOG2E
        return jnp.where(t < n_off, slope * off, 0.0)

    def stage_a(t, s_buf, mx_buf):
        kk = k_ref[pl.ds(pl.multiple_of(t * tk, tk), tk), :]
        bias = bias_ref[0, jnp.where(t < n_off, 0, t - n_off + 1)]
        for idx in range(2):
            s = scores_t(kk, qs[idx]) + bias
            s_buf[idx] = s
            mx_buf[idx] = jnp.max(s, axis=0, keepdims=True)

    def stage_b(t, s_buf, mx_buf):
        vt = vt_ref[:, pl.ds(pl.multiple_of(t * tk, tk), tk)]
        shift = tile_shift(t)
        for idx in range(2):
            m_old = m_s[idx]
            m_new = jnp.maximum(m_old, mx_buf[idx] - shift)
            p = jnp.exp2(s_buf[idx] - (m_new + shift))
            ps = jnp.sum(p, axis=0, keepdims=True)
            pv = jnp.dot(vt, p.astype(BF16), preferred_element_type=F32)
            alpha = jnp.exp2(m_old - m_new)
            l_s[idx] = alpha * l_s[idx] + ps
            acc_s[idx] = alpha * acc_s[idx] + pv
            m_s[idx] = m_new

    stage_a(0, s0, mx0)

    def body(i, carry):
        stage_a(2 * i + 1, s1, mx1)
        stage_b(2 * i, s0, mx0)
        stage_a(2 * i + 2, s0, mx0)
        stage_b(2 * i + 1, s1, mx1)
        return carry

    lax.fori_loop(0, qi, body, 0)
    stage_a(n_off + 1, s1, mx1)
    stage_b(n_off, s0, mx0)
    stage_b(n_off + 1, s1, mx1)

    o = acc_s[0] / l_s[0] - lam * (acc_s[1] / l_s[1])
    o = o * lax.rsqrt(jnp.mean(o * o, axis=0, keepdims=True) + EPS) * gain_ref[...]
    o_ref[...] = jnp.transpose(o * (1.0 - lam_init)).astype(BF16)


def _attention(lam_params, q, k, vt, k_meta, vt_meta, gain, bsz, seq, tq, lam_init):
    n, w = q.shape
    nq = seq // tq
    slopes = jnp.asarray([2.0 ** (-8.0 * (h + 1) / N_HEADS) for h in range(N_HEADS)], F32)
    tk = tq // 2
    r = jnp.arange(tq, dtype=jnp.int32)
    diff = r[None, :] - r[:, None]
    slopes = slopes * LOG2E
    boff = -slopes[:, None, None] * diff.astype(F32)[None, :tk]
    vis = (r[:, None] // CHUNK) <= (r[None, :] // CHUNK)
    bdiag = jnp.where(vis[None], -slopes[:, None, None] * jnp.abs(diff).astype(F32)[None], NEG)
    bias_all = jnp.stack([boff, bdiag[:, :tk], bdiag[:, tk:]], axis=1)
    mrow = jnp.arange(META_PAD, dtype=jnp.int32)
    dmeta = (N_META + r[None, :] - mrow[:, None]).astype(F32)
    bmeta = jnp.where((mrow < N_META)[None, :, None], -slopes[:, None, None] * dmeta[None], NEG)

    km = jnp.pad(k_meta, ((0, META_PAD - N_META), (0, 0)))
    vmt = jnp.pad(vt_meta, ((0, 0), (0, META_PAD - N_META)))

    kern = functools.partial(_attn_kernel, tq=tq, lam_init=lam_init)
    return pl.pallas_call(
        kern,
        grid=(bsz, N_HEADS, nq),
        in_specs=[
            pl.BlockSpec((4, HEAD_DIM), lambda b, h, i: (0, 0)),
            pl.BlockSpec((tq, V_DIM), lambda b, h, i: (b * nq + i, h)),
            pl.BlockSpec((seq, V_DIM), lambda b, h, i: (b, h)),
            pl.BlockSpec((V_DIM, seq), lambda b, h, i: (h, b)),
            pl.BlockSpec((META_PAD, V_DIM), lambda b, h, i: (0, h)),
            pl.BlockSpec((V_DIM, META_PAD), lambda b, h, i: (h, 0)),
            pl.BlockSpec((1, 3, tk, tq), lambda b, h, i: (h, 0, 0, 0)),
            pl.BlockSpec((1, META_PAD, tq), lambda b, h, i: (h, 0, 0)),
            pl.BlockSpec((V_DIM, 1), lambda b, h, i: (0, 0)),
        ],
        out_specs=pl.BlockSpec((tq, V_DIM), lambda b, h, i: (b * nq + i, h)),
        out_shape=jax.ShapeDtypeStruct((n, w), BF16),
        scratch_shapes=[pltpu.VMEM((2, 1, tq), F32), pltpu.VMEM((2, 1, tq), F32),
                        pltpu.VMEM((2, V_DIM, tq), F32),
                        pltpu.VMEM((2, tk, tq), F32), pltpu.VMEM((2, tk, tq), F32),
                        pltpu.VMEM((2, 1, tq), F32), pltpu.VMEM((2, 1, tq), F32)],
        compiler_params=pltpu.CompilerParams(
            dimension_semantics=("arbitrary", "arbitrary", "arbitrary"),
            vmem_limit_bytes=V7X_VMEM_LIMIT),
        name="diff_attention",
    )(lam_params, q, k, vt, km, vmt, bias_all, bmeta, gain)


def _mixer_kernel(x_ref, o_ref, u_ref, uh_ref, mh_ref, sga_ref, sgc_ref,
                  wap_ref, wpw_ref, wout_ref, dww_ref, dwb_ref, lng_ref, lnb_ref,
                  nffn_ref, wr_ref, br_ref, tri_ref,
                  h_ref, hn_ref, exp_ref, rank_ref, gate_ref, cnt_ref,
                  ext_s, y_s, cnt_s, *, tm, tiles_per_seq):
    i = pl.program_id(0)
    d = x_ref.shape[1]

    @pl.when(i == 0)
    def _():
        cnt_s[...] = jnp.zeros_like(cnt_s)

    first = (i % tiles_per_seq) == 0
    halo = jnp.where(first, mh_ref[...], uh_ref[...])
    ext_s[0, 0:HALO, :] = halo.astype(F32)
    ext_s[0, HALO:HALO + tm, :] = u_ref[...].astype(F32)
    span = tm + HALO - SUBLANES
    for b in range(1, SUBLANES):
        ext_s[b, 0:span, :] = ext_s[0, b:b + span, :]
    off = HALO - (CONV_K - 1)
    for c in range(d // 128):
        cs = slice(c * 128, (c + 1) * 128)
        acc = jnp.zeros((tm, 128), F32)
        for t in range(CONV_K):
            a, b = divmod(off + t, SUBLANES)
            acc = acc + dww_ref[t:t + 1, cs] * ext_s[b, a * SUBLANES:a * SUBLANES + tm, cs]
        y_s[:, cs] = acc
    y = y_s[...] + dwb_ref[...]
    mu = jnp.mean(y, axis=-1, keepdims=True)
    yc = y - mu
    var = jnp.mean(yc * yc, axis=-1, keepdims=True)
    y = yc * lax.rsqrt(var + EPS) * lng_ref[...] + lnb_ref[...]
    y = y * _sigmoid(y)
    y_conv = jnp.dot(y.astype(BF16), wpw_ref[...], preferred_element_type=F32)

    y_attn = jnp.dot(o_ref[...], wap_ref[...], preferred_element_type=F32)
    mixed = sga_ref[...].astype(F32) * y_attn + sgc_ref[...].astype(F32) * y_conv
    h = x_ref[...] + jnp.dot(mixed.astype(BF16), wout_ref[...], preferred_element_type=F32)
    h_ref[...] = h
    hn = h * lax.rsqrt(jnp.mean(h * h, axis=-1, keepdims=True) + EPS) * nffn_ref[...]
    _store_token_tiles(hn_ref, hn)

    logits = lax.dot_general(wr_ref[...], hn, (((1,), (1,)), ((), ())),
                             precision=lax.Precision.HIGHEST,
                             preferred_element_type=F32) + br_ref[...]
    eio = lax.broadcasted_iota(jnp.int32, logits.shape, 0)
    sels, tops, idxs = [], [], []
    cur = logits
    for _ in range(TOP_K):
        m = jnp.max(cur, axis=0, keepdims=True)
        idx = jnp.min(jnp.where(cur == m, eio, N_EXPERTS), axis=0, keepdims=True)
        sel = eio == idx
        sels.append(sel)
        tops.append(m)
        idxs.append(idx)
        cur = jnp.where(sel, -jnp.inf, cur)
    exps = [jnp.exp(t - tops[0]) for t in tops]
    denom = exps[0] + exps[1] + exps[2] + exps[3]
    hot = jnp.zeros(logits.shape, F32)
    for sel in sels:
        hot = hot + jnp.where(sel, 1.0, 0.0)
    before = jnp.dot(hot.astype(BF16), tri_ref[...], preferred_element_type=F32) + cnt_s[...]
    ranks = [jnp.sum(jnp.where(sel, before, 0.0), axis=0, keepdims=True).astype(jnp.int32)
             for sel in sels]
    zi = jnp.zeros((ROUTE_ROWS - TOP_K, tm), jnp.int32)
    exp_ref[0] = jnp.concatenate(idxs + [zi], axis=0)
    rank_ref[0] = jnp.concatenate(ranks + [zi], axis=0)
    gate_ref[0] = jnp.concatenate([e / denom for e in exps] + [zi.astype(F32)], axis=0)
    cnt_s[...] = cnt_s[...] + jnp.sum(hot, axis=1, keepdims=True)
    cnt_ref[...] = jnp.broadcast_to(cnt_s[...], cnt_ref.shape)


def _mixer(x2d, o, u, sga, sgc, meta_halo, wap, wpw, wout, dww, dwb, lng, lnb, nffn, wr_t, br,
           seq, tm):
    n, d = x2d.shape
    nt = n // tm
    hb = tm // HALO
    tri = (jnp.arange(tm)[:, None] < jnp.arange(tm)[None, :]).astype(BF16)
    row = pl.BlockSpec((tm, d), lambda i: (i, 0))
    route = pl.BlockSpec((1, ROUTE_ROWS, tm), lambda i: (i, 0, 0))

    def full(shp):
        return pl.BlockSpec(shp, lambda i: tuple(0 for _ in shp))

    kern = functools.partial(_mixer_kernel, tm=tm, tiles_per_seq=seq // tm)
    return pl.pallas_call(
        kern,
        grid=(nt,),
        in_specs=[row, row, row,
                  pl.BlockSpec((HALO, d), lambda i: (jnp.maximum(i * hb - 1, 0), 0)),
                  full((HALO, d)), row, row,
                  full((d, d)), full((d, d)), full((d, d)),
                  full((CONV_K, d)), full((1, d)), full((1, d)), full((1, d)), full((1, d)),
                  full((N_EXPERTS, d)), full((N_EXPERTS, 1)), full((tm, tm))],
        out_specs=[row, pl.BlockSpec((tm * SUBLANES, LANES), lambda i: (i, 0)),
                   route, route, route, full((N_EXPERTS, 128))],
        out_shape=[jax.ShapeDtypeStruct((n, d), F32),
                   jax.ShapeDtypeStruct((n * SUBLANES, LANES), F32),
                   jax.ShapeDtypeStruct((nt, ROUTE_ROWS, tm), jnp.int32),
                   jax.ShapeDtypeStruct((nt, ROUTE_ROWS, tm), jnp.int32),
                   jax.ShapeDtypeStruct((nt, ROUTE_ROWS, tm), F32),
                   jax.ShapeDtypeStruct((N_EXPERTS, 128), F32)],
        scratch_shapes=[pltpu.VMEM((SUBLANES, HALO + tm, d), F32), pltpu.VMEM((tm, d), F32),
                        pltpu.VMEM((N_EXPERTS, 1), F32)],
        compiler_params=pltpu.CompilerParams(
            dimension_semantics=("arbitrary",), vmem_limit_bytes=V7X_VMEM_LIMIT),
        name="mixer_router",
    )(x2d, o, u, u, meta_halo, sga, sgc, wap, wpw, wout, dww, dwb, lng, lnb, nffn, wr_t, br, tri)


def _prefetch_routes(dest_hbm, dsm_s, sem_idx):
    i = pl.program_id(0)
    slot = i % 2

    def idx_copy(tile, s):
        return pltpu.make_async_copy(dest_hbm.at[tile], dsm_s.at[s], sem_idx.at[s])

    @pl.when(i == 0)
    def _():
        idx_copy(0, 0).start()

    idx_copy(i, slot).wait()

    @pl.when(i + 1 < pl.num_programs(0))
    def _():
        idx_copy(i + 1, 1 - slot).start()

    return slot


def _dispatch_kernel(dest_hbm, hn_ref, xs_hbm, dsm_s, sem_idx, sem_rows, *, tm):
    slot = _prefetch_routes(dest_hbm, dsm_s, sem_idx)

    def issue(r, c):
        src = hn_ref.at[pl.ds(pl.multiple_of(r * SUBLANES, SUBLANES), SUBLANES), :]
        for kk in range(TOP_K):
            dst = pl.multiple_of(dsm_s[slot, kk, r] * SUBLANES, SUBLANES)
            pltpu.make_async_copy(src, xs_hbm.at[pl.ds(dst, SUBLANES), :], sem_rows).start()
        return c
    lax.fori_loop(0, tm, issue, 0, unroll=8)

    for kk in range(TOP_K):
        pltpu.make_async_copy(hn_ref, xs_hbm.at[pl.ds(0, tm * SUBLANES), :], sem_rows).wait()


def _dispatch(dest, hn, n_slots, tm):
    n = hn.shape[0] // SUBLANES
    return pl.pallas_call(
        functools.partial(_dispatch_kernel, tm=tm),
        grid=(n // tm,),
        in_specs=[pl.BlockSpec(memory_space=pl.ANY),
                  pl.BlockSpec((tm * SUBLANES, LANES), lambda i: (i, 0))],
        out_specs=pl.BlockSpec(memory_space=pl.ANY),
        out_shape=jax.ShapeDtypeStruct((n_slots * SUBLANES, LANES), F32),
        scratch_shapes=[pltpu.SMEM((2, ROUTE_ROWS, tm), jnp.int32),
                        pltpu.SemaphoreType.DMA((2,)), pltpu.SemaphoreType.DMA(())],
        compiler_params=pltpu.CompilerParams(
            dimension_semantics=("arbitrary",), vmem_limit_bytes=V7X_VMEM_LIMIT),
        name="dispatch",
    )(dest, hn)


def _expert_kernel(bexp_ref, bvalid_ref, xs_ref, wgu_ref, bgu_ref, wdn_ref, bdn_ref, ys_ref,
                   wgu_bf, wdn_bf):
    i = pl.program_id(0)
    valid = bvalid_ref[i]
    f = wdn_ref.shape[1]

    @pl.when((i == 0) | (bexp_ref[i] != bexp_ref[jnp.maximum(i - 1, 0)]))
    def _():
        wgu_bf[...] = wgu_ref[0].astype(BF16)
        wdn_bf[...] = wdn_ref[0].astype(BF16)

    @pl.when(valid > 0)
    def _():
        x = _load_token_tiles(xs_ref, EXPERT_BLOCK)
        rows = lax.broadcasted_iota(jnp.int32, x.shape, 0)
        xb = jnp.where(rows < valid, x, 0.0).astype(BF16)
        gu = jnp.dot(xb, wgu_bf[...], preferred_element_type=F32) + bgu_ref[0]
        glu = jnp.minimum(gu[:, :f], SWIGLU_LIMIT)
        lin = jnp.clip(gu[:, f:], -SWIGLU_LIMIT, SWIGLU_LIMIT)
        act = glu * _sigmoid(SWIGLU_ALPHA * glu) * (lin + 1.0)
        y = jnp.dot(act.astype(BF16), wdn_bf[...], preferred_element_type=F32) + bdn_ref[0]
        _store_token_tiles(ys_ref, y)

    @pl.when(valid <= 0)
    def _():
        ys_ref[...] = jnp.zeros_like(ys_ref)


def _experts(block_exp, block_valid, xs, wgu, bgu, wdn, bdn):
    d = wgu.shape[1]
    n_slots = xs.shape[0] // SUBLANES
    nb = n_slots // EXPERT_BLOCK
    f2 = wgu.shape[2]
    f = wdn.shape[1]
    tile_rows = EXPERT_BLOCK * SUBLANES
    grid_spec = pltpu.PrefetchScalarGridSpec(
        num_scalar_prefetch=2,
        grid=(nb,),
        in_specs=[pl.BlockSpec((tile_rows, LANES), lambda i, be, bv: (i, 0)),
                  pl.BlockSpec((1, d, f2), lambda i, be, bv: (be[i], 0, 0)),
                  pl.BlockSpec((1, 1, f2), lambda i, be, bv: (be[i], 0, 0)),
                  pl.BlockSpec((1, f, d), lambda i, be, bv: (be[i], 0, 0)),
                  pl.BlockSpec((1, 1, d), lambda i, be, bv: (be[i], 0, 0))],
        out_specs=pl.BlockSpec((tile_rows, LANES), lambda i, be, bv: (i, 0)),
        scratch_shapes=[pltpu.VMEM((d, f2), BF16), pltpu.VMEM((f, d), BF16)],
    )
    return pl.pallas_call(
        _expert_kernel,
        grid_spec=grid_spec,
        out_shape=jax.ShapeDtypeStruct(xs.shape, F32),
        compiler_params=pltpu.CompilerParams(
            dimension_semantics=("arbitrary",), vmem_limit_bytes=V7X_VMEM_LIMIT),
        name="experts",
    )(block_exp, block_valid, xs, wgu, bgu, wdn, bdn)


def _combine_kernel(dest_hbm, h_ref, gate_ref, g_ref, ys_hbm, out_ref, dsm_s, ybuf, sem_idx, sem_rows,
                    *, tm):
    i = pl.program_id(0)
    nsteps = pl.num_programs(0)
    slot = i % 2

    def idx_copy(tile, s):
        return pltpu.make_async_copy(dest_hbm.at[tile], dsm_s.at[s], sem_idx.at[s])

    def issue_rows(s):
        def issue(r, c):
            for kk in range(TOP_K):
                src = pl.multiple_of(dsm_s[s, kk, r] * SUBLANES, SUBLANES)
                dst = pl.multiple_of(r * SUBLANES, SUBLANES)
                pltpu.make_async_copy(ys_hbm.at[pl.ds(src, SUBLANES), :],
                                      ybuf.at[s, kk, pl.ds(dst, SUBLANES), :],
                                      sem_rows.at[s]).start()
            return c
        lax.fori_loop(0, tm, issue, 0, unroll=8)

    @pl.when(i == 0)
    def _():
        idx_copy(0, 0).start()
        idx_copy(0, 0).wait()
        issue_rows(0)

        @pl.when(nsteps > 1)
        def _():
            idx_copy(1, 1).start()

    @pl.when(i + 1 < nsteps)
    def _():
        idx_copy(i + 1, 1 - slot).wait()
        issue_rows(1 - slot)

        @pl.when(i + 2 < nsteps)
        def _():
            idx_copy(i + 2, slot).start()

    for kk in range(TOP_K):
        pltpu.make_async_copy(ys_hbm.at[pl.ds(0, tm * SUBLANES), :], ybuf.at[slot, kk],
                              sem_rows.at[slot]).wait()

    gates = jnp.transpose(gate_ref[0])
    acc = h_ref[...]
    for kk in range(TOP_K):
        acc = acc + gates[:, kk:kk + 1] * _load_token_tiles(ybuf.at[slot, kk], tm)
    out_ref[...] = acc * lax.rsqrt(jnp.mean(acc * acc, axis=-1, keepdims=True) + EPS) * g_ref[...]


def _combine(dest, h, gates, g_final, ys, tm):
    n, d = h.shape
    return pl.pallas_call(
        functools.partial(_combine_kernel, tm=tm),
        grid=(n // tm,),
        in_specs=[pl.BlockSpec(memory_space=pl.ANY),
                  pl.BlockSpec((tm, d), lambda i: (i, 0)),
                  pl.BlockSpec((1, ROUTE_ROWS, tm), lambda i: (i, 0, 0)),
                  pl.BlockSpec((1, d), lambda i: (0, 0)),
                  pl.BlockSpec(memory_space=pl.ANY)],
        out_specs=pl.BlockSpec((tm, d), lambda i: (i, 0)),
        out_shape=jax.ShapeDtypeStruct((n, d), F32),
        scratch_shapes=[pltpu.SMEM((2, ROUTE_ROWS, tm), jnp.int32),
                        pltpu.VMEM((2, TOP_K, tm * SUBLANES, LANES), F32),
                        pltpu.SemaphoreType.DMA((2,)), pltpu.SemaphoreType.DMA((2,))],
        compiler_params=pltpu.CompilerParams(
            dimension_semantics=("arbitrary",), vmem_limit_bytes=V7X_VMEM_LIMIT),
        name="combine",
    )(dest, h, gates, g_final, ys)


def kernel(x, meta_tokens, norm_mix, w_in, b_in, lam_params, subln_gain, w_attn_proj, conv_dw_w,
           conv_dw_b, conv_ln_g, conv_ln_b, w_conv_proj, w_out, norm_ffn, w_router, b_router,
           w_gate_up, b_gate_up, w_down, b_down, norm_final):
    bsz, seq, d = x.shape
    assert norm_mix.shape[0] == 1, "single-layer block"
    n = bsz * seq
    tm_proj = min(512, seq)
    tq = min(512, seq)
    tm = min(256, seq)
    lam_init = 0.8 - 0.6 * 1.0

    x2d = x.reshape(n, d)
    w_rest = jnp.concatenate([w_in[0][:, :2 * d], w_in[0][:, 3 * d:]], axis=1).astype(BF16)
    b_rest = jnp.concatenate([b_in[0][:2 * d], b_in[0][3 * d:]])[None]
    w_vt = w_in[0][:, 2 * d:3 * d].T.astype(BF16)
    b_vt = b_in[0][2 * d:3 * d][:, None]
    g_mix = norm_mix[0][None]
    q, k, vt, u, sga, sgc = _inproj(x2d, g_mix, w_rest, b_rest, w_vt, b_vt, tm_proj)
    _, k_m, vt_m, u_m, _, _ = _inproj(meta_tokens.astype(x.dtype), g_mix, w_rest, b_rest, w_vt, b_vt,
                                      N_META)

    o = _attention(lam_params[0], q, k, vt, k_m, vt_m, subln_gain[0][:, None], bsz, seq, tq,
                   lam_init)

    meta_halo = jnp.concatenate([jnp.zeros((HALO - N_META, d), BF16), u_m], axis=0)
    h, hn, eidx, rank, gates, counts = _mixer(
        x2d, o, u, sga, sgc, meta_halo,
        w_attn_proj[0].astype(BF16), w_conv_proj[0].astype(BF16), w_out[0].astype(BF16),
        conv_dw_w[0], conv_dw_b[0][None], conv_ln_g[0][None], conv_ln_b[0][None],
        norm_ffn[0][None], w_router[0].T, b_router[0][:, None], seq, tm)

    cnt = counts[:, 0].astype(jnp.int32)
    padded = (cnt + EXPERT_BLOCK - 1) // EXPERT_BLOCK * EXPERT_BLOCK
    pad_end = jnp.cumsum(padded)
    pad_start = pad_end - padded
    n_blocks = -(-(n * TOP_K + N_EXPERTS * (EXPERT_BLOCK - 1)) // EXPERT_BLOCK)
    n_slots = n_blocks * EXPERT_BLOCK
    dest = pad_start[eidx] + rank
    block_start = jnp.arange(n_blocks, dtype=jnp.int32) * EXPERT_BLOCK
    block_exp = jnp.minimum(jnp.sum(block_start[:, None] >= pad_end[None, :], axis=1),
                            N_EXPERTS - 1).astype(jnp.int32)
    block_valid = jnp.clip(pad_start[block_exp] + cnt[block_exp] - block_start, 0,
                           EXPERT_BLOCK).astype(jnp.int32)

    xs = _dispatch(dest, hn, n_slots, tm)
    ys = _experts(block_exp, block_valid, xs, w_gate_up[0], b_gate_up[0][:, None, :],
                  w_down[0], b_down[0][:, None, :])
    out = _combine(dest, h, gates, norm_final[None], ys, tm)
    return out.reshape(bsz, seq, d)
```

```python
import functools

import jax
import jax.numpy as jnp
from jax import lax
from jax.experimental import pallas as pl
from jax.experimental.pallas import tpu as pltpu

CHUNK = 64
N_META = 16
N_HEADS = 8
HEAD_DIM = 64
V_DIM = 2 * HEAD_DIM
CONV_K = 31
N_EXPERTS = 32
TOP_K = 4
SWIGLU_ALPHA = 1.702
SWIGLU_LIMIT = 7.0
EXPERT_BLOCK = 256
EPS = 1e-5
NEG = -1e30
LOG2E = 1.4426950408889634

V7X_VMEM_LIMIT = 56 * 1024 * 1024
SUBLANES = 8
LANES = 128
META_PAD = 128
HALO = 32
ROUTE_ROWS = 8

BF16 = jnp.bfloat16
F32 = jnp.float32


def _sigmoid(x):
    return 1.0 / (1.0 + jnp.exp(-x))


def _load_token_tiles(ref, rows):
    return jnp.concatenate([ref[pl.ds(s, rows, stride=SUBLANES), :] for s in range(SUBLANES)],
                           axis=-1)


def _store_token_tiles(ref, x):
    rows = x.shape[0]
    for s in range(SUBLANES):
        ref[pl.ds(s, rows, stride=SUBLANES), :] = x[:, s * LANES:(s + 1) * LANES]


def _inproj_kernel(x_ref, g_ref, w_ref, b_ref, wvt_ref, bvt_ref,
                   q_ref, k_ref, vt_ref, u_ref, sga_ref, sgc_ref):
    d = x_ref.shape[1]
    x = x_ref[...]
    xn = x * lax.rsqrt(jnp.mean(x * x, axis=-1, keepdims=True) + EPS) * g_ref[...]
    xb = xn.astype(BF16)

    def proj(c):
        return (jnp.dot(xb, w_ref[:, c * d:(c + 1) * d], preferred_element_type=F32)
                + b_ref[:, c * d:(c + 1) * d])

    q_ref[...] = (proj(0) * (HEAD_DIM ** -0.5 * LOG2E)).astype(BF16)
    k_ref[...] = proj(1).astype(BF16)
    vt = lax.dot_general(wvt_ref[...], xb, (((1,), (1,)), ((), ())), preferred_element_type=F32)
    vt_ref[...] = (vt + bvt_ref[...]).astype(BF16)
    u_ref[...] = (proj(2) * _sigmoid(proj(3))).astype(BF16)
    sga_ref[...] = _sigmoid(proj(4)).astype(BF16)
    sgc_ref[...] = _sigmoid(proj(5)).astype(BF16)


def _inproj(x2d, g, w_bf, b, wvt_bf, bvt, tm):
    n, d = x2d.shape
    in_w = w_bf.shape[1]
    out = jax.ShapeDtypeStruct((n, d), BF16)
    row = pl.BlockSpec((tm, d), lambda i: (i, 0))
    col = pl.BlockSpec((d, tm), lambda i: (0, i))
    return pl.pallas_call(
        _inproj_kernel,
        grid=(n // tm,),
        in_specs=[row,
                  pl.BlockSpec((1, d), lambda i: (0, 0)),
                  pl.BlockSpec((d, in_w), lambda i: (0, 0)),
                  pl.BlockSpec((1, in_w), lambda i: (0, 0)),
                  pl.BlockSpec((d, d), lambda i: (0, 0)),
                  pl.BlockSpec((d, 1), lambda i: (0, 0))],
        out_specs=[row, row, col, row, row, row],
        out_shape=[out, out, jax.ShapeDtypeStruct((d, n), BF16), out, out, out],
        compiler_params=pltpu.CompilerParams(
            dimension_semantics=("arbitrary",), vmem_limit_bytes=V7X_VMEM_LIMIT),
        name="inproj",
    )(x2d, g, w_bf, b, wvt_bf, bvt)


def _attn_kernel(lam_ref, q_ref, k_ref, vt_ref, km_ref, vmt_ref, bias_ref, bmeta_ref,
                 gain_ref, o_ref, m_s, l_s, acc_s, s0, s1, mx0, mx1, *, tq, lam_init):
    h = pl.program_id(1)
    qi = pl.program_id(2)
    slope = jnp.exp2(-(h + 1).astype(F32))
    lp = lam_ref[...]
    lam = (jnp.exp(jnp.sum(lp[0:1] * lp[1:2], axis=-1, keepdims=True))
           - jnp.exp(jnp.sum(lp[2:3] * lp[3:4], axis=-1, keepdims=True)) + lam_init)

    q = q_ref[...]
    lane = lax.broadcasted_iota(jnp.int32, q.shape, 1)
    zero = jnp.zeros_like(q)
    qs = (jnp.where(lane < HEAD_DIM, q, zero), jnp.where(lane >= HEAD_DIM, q, zero))

    def scores_t(kk, qm):
        return lax.dot_general(kk, qm, (((1,), (1,)), ((), ())), preferred_element_type=F32)

    shift0 = slope * ((qi * tq).astype(F32) * LOG2E)
    km = km_ref[...]
    vmt = vmt_ref[...]
    for idx in range(2):
        s = scores_t(km, qs[idx]) + bmeta_ref[0]
        m_new = jnp.max(s, axis=0, keepdims=True) - shift0
        p = jnp.exp2(s - (m_new + shift0))
        l_s[idx] = jnp.sum(p, axis=0, keepdims=True)
        acc_s[idx] = jnp.dot(vmt, p.astype(BF16), preferred_element_type=F32)
        m_s[idx] = m_new

    tk = tq // 2
    n_off = 2 * qi

    def tile_shift(t):
        off = (qi * tq - t * tk).astype(F32) * LOG2E
        return jnp.where(t < n_off, slope * off, 0.0)

    def stage_a(t, s_buf, mx_buf):
        kk = k_ref[pl.ds(pl.multiple_of(t * tk, tk), tk), :]
        bias = bias_ref[0, jnp.where(t < n_off, 0, t - n_off + 1)]
        for idx in range(2):
            s = scores_t(kk, qs[idx]) + bias
            s_buf[idx] = s
            mx_buf[idx] = jnp.max(s, axis=0, keepdims=True)

    def stage_b(t, s_buf, mx_buf):
        vt = vt_ref[:, pl.ds(pl.multiple_of(t * tk, tk), tk)]
        vt1 = jnp.concatenate([vt, jnp.ones((2 * SUBLANES, tk), BF16)], axis=0)
        shift = tile_shift(t)
        for idx in range(2):
            m_old = m_s[idx]
            m_new = jnp.maximum(m_old, mx_buf[idx] - shift)
            p = jnp.exp2(s_buf[idx] - (m_new + shift))
            pv1 = jnp.dot(vt1, p.astype(BF16), preferred_element_type=F32)
            pv = pv1[:V_DIM]
            ps = pv1[V_DIM:V_DIM + 1]
            alpha = jnp.exp2(m_old - m_new)
            l_s[idx] = alpha * l_s[idx] + ps
            acc_s[idx] = alpha * acc_s[idx] + pv
            m_s[idx] = m_new

    stage_a(0, s0, mx0)

    def body(i, carry):
        stage_a(2 * i + 1, s1, mx1)
        stage_b(2 * i, s0, mx0)
        stage_a(2 * i + 2, s0, mx0)
        stage_b(2 * i + 1, s1, mx1)
        return carry

    lax.fori_loop(0, qi, body, 0)
    stage_a(n_off + 1, s1, mx1)
    stage_b(n_off, s0, mx0)
    stage_b(n_off + 1, s1, mx1)

    o = acc_s[0] / l_s[0] - lam * (acc_s[1] / l_s[1])
    o = o * lax.rsqrt(jnp.mean(o * o, axis=0, keepdims=True) + EPS) * gain_ref[...]
    o_ref[...] = jnp.transpose(o * (1.0 - lam_init)).astype(BF16)


def _attention(lam_params, q, k, vt, k_meta, vt_meta, gain, bsz, seq, tq, lam_init):
    n, w = q.shape
    nq = seq // tq
    slopes = jnp.asarray([2.0 ** (-8.0 * (h + 1) / N_HEADS) for h in range(N_HEADS)], F32)
    tk = tq // 2
    r = jnp.arange(tq, dtype=jnp.int32)
    diff = r[None, :] - r[:, None]
    slopes = slopes * LOG2E
    boff = -slopes[:, None, None] * diff.astype(F32)[None, :tk]
    vis = (r[:, None] // CHUNK) <= (r[None, :] // CHUNK)
    bdiag = jnp.where(vis[None], -slopes[:, None, None] * jnp.abs(diff).astype(F32)[None], NEG)
    bias_all = jnp.stack([boff, bdiag[:, :tk], bdiag[:, tk:]], axis=1)
    mrow = jnp.arange(META_PAD, dtype=jnp.int32)
    dmeta = (N_META + r[None, :] - mrow[:, None]).astype(F32)
    bmeta = jnp.where((mrow < N_META)[None, :, None], -slopes[:, None, None] * dmeta[None], NEG)

    km = jnp.pad(k_meta, ((0, META_PAD - N_META), (0, 0)))
    vmt = jnp.pad(vt_meta, ((0, 0), (0, META_PAD - N_META)))

    kern = functools.partial(_attn_kernel, tq=tq, lam_init=lam_init)
    return pl.pallas_call(
        kern,
        grid=(bsz, N_HEADS, nq),
        in_specs=[
            pl.BlockSpec((4, HEAD_DIM), lambda b, h, i: (0, 0)),
            pl.BlockSpec((tq, V_DIM), lambda b, h, i: (b * nq + i, h)),
            pl.BlockSpec((seq, V_DIM), lambda b, h, i: (b, h)),
            pl.BlockSpec((V_DIM, seq), lambda b, h, i: (h, b)),
            pl.BlockSpec((META_PAD, V_DIM), lambda b, h, i: (0, h)),
            pl.BlockSpec((V_DIM, META_PAD), lambda b, h, i: (h, 0)),
            pl.BlockSpec((1, 3, tk, tq), lambda b, h, i: (h, 0, 0, 0)),
            pl.BlockSpec((1, META_PAD, tq), lambda b, h, i: (h, 0, 0)),
            pl.BlockSpec((V_DIM, 1), lambda b, h, i: (0, 0)),
        ],
        out_specs=pl.BlockSpec((tq, V_DIM), lambda b, h, i: (b * nq + i, h)),
        out_shape=jax.ShapeDtypeStruct((n, w), BF16),
        scratch_shapes=[pltpu.VMEM((2, 1, tq), F32), pltpu.VMEM((2, 1, tq), F32),
                        pltpu.VMEM((2, V_DIM, tq), F32),
                        pltpu.VMEM((2, tk, tq), F32), pltpu.VMEM((2, tk, tq), F32),
                        pltpu.VMEM((2, 1, tq), F32), pltpu.VMEM((2, 1, tq), F32)],
        compiler_params=pltpu.CompilerParams(
            dimension_semantics=("arbitrary", "arbitrary", "arbitrary"),
            vmem_limit_bytes=V7X_VMEM_LIMIT),
        name="diff_attention",
    )(lam_params, q, k, vt, km, vmt, bias_all, bmeta, gain)


def _mixer_kernel(x_ref, o_ref, u_ref, uh_ref, mh_ref, sga_ref, sgc_ref,
                  wap_ref, wpw_ref, wout_ref, dww_ref, dwb_ref, lng_ref, lnb_ref,
                  nffn_ref, wr_ref, br_ref, tri_ref,
                  h_ref, hn_ref, exp_ref, rank_ref, gate_ref, cnt_ref,
                  ext_s, y_s, cnt_s, *, tm, tiles_per_seq):
    i = pl.program_id(0)
    d = x_ref.shape[1]

    @pl.when(i == 0)
    def _():
        cnt_s[...] = jnp.zeros_like(cnt_s)

    first = (i % tiles_per_seq) == 0
    halo = jnp.where(first, mh_ref[...], uh_ref[...])
    ext_s[0, 0:HALO, :] = halo.astype(F32)
    ext_s[0, HALO:HALO + tm, :] = u_ref[...].astype(F32)
    span = tm + HALO - SUBLANES
    for b in range(1, SUBLANES):
        ext_s[b, 0:span, :] = ext_s[0, b:b + span, :]
    off = HALO - (CONV_K - 1)
    for c in range(d // 128):
        cs = slice(c * 128, (c + 1) * 128)
        acc = jnp.zeros((tm, 128), F32)
        for t in range(CONV_K):
            a, b = divmod(off + t, SUBLANES)
            acc = acc + dww_ref[t:t + 1, cs] * ext_s[b, a * SUBLANES:a * SUBLANES + tm, cs]
        y_s[:, cs] = acc
    y = y_s[...] + dwb_ref[...]
    mu = jnp.mean(y, axis=-1, keepdims=True)
    yc = y - mu
    var = jnp.mean(yc * yc, axis=-1, keepdims=True)
    y = yc * lax.rsqrt(var + EPS) * lng_ref[...] + lnb_ref[...]
    y = y * _sigmoid(y)
    y_conv = jnp.dot(y.astype(BF16), wpw_ref[...], preferred_element_type=F32)

    y_attn = jnp.dot(o_ref[...], wap_ref[...], preferred_element_type=F32)
    mixed = sga_ref[...].astype(F32) * y_attn + sgc_ref[...].astype(F32) * y_conv
    h = x_ref[...] + jnp.dot(mixed.astype(BF16), wout_ref[...], preferred_element_type=F32)
    h_ref[...] = h
    hn = h * lax.rsqrt(jnp.mean(h * h, axis=-1, keepdims=True) + EPS) * nffn_ref[...]
    _store_token_tiles(hn_ref, hn)

    logits = lax.dot_general(wr_ref[...], hn, (((1,), (1,)), ((), ())),
                             precision=lax.Precision.HIGHEST,
                             preferred_element_type=F32) + br_ref[...]
    eio = lax.broadcasted_iota(jnp.int32, logits.shape, 0)
    sels, tops, idxs = [], [], []
    cur = logits
    for _ in range(TOP_K):
        m = jnp.max(cur, axis=0, keepdims=True)
        idx = jnp.min(jnp.where(cur == m, eio, N_EXPERTS), axis=0, keepdims=True)
        sel = eio == idx
        sels.append(sel)
        tops.append(m)
        idxs.append(idx)
        cur = jnp.where(sel, -jnp.inf, cur)
    exps = [jnp.exp(t - tops[0]) for t in tops]
    denom = exps[0] + exps[1] + exps[2] + exps[3]
    hot = jnp.zeros(logits.shape, F32)
    for sel in sels:
        hot = hot + jnp.where(sel, 1.0, 0.0)
    before = jnp.dot(hot.astype(BF16), tri_ref[...], preferred_element_type=F32) + cnt_s[...]
    ranks = [jnp.sum(jnp.where(sel, before, 0.0), axis=0, keepdims=True).astype(jnp.int32)
             for sel in sels]
    zi = jnp.zeros((ROUTE_ROWS - TOP_K, tm), jnp.int32)
    exp_ref[0] = jnp.concatenate(idxs + [zi], axis=0)
    rank_ref[0] = jnp.concatenate(ranks + [zi], axis=0)
    gate_ref[0] = jnp.concatenate([e / denom for e in exps] + [zi.astype(F32)], axis=0)
    cnt_s[...] = cnt_s[...] + jnp.sum(hot, axis=1, keepdims=True)
    cnt_ref[...] = jnp.broadcast_to(cnt_s[...], cnt_ref.shape)


def _mixer(x2d, o, u, sga, sgc, meta_halo, wap, wpw, wout, dww, dwb, lng, lnb, nffn, wr_t, br,
           seq, tm):
    n, d = x2d.shape
    nt = n // tm
    hb = tm // HALO
    tri = (jnp.arange(tm)[:, None] < jnp.arange(tm)[None, :]).astype(BF16)
    row = pl.BlockSpec((tm, d), lambda i: (i, 0))
    route = pl.BlockSpec((1, ROUTE_ROWS, tm), lambda i: (i, 0, 0))

    def full(shp):
        return pl.BlockSpec(shp, lambda i: tuple(0 for _ in shp))

    kern = functools.partial(_mixer_kernel, tm=tm, tiles_per_seq=seq // tm)
    return pl.pallas_call(
        kern,
        grid=(nt,),
        in_specs=[row, row, row,
                  pl.BlockSpec((HALO, d), lambda i: (jnp.maximum(i * hb - 1, 0), 0)),
                  full((HALO, d)), row, row,
                  full((d, d)), full((d, d)), full((d, d)),
                  full((CONV_K, d)), full((1, d)), full((1, d)), full((1, d)), full((1, d)),
                  full((N_EXPERTS, d)), full((N_EXPERTS, 1)), full((tm, tm))],
        out_specs=[row, pl.BlockSpec((tm * SUBLANES, LANES), lambda i: (i, 0)),
                   route, route, route, full((N_EXPERTS, 128))],
        out_shape=[jax.ShapeDtypeStruct((n, d), F32),
                   jax.ShapeDtypeStruct((n * SUBLANES, LANES), F32),
                   jax.ShapeDtypeStruct((nt, ROUTE_ROWS, tm), jnp.int32),
                   jax.ShapeDtypeStruct((nt, ROUTE_ROWS, tm), jnp.int32),
                   jax.ShapeDtypeStruct((nt, ROUTE_ROWS, tm), F32),
                   jax.ShapeDtypeStruct((N_EXPERTS, 128), F32)],
        scratch_shapes=[pltpu.VMEM((SUBLANES, HALO + tm, d), F32), pltpu.VMEM((tm, d), F32),
                        pltpu.VMEM((N_EXPERTS, 1), F32)],
        compiler_params=pltpu.CompilerParams(
            dimension_semantics=("arbitrary",), vmem_limit_bytes=V7X_VMEM_LIMIT),
        name="mixer_router",
    )(x2d, o, u, u, meta_halo, sga, sgc, wap, wpw, wout, dww, dwb, lng, lnb, nffn, wr_t, br, tri)


def _prefetch_routes(dest_hbm, dsm_s, sem_idx):
    i = pl.program_id(0)
    slot = i % 2

    def idx_copy(tile, s):
        return pltpu.make_async_copy(dest_hbm.at[tile], dsm_s.at[s], sem_idx.at[s])

    @pl.when(i == 0)
    def _():
        idx_copy(0, 0).start()

    idx_copy(i, slot).wait()

    @pl.when(i + 1 < pl.num_programs(0))
    def _():
        idx_copy(i + 1, 1 - slot).start()

    return slot


def _dispatch_kernel(dest_hbm, hn_ref, xs_hbm, dsm_s, sem_idx, sem_rows, *, tm):
    slot = _prefetch_routes(dest_hbm, dsm_s, sem_idx)

    def issue(r, c):
        src = hn_ref.at[pl.ds(pl.multiple_of(r * SUBLANES, SUBLANES), SUBLANES), :]
        for kk in range(TOP_K):
            dst = pl.multiple_of(dsm_s[slot, kk, r] * SUBLANES, SUBLANES)
            pltpu.make_async_copy(src, xs_hbm.at[pl.ds(dst, SUBLANES), :],
                                  sem_rows).start(priority=kk % 2)
        return c
    lax.fori_loop(0, tm, issue, 0, unroll=8)

    for kk in range(TOP_K):
        pltpu.make_async_copy(hn_ref, xs_hbm.at[pl.ds(0, tm * SUBLANES), :], sem_rows).wait()


def _dispatch(dest, hn, n_slots, tm):
    n = hn.shape[0] // SUBLANES
    return pl.pallas_call(
        functools.partial(_dispatch_kernel, tm=tm),
        grid=(n // tm,),
        in_specs=[pl.BlockSpec(memory_space=pl.ANY),
                  pl.BlockSpec((tm * SUBLANES, LANES), lambda i: (i, 0))],
        out_specs=pl.BlockSpec(memory_space=pl.ANY),
        out_shape=jax.ShapeDtypeStruct((n_slots * SUBLANES, LANES), F32),
        scratch_shapes=[pltpu.SMEM((2, ROUTE_ROWS, tm), jnp.int32),
                        pltpu.SemaphoreType.DMA((2,)), pltpu.SemaphoreType.DMA(())],
        compiler_params=pltpu.CompilerParams(
            dimension_semantics=("arbitrary",), vmem_limit_bytes=V7X_VMEM_LIMIT),
        name="dispatch",
    )(dest, hn)


def _expert_kernel(bexp_ref, bvalid_ref, xs_ref, wgu_ref, bgu_ref, wdn_ref, bdn_ref, ys_ref,
                   wgu_bf, wdn_bf):
    i = pl.program_id(0)
    valid = bvalid_ref[i]
    f = wdn_ref.shape[1]

    @pl.when((i == 0) | (bexp_ref[i] != bexp_ref[jnp.maximum(i - 1, 0)]))
    def _():
        wgu_bf[...] = wgu_ref[0].astype(BF16)
        wdn_bf[...] = wdn_ref[0].astype(BF16)

    @pl.when(valid > 0)
    def _():
        x = _load_token_tiles(xs_ref, EXPERT_BLOCK)
        rows = lax.broadcasted_iota(jnp.int32, x.shape, 0)
        xb = jnp.where(rows < valid, x, 0.0).astype(BF16)
        gu = jnp.dot(xb, wgu_bf[...], preferred_element_type=F32) + bgu_ref[0]
        glu = jnp.minimum(gu[:, :f], SWIGLU_LIMIT)
        lin = jnp.clip(gu[:, f:], -SWIGLU_LIMIT, SWIGLU_LIMIT)
        act = glu * _sigmoid(SWIGLU_ALPHA * glu) * (lin + 1.0)
        y = jnp.dot(act.astype(BF16), wdn_bf[...], preferred_element_type=F32) + bdn_ref[0]
        _store_token_tiles(ys_ref, y)

    @pl.when(valid <= 0)
    def _():
        ys_ref[...] = jnp.zeros_like(ys_ref)


def _experts(block_exp, block_valid, xs, wgu, bgu, wdn, bdn):
    d = wgu.shape[1]
    n_slots = xs.shape[0] // SUBLANES
    nb = n_slots // EXPERT_BLOCK
    f2 = wgu.shape[2]
    f = wdn.shape[1]
    tile_rows = EXPERT_BLOCK * SUBLANES
    grid_spec = pltpu.PrefetchScalarGridSpec(
        num_scalar_prefetch=2,
        grid=(nb,),
        in_specs=[pl.BlockSpec((tile_rows, LANES), lambda i, be, bv: (i, 0)),
                  pl.BlockSpec((1, d, f2), lambda i, be, bv: (be[i], 0, 0)),
                  pl.BlockSpec((1, 1, f2), lambda i, be, bv: (be[i], 0, 0)),
                  pl.BlockSpec((1, f, d), lambda i, be, bv: (be[i], 0, 0)),
                  pl.BlockSpec((1, 1, d), lambda i, be, bv: (be[i], 0, 0))],
        out_specs=pl.BlockSpec((tile_rows, LANES), lambda i, be, bv: (i, 0)),
        scratch_shapes=[pltpu.VMEM((d, f2), BF16), pltpu.VMEM((f, d), BF16)],
    )
    return pl.pallas_call(
        _expert_kernel,
        grid_spec=grid_spec,
        out_shape=jax.ShapeDtypeStruct(xs.shape, F32),
        compiler_params=pltpu.CompilerParams(
            dimension_semantics=("arbitrary",), vmem_limit_bytes=V7X_VMEM_LIMIT),
        name="experts",
    )(block_exp, block_valid, xs, wgu, bgu, wdn, bdn)


def _combine_kernel(dest_hbm, h_ref, gate_ref, g_ref, ys_hbm, out_ref, dsm_s, ybuf, sem_idx, sem_rows,
                    *, tm):
    i = pl.program_id(0)
    nsteps = pl.num_programs(0)
    slot = i % 2

    def idx_copy(tile, s):
        return pltpu.make_async_copy(dest_hbm.at[tile], dsm_s.at[s], sem_idx.at[s])

    def issue_rows(s):
        def issue(r, c):
            for kk in range(TOP_K):
                src = pl.multiple_of(dsm_s[s, kk, r] * SUBLANES, SUBLANES)
                dst = pl.multiple_of(r * SUBLANES, SUBLANES)
                pltpu.make_async_copy(ys_hbm.at[pl.ds(src, SUBLANES), :],
                                      ybuf.at[s, kk, pl.ds(dst, SUBLANES), :],
                                      sem_rows.at[s]).start(priority=kk % 2)
            return c
        lax.fori_loop(0, tm, issue, 0, unroll=8)

    @pl.when(i == 0)
    def _():
        idx_copy(0, 0).start()
        idx_copy(0, 0).wait()
        issue_rows(0)

        @pl.when(nsteps > 1)
        def _():
            idx_copy(1, 1).start()

    @pl.when(i + 1 < nsteps)
    def _():
        idx_copy(i + 1, 1 - slot).wait()
        issue_rows(1 - slot)

        @pl.when(i + 2 < nsteps)
        def _():
            idx_copy(i + 2, slot).start()

    for kk in range(TOP_K):
        pltpu.make_async_copy(ys_hbm.at[pl.ds(0, tm * SUBLANES), :], ybuf.at[slot, kk],
                              sem_rows.at[slot]).wait()

    gates = jnp.transpose(gate_ref[0])
    acc = h_ref[...]
    for kk in range(TOP_K):
        acc = acc + gates[:, kk:kk + 1] * _load_token_tiles(ybuf.at[slot, kk], tm)
    out_ref[...] = acc * lax.rsqrt(jnp.mean(acc * acc, axis=-1, keepdims=True) + EPS) * g_ref[...]


def _combine(dest, h, gates, g_final, ys, tm):
    n, d = h.shape
    return pl.pallas_call(
        functools.partial(_combine_kernel, tm=tm),
        grid=(n // tm,),
        in_specs=[pl.BlockSpec(memory_space=pl.ANY),
                  pl.BlockSpec((tm, d), lambda i: (i, 0)),
                  pl.BlockSpec((1, ROUTE_ROWS, tm), lambda i: (i, 0, 0)),
                  pl.BlockSpec((1, d), lambda i: (0, 0)),
                  pl.BlockSpec(memory_space=pl.ANY)],
        out_specs=pl.BlockSpec((tm, d), lambda i: (i, 0)),
        out_shape=jax.ShapeDtypeStruct((n, d), F32),
        scratch_shapes=[pltpu.SMEM((2, ROUTE_ROWS, tm), jnp.int32),
                        pltpu.VMEM((2, TOP_K, tm * SUBLANES, LANES), F32),
                        pltpu.SemaphoreType.DMA((2,)), pltpu.SemaphoreType.DMA((2,))],
        compiler_params=pltpu.CompilerParams(
            dimension_semantics=("arbitrary",), vmem_limit_bytes=V7X_VMEM_LIMIT),
        name="combine",
    )(dest, h, gates, g_final, ys)


def kernel(x, meta_tokens, norm_mix, w_in, b_in, lam_params, subln_gain, w_attn_proj, conv_dw_w,
           conv_dw_b, conv_ln_g, conv_ln_b, w_conv_proj, w_out, norm_ffn, w_router, b_router,
           w_gate_up, b_gate_up, w_down, b_down, norm_final):
    bsz, seq, d = x.shape
    assert norm_mix.shape[0] == 1, "single-layer block"
    n = bsz * seq
    tm_proj = min(512, seq)
    tq = min(512, seq)
    tm = min(256, seq)
    lam_init = 0.8 - 0.6 * 1.0

    x2d = x.reshape(n, d)
    w_rest = jnp.concatenate([w_in[0][:, :2 * d], w_in[0][:, 3 * d:]], axis=1).astype(BF16)
    b_rest = jnp.concatenate([b_in[0][:2 * d], b_in[0][3 * d:]])[None]
    w_vt = w_in[0][:, 2 * d:3 * d].T.astype(BF16)
    b_vt = b_in[0][2 * d:3 * d][:, None]
    g_mix = norm_mix[0][None]
    q, k, vt, u, sga, sgc = _inproj(x2d, g_mix, w_rest, b_rest, w_vt, b_vt, tm_proj)
    _, k_m, vt_m, u_m, _, _ = _inproj(meta_tokens.astype(x.dtype), g_mix, w_rest, b_rest, w_vt, b_vt,
                                      N_META)

    o = _attention(lam_params[0], q, k, vt, k_m, vt_m, subln_gain[0][:, None], bsz, seq, tq,
                   lam_init)

    meta_halo = jnp.concatenate([jnp.zeros((HALO - N_META, d), BF16), u_m], axis=0)
    h, hn, eidx, rank, gates, counts = _mixer(
        x2d, o, u, sga, sgc, meta_halo,
        w_attn_proj[0].astype(BF16), w_conv_proj[0].astype(BF16), w_out[0].astype(BF16),
        conv_dw_w[0], conv_dw_b[0][None], conv_ln_g[0][None], conv_ln_b[0][None],
        norm_ffn[0][None], w_router[0].T, b_router[0][:, None], seq, tm)

    cnt = counts[:, 0].astype(jnp.int32)
    padded = (cnt + EXPERT_BLOCK - 1) // EXPERT_BLOCK * EXPERT_BLOCK
    pad_end = jnp.cumsum(padded)
    pad_start = pad_end - padded
    n_blocks = -(-(n * TOP_K + N_EXPERTS * (EXPERT_BLOCK - 1)) // EXPERT_BLOCK)
    n_slots = n_blocks * EXPERT_BLOCK
    dest = pad_start[eidx] + rank
    block_start = jnp.arange(n_blocks, dtype=jnp.int32) * EXPERT_BLOCK
    block_exp = jnp.minimum(jnp.sum(block_start[:, None] >= pad_end[None, :], axis=1),
                            N_EXPERTS - 1).astype(jnp.int32)
    block_valid = jnp.clip(pad_start[block_exp] + cnt[block_exp] - block_start, 0,
                           EXPERT_BLOCK).astype(jnp.int32)

    xs = _dispatch(dest, hn, n_slots, tm)
    ys = _experts(block_exp, block_valid, xs, w_gate_up[0], b_gate_up[0][:, None, :],
                  w_down[0], b_down[0][:, None, :])
    out = _combine(dest, h, gates, norm_final[None], ys, tm)
    return out.reshape(bsz, seq, d)
```

```python
import functools

import jax
import jax.numpy as jnp
from jax import lax
from jax.experimental import pallas as pl
from jax.experimental.pallas import tpu as pltpu

CHUNK = 64
N_META = 16
N_HEADS = 8
HEAD_DIM = 64
V_DIM = 2 * HEAD_DIM
CONV_K = 31
N_EXPERTS = 32
TOP_K = 4
SWIGLU_ALPHA = 1.702
SWIGLU_LIMIT = 7.0
EXPERT_BLOCK = 256
EPS = 1e-5
NEG = -1e30
LOG2E = 1.4426950408889634

V7X_VMEM_LIMIT = 56 * 1024 * 1024
SUBLANES = 8
LANES = 128
META_PAD = 128
HALO = 32
ROUTE_ROWS = 8

BF16 = jnp.bfloat16
F32 = jnp.float32


def _sigmoid(x):
    return 1.0 / (1.0 + jnp.exp(-x))


def _load_token_tiles(ref, rows):
    return jnp.concatenate([ref[pl.ds(s, rows, stride=SUBLANES), :] for s in range(SUBLANES)],
                           axis=-1)


def _store_token_tiles(ref, x):
    rows = x.shape[0]
    for s in range(SUBLANES):
        ref[pl.ds(s, rows, stride=SUBLANES), :] = x[:, s * LANES:(s + 1) * LANES]


def _inproj_kernel(x_ref, g_ref, w_ref, b_ref, wvt_ref, bvt_ref,
                   q_ref, k_ref, vt_ref, u_ref, sga_ref, sgc_ref):
    d = x_ref.shape[1]
    x = x_ref[...]
    xn = x * lax.rsqrt(jnp.mean(x * x, axis=-1, keepdims=True) + EPS) * g_ref[...]
    xb = xn.astype(BF16)

    def proj(c):
        return (jnp.dot(xb, w_ref[:, c * d:(c + 1) * d], preferred_element_type=F32)
                + b_ref[:, c * d:(c + 1) * d])

    q_ref[...] = (proj(0) * (HEAD_DIM ** -0.5 * LOG2E)).astype(BF16)
    k_ref[...] = proj(1).astype(BF16)
    vt = lax.dot_general(wvt_ref[...], xb, (((1,), (1,)), ((), ())), preferred_element_type=F32)
    vt_ref[...] = (vt + bvt_ref[...]).astype(BF16)
    u_ref[...] = (proj(2) * _sigmoid(proj(3))).astype(BF16)
    sga_ref[...] = _sigmoid(proj(4)).astype(BF16)
    sgc_ref[...] = _sigmoid(proj(5)).astype(BF16)


def _inproj(x2d, g, w_bf, b, wvt_bf, bvt, tm):
    n, d = x2d.shape
    in_w = w_bf.shape[1]
    out = jax.ShapeDtypeStruct((n, d), BF16)
    row = pl.BlockSpec((tm, d), lambda i: (i, 0))
    col = pl.BlockSpec((d, tm), lambda i: (0, i))
    return pl.pallas_call(
        _inproj_kernel,
        grid=(n // tm,),
        in_specs=[row,
                  pl.BlockSpec((1, d), lambda i: (0, 0)),
                  pl.BlockSpec((d, in_w), lambda i: (0, 0)),
                  pl.BlockSpec((1, in_w), lambda i: (0, 0)),
                  pl.BlockSpec((d, d), lambda i: (0, 0)),
                  pl.BlockSpec((d, 1), lambda i: (0, 0))],
        out_specs=[row, row, col, row, row, row],
        out_shape=[out, out, jax.ShapeDtypeStruct((d, n), BF16), out, out, out],
        compiler_params=pltpu.CompilerParams(
            dimension_semantics=("arbitrary",), vmem_limit_bytes=V7X_VMEM_LIMIT),
        name="inproj",
    )(x2d, g, w_bf, b, wvt_bf, bvt)


def _attn_kernel(lam_ref, q_ref, k_ref, vt_ref, km_ref, vmt_ref, bias_ref, bmeta_ref,
                 gain_ref, o_ref, m_s, l_s, acc_s, s0, s1, mx0, mx1, *, tq, lam_init):
    h = pl.program_id(1)
    qi = pl.program_id(2)
    slope = jnp.exp2(-(h + 1).astype(F32))
    lp = lam_ref[...]
    lam = (jnp.exp(jnp.sum(lp[0:1] * lp[1:2], axis=-1, keepdims=True))
           - jnp.exp(jnp.sum(lp[2:3] * lp[3:4], axis=-1, keepdims=True)) + lam_init)

    q = q_ref[...]
    lane = lax.broadcasted_iota(jnp.int32, q.shape, 1)
    zero = jnp.zeros_like(q)
    qs = (jnp.where(lane < HEAD_DIM, q, zero), jnp.where(lane >= HEAD_DIM, q, zero))

    def scores_t(kk, qm):
        return lax.dot_general(kk, qm, (((1,), (1,)), ((), ())), preferred_element_type=F32)

    tk = tq // 2
    n_off = 2 * qi

    def tile_shift(t):
        off = (qi * tq - t * tk).astype(F32) * LOG2E
        return jnp.where(t < n_off, slope * off, 0.0)

    def stage_a(t, s_buf, mx_buf):
        kk = k_ref[pl.ds(pl.multiple_of(t * tk, tk), tk), :]
        bias = bias_ref[0, jnp.where(t < n_off, 0, t - n_off + 1)]
        for idx in range(2):
            s = scores_t(kk, qs[idx]) + bias
            s_buf[idx] = s
            mx_buf[idx] = jnp.max(s, axis=0, keepdims=True)

    def stage_b(t, s_buf, mx_buf):
        vt = vt_ref[:, pl.ds(pl.multiple_of(t * tk, tk), tk)]
        vt1 = jnp.concatenate([vt, jnp.ones((2 * SUBLANES, tk), BF16)], axis=0)
        shift = tile_shift(t)
        for idx in range(2):
            m_old = m_s[idx]
            m_new = jnp.maximum(m_old, mx_buf[idx] - shift)
            p = jnp.exp2(s_buf[idx] - (m_new + shift))
            pv1 = jnp.dot(vt1, p.astype(BF16), preferred_element_type=F32)
            pv = pv1[:V_DIM]
            ps = pv1[V_DIM:V_DIM + 1]
            alpha = jnp.exp2(m_old - m_new)
            l_s[idx] = alpha * l_s[idx] + ps
            acc_s[idx] = alpha * acc_s[idx] + pv
            m_s[idx] = m_new

    stage_a(0, s0, mx0)

    shift0 = slope * ((qi * tq).astype(F32) * LOG2E)
    km = km_ref[...]
    vmt = vmt_ref[...]
    s_meta = [scores_t(km, qs[idx]) + bmeta_ref[0] for idx in range(2)]
    for idx in range(2):
        s = s_meta[idx]
        m_new = jnp.max(s, axis=0, keepdims=True) - shift0
        p = jnp.exp2(s - (m_new + shift0))
        l_s[idx] = jnp.sum(p, axis=0, keepdims=True)
        acc_s[idx] = jnp.dot(vmt, p.astype(BF16), preferred_element_type=F32)
        m_s[idx] = m_new

    def body(i, carry):
        stage_a(2 * i + 1, s1, mx1)
        stage_b(2 * i, s0, mx0)
        stage_a(2 * i + 2, s0, mx0)
        stage_b(2 * i + 1, s1, mx1)
        return carry

    lax.fori_loop(0, qi, body, 0)
    stage_a(n_off + 1, s1, mx1)
    stage_b(n_off, s0, mx0)
    stage_b(n_off + 1, s1, mx1)

    o = acc_s[0] / l_s[0] - lam * (acc_s[1] / l_s[1])
    o = o * lax.rsqrt(jnp.mean(o * o, axis=0, keepdims=True) + EPS) * gain_ref[...]
    o_ref[...] = jnp.transpose(o * (1.0 - lam_init)).astype(BF16)


def _attention(lam_params, q, k, vt, k_meta, vt_meta, gain, bsz, seq, tq, lam_init):
    n, w = q.shape
    nq = seq // tq
    slopes = jnp.asarray([2.0 ** (-8.0 * (h + 1) / N_HEADS) for h in range(N_HEADS)], F32)
    tk = tq // 2
    r = jnp.arange(tq, dtype=jnp.int32)
    diff = r[None, :] - r[:, None]
    slopes = slopes * LOG2E
    boff = -slopes[:, None, None] * diff.astype(F32)[None, :tk]
    vis = (r[:, None] // CHUNK) <= (r[None, :] // CHUNK)
    bdiag = jnp.where(vis[None], -slopes[:, None, None] * jnp.abs(diff).astype(F32)[None], NEG)
    bias_all = jnp.stack([boff, bdiag[:, :tk], bdiag[:, tk:]], axis=1)
    mrow = jnp.arange(META_PAD, dtype=jnp.int32)
    dmeta = (N_META + r[None, :] - mrow[:, None]).astype(F32)
    bmeta = jnp.where((mrow < N_META)[None, :, None], -slopes[:, None, None] * dmeta[None], NEG)

    km = jnp.pad(k_meta, ((0, META_PAD - N_META), (0, 0)))
    vmt = jnp.pad(vt_meta, ((0, 0), (0, META_PAD - N_META)))

    kern = functools.partial(_attn_kernel, tq=tq, lam_init=lam_init)
    return pl.pallas_call(
        kern,
        grid=(bsz, N_HEADS, nq),
        in_specs=[
            pl.BlockSpec((4, HEAD_DIM), lambda b, h, i: (0, 0)),
            pl.BlockSpec((tq, V_DIM), lambda b, h, i: (b * nq + i, h)),
            pl.BlockSpec((seq, V_DIM), lambda b, h, i: (b, h)),
            pl.BlockSpec((V_DIM, seq), lambda b, h, i: (h, b)),
            pl.BlockSpec((META_PAD, V_DIM), lambda b, h, i: (0, h)),
            pl.BlockSpec((V_DIM, META_PAD), lambda b, h, i: (h, 0)),
            pl.BlockSpec((1, 3, tk, tq), lambda b, h, i: (h, 0, 0, 0)),
            pl.BlockSpec((1, META_PAD, tq), lambda b, h, i: (h, 0, 0)),
            pl.BlockSpec((V_DIM, 1), lambda b, h, i: (0, 0)),
        ],
        out_specs=pl.BlockSpec((tq, V_DIM), lambda b, h, i: (b * nq + i, h)),
        out_shape=jax.ShapeDtypeStruct((n, w), BF16),
        scratch_shapes=[pltpu.VMEM((2, 1, tq), F32), pltpu.VMEM((2, 1, tq), F32),
                        pltpu.VMEM((2, V_DIM, tq), F32),
                        pltpu.VMEM((2, tk, tq), F32), pltpu.VMEM((2, tk, tq), F32),
                        pltpu.VMEM((2, 1, tq), F32), pltpu.VMEM((2, 1, tq), F32)],
        compiler_params=pltpu.CompilerParams(
            dimension_semantics=("arbitrary", "arbitrary", "arbitrary"),
            vmem_limit_bytes=V7X_VMEM_LIMIT),
        name="diff_attention",
    )(lam_params, q, k, vt, km, vmt, bias_all, bmeta, gain)


def _mixer_kernel(x_ref, o_ref, u_ref, uh_ref, mh_ref, sga_ref, sgc_ref,
                  wap_ref, wpw_ref, wout_ref, dww_ref, dwb_ref, lng_ref, lnb_ref,
                  nffn_ref, wr_ref, br_ref, tri_ref,
                  h_ref, hn_ref, exp_ref, rank_ref, gate_ref, cnt_ref,
                  ext_s, y_s, cnt_s, *, tm, tiles_per_seq):
    i = pl.program_id(0)
    d = x_ref.shape[1]

    @pl.when(i == 0)
    def _():
        cnt_s[...] = jnp.zeros_like(cnt_s)

    first = (i % tiles_per_seq) == 0
    halo = jnp.where(first, mh_ref[...], uh_ref[...])
    ext_s[0, 0:HALO, :] = halo.astype(F32)
    ext_s[0, HALO:HALO + tm, :] = u_ref[...].astype(F32)
    span = tm + HALO - SUBLANES
    for b in range(1, SUBLANES):
        ext_s[b, 0:span, :] = ext_s[0, b:b + span, :]
    off = HALO - (CONV_K - 1)

    y_attn = jnp.dot(o_ref[...], wap_ref[...], preferred_element_type=F32)
    half = tm // 2
    hn_parts = []
    for r0 in (0, half):
        rs = slice(r0, r0 + half)
        for c in range(d // 128):
            cs = slice(c * 128, (c + 1) * 128)
            acc = jnp.zeros((half, 128), F32)
            for t in range(CONV_K):
                a, b = divmod(off + t, SUBLANES)
                row = a * SUBLANES + r0
                acc = acc + dww_ref[t:t + 1, cs] * ext_s[b, row:row + half, cs]
            y_s[rs, cs] = acc
        y = y_s[rs, :] + dwb_ref[...]
        mu = jnp.mean(y, axis=-1, keepdims=True)
        yc = y - mu
        var = jnp.mean(yc * yc, axis=-1, keepdims=True)
        y = yc * lax.rsqrt(var + EPS) * lng_ref[...] + lnb_ref[...]
        y = y * _sigmoid(y)
        y_conv = jnp.dot(y.astype(BF16), wpw_ref[...], preferred_element_type=F32)

        mixed = sga_ref[rs, :].astype(F32) * y_attn[rs] + sgc_ref[rs, :].astype(F32) * y_conv
        h = x_ref[rs, :] + jnp.dot(mixed.astype(BF16), wout_ref[...], preferred_element_type=F32)
        h_ref[rs, :] = h
        hn_parts.append(h * lax.rsqrt(jnp.mean(h * h, axis=-1, keepdims=True) + EPS) * nffn_ref[...])
    hn = jnp.concatenate(hn_parts, axis=0)
    _store_token_tiles(hn_ref, hn)

    nt_dims = (((1,), (1,)), ((), ()))
    hn_hi = hn.astype(BF16)
    hn_lo = (hn - hn_hi.astype(F32)).astype(BF16)
    both = lax.dot_general(wr_ref[...], hn_hi, nt_dims, preferred_element_type=F32)
    logits = (both[:N_EXPERTS] + both[N_EXPERTS:]
              + lax.dot_general(wr_ref[0:N_EXPERTS, :], hn_lo, nt_dims, preferred_element_type=F32)
              + br_ref[...])
    eio = lax.broadcasted_iota(jnp.int32, logits.shape, 0)
    sels, tops, idxs = [], [], []
    cur = logits
    for _ in range(TOP_K):
        m = jnp.max(cur, axis=0, keepdims=True)
        idx = jnp.min(jnp.where(cur == m, eio, N_EXPERTS), axis=0, keepdims=True)
        sel = eio == idx
        sels.append(sel)
        tops.append(m)
        idxs.append(idx)
        cur = jnp.where(sel, -jnp.inf, cur)
    exps = [jnp.exp(t - tops[0]) for t in tops]
    denom = exps[0] + exps[1] + exps[2] + exps[3]
    hot = jnp.zeros(logits.shape, F32)
    for sel in sels:
        hot = hot + jnp.where(sel, 1.0, 0.0)
    before = jnp.dot(hot.astype(BF16), tri_ref[...], preferred_element_type=F32) + cnt_s[...]
    ranks = [jnp.sum(jnp.where(sel, before, 0.0), axis=0, keepdims=True).astype(jnp.int32)
             for sel in sels]
    zi = jnp.zeros((ROUTE_ROWS - TOP_K, tm), jnp.int32)
    exp_ref[0] = jnp.concatenate(idxs + [zi], axis=0)
    rank_ref[0] = jnp.concatenate(ranks + [zi], axis=0)
    gate_ref[0] = jnp.concatenate([e / denom for e in exps] + [zi.astype(F32)], axis=0)
    cnt_s[...] = cnt_s[...] + jnp.sum(hot, axis=1, keepdims=True)
    cnt_ref[...] = jnp.broadcast_to(cnt_s[...], cnt_ref.shape)


def _mixer(x2d, o, u, sga, sgc, meta_halo, wap, wpw, wout, dww, dwb, lng, lnb, nffn, wr_t, br,
           seq, tm):
    n, d = x2d.shape
    nt = n // tm
    hb = tm // HALO
    tri = (jnp.arange(tm)[:, None] < jnp.arange(tm)[None, :]).astype(BF16)
    row = pl.BlockSpec((tm, d), lambda i: (i, 0))
    route = pl.BlockSpec((1, ROUTE_ROWS, tm), lambda i: (i, 0, 0))

    def full(shp):
        return pl.BlockSpec(shp, lambda i: tuple(0 for _ in shp))

    kern = functools.partial(_mixer_kernel, tm=tm, tiles_per_seq=seq // tm)
    return pl.pallas_call(
        kern,
        grid=(nt,),
        in_specs=[row, row, row,
                  pl.BlockSpec((HALO, d), lambda i: (jnp.maximum(i * hb - 1, 0), 0)),
                  full((HALO, d)), row, row,
                  full((d, d)), full((d, d)), full((d, d)),
                  full((CONV_K, d)), full((1, d)), full((1, d)), full((1, d)), full((1, d)),
                  full((2 * N_EXPERTS, d)), full((N_EXPERTS, 1)), full((tm, tm))],
        out_specs=[row, pl.BlockSpec((tm * SUBLANES, LANES), lambda i: (i, 0)),
                   route, route, route, full((N_EXPERTS, 128))],
        out_shape=[jax.ShapeDtypeStruct((n, d), F32),
                   jax.ShapeDtypeStruct((n * SUBLANES, LANES), F32),
                   jax.ShapeDtypeStruct((nt, ROUTE_ROWS, tm), jnp.int32),
                   jax.ShapeDtypeStruct((nt, ROUTE_ROWS, tm), jnp.int32),
                   jax.ShapeDtypeStruct((nt, ROUTE_ROWS, tm), F32),
                   jax.ShapeDtypeStruct((N_EXPERTS, 128), F32)],
        scratch_shapes=[pltpu.VMEM((SUBLANES, HALO + tm, d), F32), pltpu.VMEM((tm, d), F32),
                        pltpu.VMEM((N_EXPERTS, 1), F32)],
        compiler_params=pltpu.CompilerParams(
            dimension_semantics=("arbitrary",), vmem_limit_bytes=V7X_VMEM_LIMIT),
        name="mixer_router",
    )(x2d, o, u, u, meta_halo, sga, sgc, wap, wpw, wout, dww, dwb, lng, lnb, nffn, wr_t, br, tri)


def _prefetch_routes(dest_hbm, dsm_s, sem_idx):
    i = pl.program_id(0)
    slot = i % 2

    def idx_copy(tile, s):
        return pltpu.make_async_copy(dest_hbm.at[tile], dsm_s.at[s], sem_idx.at[s])

    @pl.when(i == 0)
    def _():
        idx_copy(0, 0).start()

    idx_copy(i, slot).wait()

    @pl.when(i + 1 < pl.num_programs(0))
    def _():
        idx_copy(i + 1, 1 - slot).start()

    return slot


def _dispatch_kernel(dest_hbm, hn_ref, xs_hbm, dsm_s, sem_idx, sem_rows, *, tm):
    slot = _prefetch_routes(dest_hbm, dsm_s, sem_idx)

    def issue(r, c):
        src = hn_ref.at[pl.ds(pl.multiple_of(r * SUBLANES, SUBLANES), SUBLANES), :]
        for kk in range(TOP_K):
            dst = pl.multiple_of(dsm_s[slot, kk, r] * SUBLANES, SUBLANES)
            pltpu.make_async_copy(src, xs_hbm.at[pl.ds(dst, SUBLANES), :],
                                  sem_rows).start(priority=kk % 2)
        return c
    lax.fori_loop(0, tm, issue, 0, unroll=8)

    for kk in range(TOP_K):
        pltpu.make_async_copy(hn_ref, xs_hbm.at[pl.ds(0, tm * SUBLANES), :], sem_rows).wait()


def _dispatch(dest, hn, n_slots, tm):
    n = hn.shape[0] // SUBLANES
    return pl.pallas_call(
        functools.partial(_dispatch_kernel, tm=tm),
        grid=(n // tm,),
        in_specs=[pl.BlockSpec(memory_space=pl.ANY),
                  pl.BlockSpec((tm * SUBLANES, LANES), lambda i: (i, 0))],
        out_specs=pl.BlockSpec(memory_space=pl.ANY),
        out_shape=jax.ShapeDtypeStruct((n_slots * SUBLANES, LANES), F32),
        scratch_shapes=[pltpu.SMEM((2, ROUTE_ROWS, tm), jnp.int32),
                        pltpu.SemaphoreType.DMA((2,)), pltpu.SemaphoreType.DMA(())],
        compiler_params=pltpu.CompilerParams(
            dimension_semantics=("arbitrary",), vmem_limit_bytes=V7X_VMEM_LIMIT),
        name="dispatch",
    )(dest, hn)


def _expert_kernel(bexp_ref, bvalid_ref, xs_ref, wgu_ref, bgu_ref, wdn_ref, bdn_ref, ys_ref,
                   wgu_bf, wdn_bf):
    i = pl.program_id(0)
    valid = bvalid_ref[i]
    f = wdn_ref.shape[1]

    @pl.when((i == 0) | (bexp_ref[i] != bexp_ref[jnp.maximum(i - 1, 0)]))
    def _():
        wgu_bf[...] = wgu_ref[0].astype(BF16)
        wdn_bf[...] = wdn_ref[0].astype(BF16)

    @pl.when(valid > 0)
    def _():
        x = _load_token_tiles(xs_ref, EXPERT_BLOCK)
        rows = lax.broadcasted_iota(jnp.int32, x.shape, 0)
        xb = jnp.where(rows < valid, x, 0.0).astype(BF16)
        gu = jnp.dot(xb, wgu_bf[...], preferred_element_type=F32) + bgu_ref[0]
        glu = jnp.minimum(gu[:, :f], SWIGLU_LIMIT)
        lin = jnp.clip(gu[:, f:], -SWIGLU_LIMIT, SWIGLU_LIMIT)
        act = glu * _sigmoid(SWIGLU_ALPHA * glu) * (lin + 1.0)
        y = jnp.dot(act.astype(BF16), wdn_bf[...], preferred_element_type=F32) + bdn_ref[0]
        _store_token_tiles(ys_ref, y)

    @pl.when(valid <= 0)
    def _():
        ys_ref[...] = jnp.zeros_like(ys_ref)


def _experts(block_exp, block_valid, xs, wgu, bgu, wdn, bdn):
    d = wgu.shape[1]
    n_slots = xs.shape[0] // SUBLANES
    nb = n_slots // EXPERT_BLOCK
    f2 = wgu.shape[2]
    f = wdn.shape[1]
    tile_rows = EXPERT_BLOCK * SUBLANES
    grid_spec = pltpu.PrefetchScalarGridSpec(
        num_scalar_prefetch=2,
        grid=(nb,),
        in_specs=[pl.BlockSpec((tile_rows, LANES), lambda i, be, bv: (i, 0)),
                  pl.BlockSpec((1, d, f2), lambda i, be, bv: (be[i], 0, 0)),
                  pl.BlockSpec((1, 1, f2), lambda i, be, bv: (be[i], 0, 0)),
                  pl.BlockSpec((1, f, d), lambda i, be, bv: (be[i], 0, 0)),
                  pl.BlockSpec((1, 1, d), lambda i, be, bv: (be[i], 0, 0))],
        out_specs=pl.BlockSpec((tile_rows, LANES), lambda i, be, bv: (i, 0)),
        scratch_shapes=[pltpu.VMEM((d, f2), BF16), pltpu.VMEM((f, d), BF16)],
    )
    return pl.pallas_call(
        _expert_kernel,
        grid_spec=grid_spec,
        out_shape=jax.ShapeDtypeStruct(xs.shape, F32),
        compiler_params=pltpu.CompilerParams(
            dimension_semantics=("arbitrary",), vmem_limit_bytes=V7X_VMEM_LIMIT),
        name="experts",
    )(block_exp, block_valid, xs, wgu, bgu, wdn, bdn)


def _combine_kernel(dest_hbm, h_ref, gate_ref, g_ref, ys_hbm, out_ref, dsm_s, ybuf, sem_idx, sem_rows,
                    *, tm):
    i = pl.program_id(0)
    nsteps = pl.num_programs(0)
    slot = i % 2

    def idx_copy(tile, s):
        return pltpu.make_async_copy(dest_hbm.at[tile], dsm_s.at[s], sem_idx.at[s])

    def issue_rows(s):
        def issue(r, c):
            for kk in range(TOP_K):
                src = pl.multiple_of(dsm_s[s, kk, r] * SUBLANES, SUBLANES)
                dst = pl.multiple_of(r * SUBLANES, SUBLANES)
                pltpu.make_async_copy(ys_hbm.at[pl.ds(src, SUBLANES), :],
                                      ybuf.at[s, kk, pl.ds(dst, SUBLANES), :],
                                      sem_rows.at[s]).start(priority=kk % 2)
            return c
        lax.fori_loop(0, tm, issue, 0, unroll=8)

    @pl.when(i == 0)
    def _():
        idx_copy(0, 0).start()
        idx_copy(0, 0).wait()
        issue_rows(0)

        @pl.when(nsteps > 1)
        def _():
            idx_copy(1, 1).start()

    @pl.when(i + 1 < nsteps)
    def _():
        idx_copy(i + 1, 1 - slot).wait()
        issue_rows(1 - slot)

        @pl.when(i + 2 < nsteps)
        def _():
            idx_copy(i + 2, slot).start()

    for kk in range(TOP_K):
        pltpu.make_async_copy(ys_hbm.at[pl.ds(0, tm * SUBLANES), :], ybuf.at[slot, kk],
                              sem_rows.at[slot]).wait()

    gates = jnp.transpose(gate_ref[0])
    acc = h_ref[...]
    for kk in range(TOP_K):
        acc = acc + gates[:, kk:kk + 1] * _load_token_tiles(ybuf.at[slot, kk], tm)
    out_ref[...] = acc * lax.rsqrt(jnp.mean(acc * acc, axis=-1, keepdims=True) + EPS) * g_ref[...]


def _combine(dest, h, gates, g_final, ys, tm):
    n, d = h.shape
    return pl.pallas_call(
        functools.partial(_combine_kernel, tm=tm),
        grid=(n // tm,),
        in_specs=[pl.BlockSpec(memory_space=pl.ANY),
                  pl.BlockSpec((tm, d), lambda i: (i, 0)),
                  pl.BlockSpec((1, ROUTE_ROWS, tm), lambda i: (i, 0, 0)),
                  pl.BlockSpec((1, d), lambda i: (0, 0)),
                  pl.BlockSpec(memory_space=pl.ANY)],
        out_specs=pl.BlockSpec((tm, d), lambda i: (i, 0)),
        out_shape=jax.ShapeDtypeStruct((n, d), F32),
        scratch_shapes=[pltpu.SMEM((2, ROUTE_ROWS, tm), jnp.int32),
                        pltpu.VMEM((2, TOP_K, tm * SUBLANES, LANES), F32),
                        pltpu.SemaphoreType.DMA((2,)), pltpu.SemaphoreType.DMA((2,))],
        compiler_params=pltpu.CompilerParams(
            dimension_semantics=("arbitrary",), vmem_limit_bytes=V7X_VMEM_LIMIT),
        name="combine",
    )(dest, h, gates, g_final, ys)


def kernel(x, meta_tokens, norm_mix, w_in, b_in, lam_params, subln_gain, w_attn_proj, conv_dw_w,
           conv_dw_b, conv_ln_g, conv_ln_b, w_conv_proj, w_out, norm_ffn, w_router, b_router,
           w_gate_up, b_gate_up, w_down, b_down, norm_final):
    bsz, seq, d = x.shape
    assert norm_mix.shape[0] == 1, "single-layer block"
    n = bsz * seq
    tm_proj = min(512, seq)
    tq = min(512, seq)
    tm = min(256, seq)
    lam_init = 0.8 - 0.6 * 1.0

    x2d = x.reshape(n, d)
    w_rest = jnp.concatenate([w_in[0][:, :2 * d], w_in[0][:, 3 * d:]], axis=1).astype(BF16)
    b_rest = jnp.concatenate([b_in[0][:2 * d], b_in[0][3 * d:]])[None]
    w_vt = w_in[0][:, 2 * d:3 * d].T.astype(BF16)
    b_vt = b_in[0][2 * d:3 * d][:, None]
    g_mix = norm_mix[0][None]
    q, k, vt, u, sga, sgc = _inproj(x2d, g_mix, w_rest, b_rest, w_vt, b_vt, tm_proj)
    _, k_m, vt_m, u_m, _, _ = _inproj(meta_tokens.astype(x.dtype), g_mix, w_rest, b_rest, w_vt, b_vt,
                                      N_META)

    o = _attention(lam_params[0], q, k, vt, k_m, vt_m, subln_gain[0][:, None], bsz, seq, tq,
                   lam_init)

    meta_halo = jnp.concatenate([jnp.zeros((HALO - N_META, d), BF16), u_m], axis=0)
    wr_t = w_router[0].T
    wr_hi = wr_t.astype(BF16)
    wr_split = jnp.concatenate([wr_hi, (wr_t - wr_hi.astype(F32)).astype(BF16)], axis=0)
    h, hn, eidx, rank, gates, counts = _mixer(
        x2d, o, u, sga, sgc, meta_halo,
        w_attn_proj[0].astype(BF16), w_conv_proj[0].astype(BF16), w_out[0].astype(BF16),
        conv_dw_w[0], conv_dw_b[0][None], conv_ln_g[0][None], conv_ln_b[0][None],
        norm_ffn[0][None], wr_split, b_router[0][:, None], seq, tm)

    cnt = counts[:, 0].astype(jnp.int32)
    padded = (cnt + EXPERT_BLOCK - 1) // EXPERT_BLOCK * EXPERT_BLOCK
    pad_end = jnp.cumsum(padded)
    pad_start = pad_end - padded
    n_blocks = -(-(n * TOP_K + N_EXPERTS * (EXPERT_BLOCK - 1)) // EXPERT_BLOCK)
    n_slots = n_blocks * EXPERT_BLOCK
    dest = pad_start[eidx] + rank
    block_start = jnp.arange(n_blocks, dtype=jnp.int32) * EXPERT_BLOCK
    block_exp = jnp.minimum(jnp.sum(block_start[:, None] >= pad_end[None, :], axis=1),
                            N_EXPERTS - 1).astype(jnp.int32)
    block_valid = jnp.clip(pad_start[block_exp] + cnt[block_exp] - block_start, 0,
                           EXPERT_BLOCK).astype(jnp.int32)

    xs = _dispatch(dest, hn, n_slots, tm)
    ys = _experts(block_exp, block_valid, xs, w_gate_up[0], b_gate_up[0][:, None, :],
                  w_down[0], b_down[0][:, None, :])
    out = _combine(dest, h, gates, norm_final[None], ys, tm)
    return out.reshape(bsz, seq, d)
```

```python
import functools

import jax
import jax.numpy as jnp
from jax import lax
from jax.experimental import pallas as pl
from jax.experimental.pallas import tpu as pltpu

CHUNK = 64
N_META = 16
N_HEADS = 8
HEAD_DIM = 64
V_DIM = 2 * HEAD_DIM
CONV_K = 31
N_EXPERTS = 32
TOP_K = 4
SWIGLU_ALPHA = 1.702
SWIGLU_LIMIT = 7.0
EXPERT_BLOCK = 256
EPS = 1e-5
NEG = -1e30
LOG2E = 1.4426950408889634

V7X_VMEM_LIMIT = 56 * 1024 * 1024
SUBLANES = 8
LANES = 128
META_PAD = 128
HALO = 32
ROUTE_ROWS = 8

BF16 = jnp.bfloat16
F32 = jnp.float32


def _sigmoid(x):
    return 1.0 / (1.0 + jnp.exp(-x))


def _load_token_tiles(ref, rows):
    return jnp.concatenate([ref[pl.ds(s, rows, stride=SUBLANES), :] for s in range(SUBLANES)],
                           axis=-1)


def _store_token_tiles(ref, x):
    rows = x.shape[0]
    for s in range(SUBLANES):
        ref[pl.ds(s, rows, stride=SUBLANES), :] = x[:, s * LANES:(s + 1) * LANES]


def _inproj_kernel(x_ref, g_ref, w_ref, b_ref, wvt_ref, bvt_ref,
                   q_ref, k_ref, vt_ref, u_ref, sga_ref, sgc_ref):
    d = x_ref.shape[1]
    x = x_ref[...]
    xn = x * lax.rsqrt(jnp.mean(x * x, axis=-1, keepdims=True) + EPS) * g_ref[...]
    xb = xn.astype(BF16)

    def proj(c):
        return (jnp.dot(xb, w_ref[:, c * d:(c + 1) * d], preferred_element_type=F32)
                + b_ref[:, c * d:(c + 1) * d])

    qf = (proj(0) * (HEAD_DIM ** -0.5 * LOG2E)).astype(BF16)
    kf = proj(1).astype(BF16)
    for hd in range(N_HEADS):
        q_ref[hd] = qf[:, hd * V_DIM:(hd + 1) * V_DIM]
        k_ref[hd] = kf[:, hd * V_DIM:(hd + 1) * V_DIM]
    vt = lax.dot_general(wvt_ref[...], xb, (((1,), (1,)), ((), ())), preferred_element_type=F32)
    vt_ref[...] = (vt + bvt_ref[...]).astype(BF16)
    u_ref[...] = (proj(2) * _sigmoid(proj(3))).astype(BF16)
    sga_ref[...] = _sigmoid(proj(4)).astype(BF16)
    sgc_ref[...] = _sigmoid(proj(5)).astype(BF16)


def _inproj(x2d, g, w_bf, b, wvt_bf, bvt, tm):
    n, d = x2d.shape
    in_w = w_bf.shape[1]
    out = jax.ShapeDtypeStruct((n, d), BF16)
    row = pl.BlockSpec((tm, d), lambda i: (i, 0))
    col = pl.BlockSpec((d, tm), lambda i: (0, i))
    heads = pl.BlockSpec((N_HEADS, tm, V_DIM), lambda i: (0, i, 0))
    out_heads = jax.ShapeDtypeStruct((N_HEADS, n, V_DIM), BF16)
    return pl.pallas_call(
        _inproj_kernel,
        grid=(n // tm,),
        in_specs=[row,
                  pl.BlockSpec((1, d), lambda i: (0, 0)),
                  pl.BlockSpec((d, in_w), lambda i: (0, 0)),
                  pl.BlockSpec((1, in_w), lambda i: (0, 0)),
                  pl.BlockSpec((d, d), lambda i: (0, 0)),
                  pl.BlockSpec((d, 1), lambda i: (0, 0))],
        out_specs=[heads, heads, col, row, row, row],
        out_shape=[out_heads, out_heads, jax.ShapeDtypeStruct((d, n), BF16), out, out, out],
        compiler_params=pltpu.CompilerParams(
            dimension_semantics=("arbitrary",), vmem_limit_bytes=V7X_VMEM_LIMIT),
        name="inproj",
    )(x2d, g, w_bf, b, wvt_bf, bvt)


def _attn_kernel(lam_ref, q_ref, k_ref, vt_ref, km_ref, vmt_ref, bias_ref, bmeta_ref,
                 gain_ref, o_ref, m_s, l_s, acc_s, s0, s1, mx0, mx1, *, tq, lam_init):
    h = pl.program_id(1)
    qi = pl.program_id(2)
    slope = jnp.exp2(-(h + 1).astype(F32))
    lp = lam_ref[...]
    lam = (jnp.exp(jnp.sum(lp[0:1] * lp[1:2], axis=-1, keepdims=True))
           - jnp.exp(jnp.sum(lp[2:3] * lp[3:4], axis=-1, keepdims=True)) + lam_init)

    q = q_ref[0]
    lane = lax.broadcasted_iota(jnp.int32, q.shape, 1)
    zero = jnp.zeros_like(q)
    qs = (jnp.where(lane < HEAD_DIM, q, zero), jnp.where(lane >= HEAD_DIM, q, zero))

    def scores_t(kk, qm):
        return lax.dot_general(kk, qm, (((1,), (1,)), ((), ())), preferred_element_type=F32)

    tk = tq // 2
    n_off = 2 * qi

    def tile_shift(t):
        off = (qi * tq - t * tk).astype(F32) * LOG2E
        return jnp.where(t < n_off, slope * off, 0.0)

    def stage_a(t, s_buf, mx_buf):
        kk = k_ref[0, pl.ds(pl.multiple_of(t * tk, tk), tk), :]
        bias = bias_ref[0, jnp.where(t < n_off, 0, t - n_off + 1)]
        for idx in range(2):
            s = scores_t(kk, qs[idx]) + bias
            s_buf[idx] = s
            mx_buf[idx] = jnp.max(s, axis=0, keepdims=True)

    def stage_b(t, s_buf, mx_buf):
        vt = vt_ref[:, pl.ds(pl.multiple_of(t * tk, tk), tk)]
        vt1 = jnp.concatenate([vt, jnp.ones((2 * SUBLANES, tk), BF16)], axis=0)
        shift = tile_shift(t)
        for idx in range(2):
            m_old = m_s[idx]
            m_new = jnp.maximum(m_old, mx_buf[idx] - shift)
            p = jnp.exp2(s_buf[idx] - (m_new + shift))
            pv1 = jnp.dot(vt1, p.astype(BF16), preferred_element_type=F32)
            pv = pv1[:V_DIM]
            ps = pv1[V_DIM:V_DIM + 1]
            alpha = jnp.exp2(m_old - m_new)
            l_s[idx] = alpha * l_s[idx] + ps
            acc_s[idx] = alpha * acc_s[idx] + pv
            m_s[idx] = m_new

    stage_a(0, s0, mx0)

    shift0 = slope * ((qi * tq).astype(F32) * LOG2E)
    km = km_ref[0]
    vmt = vmt_ref[...]
    s_meta = [scores_t(km, qs[idx]) + bmeta_ref[0] for idx in range(2)]
    for idx in range(2):
        s = s_meta[idx]
        m_new = jnp.max(s, axis=0, keepdims=True) - shift0
        p = jnp.exp2(s - (m_new + shift0))
        l_s[idx] = jnp.sum(p, axis=0, keepdims=True)
        acc_s[idx] = jnp.dot(vmt, p.astype(BF16), preferred_element_type=F32)
        m_s[idx] = m_new

    def body(i, carry):
        stage_a(2 * i + 1, s1, mx1)
        stage_b(2 * i, s0, mx0)
        stage_a(2 * i + 2, s0, mx0)
        stage_b(2 * i + 1, s1, mx1)
        return carry

    lax.fori_loop(0, qi, body, 0)
    stage_a(n_off + 1, s1, mx1)
    stage_b(n_off, s0, mx0)
    stage_b(n_off + 1, s1, mx1)

    o = acc_s[0] / l_s[0] - lam * (acc_s[1] / l_s[1])
    o = o * lax.rsqrt(jnp.mean(o * o, axis=0, keepdims=True) + EPS) * gain_ref[...]
    o_ref[0] = jnp.transpose(o * (1.0 - lam_init)).astype(BF16)


def _attention(lam_params, q, k, vt, k_meta, vt_meta, gain, bsz, seq, tq, lam_init):
    n = q.shape[1]
    nq = seq // tq
    slopes = jnp.asarray([2.0 ** (-8.0 * (h + 1) / N_HEADS) for h in range(N_HEADS)], F32)
    tk = tq // 2
    r = jnp.arange(tq, dtype=jnp.int32)
    diff = r[None, :] - r[:, None]
    slopes = slopes * LOG2E
    boff = -slopes[:, None, None] * diff.astype(F32)[None, :tk]
    vis = (r[:, None] // CHUNK) <= (r[None, :] // CHUNK)
    bdiag = jnp.where(vis[None], -slopes[:, None, None] * jnp.abs(diff).astype(F32)[None], NEG)
    bias_all = jnp.stack([boff, bdiag[:, :tk], bdiag[:, tk:]], axis=1)
    mrow = jnp.arange(META_PAD, dtype=jnp.int32)
    dmeta = (N_META + r[None, :] - mrow[:, None]).astype(F32)
    bmeta = jnp.where((mrow < N_META)[None, :, None], -slopes[:, None, None] * dmeta[None], NEG)

    km = jnp.pad(k_meta, ((0, 0), (0, META_PAD - N_META), (0, 0)))
    vmt = jnp.pad(vt_meta, ((0, 0), (0, META_PAD - N_META)))

    kern = functools.partial(_attn_kernel, tq=tq, lam_init=lam_init)
    return pl.pallas_call(
        kern,
        grid=(bsz, N_HEADS, nq),
        in_specs=[
            pl.BlockSpec((4, HEAD_DIM), lambda b, h, i: (0, 0)),
            pl.BlockSpec((1, tq, V_DIM), lambda b, h, i: (h, b * nq + i, 0)),
            pl.BlockSpec((1, seq, V_DIM), lambda b, h, i: (h, b, 0)),
            pl.BlockSpec((V_DIM, seq), lambda b, h, i: (h, b)),
            pl.BlockSpec((1, META_PAD, V_DIM), lambda b, h, i: (h, 0, 0)),
            pl.BlockSpec((V_DIM, META_PAD), lambda b, h, i: (h, 0)),
            pl.BlockSpec((1, 3, tk, tq), lambda b, h, i: (h, 0, 0, 0)),
            pl.BlockSpec((1, META_PAD, tq), lambda b, h, i: (h, 0, 0)),
            pl.BlockSpec((V_DIM, 1), lambda b, h, i: (0, 0)),
        ],
        out_specs=pl.BlockSpec((1, tq, V_DIM), lambda b, h, i: (h, b * nq + i, 0)),
        out_shape=jax.ShapeDtypeStruct((N_HEADS, n, V_DIM), BF16),
        scratch_shapes=[pltpu.VMEM((2, 1, tq), F32), pltpu.VMEM((2, 1, tq), F32),
                        pltpu.VMEM((2, V_DIM, tq), F32),
                        pltpu.VMEM((2, tk, tq), F32), pltpu.VMEM((2, tk, tq), F32),
                        pltpu.VMEM((2, 1, tq), F32), pltpu.VMEM((2, 1, tq), F32)],
        compiler_params=pltpu.CompilerParams(
            dimension_semantics=("arbitrary", "arbitrary", "arbitrary"),
            vmem_limit_bytes=V7X_VMEM_LIMIT),
        name="diff_attention",
    )(lam_params, q, k, vt, km, vmt, bias_all, bmeta, gain)


def _mixer_kernel(x_ref, o_ref, u_ref, uh_ref, mh_ref, sga_ref, sgc_ref,
                  wap_ref, wpw_ref, wout_ref, dww_ref, dwb_ref, lng_ref, lnb_ref,
                  nffn_ref, wr_ref, br_ref, tri_ref,
                  h_ref, hn_ref, exp_ref, rank_ref, gate_ref, cnt_ref,
                  ext_s, y_s, cnt_s, *, tm, tiles_per_seq):
    i = pl.program_id(0)
    d = x_ref.shape[1]

    @pl.when(i == 0)
    def _():
        cnt_s[...] = jnp.zeros_like(cnt_s)

    first = (i % tiles_per_seq) == 0
    halo = jnp.where(first, mh_ref[...], uh_ref[...])
    ext_s[0, 0:HALO, :] = halo.astype(F32)
    ext_s[0, HALO:HALO + tm, :] = u_ref[...].astype(F32)
    span = tm + HALO - SUBLANES
    for b in range(1, SUBLANES):
        ext_s[b, 0:span, :] = ext_s[0, b:b + span, :]
    off = HALO - (CONV_K - 1)

    o = jnp.concatenate([o_ref[hd] for hd in range(N_HEADS)], axis=-1)
    y_attn = jnp.dot(o, wap_ref[...], preferred_element_type=F32)
    half = tm // 2
    hn_parts = []
    for r0 in (0, half):
        rs = slice(r0, r0 + half)
        for c in range(d // 128):
            cs = slice(c * 128, (c + 1) * 128)
            acc = jnp.zeros((half, 128), F32)
            for t in range(CONV_K):
                a, b = divmod(off + t, SUBLANES)
                row = a * SUBLANES + r0
                acc = acc + dww_ref[t:t + 1, cs] * ext_s[b, row:row + half, cs]
            y_s[rs, cs] = acc
        y = y_s[rs, :] + dwb_ref[...]
        mu = jnp.mean(y, axis=-1, keepdims=True)
        yc = y - mu
        var = jnp.mean(yc * yc, axis=-1, keepdims=True)
        y = yc * lax.rsqrt(var + EPS) * lng_ref[...] + lnb_ref[...]
        y = y * _sigmoid(y)
        y_conv = jnp.dot(y.astype(BF16), wpw_ref[...], preferred_element_type=F32)

        mixed = sga_ref[rs, :].astype(F32) * y_attn[rs] + sgc_ref[rs, :].astype(F32) * y_conv
        h = x_ref[rs, :] + jnp.dot(mixed.astype(BF16), wout_ref[...], preferred_element_type=F32)
        h_ref[rs, :] = h
        hn_parts.append(h * lax.rsqrt(jnp.mean(h * h, axis=-1, keepdims=True) + EPS) * nffn_ref[...])
    hn = jnp.concatenate(hn_parts, axis=0)
    _store_token_tiles(hn_ref, hn)

    nt_dims = (((1,), (1,)), ((), ()))
    hn_hi = hn.astype(BF16)
    hn_lo = (hn - hn_hi.astype(F32)).astype(BF16)
    both = lax.dot_general(wr_ref[...], hn_hi, nt_dims, preferred_element_type=F32)
    logits = (both[:N_EXPERTS] + both[N_EXPERTS:]
              + lax.dot_general(wr_ref[0:N_EXPERTS, :], hn_lo, nt_dims, preferred_element_type=F32)
              + br_ref[...])
    eio = lax.broadcasted_iota(jnp.int32, logits.shape, 0)
    sels, tops, idxs = [], [], []
    cur = logits
    for _ in range(TOP_K):
        m = jnp.max(cur, axis=0, keepdims=True)
        idx = jnp.min(jnp.where(cur == m, eio, N_EXPERTS), axis=0, keepdims=True)
        sel = eio == idx
        sels.append(sel)
        tops.append(m)
        idxs.append(idx)
        cur = jnp.where(sel, -jnp.inf, cur)
    exps = [jnp.exp(t - tops[0]) for t in tops]
    denom = exps[0] + exps[1] + exps[2] + exps[3]
    hot = jnp.zeros(logits.shape, F32)
    for sel in sels:
        hot = hot + jnp.where(sel, 1.0, 0.0)
    before = jnp.dot(hot.astype(BF16), tri_ref[...], preferred_element_type=F32) + cnt_s[...]
    ranks = [jnp.sum(jnp.where(sel, before, 0.0), axis=0, keepdims=True).astype(jnp.int32)
             for sel in sels]
    zi = jnp.zeros((ROUTE_ROWS - TOP_K, tm), jnp.int32)
    exp_ref[0] = jnp.concatenate(idxs + [zi], axis=0)
    rank_ref[0] = jnp.concatenate(ranks + [zi], axis=0)
    gate_ref[0] = jnp.concatenate([e / denom for e in exps] + [zi.astype(F32)], axis=0)
    cnt_s[...] = cnt_s[...] + jnp.sum(hot, axis=1, keepdims=True)
    cnt_ref[...] = jnp.broadcast_to(cnt_s[...], cnt_ref.shape)


def _mixer(x2d, o, u, sga, sgc, meta_halo, wap, wpw, wout, dww, dwb, lng, lnb, nffn, wr_t, br,
           seq, tm):
    n, d = x2d.shape
    nt = n // tm
    hb = tm // HALO
    tri = (jnp.arange(tm)[:, None] < jnp.arange(tm)[None, :]).astype(BF16)
    row = pl.BlockSpec((tm, d), lambda i: (i, 0))
    route = pl.BlockSpec((1, ROUTE_ROWS, tm), lambda i: (i, 0, 0))

    def full(shp):
        return pl.BlockSpec(shp, lambda i: tuple(0 for _ in shp))

    kern = functools.partial(_mixer_kernel, tm=tm, tiles_per_seq=seq // tm)
    return pl.pallas_call(
        kern,
        grid=(nt,),
        in_specs=[row, pl.BlockSpec((N_HEADS, tm, V_DIM), lambda i: (0, i, 0)), row,
                  pl.BlockSpec((HALO, d), lambda i: (jnp.maximum(i * hb - 1, 0), 0)),
                  full((HALO, d)), row, row,
                  full((d, d)), full((d, d)), full((d, d)),
                  full((CONV_K, d)), full((1, d)), full((1, d)), full((1, d)), full((1, d)),
                  full((2 * N_EXPERTS, d)), full((N_EXPERTS, 1)), full((tm, tm))],
        out_specs=[row, pl.BlockSpec((tm * SUBLANES, LANES), lambda i: (i, 0)),
                   route, route, route, full((N_EXPERTS, 128))],
        out_shape=[jax.ShapeDtypeStruct((n, d), F32),
                   jax.ShapeDtypeStruct((n * SUBLANES, LANES), F32),
                   jax.ShapeDtypeStruct((nt, ROUTE_ROWS, tm), jnp.int32),
                   jax.ShapeDtypeStruct((nt, ROUTE_ROWS, tm), jnp.int32),
                   jax.ShapeDtypeStruct((nt, ROUTE_ROWS, tm), F32),
                   jax.ShapeDtypeStruct((N_EXPERTS, 128), F32)],
        scratch_shapes=[pltpu.VMEM((SUBLANES, HALO + tm, d), F32), pltpu.VMEM((tm, d), F32),
                        pltpu.VMEM((N_EXPERTS, 1), F32)],
        compiler_params=pltpu.CompilerParams(
            dimension_semantics=("arbitrary",), vmem_limit_bytes=V7X_VMEM_LIMIT),
        name="mixer_router",
    )(x2d, o, u, u, meta_halo, sga, sgc, wap, wpw, wout, dww, dwb, lng, lnb, nffn, wr_t, br, tri)


def _prefetch_routes(dest_hbm, dsm_s, sem_idx):
    i = pl.program_id(0)
    slot = i % 2

    def idx_copy(tile, s):
        return pltpu.make_async_copy(dest_hbm.at[tile], dsm_s.at[s], sem_idx.at[s])

    @pl.when(i == 0)
    def _():
        idx_copy(0, 0).start()

    idx_copy(i, slot).wait()

    @pl.when(i + 1 < pl.num_programs(0))
    def _():
        idx_copy(i + 1, 1 - slot).start()

    return slot


def _dispatch_kernel(dest_hbm, hn_ref, xs_hbm, dsm_s, sem_idx, sem_rows, *, tm):
    slot = _prefetch_routes(dest_hbm, dsm_s, sem_idx)

    def issue(r, c):
        src = hn_ref.at[pl.ds(pl.multiple_of(r * SUBLANES, SUBLANES), SUBLANES), :]
        for kk in range(TOP_K):
            dst = pl.multiple_of(dsm_s[slot, kk, r] * SUBLANES, SUBLANES)
            pltpu.make_async_copy(src, xs_hbm.at[pl.ds(dst, SUBLANES), :],
                                  sem_rows).start(priority=kk % 2)
        return c
    lax.fori_loop(0, tm, issue, 0, unroll=8)

    for kk in range(TOP_K):
        pltpu.make_async_copy(hn_ref, xs_hbm.at[pl.ds(0, tm * SUBLANES), :], sem_rows).wait()


def _dispatch(dest, hn, n_slots, tm):
    n = hn.shape[0] // SUBLANES
    return pl.pallas_call(
        functools.partial(_dispatch_kernel, tm=tm),
        grid=(n // tm,),
        in_specs=[pl.BlockSpec(memory_space=pl.ANY),
                  pl.BlockSpec((tm * SUBLANES, LANES), lambda i: (i, 0))],
        out_specs=pl.BlockSpec(memory_space=pl.ANY),
        out_shape=jax.ShapeDtypeStruct((n_slots * SUBLANES, LANES), F32),
        scratch_shapes=[pltpu.SMEM((2, ROUTE_ROWS, tm), jnp.int32),
                        pltpu.SemaphoreType.DMA((2,)), pltpu.SemaphoreType.DMA(())],
        compiler_params=pltpu.CompilerParams(
            dimension_semantics=("arbitrary",), vmem_limit_bytes=V7X_VMEM_LIMIT),
        name="dispatch",
    )(dest, hn)


def _expert_kernel(bexp_ref, bvalid_ref, xs_ref, wgu_ref, bgu_ref, wdn_ref, bdn_ref, ys_ref,
                   wgu_bf, wdn_bf):
    i = pl.program_id(0)
    valid = bvalid_ref[i]
    f = wdn_ref.shape[1]

    @pl.when((i == 0) | (bexp_ref[i] != bexp_ref[jnp.maximum(i - 1, 0)]))
    def _():
        wgu_bf[...] = wgu_ref[0].astype(BF16)
        wdn_bf[...] = wdn_ref[0].astype(BF16)

    @pl.when(valid > 0)
    def _():
        x = _load_token_tiles(xs_ref, EXPERT_BLOCK)
        rows = lax.broadcasted_iota(jnp.int32, x.shape, 0)
        xb = jnp.where(rows < valid, x, 0.0).astype(BF16)
        gu = jnp.dot(xb, wgu_bf[...], preferred_element_type=F32) + bgu_ref[0]
        glu = jnp.minimum(gu[:, :f], SWIGLU_LIMIT)
        lin = jnp.clip(gu[:, f:], -SWIGLU_LIMIT, SWIGLU_LIMIT)
        act = glu * _sigmoid(SWIGLU_ALPHA * glu) * (lin + 1.0)
        y = jnp.dot(act.astype(BF16), wdn_bf[...], preferred_element_type=F32) + bdn_ref[0]
        _store_token_tiles(ys_ref, y)

    @pl.when(valid <= 0)
    def _():
        ys_ref[...] = jnp.zeros_like(ys_ref)


def _experts(block_exp, block_valid, xs, wgu, bgu, wdn, bdn):
    d = wgu.shape[1]
    n_slots = xs.shape[0] // SUBLANES
    nb = n_slots // EXPERT_BLOCK
    f2 = wgu.shape[2]
    f = wdn.shape[1]
    tile_rows = EXPERT_BLOCK * SUBLANES
    grid_spec = pltpu.PrefetchScalarGridSpec(
        num_scalar_prefetch=2,
        grid=(nb,),
        in_specs=[pl.BlockSpec((tile_rows, LANES), lambda i, be, bv: (i, 0)),
                  pl.BlockSpec((1, d, f2), lambda i, be, bv: (be[i], 0, 0)),
                  pl.BlockSpec((1, 1, f2), lambda i, be, bv: (be[i], 0, 0)),
                  pl.BlockSpec((1, f, d), lambda i, be, bv: (be[i], 0, 0)),
                  pl.BlockSpec((1, 1, d), lambda i, be, bv: (be[i], 0, 0))],
        out_specs=pl.BlockSpec((tile_rows, LANES), lambda i, be, bv: (i, 0)),
        scratch_shapes=[pltpu.VMEM((d, f2), BF16), pltpu.VMEM((f, d), BF16)],
    )
    return pl.pallas_call(
        _expert_kernel,
        grid_spec=grid_spec,
        out_shape=jax.ShapeDtypeStruct(xs.shape, F32),
        compiler_params=pltpu.CompilerParams(
            dimension_semantics=("arbitrary",), vmem_limit_bytes=V7X_VMEM_LIMIT),
        name="experts",
    )(block_exp, block_valid, xs, wgu, bgu, wdn, bdn)


def _combine_kernel(dest_hbm, h_ref, gate_ref, g_ref, ys_hbm, out_ref, dsm_s, ybuf, sem_idx, sem_rows,
                    *, tm):
    i = pl.program_id(0)
    nsteps = pl.num_programs(0)
    slot = i % 2

    def idx_copy(tile, s):
        return pltpu.make_async_copy(dest_hbm.at[tile], dsm_s.at[s], sem_idx.at[s])

    def issue_rows(s):
        def issue(r, c):
            for kk in range(TOP_K):
                src = pl.multiple_of(dsm_s[s, kk, r] * SUBLANES, SUBLANES)
                dst = pl.multiple_of(r * SUBLANES, SUBLANES)
                pltpu.make_async_copy(ys_hbm.at[pl.ds(src, SUBLANES), :],
                                      ybuf.at[s, kk, pl.ds(dst, SUBLANES), :],
                                      sem_rows.at[s]).start(priority=kk % 2)
            return c
        lax.fori_loop(0, tm, issue, 0, unroll=8)

    @pl.when(i == 0)
    def _():
        idx_copy(0, 0).start()
        idx_copy(0, 0).wait()
        issue_rows(0)

        @pl.when(nsteps > 1)
        def _():
            idx_copy(1, 1).start()

    @pl.when(i + 1 < nsteps)
    def _():
        idx_copy(i + 1, 1 - slot).wait()
        issue_rows(1 - slot)

        @pl.when(i + 2 < nsteps)
        def _():
            idx_copy(i + 2, slot).start()

    for kk in range(TOP_K):
        pltpu.make_async_copy(ys_hbm.at[pl.ds(0, tm * SUBLANES), :], ybuf.at[slot, kk],
                              sem_rows.at[slot]).wait()

    gates = jnp.transpose(gate_ref[0])
    acc = h_ref[...]
    for kk in range(TOP_K):
        acc = acc + gates[:, kk:kk + 1] * _load_token_tiles(ybuf.at[slot, kk], tm)
    out_ref[...] = acc * lax.rsqrt(jnp.mean(acc * acc, axis=-1, keepdims=True) + EPS) * g_ref[...]


def _combine(dest, h, gates, g_final, ys, tm):
    n, d = h.shape
    return pl.pallas_call(
        functools.partial(_combine_kernel, tm=tm),
        grid=(n // tm,),
        in_specs=[pl.BlockSpec(memory_space=pl.ANY),
                  pl.BlockSpec((tm, d), lambda i: (i, 0)),
                  pl.BlockSpec((1, ROUTE_ROWS, tm), lambda i: (i, 0, 0)),
                  pl.BlockSpec((1, d), lambda i: (0, 0)),
                  pl.BlockSpec(memory_space=pl.ANY)],
        out_specs=pl.BlockSpec((tm, d), lambda i: (i, 0)),
        out_shape=jax.ShapeDtypeStruct((n, d), F32),
        scratch_shapes=[pltpu.SMEM((2, ROUTE_ROWS, tm), jnp.int32),
                        pltpu.VMEM((2, TOP_K, tm * SUBLANES, LANES), F32),
                        pltpu.SemaphoreType.DMA((2,)), pltpu.SemaphoreType.DMA((2,))],
        compiler_params=pltpu.CompilerParams(
            dimension_semantics=("arbitrary",), vmem_limit_bytes=V7X_VMEM_LIMIT),
        name="combine",
    )(dest, h, gates, g_final, ys)


def kernel(x, meta_tokens, norm_mix, w_in, b_in, lam_params, subln_gain, w_attn_proj, conv_dw_w,
           conv_dw_b, conv_ln_g, conv_ln_b, w_conv_proj, w_out, norm_ffn, w_router, b_router,
           w_gate_up, b_gate_up, w_down, b_down, norm_final):
    bsz, seq, d = x.shape
    assert norm_mix.shape[0] == 1, "single-layer block"
    n = bsz * seq
    tm_proj = min(512, seq)
    tq = min(512, seq)
    tm = min(256, seq)
    lam_init = 0.8 - 0.6 * 1.0

    x2d = x.reshape(n, d)
    w_rest = jnp.concatenate([w_in[0][:, :2 * d], w_in[0][:, 3 * d:]], axis=1).astype(BF16)
    b_rest = jnp.concatenate([b_in[0][:2 * d], b_in[0][3 * d:]])[None]
    w_vt = w_in[0][:, 2 * d:3 * d].T.astype(BF16)
    b_vt = b_in[0][2 * d:3 * d][:, None]
    g_mix = norm_mix[0][None]
    q, k, vt, u, sga, sgc = _inproj(x2d, g_mix, w_rest, b_rest, w_vt, b_vt, tm_proj)
    _, k_m, vt_m, u_m, _, _ = _inproj(meta_tokens.astype(x.dtype), g_mix, w_rest, b_rest, w_vt, b_vt,
                                      N_META)

    o = _attention(lam_params[0], q, k, vt, k_m, vt_m, subln_gain[0][:, None], bsz, seq, tq,
                   lam_init)

    meta_halo = jnp.concatenate([jnp.zeros((HALO - N_META, d), BF16), u_m], axis=0)
    wr_t = w_router[0].T
    wr_hi = wr_t.astype(BF16)
    wr_split = jnp.concatenate([wr_hi, (wr_t - wr_hi.astype(F32)).astype(BF16)], axis=0)
    h, hn, eidx, rank, gates, counts = _mixer(
        x2d, o, u, sga, sgc, meta_halo,
        w_attn_proj[0].astype(BF16), w_conv_proj[0].astype(BF16), w_out[0].astype(BF16),
        conv_dw_w[0], conv_dw_b[0][None], conv_ln_g[0][None], conv_ln_b[0][None],
        norm_ffn[0][None], wr_split, b_router[0][:, None], seq, tm)

    cnt = counts[:, 0].astype(jnp.int32)
    padded = (cnt + EXPERT_BLOCK - 1) // EXPERT_BLOCK * EXPERT_BLOCK
    pad_end = jnp.cumsum(padded)
    pad_start = pad_end - padded
    n_blocks = -(-(n * TOP_K + N_EXPERTS * (EXPERT_BLOCK - 1)) // EXPERT_BLOCK)
    n_slots = n_blocks * EXPERT_BLOCK
    dest = pad_start[eidx] + rank
    block_start = jnp.arange(n_blocks, dtype=jnp.int32) * EXPERT_BLOCK
    block_exp = jnp.minimum(jnp.sum(block_start[:, None] >= pad_end[None, :], axis=1),
                            N_EXPERTS - 1).astype(jnp.int32)
    block_valid = jnp.clip(pad_start[block_exp] + cnt[block_exp] - block_start, 0,
                           EXPERT_BLOCK).astype(jnp.int32)

    xs = _dispatch(dest, hn, n_slots, tm)
    ys = _experts(block_exp, block_valid, xs, w_gate_up[0], b_gate_up[0][:, None, :],
                  w_down[0], b_down[0][:, None, :])
    out = _combine(dest, h, gates, norm_final[None], ys, tm)
    return out.reshape(bsz, seq, d)
```

```python
import functools

import jax
import jax.numpy as jnp
from jax import lax
from jax.experimental import pallas as pl
from jax.experimental.pallas import tpu as pltpu

CHUNK = 64
N_META = 16
N_HEADS = 8
HEAD_DIM = 64
V_DIM = 2 * HEAD_DIM
CONV_K = 31
N_EXPERTS = 32
TOP_K = 4
SWIGLU_ALPHA = 1.702
SWIGLU_LIMIT = 7.0
EXPERT_BLOCK = 256
EPS = 1e-5
NEG = -1e30
LOG2E = 1.4426950408889634

V7X_VMEM_LIMIT = 56 * 1024 * 1024
SUBLANES = 8
LANES = 128
META_PAD = 128
HALO = 32
ROUTE_ROWS = 8

BF16 = jnp.bfloat16
F32 = jnp.float32


def _sigmoid(x):
    return 1.0 / (1.0 + jnp.exp(-x))


def _load_token_tiles(ref, rows):
    return jnp.concatenate([ref[pl.ds(s, rows, stride=SUBLANES), :] for s in range(SUBLANES)],
                           axis=-1)


def _store_token_tiles(ref, x):
    rows = x.shape[0]
    for s in range(SUBLANES):
        ref[pl.ds(s, rows, stride=SUBLANES), :] = x[:, s * LANES:(s + 1) * LANES]


def _inproj_kernel(x_ref, g_ref, w_ref, b_ref, wt_ref, bt_ref,
                   qt_ref, k_ref, vt_ref, u_ref, sga_ref, sgc_ref):
    d = x_ref.shape[1]
    x = x_ref[...]
    xn = x * lax.rsqrt(jnp.mean(x * x, axis=-1, keepdims=True) + EPS) * g_ref[...]
    xb = xn.astype(BF16)

    def proj(c):
        return (jnp.dot(xb, w_ref[:, c * d:(c + 1) * d], preferred_element_type=F32)
                + b_ref[:, c * d:(c + 1) * d])

    kf = proj(0).astype(BF16)
    for hd in range(N_HEADS):
        k_ref[hd] = kf[:, hd * V_DIM:(hd + 1) * V_DIM]
    t = lax.dot_general(wt_ref[...], xb, (((1,), (1,)), ((), ())), preferred_element_type=F32)
    t = t + bt_ref[...]
    qt_ref[...] = (t[:d] * (HEAD_DIM ** -0.5 * LOG2E)).astype(BF16)
    vt_ref[...] = t[d:].astype(BF16)
    u_ref[...] = (proj(1) * _sigmoid(proj(2))).astype(BF16)
    sga_ref[...] = _sigmoid(proj(3)).astype(BF16)
    sgc_ref[...] = _sigmoid(proj(4)).astype(BF16)


def _inproj(x2d, g, w_bf, b, wt_bf, bt, tm):
    n, d = x2d.shape
    in_w = w_bf.shape[1]
    out = jax.ShapeDtypeStruct((n, d), BF16)
    row = pl.BlockSpec((tm, d), lambda i: (i, 0))
    col = pl.BlockSpec((d, tm), lambda i: (0, i))
    heads = pl.BlockSpec((N_HEADS, tm, V_DIM), lambda i: (0, i, 0))
    out_heads = jax.ShapeDtypeStruct((N_HEADS, n, V_DIM), BF16)
    out_t = jax.ShapeDtypeStruct((d, n), BF16)
    return pl.pallas_call(
        _inproj_kernel,
        grid=(n // tm,),
        in_specs=[row,
                  pl.BlockSpec((1, d), lambda i: (0, 0)),
                  pl.BlockSpec((d, in_w), lambda i: (0, 0)),
                  pl.BlockSpec((1, in_w), lambda i: (0, 0)),
                  pl.BlockSpec((2 * d, d), lambda i: (0, 0)),
                  pl.BlockSpec((2 * d, 1), lambda i: (0, 0))],
        out_specs=[col, heads, col, row, row, row],
        out_shape=[out_t, out_heads, out_t, out, out, out],
        compiler_params=pltpu.CompilerParams(
            dimension_semantics=("arbitrary",), vmem_limit_bytes=V7X_VMEM_LIMIT),
        name="inproj",
    )(x2d, g, w_bf, b, wt_bf, bt)


def _attn_kernel(lam_ref, qt_ref, k_ref, vt_ref, km_ref, vmt_ref, bias_ref, bmeta_ref,
                 gain_ref, o_ref, m_s, l_s, acc_s, s0, s1, mx0, mx1, *, tq, lam_init):
    h = pl.program_id(1)
    qi = pl.program_id(2)
    slope = jnp.exp2(-(h + 1).astype(F32))
    lp = lam_ref[...]
    lam = (jnp.exp(jnp.sum(lp[0:1] * lp[1:2], axis=-1, keepdims=True))
           - jnp.exp(jnp.sum(lp[2:3] * lp[3:4], axis=-1, keepdims=True)) + lam_init)

    qt = qt_ref[...]
    feat = lax.broadcasted_iota(jnp.int32, qt.shape, 0)
    zero = jnp.zeros_like(qt)
    qs = (jnp.where(feat < HEAD_DIM, qt, zero), jnp.where(feat >= HEAD_DIM, qt, zero))

    def scores_t(kk, qm):
        return jnp.dot(kk, qm, preferred_element_type=F32)

    tk = tq // 2
    n_off = 2 * qi

    def tile_shift(t):
        off = (qi * tq - t * tk).astype(F32) * LOG2E
        return jnp.where(t < n_off, slope * off, 0.0)

    def stage_a(t, s_buf, mx_buf):
        kk = k_ref[0, pl.ds(pl.multiple_of(t * tk, tk), tk), :]
        bias = bias_ref[0, jnp.where(t < n_off, 0, t - n_off + 1)]
        for idx in range(2):
            s = scores_t(kk, qs[idx]) + bias
            s_buf[idx] = s
            mx_buf[idx] = jnp.max(s, axis=0, keepdims=True)

    def stage_b(t, s_buf, mx_buf):
        vt = vt_ref[:, pl.ds(pl.multiple_of(t * tk, tk), tk)]
        vt1 = jnp.concatenate([vt, jnp.ones((2 * SUBLANES, tk), BF16)], axis=0)
        shift = tile_shift(t)
        for idx in range(2):
            m_old = m_s[idx]
            m_new = jnp.maximum(m_old, mx_buf[idx] - shift)
            p = jnp.exp2(s_buf[idx] - (m_new + shift))
            pv1 = jnp.dot(vt1, p.astype(BF16), preferred_element_type=F32)
            pv = pv1[:V_DIM]
            ps = pv1[V_DIM:V_DIM + 1]
            alpha = jnp.exp2(m_old - m_new)
            l_s[idx] = alpha * l_s[idx] + ps
            acc_s[idx] = alpha * acc_s[idx] + pv
            m_s[idx] = m_new

    stage_a(0, s0, mx0)

    shift0 = slope * ((qi * tq).astype(F32) * LOG2E)
    km = km_ref[0]
    vmt = vmt_ref[...]
    s_meta = [scores_t(km, qs[idx]) + bmeta_ref[0] for idx in range(2)]
    for idx in range(2):
        s = s_meta[idx]
        m_new = jnp.max(s, axis=0, keepdims=True) - shift0
        p = jnp.exp2(s - (m_new + shift0))
        l_s[idx] = jnp.sum(p, axis=0, keepdims=True)
        acc_s[idx] = jnp.dot(vmt, p.astype(BF16), preferred_element_type=F32)
        m_s[idx] = m_new

    def body(i, carry):
        stage_a(2 * i + 1, s1, mx1)
        stage_b(2 * i, s0, mx0)
        stage_a(2 * i + 2, s0, mx0)
        stage_b(2 * i + 1, s1, mx1)
        return carry

    lax.fori_loop(0, qi, body, 0)
    stage_a(n_off + 1, s1, mx1)
    stage_b(n_off, s0, mx0)
    stage_b(n_off + 1, s1, mx1)

    o = acc_s[0] / l_s[0] - lam * (acc_s[1] / l_s[1])
    o = o * lax.rsqrt(jnp.mean(o * o, axis=0, keepdims=True) + EPS) * gain_ref[...]
    o_ref[0] = jnp.transpose(o * (1.0 - lam_init)).astype(BF16)


def _attention(lam_params, qt, k, vt, k_meta, vt_meta, gain, bsz, seq, tq, lam_init):
    n = qt.shape[1]
    nq = seq // tq
    slopes = jnp.asarray([2.0 ** (-8.0 * (h + 1) / N_HEADS) for h in range(N_HEADS)], F32)
    tk = tq // 2
    r = jnp.arange(tq, dtype=jnp.int32)
    diff = r[None, :] - r[:, None]
    slopes = slopes * LOG2E
    boff = -slopes[:, None, None] * diff.astype(F32)[None, :tk]
    vis = (r[:, None] // CHUNK) <= (r[None, :] // CHUNK)
    bdiag = jnp.where(vis[None], -slopes[:, None, None] * jnp.abs(diff).astype(F32)[None], NEG)
    bias_all = jnp.stack([boff, bdiag[:, :tk], bdiag[:, tk:]], axis=1)
    mrow = jnp.arange(META_PAD, dtype=jnp.int32)
    dmeta = (N_META + r[None, :] - mrow[:, None]).astype(F32)
    bmeta = jnp.where((mrow < N_META)[None, :, None], -slopes[:, None, None] * dmeta[None], NEG)

    km = jnp.pad(k_meta, ((0, 0), (0, META_PAD - N_META), (0, 0)))
    vmt = jnp.pad(vt_meta, ((0, 0), (0, META_PAD - N_META)))

    kern = functools.partial(_attn_kernel, tq=tq, lam_init=lam_init)
    return pl.pallas_call(
        kern,
        grid=(bsz, N_HEADS, nq),
        in_specs=[
            pl.BlockSpec((4, HEAD_DIM), lambda b, h, i: (0, 0)),
            pl.BlockSpec((V_DIM, tq), lambda b, h, i: (h, b * nq + i)),
            pl.BlockSpec((1, seq, V_DIM), lambda b, h, i: (h, b, 0)),
            pl.BlockSpec((V_DIM, seq), lambda b, h, i: (h, b)),
            pl.BlockSpec((1, META_PAD, V_DIM), lambda b, h, i: (h, 0, 0)),
            pl.BlockSpec((V_DIM, META_PAD), lambda b, h, i: (h, 0)),
            pl.BlockSpec((1, 3, tk, tq), lambda b, h, i: (h, 0, 0, 0)),
            pl.BlockSpec((1, META_PAD, tq), lambda b, h, i: (h, 0, 0)),
            pl.BlockSpec((V_DIM, 1), lambda b, h, i: (0, 0)),
        ],
        out_specs=pl.BlockSpec((1, tq, V_DIM), lambda b, h, i: (h, b * nq + i, 0)),
        out_shape=jax.ShapeDtypeStruct((N_HEADS, n, V_DIM), BF16),
        scratch_shapes=[pltpu.VMEM((2, 1, tq), F32), pltpu.VMEM((2, 1, tq), F32),
                        pltpu.VMEM((2, V_DIM, tq), F32),
                        pltpu.VMEM((2, tk, tq), F32), pltpu.VMEM((2, tk, tq), F32),
                        pltpu.VMEM((2, 1, tq), F32), pltpu.VMEM((2, 1, tq), F32)],
        compiler_params=pltpu.CompilerParams(
            dimension_semantics=("arbitrary", "arbitrary", "arbitrary"),
            vmem_limit_bytes=V7X_VMEM_LIMIT),
        name="diff_attention",
    )(lam_params, qt, k, vt, km, vmt, bias_all, bmeta, gain)


def _mixer_kernel(x_ref, o_ref, u_ref, uh_ref, mh_ref, sga_ref, sgc_ref,
                  wap_ref, wpw_ref, wout_ref, dww_ref, dwb_ref, lng_ref, lnb_ref,
                  nffn_ref, wr_ref, br_ref, tri_ref,
                  h_ref, hn_ref, exp_ref, rank_ref, gate_ref, cnt_ref,
                  ext_s, y_s, cnt_s, *, tm, tiles_per_seq):
    i = pl.program_id(0)
    d = x_ref.shape[1]

    @pl.when(i == 0)
    def _():
        cnt_s[...] = jnp.zeros_like(cnt_s)

    first = (i % tiles_per_seq) == 0
    halo = jnp.where(first, mh_ref[...], uh_ref[...])
    ext_s[0, 0:HALO, :] = halo.astype(F32)
    ext_s[0, HALO:HALO + tm, :] = u_ref[...].astype(F32)
    span = tm + HALO - SUBLANES
    for b in range(1, SUBLANES):
        ext_s[b, 0:span, :] = ext_s[0, b:b + span, :]
    off = HALO - (CONV_K - 1)

    o = jnp.concatenate([o_ref[hd] for hd in range(N_HEADS)], axis=-1)
    y_attn = jnp.dot(o, wap_ref[...], preferred_element_type=F32)
    half = tm // 2
    hn_parts = []
    for r0 in (0, half):
        rs = slice(r0, r0 + half)
        for c in range(d // 128):
            cs = slice(c * 128, (c + 1) * 128)
            acc = jnp.zeros((half, 128), F32)
            for t in range(CONV_K):
                a, b = divmod(off + t, SUBLANES)
                row = a * SUBLANES + r0
                acc = acc + dww_ref[t:t + 1, cs] * ext_s[b, row:row + half, cs]
            y_s[rs, cs] = acc
        y = y_s[rs, :] + dwb_ref[...]
        mu = jnp.mean(y, axis=-1, keepdims=True)
        yc = y - mu
        var = jnp.mean(yc * yc, axis=-1, keepdims=True)
        y = yc * lax.rsqrt(var + EPS) * lng_ref[...] + lnb_ref[...]
        y = y * _sigmoid(y)
        y_conv = jnp.dot(y.astype(BF16), wpw_ref[...], preferred_element_type=F32)

        mixed = sga_ref[rs, :].astype(F32) * y_attn[rs] + sgc_ref[rs, :].astype(F32) * y_conv
        h = x_ref[rs, :] + jnp.dot(mixed.astype(BF16), wout_ref[...], preferred_element_type=F32)
        h_ref[rs, :] = h
        hn_parts.append(h * lax.rsqrt(jnp.mean(h * h, axis=-1, keepdims=True) + EPS) * nffn_ref[...])
    hn = jnp.concatenate(hn_parts, axis=0)
    _store_token_tiles(hn_ref, hn)

    nt_dims = (((1,), (1,)), ((), ()))
    hn_hi = hn.astype(BF16)
    hn_lo = (hn - hn_hi.astype(F32)).astype(BF16)
    both = lax.dot_general(wr_ref[...], hn_hi, nt_dims, preferred_element_type=F32)
    logits = (both[:N_EXPERTS] + both[N_EXPERTS:]
              + lax.dot_general(wr_ref[0:N_EXPERTS, :], hn_lo, nt_dims, preferred_element_type=F32)
              + br_ref[...])
    eio = lax.broadcasted_iota(jnp.int32, logits.shape, 0)
    sels, tops, idxs = [], [], []
    cur = logits
    for _ in range(TOP_K):
        m = jnp.max(cur, axis=0, keepdims=True)
        idx = jnp.min(jnp.where(cur == m, eio, N_EXPERTS), axis=0, keepdims=True)
        sel = eio == idx
        sels.append(sel)
        tops.append(m)
        idxs.append(idx)
        cur = jnp.where(sel, -jnp.inf, cur)
    exps = [jnp.exp(t - tops[0]) for t in tops]
    denom = exps[0] + exps[1] + exps[2] + exps[3]
    hot = jnp.zeros(logits.shape, F32)
    for sel in sels:
        hot = hot + jnp.where(sel, 1.0, 0.0)
    before = jnp.dot(hot.astype(BF16), tri_ref[...], preferred_element_type=F32) + cnt_s[...]
    ranks = [jnp.sum(jnp.where(sel, before, 0.0), axis=0, keepdims=True).astype(jnp.int32)
             for sel in sels]
    zi = jnp.zeros((ROUTE_ROWS - TOP_K, tm), jnp.int32)
    exp_ref[0] = jnp.concatenate(idxs + [zi], axis=0)
    rank_ref[0] = jnp.concatenate(ranks + [zi], axis=0)
    gate_ref[0] = jnp.concatenate([e / denom for e in exps] + [zi.astype(F32)], axis=0)
    cnt_s[...] = cnt_s[...] + jnp.sum(hot, axis=1, keepdims=True)
    cnt_ref[...] = jnp.broadcast_to(cnt_s[...], cnt_ref.shape)


def _mixer(x2d, o, u, sga, sgc, meta_halo, wap, wpw, wout, dww, dwb, lng, lnb, nffn, wr_t, br,
           seq, tm):
    n, d = x2d.shape
    nt = n // tm
    hb = tm // HALO
    tri = (jnp.arange(tm)[:, None] < jnp.arange(tm)[None, :]).astype(BF16)
    row = pl.BlockSpec((tm, d), lambda i: (i, 0))
    route = pl.BlockSpec((1, ROUTE_ROWS, tm), lambda i: (i, 0, 0))

    def full(shp):
        return pl.BlockSpec(shp, lambda i: tuple(0 for _ in shp))

    kern = functools.partial(_mixer_kernel, tm=tm, tiles_per_seq=seq // tm)
    return pl.pallas_call(
        kern,
        grid=(nt,),
        in_specs=[row, pl.BlockSpec((N_HEADS, tm, V_DIM), lambda i: (0, i, 0)), row,
                  pl.BlockSpec((HALO, d), lambda i: (jnp.maximum(i * hb - 1, 0), 0)),
                  full((HALO, d)), row, row,
                  full((d, d)), full((d, d)), full((d, d)),
                  full((CONV_K, d)), full((1, d)), full((1, d)), full((1, d)), full((1, d)),
                  full((2 * N_EXPERTS, d)), full((N_EXPERTS, 1)), full((tm, tm))],
        out_specs=[row, pl.BlockSpec((tm * SUBLANES, LANES), lambda i: (i, 0)),
                   route, route, route, full((N_EXPERTS, 128))],
        out_shape=[jax.ShapeDtypeStruct((n, d), F32),
                   jax.ShapeDtypeStruct((n * SUBLANES, LANES), F32),
                   jax.ShapeDtypeStruct((nt, ROUTE_ROWS, tm), jnp.int32),
                   jax.ShapeDtypeStruct((nt, ROUTE_ROWS, tm), jnp.int32),
                   jax.ShapeDtypeStruct((nt, ROUTE_ROWS, tm), F32),
                   jax.ShapeDtypeStruct((N_EXPERTS, 128), F32)],
        scratch_shapes=[pltpu.VMEM((SUBLANES, HALO + tm, d), F32), pltpu.VMEM((tm, d), F32),
                        pltpu.VMEM((N_EXPERTS, 1), F32)],
        compiler_params=pltpu.CompilerParams(
            dimension_semantics=("arbitrary",), vmem_limit_bytes=V7X_VMEM_LIMIT),
        name="mixer_router",
    )(x2d, o, u, u, meta_halo, sga, sgc, wap, wpw, wout, dww, dwb, lng, lnb, nffn, wr_t, br, tri)


def _prefetch_routes(dest_hbm, dsm_s, sem_idx):
    i = pl.program_id(0)
    slot = i % 2

    def idx_copy(tile, s):
        return pltpu.make_async_copy(dest_hbm.at[tile], dsm_s.at[s], sem_idx.at[s])

    @pl.when(i == 0)
    def _():
        idx_copy(0, 0).start()

    idx_copy(i, slot).wait()

    @pl.when(i + 1 < pl.num_programs(0))
    def _():
        idx_copy(i + 1, 1 - slot).start()

    return slot


def _dispatch_kernel(dest_hbm, hn_ref, xs_hbm, dsm_s, sem_idx, sem_rows, *, tm):
    slot = _prefetch_routes(dest_hbm, dsm_s, sem_idx)

    def issue(r, c):
        src = hn_ref.at[pl.ds(pl.multiple_of(r * SUBLANES, SUBLANES), SUBLANES), :]
        for kk in range(TOP_K):
            dst = pl.multiple_of(dsm_s[slot, kk, r] * SUBLANES, SUBLANES)
            pltpu.make_async_copy(src, xs_hbm.at[pl.ds(dst, SUBLANES), :],
                                  sem_rows).start(priority=kk % 2)
        return c
    lax.fori_loop(0, tm, issue, 0, unroll=8)

    for kk in range(TOP_K):
        pltpu.make_async_copy(hn_ref, xs_hbm.at[pl.ds(0, tm * SUBLANES), :], sem_rows).wait()


def _dispatch(dest, hn, n_slots, tm):
    n = hn.shape[0] // SUBLANES
    return pl.pallas_call(
        functools.partial(_dispatch_kernel, tm=tm),
        grid=(n // tm,),
        in_specs=[pl.BlockSpec(memory_space=pl.ANY),
                  pl.BlockSpec((tm * SUBLANES, LANES), lambda i: (i, 0))],
        out_specs=pl.BlockSpec(memory_space=pl.ANY),
        out_shape=jax.ShapeDtypeStruct((n_slots * SUBLANES, LANES), F32),
        scratch_shapes=[pltpu.SMEM((2, ROUTE_ROWS, tm), jnp.int32),
                        pltpu.SemaphoreType.DMA((2,)), pltpu.SemaphoreType.DMA(())],
        compiler_params=pltpu.CompilerParams(
            dimension_semantics=("arbitrary",), vmem_limit_bytes=V7X_VMEM_LIMIT),
        name="dispatch",
    )(dest, hn)


def _expert_kernel(bexp_ref, bvalid_ref, xs_ref, wgu_ref, bgu_ref, wdn_ref, bdn_ref, ys_ref,
                   wgu_bf, wdn_bf):
    i = pl.program_id(0)
    valid = bvalid_ref[i]
    f = wdn_ref.shape[1]

    @pl.when((i == 0) | (bexp_ref[i] != bexp_ref[jnp.maximum(i - 1, 0)]))
    def _():
        wgu_bf[...] = wgu_ref[0].astype(BF16)
        wdn_bf[...] = wdn_ref[0].astype(BF16)

    @pl.when(valid > 0)
    def _():
        x = _load_token_tiles(xs_ref, EXPERT_BLOCK)
        rows = lax.broadcasted_iota(jnp.int32, x.shape, 0)
        xb = jnp.where(rows < valid, x, 0.0).astype(BF16)
        gu = jnp.dot(xb, wgu_bf[...], preferred_element_type=F32) + bgu_ref[0]
        glu = jnp.minimum(gu[:, :f], SWIGLU_LIMIT)
        lin = jnp.clip(gu[:, f:], -SWIGLU_LIMIT, SWIGLU_LIMIT)
        act = glu * _sigmoid(SWIGLU_ALPHA * glu) * (lin + 1.0)
        y = jnp.dot(act.astype(BF16), wdn_bf[...], preferred_element_type=F32) + bdn_ref[0]
        _store_token_tiles(ys_ref, y)

    @pl.when(valid <= 0)
    def _():
        ys_ref[...] = jnp.zeros_like(ys_ref)


def _experts(block_exp, block_valid, xs, wgu, bgu, wdn, bdn):
    d = wgu.shape[1]
    n_slots = xs.shape[0] // SUBLANES
    nb = n_slots // EXPERT_BLOCK
    f2 = wgu.shape[2]
    f = wdn.shape[1]
    tile_rows = EXPERT_BLOCK * SUBLANES
    grid_spec = pltpu.PrefetchScalarGridSpec(
        num_scalar_prefetch=2,
        grid=(nb,),
        in_specs=[pl.BlockSpec((tile_rows, LANES), lambda i, be, bv: (i, 0)),
                  pl.BlockSpec((1, d, f2), lambda i, be, bv: (be[i], 0, 0)),
                  pl.BlockSpec((1, 1, f2), lambda i, be, bv: (be[i], 0, 0)),
                  pl.BlockSpec((1, f, d), lambda i, be, bv: (be[i], 0, 0)),
                  pl.BlockSpec((1, 1, d), lambda i, be, bv: (be[i], 0, 0))],
        out_specs=pl.BlockSpec((tile_rows, LANES), lambda i, be, bv: (i, 0)),
        scratch_shapes=[pltpu.VMEM((d, f2), BF16), pltpu.VMEM((f, d), BF16)],
    )
    return pl.pallas_call(
        _expert_kernel,
        grid_spec=grid_spec,
        out_shape=jax.ShapeDtypeStruct(xs.shape, F32),
        compiler_params=pltpu.CompilerParams(
            dimension_semantics=("arbitrary",), vmem_limit_bytes=V7X_VMEM_LIMIT),
        name="experts",
    )(block_exp, block_valid, xs, wgu, bgu, wdn, bdn)


def _combine_kernel(dest_hbm, h_ref, gate_ref, g_ref, ys_hbm, out_ref, dsm_s, ybuf, sem_idx, sem_rows,
                    *, tm):
    i = pl.program_id(0)
    nsteps = pl.num_programs(0)
    slot = i % 2

    def idx_copy(tile, s):
        return pltpu.make_async_copy(dest_hbm.at[tile], dsm_s.at[s], sem_idx.at[s])

    def issue_rows(s):
        def issue(r, c):
            for kk in range(TOP_K):
                src = pl.multiple_of(dsm_s[s, kk, r] * SUBLANES, SUBLANES)
                dst = pl.multiple_of(r * SUBLANES, SUBLANES)
                pltpu.make_async_copy(ys_hbm.at[pl.ds(src, SUBLANES), :],
                                      ybuf.at[s, kk, pl.ds(dst, SUBLANES), :],
                                      sem_rows.at[s]).start(priority=kk % 2)
            return c
        lax.fori_loop(0, tm, issue, 0, unroll=8)

    @pl.when(i == 0)
    def _():
        idx_copy(0, 0).start()
        idx_copy(0, 0).wait()
        issue_rows(0)

        @pl.when(nsteps > 1)
        def _():
            idx_copy(1, 1).start()

    @pl.when(i + 1 < nsteps)
    def _():
        idx_copy(i + 1, 1 - slot).wait()
        issue_rows(1 - slot)

        @pl.when(i + 2 < nsteps)
        def _():
            idx_copy(i + 2, slot).start()

    for kk in range(TOP_K):
        pltpu.make_async_copy(ys_hbm.at[pl.ds(0, tm * SUBLANES), :], ybuf.at[slot, kk],
                              sem_rows.at[slot]).wait()

    gates = jnp.transpose(gate_ref[0])
    acc = h_ref[...]
    for kk in range(TOP_K):
        acc = acc + gates[:, kk:kk + 1] * _load_token_tiles(ybuf.at[slot, kk], tm)
    out_ref[...] = acc * lax.rsqrt(jnp.mean(acc * acc, axis=-1, keepdims=True) + EPS) * g_ref[...]


def _combine(dest, h, gates, g_final, ys, tm):
    n, d = h.shape
    return pl.pallas_call(
        functools.partial(_combine_kernel, tm=tm),
        grid=(n // tm,),
        in_specs=[pl.BlockSpec(memory_space=pl.ANY),
                  pl.BlockSpec((tm, d), lambda i: (i, 0)),
                  pl.BlockSpec((1, ROUTE_ROWS, tm), lambda i: (i, 0, 0)),
                  pl.BlockSpec((1, d), lambda i: (0, 0)),
                  pl.BlockSpec(memory_space=pl.ANY)],
        out_specs=pl.BlockSpec((tm, d), lambda i: (i, 0)),
        out_shape=jax.ShapeDtypeStruct((n, d), F32),
        scratch_shapes=[pltpu.SMEM((2, ROUTE_ROWS, tm), jnp.int32),
                        pltpu.VMEM((2, TOP_K, tm * SUBLANES, LANES), F32),
                        pltpu.SemaphoreType.DMA((2,)), pltpu.SemaphoreType.DMA((2,))],
        compiler_params=pltpu.CompilerParams(
            dimension_semantics=("arbitrary",), vmem_limit_bytes=V7X_VMEM_LIMIT),
        name="combine",
    )(dest, h, gates, g_final, ys)


def kernel(x, meta_tokens, norm_mix, w_in, b_in, lam_params, subln_gain, w_attn_proj, conv_dw_w,
           conv_dw_b, conv_ln_g, conv_ln_b, w_conv_proj, w_out, norm_ffn, w_router, b_router,
           w_gate_up, b_gate_up, w_down, b_down, norm_final):
    bsz, seq, d = x.shape
    assert norm_mix.shape[0] == 1, "single-layer block"
    n = bsz * seq
    tm_proj = min(512, seq)
    tq = min(512, seq)
    tm = min(256, seq)
    lam_init = 0.8 - 0.6 * 1.0

    x2d = x.reshape(n, d)
    w_rest = jnp.concatenate([w_in[0][:, d:2 * d], w_in[0][:, 3 * d:]], axis=1).astype(BF16)
    b_rest = jnp.concatenate([b_in[0][d:2 * d], b_in[0][3 * d:]])[None]
    w_t = jnp.concatenate([w_in[0][:, :d], w_in[0][:, 2 * d:3 * d]], axis=1).T.astype(BF16)
    b_t = jnp.concatenate([b_in[0][:d], b_in[0][2 * d:3 * d]])[:, None]
    g_mix = norm_mix[0][None]
    qt, k, vt, u, sga, sgc = _inproj(x2d, g_mix, w_rest, b_rest, w_t, b_t, tm_proj)
    _, k_m, vt_m, u_m, _, _ = _inproj(meta_tokens.astype(x.dtype), g_mix, w_rest, b_rest, w_t, b_t,
                                      N_META)

    o = _attention(lam_params[0], qt, k, vt, k_m, vt_m, subln_gain[0][:, None], bsz, seq, tq,
                   lam_init)

    meta_halo = jnp.concatenate([jnp.zeros((HALO - N_META, d), BF16), u_m], axis=0)
    wr_t = w_router[0].T
    wr_hi = wr_t.astype(BF16)
    wr_split = jnp.concatenate([wr_hi, (wr_t - wr_hi.astype(F32)).astype(BF16)], axis=0)
    h, hn, eidx, rank, gates, counts = _mixer(
        x2d, o, u, sga, sgc, meta_halo,
        w_attn_proj[0].astype(BF16), w_conv_proj[0].astype(BF16), w_out[0].astype(BF16),
        conv_dw_w[0], conv_dw_b[0][None], conv_ln_g[0][None], conv_ln_b[0][None],
        norm_ffn[0][None], wr_split, b_router[0][:, None], seq, tm)

    cnt = counts[:, 0].astype(jnp.int32)
    padded = (cnt + EXPERT_BLOCK - 1) // EXPERT_BLOCK * EXPERT_BLOCK
    pad_end = jnp.cumsum(padded)
    pad_start = pad_end - padded
    n_blocks = -(-(n * TOP_K + N_EXPERTS * (EXPERT_BLOCK - 1)) // EXPERT_BLOCK)
    n_slots = n_blocks * EXPERT_BLOCK
    dest = pad_start[eidx] + rank
    block_start = jnp.arange(n_blocks, dtype=jnp.int32) * EXPERT_BLOCK
    block_exp = jnp.minimum(jnp.sum(block_start[:, None] >= pad_end[None, :], axis=1),
                            N_EXPERTS - 1).astype(jnp.int32)
    block_valid = jnp.clip(pad_start[block_exp] + cnt[block_exp] - block_start, 0,
                           EXPERT_BLOCK).astype(jnp.int32)

    xs = _dispatch(dest, hn, n_slots, tm)
    ys = _experts(block_exp, block_valid, xs, w_gate_up[0], b_gate_up[0][:, None, :],
                  w_down[0], b_down[0][:, None, :])
    out = _combine(dest, h, gates, norm_final[None], ys, tm)
    return out.reshape(bsz, seq, d)
```

```python
import functools

import jax
import jax.numpy as jnp
from jax import lax
from jax.experimental import pallas as pl
from jax.experimental.pallas import tpu as pltpu

CHUNK = 64
N_META = 16
N_HEADS = 8
HEAD_DIM = 64
V_DIM = 2 * HEAD_DIM
CONV_K = 31
N_EXPERTS = 32
TOP_K = 4
SWIGLU_ALPHA = 1.702
SWIGLU_LIMIT = 7.0
EXPERT_BLOCK = 256
EPS = 1e-5
NEG = -1e30
LOG2E = 1.4426950408889634

V7X_VMEM_LIMIT = 56 * 1024 * 1024
SUBLANES = 8
LANES = 128
META_PAD = 128
HALO = 32
ROUTE_ROWS = 8

BF16 = jnp.bfloat16
F32 = jnp.float32


def _sigmoid(x):
    return 1.0 / (1.0 + jnp.exp(-x))


def _load_token_tiles(ref, rows):
    return jnp.concatenate([ref[pl.ds(s, rows, stride=SUBLANES), :] for s in range(SUBLANES)],
                           axis=-1)


def _store_token_tiles(ref, x):
    rows = x.shape[0]
    for s in range(SUBLANES):
        ref[pl.ds(s, rows, stride=SUBLANES), :] = x[:, s * LANES:(s + 1) * LANES]


def _inproj_kernel(x_ref, g_ref, w_ref, b_ref, wt_ref, bt_ref,
                   qt_ref, k_ref, vt_ref, u_ref, sga_ref, sgc_ref):
    d = x_ref.shape[1]
    x = x_ref[...]
    xn = x * lax.rsqrt(jnp.mean(x * x, axis=-1, keepdims=True) + EPS) * g_ref[...]
    xb = xn.astype(BF16)

    def proj(c):
        return (jnp.dot(xb, w_ref[:, c * d:(c + 1) * d], preferred_element_type=F32)
                + b_ref[:, c * d:(c + 1) * d])

    kf = proj(0).astype(BF16)
    for hd in range(N_HEADS):
        k_ref[hd] = kf[:, hd * V_DIM:(hd + 1) * V_DIM]
    t = lax.dot_general(wt_ref[...], xb, (((1,), (1,)), ((), ())), preferred_element_type=F32)
    t = t + bt_ref[...]
    qt_ref[...] = (t[:d] * (HEAD_DIM ** -0.5 * LOG2E)).astype(BF16)
    vt_ref[...] = t[d:].astype(BF16)
    u_ref[...] = (proj(1) * _sigmoid(proj(2))).astype(BF16)
    sga_ref[...] = _sigmoid(proj(3)).astype(BF16)
    sgc_ref[...] = _sigmoid(proj(4)).astype(BF16)


def _inproj(x2d, g, w_bf, b, wt_bf, bt, tm):
    n, d = x2d.shape
    in_w = w_bf.shape[1]
    out = jax.ShapeDtypeStruct((n, d), BF16)
    row = pl.BlockSpec((tm, d), lambda i: (i, 0))
    col = pl.BlockSpec((d, tm), lambda i: (0, i))
    heads = pl.BlockSpec((N_HEADS, tm, V_DIM), lambda i: (0, i, 0))
    out_heads = jax.ShapeDtypeStruct((N_HEADS, n, V_DIM), BF16)
    out_t = jax.ShapeDtypeStruct((d, n), BF16)
    return pl.pallas_call(
        _inproj_kernel,
        grid=(n // tm,),
        in_specs=[row,
                  pl.BlockSpec((1, d), lambda i: (0, 0)),
                  pl.BlockSpec((d, in_w), lambda i: (0, 0)),
                  pl.BlockSpec((1, in_w), lambda i: (0, 0)),
                  pl.BlockSpec((2 * d, d), lambda i: (0, 0)),
                  pl.BlockSpec((2 * d, 1), lambda i: (0, 0))],
        out_specs=[col, heads, col, row, row, row],
        out_shape=[out_t, out_heads, out_t, out, out, out],
        compiler_params=pltpu.CompilerParams(
            dimension_semantics=("arbitrary",), vmem_limit_bytes=V7X_VMEM_LIMIT),
        name="inproj",
    )(x2d, g, w_bf, b, wt_bf, bt)


def _attn_kernel(lam_ref, qt_ref, k_ref, vt_ref, km_ref, vmt_ref, bias_ref, bmeta_ref,
                 gain_ref, o_ref, m_s, l_s, acc_s, s0, s1, mx0, mx1, *, tq, lam_init):
    h = pl.program_id(1)
    qi = pl.program_id(2)
    slope = jnp.exp2(-(h + 1).astype(F32))
    lp = lam_ref[...]
    lam = (jnp.exp(jnp.sum(lp[0:1] * lp[1:2], axis=-1, keepdims=True))
           - jnp.exp(jnp.sum(lp[2:3] * lp[3:4], axis=-1, keepdims=True)) + lam_init)

    qt = qt_ref[...]
    feat = lax.broadcasted_iota(jnp.int32, qt.shape, 0)
    zero = jnp.zeros_like(qt)
    qs = (jnp.where(feat < HEAD_DIM, qt, zero), jnp.where(feat >= HEAD_DIM, qt, zero))

    def scores_t(kk, qm):
        return jnp.dot(kk, qm, preferred_element_type=F32)

    tk = tq // 2
    n_off = 2 * qi

    def tile_shift(t):
        off = (qi * tq - t * tk).astype(F32) * LOG2E
        return jnp.where(t < n_off, slope * off, 0.0)

    def stage_a(t, s_buf, mx_buf):
        kk = k_ref[0, pl.ds(pl.multiple_of(t * tk, tk), tk), :]
        bias = bias_ref[0, jnp.where(t < n_off, 0, t - n_off + 1)]
        for idx in range(2):
            s = scores_t(kk, qs[idx]) + bias
            s_buf[idx] = s
            mx_buf[idx] = jnp.max(s, axis=0, keepdims=True)

    def stage_b(t, s_buf, mx_buf, meta=None):
        vt = vt_ref[:, pl.ds(pl.multiple_of(t * tk, tk), tk)]
        vt1 = jnp.concatenate([vt, jnp.ones((2 * SUBLANES, tk), BF16)], axis=0)
        shift = tile_shift(t)
        for idx in range(2):
            m_old = m_s[idx]
            m_new = jnp.maximum(m_old, mx_buf[idx] - shift)
            if meta is not None:
                s_meta, vmt1, shift_m = meta
                m_new = jnp.maximum(m_new, jnp.max(s_meta[idx], axis=0, keepdims=True) - shift_m)
            p = jnp.exp2(s_buf[idx] - (m_new + shift))
            pv1 = jnp.dot(vt1, p.astype(BF16), preferred_element_type=F32)
            if meta is not None:
                pm = jnp.exp2(s_meta[idx] - (m_new + shift_m))
                pv1 = pv1 + jnp.dot(vmt1, pm.astype(BF16), preferred_element_type=F32)
            pv = pv1[:V_DIM]
            ps = pv1[V_DIM:V_DIM + 1]
            alpha = jnp.exp2(m_old - m_new)
            l_s[idx] = alpha * l_s[idx] + ps
            acc_s[idx] = alpha * acc_s[idx] + pv
            m_s[idx] = m_new

    m_s[...] = jnp.full(m_s.shape, NEG, F32)
    l_s[...] = jnp.zeros_like(l_s)
    acc_s[...] = jnp.zeros_like(acc_s)
    stage_a(0, s0, mx0)

    def body(i, carry):
        stage_a(2 * i + 1, s1, mx1)
        stage_b(2 * i, s0, mx0)
        stage_a(2 * i + 2, s0, mx0)
        stage_b(2 * i + 1, s1, mx1)
        return carry

    lax.fori_loop(0, qi, body, 0)
    stage_a(n_off + 1, s1, mx1)
    s_meta = [scores_t(km_ref[0], qs[idx]) + bmeta_ref[0] for idx in range(2)]
    vmt1 = jnp.concatenate([vmt_ref[...], jnp.ones((2 * SUBLANES, META_PAD), BF16)], axis=0)
    shift0 = slope * ((qi * tq).astype(F32) * LOG2E)
    stage_b(n_off, s0, mx0)
    stage_b(n_off + 1, s1, mx1, meta=(s_meta, vmt1, shift0))

    o = acc_s[0] / l_s[0] - lam * (acc_s[1] / l_s[1])
    o = o * lax.rsqrt(jnp.mean(o * o, axis=0, keepdims=True) + EPS) * gain_ref[...]
    o_ref[0] = jnp.transpose(o * (1.0 - lam_init)).astype(BF16)


def _attention(lam_params, qt, k, vt, k_meta, vt_meta, gain, bsz, seq, tq, lam_init):
    n = qt.shape[1]
    nq = seq // tq
    slopes = jnp.asarray([2.0 ** (-8.0 * (h + 1) / N_HEADS) for h in range(N_HEADS)], F32)
    tk = tq // 2
    r = jnp.arange(tq, dtype=jnp.int32)
    diff = r[None, :] - r[:, None]
    slopes = slopes * LOG2E
    boff = -slopes[:, None, None] * diff.astype(F32)[None, :tk]
    vis = (r[:, None] // CHUNK) <= (r[None, :] // CHUNK)
    bdiag = jnp.where(vis[None], -slopes[:, None, None] * jnp.abs(diff).astype(F32)[None], NEG)
    bias_all = jnp.stack([boff, bdiag[:, :tk], bdiag[:, tk:]], axis=1)
    mrow = jnp.arange(META_PAD, dtype=jnp.int32)
    dmeta = (N_META + r[None, :] - mrow[:, None]).astype(F32)
    bmeta = jnp.where((mrow < N_META)[None, :, None], -slopes[:, None, None] * dmeta[None], NEG)

    km = jnp.pad(k_meta, ((0, 0), (0, META_PAD - N_META), (0, 0)))
    vmt = jnp.pad(vt_meta, ((0, 0), (0, META_PAD - N_META)))

    kern = functools.partial(_attn_kernel, tq=tq, lam_init=lam_init)
    return pl.pallas_call(
        kern,
        grid=(bsz, N_HEADS, nq),
        in_specs=[
            pl.BlockSpec((4, HEAD_DIM), lambda b, h, i: (0, 0)),
            pl.BlockSpec((V_DIM, tq), lambda b, h, i: (h, b * nq + i)),
            pl.BlockSpec((1, seq, V_DIM), lambda b, h, i: (h, b, 0)),
            pl.BlockSpec((V_DIM, seq), lambda b, h, i: (h, b)),
            pl.BlockSpec((1, META_PAD, V_DIM), lambda b, h, i: (h, 0, 0)),
            pl.BlockSpec((V_DIM, META_PAD), lambda b, h, i: (h, 0)),
            pl.BlockSpec((1, 3, tk, tq), lambda b, h, i: (h, 0, 0, 0)),
            pl.BlockSpec((1, META_PAD, tq), lambda b, h, i: (h, 0, 0)),
            pl.BlockSpec((V_DIM, 1), lambda b, h, i: (0, 0)),
        ],
        out_specs=pl.BlockSpec((1, tq, V_DIM), lambda b, h, i: (h, b * nq + i, 0)),
        out_shape=jax.ShapeDtypeStruct((N_HEADS, n, V_DIM), BF16),
        scratch_shapes=[pltpu.VMEM((2, 1, tq), F32), pltpu.VMEM((2, 1, tq), F32),
                        pltpu.VMEM((2, V_DIM, tq), F32),
                        pltpu.VMEM((2, tk, tq), F32), pltpu.VMEM((2, tk, tq), F32),
                        pltpu.VMEM((2, 1, tq), F32), pltpu.VMEM((2, 1, tq), F32)],
        compiler_params=pltpu.CompilerParams(
            dimension_semantics=("arbitrary", "arbitrary", "arbitrary"),
            vmem_limit_bytes=V7X_VMEM_LIMIT),
        name="diff_attention",
    )(lam_params, qt, k, vt, km, vmt, bias_all, bmeta, gain)


def _mixer_kernel(x_ref, o_ref, u_ref, uh_ref, mh_ref, sga_ref, sgc_ref,
                  wap_ref, wpw_ref, wout_ref, dww_ref, dwb_ref, lng_ref, lnb_ref,
                  nffn_ref, wr_ref, br_ref, tri_ref,
                  h_ref, hn_ref, exp_ref, rank_ref, gate_ref, cnt_ref,
                  ext_s, y_s, cnt_s, *, tm, tiles_per_seq):
    i = pl.program_id(0)
    d = x_ref.shape[1]

    @pl.when(i == 0)
    def _():
        cnt_s[...] = jnp.zeros_like(cnt_s)

    first = (i % tiles_per_seq) == 0
    halo = jnp.where(first, mh_ref[...], uh_ref[...])
    ext_s[0, 0:HALO, :] = halo.astype(F32)
    ext_s[0, HALO:HALO + tm, :] = u_ref[...].astype(F32)
    span = tm + HALO - SUBLANES
    for b in range(1, SUBLANES):
        ext_s[b, 0:span, :] = ext_s[0, b:b + span, :]
    off = HALO - (CONV_K - 1)

    o = jnp.concatenate([o_ref[hd] for hd in range(N_HEADS)], axis=-1)
    y_attn = jnp.dot(o, wap_ref[...], preferred_element_type=F32)
    half = tm // 2
    hn_parts = []
    for r0 in (0, half):
        rs = slice(r0, r0 + half)
        for c in range(d // 128):
            cs = slice(c * 128, (c + 1) * 128)
            acc = jnp.zeros((half, 128), F32)
            for t in range(CONV_K):
                a, b = divmod(off + t, SUBLANES)
                row = a * SUBLANES + r0
                acc = acc + dww_ref[t:t + 1, cs] * ext_s[b, row:row + half, cs]
            y_s[rs, cs] = acc
        y = y_s[rs, :] + dwb_ref[...]
        mu = jnp.mean(y, axis=-1, keepdims=True)
        yc = y - mu
        var = jnp.mean(yc * yc, axis=-1, keepdims=True)
        y = yc * lax.rsqrt(var + EPS) * lng_ref[...] + lnb_ref[...]
        y = y * _sigmoid(y)
        y_conv = jnp.dot(y.astype(BF16), wpw_ref[...], preferred_element_type=F32)

        mixed = sga_ref[rs, :].astype(F32) * y_attn[rs] + sgc_ref[rs, :].astype(F32) * y_conv
        h = x_ref[rs, :] + jnp.dot(mixed.astype(BF16), wout_ref[...], preferred_element_type=F32)
        h_ref[rs, :] = h
        hn_parts.append(h * lax.rsqrt(jnp.mean(h * h, axis=-1, keepdims=True) + EPS) * nffn_ref[...])
    hn = jnp.concatenate(hn_parts, axis=0)
    _store_token_tiles(hn_ref, hn)

    nt_dims = (((1,), (1,)), ((), ()))
    hn_hi = hn.astype(BF16)
    hn_lo = (hn - hn_hi.astype(F32)).astype(BF16)
    both = lax.dot_general(wr_ref[...], hn_hi, nt_dims, preferred_element_type=F32)
    logits = (both[:N_EXPERTS] + both[N_EXPERTS:]
              + lax.dot_general(wr_ref[0:N_EXPERTS, :], hn_lo, nt_dims, preferred_element_type=F32)
              + br_ref[...])
    eio = lax.broadcasted_iota(jnp.int32, logits.shape, 0)
    sels, tops, idxs = [], [], []
    cur = logits
    for _ in range(TOP_K):
        m = jnp.max(cur, axis=0, keepdims=True)
        idx = jnp.min(jnp.where(cur == m, eio, N_EXPERTS), axis=0, keepdims=True)
        sel = eio == idx
        sels.append(sel)
        tops.append(m)
        idxs.append(idx)
        cur = jnp.where(sel, -jnp.inf, cur)
    exps = [jnp.exp(t - tops[0]) for t in tops]
    denom = exps[0] + exps[1] + exps[2] + exps[3]
    hot = jnp.zeros(logits.shape, F32)
    for sel in sels:
        hot = hot + jnp.where(sel, 1.0, 0.0)
    before = jnp.dot(hot.astype(BF16), tri_ref[...], preferred_element_type=F32) + cnt_s[...]
    ranks = [jnp.sum(jnp.where(sel, before, 0.0), axis=0, keepdims=True).astype(jnp.int32)
             for sel in sels]
    zi = jnp.zeros((ROUTE_ROWS - TOP_K, tm), jnp.int32)
    exp_ref[0] = jnp.concatenate(idxs + [zi], axis=0)
    rank_ref[0] = jnp.concatenate(ranks + [zi], axis=0)
    gate_ref[0] = jnp.concatenate([e / denom for e in exps] + [zi.astype(F32)], axis=0)
    cnt_s[...] = cnt_s[...] + jnp.sum(hot, axis=1, keepdims=True)
    cnt_ref[...] = jnp.broadcast_to(cnt_s[...], cnt_ref.shape)


def _mixer(x2d, o, u, sga, sgc, meta_halo, wap, wpw, wout, dww, dwb, lng, lnb, nffn, wr_t, br,
           seq, tm):
    n, d = x2d.shape
    nt = n // tm
    hb = tm // HALO
    tri = (jnp.arange(tm)[:, None] < jnp.arange(tm)[None, :]).astype(BF16)
    row = pl.BlockSpec((tm, d), lambda i: (i, 0))
    route = pl.BlockSpec((1, ROUTE_ROWS, tm), lambda i: (i, 0, 0))

    def full(shp):
        return pl.BlockSpec(shp, lambda i: tuple(0 for _ in shp))

    kern = functools.partial(_mixer_kernel, tm=tm, tiles_per_seq=seq // tm)
    return pl.pallas_call(
        kern,
        grid=(nt,),
        in_specs=[row, pl.BlockSpec((N_HEADS, tm, V_DIM), lambda i: (0, i, 0)), row,
                  pl.BlockSpec((HALO, d), lambda i: (jnp.maximum(i * hb - 1, 0), 0)),
                  full((HALO, d)), row, row,
                  full((d, d)), full((d, d)), full((d, d)),
                  full((CONV_K, d)), full((1, d)), full((1, d)), full((1, d)), full((1, d)),
                  full((2 * N_EXPERTS, d)), full((N_EXPERTS, 1)), full((tm, tm))],
        out_specs=[row, pl.BlockSpec((tm * SUBLANES, LANES), lambda i: (i, 0)),
                   route, route, route, full((N_EXPERTS, 128))],
        out_shape=[jax.ShapeDtypeStruct((n, d), F32),
                   jax.ShapeDtypeStruct((n * SUBLANES, LANES), F32),
                   jax.ShapeDtypeStruct((nt, ROUTE_ROWS, tm), jnp.int32),
                   jax.ShapeDtypeStruct((nt, ROUTE_ROWS, tm), jnp.int32),
                   jax.ShapeDtypeStruct((nt, ROUTE_ROWS, tm), F32),
                   jax.ShapeDtypeStruct((N_EXPERTS, 128), F32)],
        scratch_shapes=[pltpu.VMEM((SUBLANES, HALO + tm, d), F32), pltpu.VMEM((tm, d), F32),
                        pltpu.VMEM((N_EXPERTS, 1), F32)],
        compiler_params=pltpu.CompilerParams(
            dimension_semantics=("arbitrary",), vmem_limit_bytes=V7X_VMEM_LIMIT),
        name="mixer_router",
    )(x2d, o, u, u, meta_halo, sga, sgc, wap, wpw, wout, dww, dwb, lng, lnb, nffn, wr_t, br, tri)


def _prefetch_routes(dest_hbm, dsm_s, sem_idx):
    i = pl.program_id(0)
    slot = i % 2

    def idx_copy(tile, s):
        return pltpu.make_async_copy(dest_hbm.at[tile], dsm_s.at[s], sem_idx.at[s])

    @pl.when(i == 0)
    def _():
        idx_copy(0, 0).start()

    idx_copy(i, slot).wait()

    @pl.when(i + 1 < pl.num_programs(0))
    def _():
        idx_copy(i + 1, 1 - slot).start()

    return slot


def _dispatch_kernel(dest_hbm, hn_ref, xs_hbm, dsm_s, sem_idx, sem_rows, *, tm):
    slot = _prefetch_routes(dest_hbm, dsm_s, sem_idx)

    def issue(r, c):
        src = hn_ref.at[pl.ds(pl.multiple_of(r * SUBLANES, SUBLANES), SUBLANES), :]
        for kk in range(TOP_K):
            dst = pl.multiple_of(dsm_s[slot, kk, r] * SUBLANES, SUBLANES)
            pltpu.make_async_copy(src, xs_hbm.at[pl.ds(dst, SUBLANES), :],
                                  sem_rows).start(priority=kk % 2)
        return c
    lax.fori_loop(0, tm, issue, 0, unroll=8)

    for kk in range(TOP_K):
        pltpu.make_async_copy(hn_ref, xs_hbm.at[pl.ds(0, tm * SUBLANES), :], sem_rows).wait()


def _dispatch(dest, hn, n_slots, tm):
    n = hn.shape[0] // SUBLANES
    return pl.pallas_call(
        functools.partial(_dispatch_kernel, tm=tm),
        grid=(n // tm,),
        in_specs=[pl.BlockSpec(memory_space=pl.ANY),
                  pl.BlockSpec((tm * SUBLANES, LANES), lambda i: (i, 0))],
        out_specs=pl.BlockSpec(memory_space=pl.ANY),
        out_shape=jax.ShapeDtypeStruct((n_slots * SUBLANES, LANES), F32),
        scratch_shapes=[pltpu.SMEM((2, ROUTE_ROWS, tm), jnp.int32),
                        pltpu.SemaphoreType.DMA((2,)), pltpu.SemaphoreType.DMA(())],
        compiler_params=pltpu.CompilerParams(
            dimension_semantics=("arbitrary",), vmem_limit_bytes=V7X_VMEM_LIMIT),
        name="dispatch",
    )(dest, hn)


def _expert_kernel(bexp_ref, bvalid_ref, xs_ref, wgu_ref, bgu_ref, wdn_ref, bdn_ref, ys_ref,
                   wgu_bf, wdn_bf):
    i = pl.program_id(0)
    valid = bvalid_ref[i]
    f = wdn_ref.shape[1]

    @pl.when((i == 0) | (bexp_ref[i] != bexp_ref[jnp.maximum(i - 1, 0)]))
    def _():
        wgu_bf[...] = wgu_ref[0].astype(BF16)
        wdn_bf[...] = wdn_ref[0].astype(BF16)

    @pl.when(valid > 0)
    def _():
        x = _load_token_tiles(xs_ref, EXPERT_BLOCK)
        rows = lax.broadcasted_iota(jnp.int32, x.shape, 0)
        xb = jnp.where(rows < valid, x, 0.0).astype(BF16)
        gu = jnp.dot(xb, wgu_bf[...], preferred_element_type=F32) + bgu_ref[0]
        glu = jnp.minimum(gu[:, :f], SWIGLU_LIMIT)
        lin = jnp.clip(gu[:, f:], -SWIGLU_LIMIT, SWIGLU_LIMIT)
        act = glu * _sigmoid(SWIGLU_ALPHA * glu) * (lin + 1.0)
        y = jnp.dot(act.astype(BF16), wdn_bf[...], preferred_element_type=F32) + bdn_ref[0]
        _store_token_tiles(ys_ref, y)

    @pl.when(valid <= 0)
    def _():
        ys_ref[...] = jnp.zeros_like(ys_ref)


def _experts(block_exp, block_valid, xs, wgu, bgu, wdn, bdn):
    d = wgu.shape[1]
    n_slots = xs.shape[0] // SUBLANES
    nb = n_slots // EXPERT_BLOCK
    f2 = wgu.shape[2]
    f = wdn.shape[1]
    tile_rows = EXPERT_BLOCK * SUBLANES
    grid_spec = pltpu.PrefetchScalarGridSpec(
        num_scalar_prefetch=2,
        grid=(nb,),
        in_specs=[pl.BlockSpec((tile_rows, LANES), lambda i, be, bv: (i, 0)),
                  pl.BlockSpec((1, d, f2), lambda i, be, bv: (be[i], 0, 0)),
                  pl.BlockSpec((1, 1, f2), lambda i, be, bv: (be[i], 0, 0)),
                  pl.BlockSpec((1, f, d), lambda i, be, bv: (be[i], 0, 0)),
                  pl.BlockSpec((1, 1, d), lambda i, be, bv: (be[i], 0, 0))],
        out_specs=pl.BlockSpec((tile_rows, LANES), lambda i, be, bv: (i, 0)),
        scratch_shapes=[pltpu.VMEM((d, f2), BF16), pltpu.VMEM((f, d), BF16)],
    )
    return pl.pallas_call(
        _expert_kernel,
        grid_spec=grid_spec,
        out_shape=jax.ShapeDtypeStruct(xs.shape, F32),
        compiler_params=pltpu.CompilerParams(
            dimension_semantics=("arbitrary",), vmem_limit_bytes=V7X_VMEM_LIMIT),
        name="experts",
    )(block_exp, block_valid, xs, wgu, bgu, wdn, bdn)


def _combine_kernel(dest_hbm, h_ref, gate_ref, g_ref, ys_hbm, out_ref, dsm_s, ybuf, sem_idx, sem_rows,
                    *, tm):
    i = pl.program_id(0)
    nsteps = pl.num_programs(0)
    slot = i % 2

    def idx_copy(tile, s):
        return pltpu.make_async_copy(dest_hbm.at[tile], dsm_s.at[s], sem_idx.at[s])

    def issue_rows(s):
        def issue(r, c):
            for kk in range(TOP_K):
                src = pl.multiple_of(dsm_s[s, kk, r] * SUBLANES, SUBLANES)
                dst = pl.multiple_of(r * SUBLANES, SUBLANES)
                pltpu.make_async_copy(ys_hbm.at[pl.ds(src, SUBLANES), :],
                                      ybuf.at[s, kk, pl.ds(dst, SUBLANES), :],
                                      sem_rows.at[s]).start(priority=kk % 2)
            return c
        lax.fori_loop(0, tm, issue, 0, unroll=8)

    @pl.when(i == 0)
    def _():
        idx_copy(0, 0).start()
        idx_copy(0, 0).wait()
        issue_rows(0)

        @pl.when(nsteps > 1)
        def _():
            idx_copy(1, 1).start()

    @pl.when(i + 1 < nsteps)
    def _():
        idx_copy(i + 1, 1 - slot).wait()
        issue_rows(1 - slot)

        @pl.when(i + 2 < nsteps)
        def _():
            idx_copy(i + 2, slot).start()

    for kk in range(TOP_K):
        pltpu.make_async_copy(ys_hbm.at[pl.ds(0, tm * SUBLANES), :], ybuf.at[slot, kk],
                              sem_rows.at[slot]).wait()

    gates = jnp.transpose(gate_ref[0])
    acc = h_ref[...]
    for kk in range(TOP_K):
        acc = acc + gates[:, kk:kk + 1] * _load_token_tiles(ybuf.at[slot, kk], tm)
    out_ref[...] = acc * lax.rsqrt(jnp.mean(acc * acc, axis=-1, keepdims=True) + EPS) * g_ref[...]


def _combine(dest, h, gates, g_final, ys, tm):
    n, d = h.shape
    return pl.pallas_call(
        functools.partial(_combine_kernel, tm=tm),
        grid=(n // tm,),
        in_specs=[pl.BlockSpec(memory_space=pl.ANY),
                  pl.BlockSpec((tm, d), lambda i: (i, 0)),
                  pl.BlockSpec((1, ROUTE_ROWS, tm), lambda i: (i, 0, 0)),
                  pl.BlockSpec((1, d), lambda i: (0, 0)),
                  pl.BlockSpec(memory_space=pl.ANY)],
        out_specs=pl.BlockSpec((tm, d), lambda i: (i, 0)),
        out_shape=jax.ShapeDtypeStruct((n, d), F32),
        scratch_shapes=[pltpu.SMEM((2, ROUTE_ROWS, tm), jnp.int32),
                        pltpu.VMEM((2, TOP_K, tm * SUBLANES, LANES), F32),
                        pltpu.SemaphoreType.DMA((2,)), pltpu.SemaphoreType.DMA((2,))],
        compiler_params=pltpu.CompilerParams(
            dimension_semantics=("arbitrary",), vmem_limit_bytes=V7X_VMEM_LIMIT),
        name="combine",
    )(dest, h, gates, g_final, ys)


def kernel(x, meta_tokens, norm_mix, w_in, b_in, lam_params, subln_gain, w_attn_proj, conv_dw_w,
           conv_dw_b, conv_ln_g, conv_ln_b, w_conv_proj, w_out, norm_ffn, w_router, b_router,
           w_gate_up, b_gate_up, w_down, b_down, norm_final):
    bsz, seq, d = x.shape
    assert norm_mix.shape[0] == 1, "single-layer block"
    n = bsz * seq
    tm_proj = min(512, seq)
    tq = min(512, seq)
    tm = min(256, seq)
    lam_init = 0.8 - 0.6 * 1.0

    x2d = x.reshape(n, d)
    w_rest = jnp.concatenate([w_in[0][:, d:2 * d], w_in[0][:, 3 * d:]], axis=1).astype(BF16)
    b_rest = jnp.concatenate([b_in[0][d:2 * d], b_in[0][3 * d:]])[None]
    w_t = jnp.concatenate([w_in[0][:, :d], w_in[0][:, 2 * d:3 * d]], axis=1).T.astype(BF16)
    b_t = jnp.concatenate([b_in[0][:d], b_in[0][2 * d:3 * d]])[:, None]
    g_mix = norm_mix[0][None]
    qt, k, vt, u, sga, sgc = _inproj(x2d, g_mix, w_rest, b_rest, w_t, b_t, tm_proj)
    _, k_m, vt_m, u_m, _, _ = _inproj(meta_tokens.astype(x.dtype), g_mix, w_rest, b_rest, w_t, b_t,
                                      N_META)

    o = _attention(lam_params[0], qt, k, vt, k_m, vt_m, subln_gain[0][:, None], bsz, seq, tq,
                   lam_init)

    meta_halo = jnp.concatenate([jnp.zeros((HALO - N_META, d), BF16), u_m], axis=0)
    wr_t = w_router[0].T
    wr_hi = wr_t.astype(BF16)
    wr_split = jnp.concatenate([wr_hi, (wr_t - wr_hi.astype(F32)).astype(BF16)], axis=0)
    h, hn, eidx, rank, gates, counts = _mixer(
        x2d, o, u, sga, sgc, meta_halo,
        w_attn_proj[0].astype(BF16), w_conv_proj[0].astype(BF16), w_out[0].astype(BF16),
        conv_dw_w[0], conv_dw_b[0][None], conv_ln_g[0][None], conv_ln_b[0][None],
        norm_ffn[0][None], wr_split, b_router[0][:, None], seq, tm)

    cnt = counts[:, 0].astype(jnp.int32)
    padded = (cnt + EXPERT_BLOCK - 1) // EXPERT_BLOCK * EXPERT_BLOCK
    pad_end = jnp.cumsum(padded)
    pad_start = pad_end - padded
    n_blocks = -(-(n * TOP_K + N_EXPERTS * (EXPERT_BLOCK - 1)) // EXPERT_BLOCK)
    n_slots = n_blocks * EXPERT_BLOCK
    dest = pad_start[eidx] + rank
    block_start = jnp.arange(n_blocks, dtype=jnp.int32) * EXPERT_BLOCK
    block_exp = jnp.minimum(jnp.sum(block_start[:, None] >= pad_end[None, :], axis=1),
                            N_EXPERTS - 1).astype(jnp.int32)
    block_valid = jnp.clip(pad_start[block_exp] + cnt[block_exp] - block_start, 0,
                           EXPERT_BLOCK).astype(jnp.int32)

    xs = _dispatch(dest, hn, n_slots, tm)
    ys = _experts(block_exp, block_valid, xs, w_gate_up[0], b_gate_up[0][:, None, :],
                  w_down[0], b_down[0][:, None, :])
    out = _combine(dest, h, gates, norm_final[None], ys, tm)
    return out.reshape(bsz, seq, d)
```

```python
import functools

import jax
import jax.numpy as jnp
from jax import lax
from jax.experimental import pallas as pl
from jax.experimental.pallas import tpu as pltpu

CHUNK = 64
N_META = 16
N_HEADS = 8
HEAD_DIM = 64
V_DIM = 2 * HEAD_DIM
CONV_K = 31
N_EXPERTS = 32
TOP_K = 4
SWIGLU_ALPHA = 1.702
SWIGLU_LIMIT = 7.0
EXPERT_BLOCK = 256
EPS = 1e-5
NEG = -1e30
LOG2E = 1.4426950408889634

V7X_VMEM_LIMIT = 56 * 1024 * 1024
SUBLANES = 8
LANES = 128
META_PAD = 128
HALO = 32
ROUTE_ROWS = 8

BF16 = jnp.bfloat16
F32 = jnp.float32


def _sigmoid(x):
    return 1.0 / (1.0 + jnp.exp(-x))


def _load_token_tiles(ref, rows):
    return jnp.concatenate([ref[pl.ds(s, rows, stride=SUBLANES), :] for s in range(SUBLANES)],
                           axis=-1)


def _store_token_tiles(ref, x):
    rows = x.shape[0]
    for s in range(SUBLANES):
        ref[pl.ds(s, rows, stride=SUBLANES), :] = x[:, s * LANES:(s + 1) * LANES]


def _inproj_kernel(x_ref, g_ref, w_ref, b_ref, wt_ref, bt_ref,
                   qt_ref, k_ref, vt_ref, u_ref, sga_ref, sgc_ref):
    d = x_ref.shape[1]
    x = x_ref[...]
    xn = x * lax.rsqrt(jnp.mean(x * x, axis=-1, keepdims=True) + EPS) * g_ref[...]
    xb = xn.astype(BF16)

    def proj(c):
        return (jnp.dot(xb, w_ref[:, c * d:(c + 1) * d], preferred_element_type=F32)
                + b_ref[:, c * d:(c + 1) * d])

    kf = proj(0).astype(BF16)
    for hd in range(N_HEADS):
        k_ref[hd] = kf[:, hd * V_DIM:(hd + 1) * V_DIM]
    t = lax.dot_general(wt_ref[...], xb, (((1,), (1,)), ((), ())), preferred_element_type=F32)
    t = t + bt_ref[...]
    qt_ref[...] = (t[:d] * (HEAD_DIM ** -0.5 * LOG2E)).astype(BF16)
    vt_ref[...] = t[d:].astype(BF16)
    u_ref[...] = (proj(1) * _sigmoid(proj(2))).astype(BF16)
    sga_ref[...] = _sigmoid(proj(3)).astype(BF16)
    sgc_ref[...] = _sigmoid(proj(4)).astype(BF16)


def _inproj(x2d, g, w_bf, b, wt_bf, bt, tm):
    n, d = x2d.shape
    in_w = w_bf.shape[1]
    out = jax.ShapeDtypeStruct((n, d), BF16)
    row = pl.BlockSpec((tm, d), lambda i: (i, 0))
    col = pl.BlockSpec((d, tm), lambda i: (0, i))
    heads = pl.BlockSpec((N_HEADS, tm, V_DIM), lambda i: (0, i, 0))
    out_heads = jax.ShapeDtypeStruct((N_HEADS, n, V_DIM), BF16)
    out_t = jax.ShapeDtypeStruct((d, n), BF16)
    return pl.pallas_call(
        _inproj_kernel,
        grid=(n // tm,),
        in_specs=[row,
                  pl.BlockSpec((1, d), lambda i: (0, 0)),
                  pl.BlockSpec((d, in_w), lambda i: (0, 0)),
                  pl.BlockSpec((1, in_w), lambda i: (0, 0)),
                  pl.BlockSpec((2 * d, d), lambda i: (0, 0)),
                  pl.BlockSpec((2 * d, 1), lambda i: (0, 0))],
        out_specs=[col, heads, col, row, row, row],
        out_shape=[out_t, out_heads, out_t, out, out, out],
        compiler_params=pltpu.CompilerParams(
            dimension_semantics=("arbitrary",), vmem_limit_bytes=V7X_VMEM_LIMIT),
        name="inproj",
    )(x2d, g, w_bf, b, wt_bf, bt)


def _attn_kernel(lam_ref, qt_ref, k_ref, vt_ref, km_ref, vmt_ref, bias_ref, bmeta_ref,
                 gain_ref, o_ref, m_s, l_s, acc_s, s0, s1, mx0, mx1, *, tq, lam_init):
    h = pl.program_id(1)
    qi = pl.program_id(2)
    slope = jnp.exp2(-(h + 1).astype(F32))
    lp = lam_ref[...]
    lam = (jnp.exp(jnp.sum(lp[0:1] * lp[1:2], axis=-1, keepdims=True))
           - jnp.exp(jnp.sum(lp[2:3] * lp[3:4], axis=-1, keepdims=True)) + lam_init)

    qt = qt_ref[...]
    feat = lax.broadcasted_iota(jnp.int32, qt.shape, 0)
    zero = jnp.zeros_like(qt)
    qs = (jnp.where(feat < HEAD_DIM, qt, zero), jnp.where(feat >= HEAD_DIM, qt, zero))

    def scores_t(kk, qm):
        return jnp.dot(kk, qm, preferred_element_type=F32)

    tk = tq // 2
    n_off = 2 * qi

    def tile_shift(t):
        off = (qi * tq - t * tk).astype(F32) * LOG2E
        return jnp.where(t < n_off, slope * off, 0.0)

    def stage_a(t, s_buf, mx_buf, cols=slice(None)):
        kk = k_ref[0, pl.ds(pl.multiple_of(t * tk, tk), tk), :]
        bias = bias_ref[0, jnp.where(t < n_off, 0, t - n_off + 1), :, cols]
        for idx in range(2):
            s = scores_t(kk, qs[idx][:, cols]) + bias
            s_buf[idx, :, cols] = s
            mx_buf[idx, :, cols] = jnp.max(s, axis=0, keepdims=True)

    def stage_b(t, s_buf, mx_buf, cols=slice(None), meta=None):
        vt = vt_ref[:, pl.ds(pl.multiple_of(t * tk, tk), tk)]
        vt1 = jnp.concatenate([vt, jnp.ones((2 * SUBLANES, tk), BF16)], axis=0)
        shift = tile_shift(t)
        for idx in range(2):
            m_old = m_s[idx, :, cols]
            m_new = jnp.maximum(m_old, mx_buf[idx, :, cols] - shift)
            if meta is not None:
                s_meta, vmt1, shift_m = meta
                m_new = jnp.maximum(m_new, jnp.max(s_meta[idx], axis=0, keepdims=True) - shift_m)
            p = jnp.exp2(s_buf[idx, :, cols] - (m_new + shift))
            pv1 = jnp.dot(vt1, p.astype(BF16), preferred_element_type=F32)
            if meta is not None:
                pm = jnp.exp2(s_meta[idx] - (m_new + shift_m))
                pv1 = pv1 + jnp.dot(vmt1, pm.astype(BF16), preferred_element_type=F32)
            pv = pv1[:V_DIM]
            ps = pv1[V_DIM:V_DIM + 1]
            alpha = jnp.exp2(m_old - m_new)
            l_s[idx, :, cols] = alpha * l_s[idx, :, cols] + ps
            acc_s[idx, :, cols] = alpha * acc_s[idx, :, cols] + pv
            m_s[idx, :, cols] = m_new

    m_s[...] = jnp.full(m_s.shape, NEG, F32)
    l_s[...] = jnp.zeros_like(l_s)
    acc_s[...] = jnp.zeros_like(acc_s)
    stage_a(0, s0, mx0)

    def pair(i):
        stage_a(2 * i + 1, s1, mx1)
        stage_b(2 * i, s0, mx0)
        stage_a(2 * i + 2, s0, mx0)
        stage_b(2 * i + 1, s1, mx1)

    def body(ii, carry):
        pair(2 * ii)
        pair(2 * ii + 1)
        return carry

    lax.fori_loop(0, qi // 2, body, 0)

    @pl.when(qi % 2 == 1)
    def _():
        pair(qi - 1)

    late = slice(tq // 2, tq)
    stage_a(n_off + 1, s1, mx1, late)
    s_meta = [scores_t(km_ref[0], qs[idx]) + bmeta_ref[0] for idx in range(2)]
    vmt1 = jnp.concatenate([vmt_ref[...], jnp.ones((2 * SUBLANES, META_PAD), BF16)], axis=0)
    shift0 = slope * ((qi * tq).astype(F32) * LOG2E)
    stage_b(n_off, s0, mx0, meta=(s_meta, vmt1, shift0))
    stage_b(n_off + 1, s1, mx1, late)

    o = acc_s[0] / l_s[0] - lam * (acc_s[1] / l_s[1])
    o = o * lax.rsqrt(jnp.mean(o * o, axis=0, keepdims=True) + EPS) * gain_ref[...]
    o_ref[0] = jnp.transpose(o * (1.0 - lam_init)).astype(BF16)


def _attention(lam_params, qt, k, vt, k_meta, vt_meta, gain, bsz, seq, tq, lam_init):
    n = qt.shape[1]
    nq = seq // tq
    slopes = jnp.asarray([2.0 ** (-8.0 * (h + 1) / N_HEADS) for h in range(N_HEADS)], F32)
    tk = tq // 2
    r = jnp.arange(tq, dtype=jnp.int32)
    diff = r[None, :] - r[:, None]
    slopes = slopes * LOG2E
    boff = -slopes[:, None, None] * diff.astype(F32)[None, :tk]
    vis = (r[:, None] // CHUNK) <= (r[None, :] // CHUNK)
    bdiag = jnp.where(vis[None], -slopes[:, None, None] * jnp.abs(diff).astype(F32)[None], NEG)
    bias_all = jnp.stack([boff, bdiag[:, :tk], bdiag[:, tk:]], axis=1)
    mrow = jnp.arange(META_PAD, dtype=jnp.int32)
    dmeta = (N_META + r[None, :] - mrow[:, None]).astype(F32)
    bmeta = jnp.where((mrow < N_META)[None, :, None], -slopes[:, None, None] * dmeta[None], NEG)

    km = jnp.pad(k_meta, ((0, 0), (0, META_PAD - N_META), (0, 0)))
    vmt = jnp.pad(vt_meta, ((0, 0), (0, META_PAD - N_META)))

    kern = functools.partial(_attn_kernel, tq=tq, lam_init=lam_init)
    return pl.pallas_call(
        kern,
        grid=(bsz, N_HEADS, nq),
        in_specs=[
            pl.BlockSpec((4, HEAD_DIM), lambda b, h, i: (0, 0)),
            pl.BlockSpec((V_DIM, tq), lambda b, h, i: (h, b * nq + i)),
            pl.BlockSpec((1, seq, V_DIM), lambda b, h, i: (h, b, 0)),
            pl.BlockSpec((V_DIM, seq), lambda b, h, i: (h, b)),
            pl.BlockSpec((1, META_PAD, V_DIM), lambda b, h, i: (h, 0, 0)),
            pl.BlockSpec((V_DIM, META_PAD), lambda b, h, i: (h, 0)),
            pl.BlockSpec((1, 3, tk, tq), lambda b, h, i: (h, 0, 0, 0)),
            pl.BlockSpec((1, META_PAD, tq), lambda b, h, i: (h, 0, 0)),
            pl.BlockSpec((V_DIM, 1), lambda b, h, i: (0, 0)),
        ],
        out_specs=pl.BlockSpec((1, tq, V_DIM), lambda b, h, i: (h, b * nq + i, 0)),
        out_shape=jax.ShapeDtypeStruct((N_HEADS, n, V_DIM), BF16),
        scratch_shapes=[pltpu.VMEM((2, 1, tq), F32), pltpu.VMEM((2, 1, tq), F32),
                        pltpu.VMEM((2, V_DIM, tq), F32),
                        pltpu.VMEM((2, tk, tq), F32), pltpu.VMEM((2, tk, tq), F32),
                        pltpu.VMEM((2, 1, tq), F32), pltpu.VMEM((2, 1, tq), F32)],
        compiler_params=pltpu.CompilerParams(
            dimension_semantics=("arbitrary", "arbitrary", "arbitrary"),
            vmem_limit_bytes=V7X_VMEM_LIMIT),
        name="diff_attention",
    )(lam_params, qt, k, vt, km, vmt, bias_all, bmeta, gain)


def _mixer_kernel(x_ref, o_ref, u_ref, uh_ref, mh_ref, sga_ref, sgc_ref,
                  wap_ref, wpw_ref, wout_ref, dww_ref, dwb_ref, lng_ref, lnb_ref,
                  nffn_ref, wr_ref, br_ref, tri_ref,
                  h_ref, hn_ref, exp_ref, rank_ref, gate_ref, cnt_ref,
                  ext_s, y_s, cnt_s, *, tm, tiles_per_seq):
    i = pl.program_id(0)
    d = x_ref.shape[1]

    @pl.when(i == 0)
    def _():
        cnt_s[...] = jnp.zeros_like(cnt_s)

    first = (i % tiles_per_seq) == 0
    halo = jnp.where(first, mh_ref[...], uh_ref[...])
    ext_s[0, 0:HALO, :] = halo.astype(F32)
    ext_s[0, HALO:HALO + tm, :] = u_ref[...].astype(F32)
    span = tm + HALO - SUBLANES
    for b in range(1, SUBLANES):
        ext_s[b, 0:span, :] = ext_s[0, b:b + span, :]
    off = HALO - (CONV_K - 1)

    o = jnp.concatenate([o_ref[hd] for hd in range(N_HEADS)], axis=-1)
    y_attn = jnp.dot(o, wap_ref[...], preferred_element_type=F32)
    half = tm // 2
    hn_parts = []
    for r0 in (0, half):
        rs = slice(r0, r0 + half)
        for c in range(d // 128):
            cs = slice(c * 128, (c + 1) * 128)
            acc = jnp.zeros((half, 128), F32)
            for t in range(CONV_K):
                a, b = divmod(off + t, SUBLANES)
                row = a * SUBLANES + r0
                acc = acc + dww_ref[t:t + 1, cs] * ext_s[b, row:row + half, cs]
            y_s[rs, cs] = acc
        y = y_s[rs, :] + dwb_ref[...]
        mu = jnp.mean(y, axis=-1, keepdims=True)
        yc = y - mu
        var = jnp.mean(yc * yc, axis=-1, keepdims=True)
        y = yc * lax.rsqrt(var + EPS) * lng_ref[...] + lnb_ref[...]
        y = y * _sigmoid(y)
        y_conv = jnp.dot(y.astype(BF16), wpw_ref[...], preferred_element_type=F32)

        mixed = sga_ref[rs, :].astype(F32) * y_attn[rs] + sgc_ref[rs, :].astype(F32) * y_conv
        h = x_ref[rs, :] + jnp.dot(mixed.astype(BF16), wout_ref[...], preferred_element_type=F32)
        h_ref[rs, :] = h
        hn_parts.append(h * lax.rsqrt(jnp.mean(h * h, axis=-1, keepdims=True) + EPS) * nffn_ref[...])
    hn = jnp.concatenate(hn_parts, axis=0)
    _store_token_tiles(hn_ref, hn)

    nt_dims = (((1,), (1,)), ((), ()))
    hn_hi = hn.astype(BF16)
    hn_lo = (hn - hn_hi.astype(F32)).astype(BF16)
    both = lax.dot_general(wr_ref[...], hn_hi, nt_dims, preferred_element_type=F32)
    logits = (both[:N_EXPERTS] + both[N_EXPERTS:]
              + lax.dot_general(wr_ref[0:N_EXPERTS, :], hn_lo, nt_dims, preferred_element_type=F32)
              + br_ref[...])
    eio = lax.broadcasted_iota(jnp.int32, logits.shape, 0)
    sels, tops, idxs = [], [], []
    cur = logits
    for _ in range(TOP_K):
        m = jnp.max(cur, axis=0, keepdims=True)
        idx = jnp.min(jnp.where(cur == m, eio, N_EXPERTS), axis=0, keepdims=True)
        sel = eio == idx
        sels.append(sel)
        tops.append(m)
        idxs.append(idx)
        cur = jnp.where(sel, -jnp.inf, cur)
    exps = [jnp.exp(t - tops[0]) for t in tops]
    denom = exps[0] + exps[1] + exps[2] + exps[3]
    hot = jnp.zeros(logits.shape, F32)
    for sel in sels:
        hot = hot + jnp.where(sel, 1.0, 0.0)
    before = jnp.dot(hot.astype(BF16), tri_ref[...], preferred_element_type=F32) + cnt_s[...]
    ranks = [jnp.sum(jnp.where(sel, before, 0.0), axis=0, keepdims=True).astype(jnp.int32)
             for sel in sels]
    zi = jnp.zeros((ROUTE_ROWS - TOP_K, tm), jnp.int32)
    exp_ref[0] = jnp.concatenate(idxs + [zi], axis=0)
    rank_ref[0] = jnp.concatenate(ranks + [zi], axis=0)
    gate_ref[0] = jnp.concatenate([e / denom for e in exps] + [zi.astype(F32)], axis=0)
    cnt_s[...] = cnt_s[...] + jnp.sum(hot, axis=1, keepdims=True)
    cnt_ref[...] = jnp.broadcast_to(cnt_s[...], cnt_ref.shape)


def _mixer(x2d, o, u, sga, sgc, meta_halo, wap, wpw, wout, dww, dwb, lng, lnb, nffn, wr_t, br,
           seq, tm):
    n, d = x2d.shape
    nt = n // tm
    hb = tm // HALO
    tri = (jnp.arange(tm)[:, None] < jnp.arange(tm)[None, :]).astype(BF16)
    row = pl.BlockSpec((tm, d), lambda i: (i, 0))
    route = pl.BlockSpec((1, ROUTE_ROWS, tm), lambda i: (i, 0, 0))

    def full(shp):
        return pl.BlockSpec(shp, lambda i: tuple(0 for _ in shp))

    kern = functools.partial(_mixer_kernel, tm=tm, tiles_per_seq=seq // tm)
    return pl.pallas_call(
        kern,
        grid=(nt,),
        in_specs=[row, pl.BlockSpec((N_HEADS, tm, V_DIM), lambda i: (0, i, 0)), row,
                  pl.BlockSpec((HALO, d), lambda i: (jnp.maximum(i * hb - 1, 0), 0)),
                  full((HALO, d)), row, row,
                  full((d, d)), full((d, d)), full((d, d)),
                  full((CONV_K, d)), full((1, d)), full((1, d)), full((1, d)), full((1, d)),
                  full((2 * N_EXPERTS, d)), full((N_EXPERTS, 1)), full((tm, tm))],
        out_specs=[row, pl.BlockSpec((tm * SUBLANES, LANES), lambda i: (i, 0)),
                   route, route, route, full((N_EXPERTS, 128))],
        out_shape=[jax.ShapeDtypeStruct((n, d), F32),
                   jax.ShapeDtypeStruct((n * SUBLANES, LANES), F32),
                   jax.ShapeDtypeStruct((nt, ROUTE_ROWS, tm), jnp.int32),
                   jax.ShapeDtypeStruct((nt, ROUTE_ROWS, tm), jnp.int32),
                   jax.ShapeDtypeStruct((nt, ROUTE_ROWS, tm), F32),
                   jax.ShapeDtypeStruct((N_EXPERTS, 128), F32)],
        scratch_shapes=[pltpu.VMEM((SUBLANES, HALO + tm, d), F32), pltpu.VMEM((tm, d), F32),
                        pltpu.VMEM((N_EXPERTS, 1), F32)],
        compiler_params=pltpu.CompilerParams(
            dimension_semantics=("arbitrary",), vmem_limit_bytes=V7X_VMEM_LIMIT),
        name="mixer_router",
    )(x2d, o, u, u, meta_halo, sga, sgc, wap, wpw, wout, dww, dwb, lng, lnb, nffn, wr_t, br, tri)


def _prefetch_routes(dest_hbm, dsm_s, sem_idx):
    i = pl.program_id(0)
    slot = i % 2

    def idx_copy(tile, s):
        return pltpu.make_async_copy(dest_hbm.at[tile], dsm_s.at[s], sem_idx.at[s])

    @pl.when(i == 0)
    def _():
        idx_copy(0, 0).start()

    idx_copy(i, slot).wait()

    @pl.when(i + 1 < pl.num_programs(0))
    def _():
        idx_copy(i + 1, 1 - slot).start()

    return slot


def _dispatch_kernel(dest_hbm, hn_ref, xs_hbm, dsm_s, sem_idx, sem_rows, *, tm):
    slot = _prefetch_routes(dest_hbm, dsm_s, sem_idx)

    def issue(r, c):
        src = hn_ref.at[pl.ds(pl.multiple_of(r * SUBLANES, SUBLANES), SUBLANES), :]
        for kk in range(TOP_K):
            dst = pl.multiple_of(dsm_s[slot, kk, r] * SUBLANES, SUBLANES)
            pltpu.make_async_copy(src, xs_hbm.at[pl.ds(dst, SUBLANES), :],
                                  sem_rows).start(priority=kk % 2)
        return c
    lax.fori_loop(0, tm, issue, 0, unroll=8)

    for kk in range(TOP_K):
        pltpu.make_async_copy(hn_ref, xs_hbm.at[pl.ds(0, tm * SUBLANES), :], sem_rows).wait()


def _dispatch(dest, hn, n_slots, tm):
    n = hn.shape[0] // SUBLANES
    return pl.pallas_call(
        functools.partial(_dispatch_kernel, tm=tm),
        grid=(n // tm,),
        in_specs=[pl.BlockSpec(memory_space=pl.ANY),
                  pl.BlockSpec((tm * SUBLANES, LANES), lambda i: (i, 0))],
        out_specs=pl.BlockSpec(memory_space=pl.ANY),
        out_shape=jax.ShapeDtypeStruct((n_slots * SUBLANES, LANES), F32),
        scratch_shapes=[pltpu.SMEM((2, ROUTE_ROWS, tm), jnp.int32),
                        pltpu.SemaphoreType.DMA((2,)), pltpu.SemaphoreType.DMA(())],
        compiler_params=pltpu.CompilerParams(
            dimension_semantics=("arbitrary",), vmem_limit_bytes=V7X_VMEM_LIMIT),
        name="dispatch",
    )(dest, hn)


def _expert_kernel(bexp_ref, bvalid_ref, xs_ref, wgu_ref, bgu_ref, wdn_ref, bdn_ref, ys_ref,
                   wgu_bf, wdn_bf):
    i = pl.program_id(0)
    valid = bvalid_ref[i]
    f = wdn_ref.shape[1]

    @pl.when((i == 0) | (bexp_ref[i] != bexp_ref[jnp.maximum(i - 1, 0)]))
    def _():
        wgu_bf[...] = wgu_ref[0].astype(BF16)
        wdn_bf[...] = wdn_ref[0].astype(BF16)

    @pl.when(valid > 0)
    def _():
        x = _load_token_tiles(xs_ref, EXPERT_BLOCK)
        rows = lax.broadcasted_iota(jnp.int32, x.shape, 0)
        xb = jnp.where(rows < valid, x, 0.0).astype(BF16)
        gu = jnp.dot(xb, wgu_bf[...], preferred_element_type=F32) + bgu_ref[0]
        glu = jnp.minimum(gu[:, :f], SWIGLU_LIMIT)
        lin = jnp.clip(gu[:, f:], -SWIGLU_LIMIT, SWIGLU_LIMIT)
        act = glu * _sigmoid(SWIGLU_ALPHA * glu) * (lin + 1.0)
        y = jnp.dot(act.astype(BF16), wdn_bf[...], preferred_element_type=F32) + bdn_ref[0]
        _store_token_tiles(ys_ref, y)

    @pl.when(valid <= 0)
    def _():
        ys_ref[...] = jnp.zeros_like(ys_ref)


def _experts(block_exp, block_valid, xs, wgu, bgu, wdn, bdn):
    d = wgu.shape[1]
    n_slots = xs.shape[0] // SUBLANES
    nb = n_slots // EXPERT_BLOCK
    f2 = wgu.shape[2]
    f = wdn.shape[1]
    tile_rows = EXPERT_BLOCK * SUBLANES
    grid_spec = pltpu.PrefetchScalarGridSpec(
        num_scalar_prefetch=2,
        grid=(nb,),
        in_specs=[pl.BlockSpec((tile_rows, LANES), lambda i, be, bv: (i, 0)),
                  pl.BlockSpec((1, d, f2), lambda i, be, bv: (be[i], 0, 0)),
                  pl.BlockSpec((1, 1, f2), lambda i, be, bv: (be[i], 0, 0)),
                  pl.BlockSpec((1, f, d), lambda i, be, bv: (be[i], 0, 0)),
                  pl.BlockSpec((1, 1, d), lambda i, be, bv: (be[i], 0, 0))],
        out_specs=pl.BlockSpec((tile_rows, LANES), lambda i, be, bv: (i, 0)),
        scratch_shapes=[pltpu.VMEM((d, f2), BF16), pltpu.VMEM((f, d), BF16)],
    )
    return pl.pallas_call(
        _expert_kernel,
        grid_spec=grid_spec,
        out_shape=jax.ShapeDtypeStruct(xs.shape, F32),
        compiler_params=pltpu.CompilerParams(
            dimension_semantics=("arbitrary",), vmem_limit_bytes=V7X_VMEM_LIMIT),
        name="experts",
    )(block_exp, block_valid, xs, wgu, bgu, wdn, bdn)


def _combine_kernel(dest_hbm, h_ref, gate_ref, g_ref, ys_hbm, out_ref, dsm_s, ybuf, sem_idx, sem_rows,
                    *, tm):
    i = pl.program_id(0)
    nsteps = pl.num_programs(0)
    slot = i % 2

    def idx_copy(tile, s):
        return pltpu.make_async_copy(dest_hbm.at[tile], dsm_s.at[s], sem_idx.at[s])

    def issue_rows(s):
        def issue(r, c):
            for kk in range(TOP_K):
                src = pl.multiple_of(dsm_s[s, kk, r] * SUBLANES, SUBLANES)
                dst = pl.multiple_of(r * SUBLANES, SUBLANES)
                pltpu.make_async_copy(ys_hbm.at[pl.ds(src, SUBLANES), :],
                                      ybuf.at[s, kk, pl.ds(dst, SUBLANES), :],
                                      sem_rows.at[s]).start(priority=kk % 2)
            return c
        lax.fori_loop(0, tm, issue, 0, unroll=8)

    @pl.when(i == 0)
    def _():
        idx_copy(0, 0).start()
        idx_copy(0, 0).wait()
        issue_rows(0)

        @pl.when(nsteps > 1)
        def _():
            idx_copy(1, 1).start()

    @pl.when(i + 1 < nsteps)
    def _():
        idx_copy(i + 1, 1 - slot).wait()
        issue_rows(1 - slot)

        @pl.when(i + 2 < nsteps)
        def _():
            idx_copy(i + 2, slot).start()

    for kk in range(TOP_K):
        pltpu.make_async_copy(ys_hbm.at[pl.ds(0, tm * SUBLANES), :], ybuf.at[slot, kk],
                              sem_rows.at[slot]).wait()

    gates = jnp.transpose(gate_ref[0])
    acc = h_ref[...]
    for kk in range(TOP_K):
        acc = acc + gates[:, kk:kk + 1] * _load_token_tiles(ybuf.at[slot, kk], tm)
    out_ref[...] = acc * lax.rsqrt(jnp.mean(acc * acc, axis=-1, keepdims=True) + EPS) * g_ref[...]


def _combine(dest, h, gates, g_final, ys, tm):
    n, d = h.shape
    return pl.pallas_call(
        functools.partial(_combine_kernel, tm=tm),
        grid=(n // tm,),
        in_specs=[pl.BlockSpec(memory_space=pl.ANY),
                  pl.BlockSpec((tm, d), lambda i: (i, 0)),
                  pl.BlockSpec((1, ROUTE_ROWS, tm), lambda i: (i, 0, 0)),
                  pl.BlockSpec((1, d), lambda i: (0, 0)),
                  pl.BlockSpec(memory_space=pl.ANY)],
        out_specs=pl.BlockSpec((tm, d), lambda i: (i, 0)),
        out_shape=jax.ShapeDtypeStruct((n, d), F32),
        scratch_shapes=[pltpu.SMEM((2, ROUTE_ROWS, tm), jnp.int32),
                        pltpu.VMEM((2, TOP_K, tm * SUBLANES, LANES), F32),
                        pltpu.SemaphoreType.DMA((2,)), pltpu.SemaphoreType.DMA((2,))],
        compiler_params=pltpu.CompilerParams(
            dimension_semantics=("arbitrary",), vmem_limit_bytes=V7X_VMEM_LIMIT),
        name="combine",
    )(dest, h, gates, g_final, ys)


def kernel(x, meta_tokens, norm_mix, w_in, b_in, lam_params, subln_gain, w_attn_proj, conv_dw_w,
           conv_dw_b, conv_ln_g, conv_ln_b, w_conv_proj, w_out, norm_ffn, w_router, b_router,
           w_gate_up, b_gate_up, w_down, b_down, norm_final):
    bsz, seq, d = x.shape
    assert norm_mix.shape[0] == 1, "single-layer block"
    n = bsz * seq
    tm_proj = min(512, seq)
    tq = min(512, seq)
    tm = min(256, seq)
    lam_init = 0.8 - 0.6 * 1.0

    x2d = x.reshape(n, d)
    w_rest = jnp.concatenate([w_in[0][:, d:2 * d], w_in[0][:, 3 * d:]], axis=1).astype(BF16)
    b_rest = jnp.concatenate([b_in[0][d:2 * d], b_in[0][3 * d:]])[None]
    w_t = jnp.concatenate([w_in[0][:, :d], w_in[0][:, 2 * d:3 * d]], axis=1).T.astype(BF16)
    b_t = jnp.concatenate([b_in[0][:d], b_in[0][2 * d:3 * d]])[:, None]
    g_mix = norm_mix[0][None]
    qt, k, vt, u, sga, sgc = _inproj(x2d, g_mix, w_rest, b_rest, w_t, b_t, tm_proj)
    _, k_m, vt_m, u_m, _, _ = _inproj(meta_tokens.astype(x.dtype), g_mix, w_rest, b_rest, w_t, b_t,
                                      N_META)

    o = _attention(lam_params[0], qt, k, vt, k_m, vt_m, subln_gain[0][:, None], bsz, seq, tq,
                   lam_init)

    meta_halo = jnp.concatenate([jnp.zeros((HALO - N_META, d), BF16), u_m], axis=0)
    wr_t = w_router[0].T
    wr_hi = wr_t.astype(BF16)
    wr_split = jnp.concatenate([wr_hi, (wr_t - wr_hi.astype(F32)).astype(BF16)], axis=0)
    h, hn, eidx, rank, gates, counts = _mixer(
        x2d, o, u, sga, sgc, meta_halo,
        w_attn_proj[0].astype(BF16), w_conv_proj[0].astype(BF16), w_out[0].astype(BF16),
        conv_dw_w[0], conv_dw_b[0][None], conv_ln_g[0][None], conv_ln_b[0][None],
        norm_ffn[0][None], wr_split, b_router[0][:, None], seq, tm)

    cnt = counts[:, 0].astype(jnp.int32)
    padded = (cnt + EXPERT_BLOCK - 1) // EXPERT_BLOCK * EXPERT_BLOCK
    pad_end = jnp.cumsum(padded)
    pad_start = pad_end - padded
    n_blocks = -(-(n * TOP_K + N_EXPERTS * (EXPERT_BLOCK - 1)) // EXPERT_BLOCK)
    n_slots = n_blocks * EXPERT_BLOCK
    dest = pad_start[eidx] + rank
    block_start = jnp.arange(n_blocks, dtype=jnp.int32) * EXPERT_BLOCK
    block_exp = jnp.minimum(jnp.sum(block_start[:, None] >= pad_end[None, :], axis=1),
                            N_EXPERTS - 1).astype(jnp.int32)
    block_valid = jnp.clip(pad_start[block_exp] + cnt[block_exp] - block_start, 0,
                           EXPERT_BLOCK).astype(jnp.int32)

    xs = _dispatch(dest, hn, n_slots, tm)
    ys = _experts(block_exp, block_valid, xs, w_gate_up[0], b_gate_up[0][:, None, :],
                  w_down[0], b_down[0][:, None, :])
    out = _combine(dest, h, gates, norm_final[None], ys, tm)
    return out.reshape(bsz, seq, d)
```

```python
import functools

import jax
import jax.numpy as jnp
from jax import lax
from jax.experimental import pallas as pl
from jax.experimental.pallas import tpu as pltpu

CHUNK = 64
N_META = 16
N_HEADS = 8
HEAD_DIM = 64
V_DIM = 2 * HEAD_DIM
CONV_K = 31
N_EXPERTS = 32
TOP_K = 4
SWIGLU_ALPHA = 1.702
SWIGLU_LIMIT = 7.0
EXPERT_BLOCK = 512
EPS = 1e-5
NEG = -1e30
LOG2E = 1.4426950408889634

V7X_VMEM_LIMIT = 56 * 1024 * 1024
SUBLANES = 8
LANES = 128
META_PAD = 128
HALO = 32
ROUTE_ROWS = 8

BF16 = jnp.bfloat16
F32 = jnp.float32


def _sigmoid(x):
    return 1.0 / (1.0 + jnp.exp(-x))


def _load_token_tiles(ref, rows):
    return jnp.concatenate([ref[pl.ds(s, rows, stride=SUBLANES), :] for s in range(SUBLANES)],
                           axis=-1)


def _store_token_tiles(ref, x):
    rows = x.shape[0]
    for s in range(SUBLANES):
        ref[pl.ds(s, rows, stride=SUBLANES), :] = x[:, s * LANES:(s + 1) * LANES]


def _inproj_kernel(x_ref, g_ref, w_ref, b_ref, wt_ref, bt_ref,
                   qt_ref, k_ref, vt_ref, u_ref, sga_ref, sgc_ref):
    d = x_ref.shape[1]
    x = x_ref[...]
    xn = x * lax.rsqrt(jnp.mean(x * x, axis=-1, keepdims=True) + EPS) * g_ref[...]
    xb = xn.astype(BF16)

    def proj(c):
        return (jnp.dot(xb, w_ref[:, c * d:(c + 1) * d], preferred_element_type=F32)
                + b_ref[:, c * d:(c + 1) * d])

    kf = proj(0).astype(BF16)
    for hd in range(N_HEADS):
        k_ref[hd] = kf[:, hd * V_DIM:(hd + 1) * V_DIM]
    t = lax.dot_general(wt_ref[...], xb, (((1,), (1,)), ((), ())), preferred_element_type=F32)
    t = t + bt_ref[...]
    qt_ref[...] = (t[:d] * (HEAD_DIM ** -0.5 * LOG2E)).astype(BF16)
    vt_ref[...] = t[d:].astype(BF16)
    u_ref[...] = (proj(1) * _sigmoid(proj(2))).astype(BF16)
    sga_ref[...] = _sigmoid(proj(3)).astype(BF16)
    sgc_ref[...] = _sigmoid(proj(4)).astype(BF16)


def _inproj(x2d, g, w_bf, b, wt_bf, bt, tm):
    n, d = x2d.shape
    in_w = w_bf.shape[1]
    out = jax.ShapeDtypeStruct((n, d), BF16)
    row = pl.BlockSpec((tm, d), lambda i: (i, 0))
    col = pl.BlockSpec((d, tm), lambda i: (0, i))
    heads = pl.BlockSpec((N_HEADS, tm, V_DIM), lambda i: (0, i, 0))
    out_heads = jax.ShapeDtypeStruct((N_HEADS, n, V_DIM), BF16)
    out_t = jax.ShapeDtypeStruct((d, n), BF16)
    return pl.pallas_call(
        _inproj_kernel,
        grid=(n // tm,),
        in_specs=[row,
                  pl.BlockSpec((1, d), lambda i: (0, 0)),
                  pl.BlockSpec((d, in_w), lambda i: (0, 0)),
                  pl.BlockSpec((1, in_w), lambda i: (0, 0)),
                  pl.BlockSpec((2 * d, d), lambda i: (0, 0)),
                  pl.BlockSpec((2 * d, 1), lambda i: (0, 0))],
        out_specs=[col, heads, col, row, row, row],
        out_shape=[out_t, out_heads, out_t, out, out, out],
        compiler_params=pltpu.CompilerParams(
            dimension_semantics=("arbitrary",), vmem_limit_bytes=V7X_VMEM_LIMIT),
        name="inproj",
    )(x2d, g, w_bf, b, wt_bf, bt)


def _attn_kernel(lam_ref, qt_ref, k_ref, vt_ref, km_ref, vmt_ref, bias_ref, bmeta_ref,
                 gain_ref, o_ref, m_s, l_s, acc_s, s0, s1, mx0, mx1, *, tq, lam_init):
    h = pl.program_id(1)
    qi = pl.program_id(2)
    slope = jnp.exp2(-(h + 1).astype(F32))
    lp = lam_ref[...]
    lam = (jnp.exp(jnp.sum(lp[0:1] * lp[1:2], axis=-1, keepdims=True))
           - jnp.exp(jnp.sum(lp[2:3] * lp[3:4], axis=-1, keepdims=True)) + lam_init)

    qt = qt_ref[...]
    feat = lax.broadcasted_iota(jnp.int32, qt.shape, 0)
    zero = jnp.zeros_like(qt)
    qs = (jnp.where(feat < HEAD_DIM, qt, zero), jnp.where(feat >= HEAD_DIM, qt, zero))

    def scores_t(kk, qm):
        return jnp.dot(kk, qm, preferred_element_type=F32)

    tk = tq // 2
    n_off = 2 * qi

    def tile_shift(t):
        off = (qi * tq - t * tk).astype(F32) * LOG2E
        return jnp.where(t < n_off, slope * off, 0.0)

    def stage_a(t, s_buf, mx_buf, cols=slice(None)):
        kk = k_ref[0, pl.ds(pl.multiple_of(t * tk, tk), tk), :]
        bias = bias_ref[0, jnp.where(t < n_off, 0, t - n_off + 1), :, cols]
        for idx in range(2):
            s = scores_t(kk, qs[idx][:, cols]) + bias
            s_buf[idx, :, cols] = s
            mx_buf[idx, :, cols] = jnp.max(s, axis=0, keepdims=True)

    def stage_b(t, s_buf, mx_buf, cols=slice(None), meta=None):
        vt = vt_ref[:, pl.ds(pl.multiple_of(t * tk, tk), tk)]
        vt1 = jnp.concatenate([vt, jnp.ones((2 * SUBLANES, tk), BF16)], axis=0)
        shift = tile_shift(t)
        for idx in range(2):
            m_old = m_s[idx, :, cols]
            m_new = jnp.maximum(m_old, mx_buf[idx, :, cols] - shift)
            if meta is not None:
                s_meta, vmt1, shift_m = meta
                m_new = jnp.maximum(m_new, jnp.max(s_meta[idx], axis=0, keepdims=True) - shift_m)
            p = jnp.exp2(s_buf[idx, :, cols] - (m_new + shift))
            pv1 = jnp.dot(vt1, p.astype(BF16), preferred_element_type=F32)
            if meta is not None:
                pm = jnp.exp2(s_meta[idx] - (m_new + shift_m))
                pv1 = pv1 + jnp.dot(vmt1, pm.astype(BF16), preferred_element_type=F32)
            pv = pv1[:V_DIM]
            ps = pv1[V_DIM:V_DIM + 1]
            alpha = jnp.exp2(m_old - m_new)
            l_s[idx, :, cols] = alpha * l_s[idx, :, cols] + ps
            acc_s[idx, :, cols] = alpha * acc_s[idx, :, cols] + pv
            m_s[idx, :, cols] = m_new

    m_s[...] = jnp.full(m_s.shape, NEG, F32)
    l_s[...] = jnp.zeros_like(l_s)
    acc_s[...] = jnp.zeros_like(acc_s)
    stage_a(0, s0, mx0)

    def pair(i):
        stage_a(2 * i + 1, s1, mx1)
        stage_b(2 * i, s0, mx0)
        stage_a(2 * i + 2, s0, mx0)
        stage_b(2 * i + 1, s1, mx1)

    def body(ii, carry):
        pair(2 * ii)
        pair(2 * ii + 1)
        return carry

    lax.fori_loop(0, qi // 2, body, 0)

    @pl.when(qi % 2 == 1)
    def _():
        pair(qi - 1)

    late = slice(tq // 2, tq)
    stage_a(n_off + 1, s1, mx1, late)
    s_meta = [scores_t(km_ref[0], qs[idx]) + bmeta_ref[0] for idx in range(2)]
    vmt1 = jnp.concatenate([vmt_ref[...], jnp.ones((2 * SUBLANES, META_PAD), BF16)], axis=0)
    shift0 = slope * ((qi * tq).astype(F32) * LOG2E)
    stage_b(n_off, s0, mx0, meta=(s_meta, vmt1, shift0))
    stage_b(n_off + 1, s1, mx1, late)

    o = acc_s[0] / l_s[0] - lam * (acc_s[1] / l_s[1])
    o = o * lax.rsqrt(jnp.mean(o * o, axis=0, keepdims=True) + EPS) * gain_ref[...]
    o_ref[0] = jnp.transpose(o * (1.0 - lam_init)).astype(BF16)


def _attention(lam_params, qt, k, vt, k_meta, vt_meta, gain, bsz, seq, tq, lam_init):
    n = qt.shape[1]
    nq = seq // tq
    slopes = jnp.asarray([2.0 ** (-8.0 * (h + 1) / N_HEADS) for h in range(N_HEADS)], F32)
    tk = tq // 2
    r = jnp.arange(tq, dtype=jnp.int32)
    diff = r[None, :] - r[:, None]
    slopes = slopes * LOG2E
    boff = -slopes[:, None, None] * diff.astype(F32)[None, :tk]
    vis = (r[:, None] // CHUNK) <= (r[None, :] // CHUNK)
    bdiag = jnp.where(vis[None], -slopes[:, None, None] * jnp.abs(diff).astype(F32)[None], NEG)
    bias_all = jnp.stack([boff, bdiag[:, :tk], bdiag[:, tk:]], axis=1)
    mrow = jnp.arange(META_PAD, dtype=jnp.int32)
    dmeta = (N_META + r[None, :] - mrow[:, None]).astype(F32)
    bmeta = jnp.where((mrow < N_META)[None, :, None], -slopes[:, None, None] * dmeta[None], NEG)

    km = jnp.pad(k_meta, ((0, 0), (0, META_PAD - N_META), (0, 0)))
    vmt = jnp.pad(vt_meta, ((0, 0), (0, META_PAD - N_META)))

    kern = functools.partial(_attn_kernel, tq=tq, lam_init=lam_init)
    return pl.pallas_call(
        kern,
        grid=(bsz, N_HEADS, nq),
        in_specs=[
            pl.BlockSpec((4, HEAD_DIM), lambda b, h, i: (0, 0)),
            pl.BlockSpec((V_DIM, tq), lambda b, h, i: (h, b * nq + i)),
            pl.BlockSpec((1, seq, V_DIM), lambda b, h, i: (h, b, 0)),
            pl.BlockSpec((V_DIM, seq), lambda b, h, i: (h, b)),
            pl.BlockSpec((1, META_PAD, V_DIM), lambda b, h, i: (h, 0, 0)),
            pl.BlockSpec((V_DIM, META_PAD), lambda b, h, i: (h, 0)),
            pl.BlockSpec((1, 3, tk, tq), lambda b, h, i: (h, 0, 0, 0)),
            pl.BlockSpec((1, META_PAD, tq), lambda b, h, i: (h, 0, 0)),
            pl.BlockSpec((V_DIM, 1), lambda b, h, i: (0, 0)),
        ],
        out_specs=pl.BlockSpec((1, tq, V_DIM), lambda b, h, i: (h, b * nq + i, 0)),
        out_shape=jax.ShapeDtypeStruct((N_HEADS, n, V_DIM), BF16),
        scratch_shapes=[pltpu.VMEM((2, 1, tq), F32), pltpu.VMEM((2, 1, tq), F32),
                        pltpu.VMEM((2, V_DIM, tq), F32),
                        pltpu.VMEM((2, tk, tq), F32), pltpu.VMEM((2, tk, tq), F32),
                        pltpu.VMEM((2, 1, tq), F32), pltpu.VMEM((2, 1, tq), F32)],
        compiler_params=pltpu.CompilerParams(
            dimension_semantics=("arbitrary", "arbitrary", "arbitrary"),
            vmem_limit_bytes=V7X_VMEM_LIMIT),
        name="diff_attention",
    )(lam_params, qt, k, vt, km, vmt, bias_all, bmeta, gain)


def _mixer_kernel(x_ref, o_ref, u_ref, uh_ref, mh_ref, sga_ref, sgc_ref,
                  wap_ref, wpw_ref, wout_ref, dww_ref, dwb_ref, lng_ref, lnb_ref,
                  nffn_ref, wr_ref, br_ref, tri_ref,
                  h_ref, hn_ref, exp_ref, rank_ref, gate_ref, cnt_ref,
                  ext_s, y_s, cnt_s, *, tm, tiles_per_seq):
    i = pl.program_id(0)
    d = x_ref.shape[1]

    @pl.when(i == 0)
    def _():
        cnt_s[...] = jnp.zeros_like(cnt_s)

    first = (i % tiles_per_seq) == 0
    halo = jnp.where(first, mh_ref[...], uh_ref[...])
    ext_s[0, 0:HALO, :] = halo.astype(F32)
    ext_s[0, HALO:HALO + tm, :] = u_ref[...].astype(F32)
    span = tm + HALO - SUBLANES
    for b in range(1, SUBLANES):
        ext_s[b, 0:span, :] = ext_s[0, b:b + span, :]
    off = HALO - (CONV_K - 1)

    o = jnp.concatenate([o_ref[hd] for hd in range(N_HEADS)], axis=-1)
    y_attn = jnp.dot(o, wap_ref[...], preferred_element_type=F32)
    for c in range(d // 128):
        cs = slice(c * 128, (c + 1) * 128)
        for r0 in (0, tm // 2):
            acc = jnp.zeros((tm // 2, 128), F32)
            for t in range(CONV_K):
                a, b = divmod(off + t, SUBLANES)
                row = a * SUBLANES + r0
                acc = acc + dww_ref[t:t + 1, cs] * ext_s[b, row:row + tm // 2, cs]
            y_s[r0:r0 + tm // 2, cs] = acc
    y = y_s[...] + dwb_ref[...]
    mu = jnp.mean(y, axis=-1, keepdims=True)
    yc = y - mu
    var = jnp.mean(yc * yc, axis=-1, keepdims=True)
    y = yc * lax.rsqrt(var + EPS) * lng_ref[...] + lnb_ref[...]
    y = y * _sigmoid(y)
    y_conv = jnp.dot(y.astype(BF16), wpw_ref[...], preferred_element_type=F32)

    mixed = sga_ref[...].astype(F32) * y_attn + sgc_ref[...].astype(F32) * y_conv
    h = x_ref[...] + jnp.dot(mixed.astype(BF16), wout_ref[...], preferred_element_type=F32)
    h_ref[...] = h
    hn = h * lax.rsqrt(jnp.mean(h * h, axis=-1, keepdims=True) + EPS) * nffn_ref[...]
    _store_token_tiles(hn_ref, hn)

    nt_dims = (((1,), (1,)), ((), ()))
    hn_hi = hn.astype(BF16)
    hn_lo = (hn - hn_hi.astype(F32)).astype(BF16)
    both = lax.dot_general(wr_ref[...], hn_hi, nt_dims, preferred_element_type=F32)
    logits = (both[:N_EXPERTS] + both[N_EXPERTS:]
              + lax.dot_general(wr_ref[0:N_EXPERTS, :], hn_lo, nt_dims, preferred_element_type=F32)
              + br_ref[...])
    eio = lax.broadcasted_iota(jnp.int32, logits.shape, 0)
    sels, tops, idxs = [], [], []
    cur = logits
    for _ in range(TOP_K):
        m = jnp.max(cur, axis=0, keepdims=True)
        idx = jnp.min(jnp.where(cur == m, eio, N_EXPERTS), axis=0, keepdims=True)
        sel = eio == idx
        sels.append(sel)
        tops.append(m)
        idxs.append(idx)
        cur = jnp.where(sel, -jnp.inf, cur)
    exps = [jnp.exp(t - tops[0]) for t in tops]
    denom = exps[0] + exps[1] + exps[2] + exps[3]
    hot = jnp.zeros(logits.shape, F32)
    for sel in sels:
        hot = hot + jnp.where(sel, 1.0, 0.0)
    before = jnp.dot(hot.astype(BF16), tri_ref[...], preferred_element_type=F32) + cnt_s[...]
    ranks = [jnp.sum(jnp.where(sel, before, 0.0), axis=0, keepdims=True).astype(jnp.int32)
             for sel in sels]
    zi = jnp.zeros((ROUTE_ROWS - TOP_K, tm), jnp.int32)
    exp_ref[0] = jnp.concatenate(idxs + [zi], axis=0)
    rank_ref[0] = jnp.concatenate(ranks + [zi], axis=0)
    gate_ref[0] = jnp.concatenate([e / denom for e in exps] + [zi.astype(F32)], axis=0)
    cnt_s[...] = cnt_s[...] + jnp.sum(hot, axis=1, keepdims=True)
    cnt_ref[...] = jnp.broadcast_to(cnt_s[...], cnt_ref.shape)


def _mixer(x2d, o, u, sga, sgc, meta_halo, wap, wpw, wout, dww, dwb, lng, lnb, nffn, wr_t, br,
           seq, tm):
    n, d = x2d.shape
    nt = n // tm
    hb = tm // HALO
    tri = (jnp.arange(tm)[:, None] < jnp.arange(tm)[None, :]).astype(BF16)
    row = pl.BlockSpec((tm, d), lambda i: (i, 0))
    route = pl.BlockSpec((1, ROUTE_ROWS, tm), lambda i: (i, 0, 0))

    def full(shp):
        return pl.BlockSpec(shp, lambda i: tuple(0 for _ in shp))

    kern = functools.partial(_mixer_kernel, tm=tm, tiles_per_seq=seq // tm)
    return pl.pallas_call(
        kern,
        grid=(nt,),
        in_specs=[row, pl.BlockSpec((N_HEADS, tm, V_DIM), lambda i: (0, i, 0)), row,
                  pl.BlockSpec((HALO, d), lambda i: (jnp.maximum(i * hb - 1, 0), 0)),
                  full((HALO, d)), row, row,
                  full((d, d)), full((d, d)), full((d, d)),
                  full((CONV_K, d)), full((1, d)), full((1, d)), full((1, d)), full((1, d)),
                  full((2 * N_EXPERTS, d)), full((N_EXPERTS, 1)), full((tm, tm))],
        out_specs=[row, pl.BlockSpec((tm * SUBLANES, LANES), lambda i: (i, 0)),
                   route, route, route, full((N_EXPERTS, 128))],
        out_shape=[jax.ShapeDtypeStruct((n, d), F32),
                   jax.ShapeDtypeStruct((n * SUBLANES, LANES), F32),
                   jax.ShapeDtypeStruct((nt, ROUTE_ROWS, tm), jnp.int32),
                   jax.ShapeDtypeStruct((nt, ROUTE_ROWS, tm), jnp.int32),
                   jax.ShapeDtypeStruct((nt, ROUTE_ROWS, tm), F32),
                   jax.ShapeDtypeStruct((N_EXPERTS, 128), F32)],
        scratch_shapes=[pltpu.VMEM((SUBLANES, HALO + tm, d), F32), pltpu.VMEM((tm, d), F32),
                        pltpu.VMEM((N_EXPERTS, 1), F32)],
        compiler_params=pltpu.CompilerParams(
            dimension_semantics=("arbitrary",), vmem_limit_bytes=V7X_VMEM_LIMIT),
        name="mixer_router",
    )(x2d, o, u, u, meta_halo, sga, sgc, wap, wpw, wout, dww, dwb, lng, lnb, nffn, wr_t, br, tri)


def _prefetch_routes(dest_hbm, dsm_s, sem_idx):
    i = pl.program_id(0)
    slot = i % 2

    def idx_copy(tile, s):
        return pltpu.make_async_copy(dest_hbm.at[tile], dsm_s.at[s], sem_idx.at[s])

    @pl.when(i == 0)
    def _():
        idx_copy(0, 0).start()

    idx_copy(i, slot).wait()

    @pl.when(i + 1 < pl.num_programs(0))
    def _():
        idx_copy(i + 1, 1 - slot).start()

    return slot


def _dispatch_kernel(dest_hbm, hn_ref, xs_hbm, dsm_s, sem_idx, sem_rows, *, tm):
    slot = _prefetch_routes(dest_hbm, dsm_s, sem_idx)

    def issue(r, c):
        src = hn_ref.at[pl.ds(pl.multiple_of(r * SUBLANES, SUBLANES), SUBLANES), :]
        for kk in range(TOP_K):
            dst = pl.multiple_of(dsm_s[slot, kk, r] * SUBLANES, SUBLANES)
            pltpu.make_async_copy(src, xs_hbm.at[pl.ds(dst, SUBLANES), :],
                                  sem_rows).start(priority=kk % 2)
        return c
    lax.fori_loop(0, tm, issue, 0, unroll=8)

    for kk in range(TOP_K):
        pltpu.make_async_copy(hn_ref, xs_hbm.at[pl.ds(0, tm * SUBLANES), :], sem_rows).wait()


def _dispatch(dest, hn, n_slots, tm):
    n = hn.shape[0] // SUBLANES
    return pl.pallas_call(
        functools.partial(_dispatch_kernel, tm=tm),
        grid=(n // tm,),
        in_specs=[pl.BlockSpec(memory_space=pl.ANY),
                  pl.BlockSpec((tm * SUBLANES, LANES), lambda i: (i, 0))],
        out_specs=pl.BlockSpec(memory_space=pl.ANY),
        out_shape=jax.ShapeDtypeStruct((n_slots * SUBLANES, LANES), F32),
        scratch_shapes=[pltpu.SMEM((2, ROUTE_ROWS, tm), jnp.int32),
                        pltpu.SemaphoreType.DMA((2,)), pltpu.SemaphoreType.DMA(())],
        compiler_params=pltpu.CompilerParams(
            dimension_semantics=("arbitrary",), vmem_limit_bytes=V7X_VMEM_LIMIT),
        name="dispatch",
    )(dest, hn)


def _expert_kernel(bexp_ref, bvalid_ref, xs_ref, wgu_ref, bgu_ref, wdn_ref, bdn_ref, ys_ref,
                   wgu_bf, wdn_bf):
    i = pl.program_id(0)
    valid = bvalid_ref[i]
    f = wdn_ref.shape[1]

    @pl.when((i == 0) | (bexp_ref[i] != bexp_ref[jnp.maximum(i - 1, 0)]))
    def _():
        wgu_bf[...] = wgu_ref[0].astype(BF16)
        wdn_bf[...] = wdn_ref[0].astype(BF16)

    @pl.when(valid > 0)
    def _():
        x = _load_token_tiles(xs_ref, EXPERT_BLOCK)
        rows = lax.broadcasted_iota(jnp.int32, x.shape, 0)
        xb = jnp.where(rows < valid, x, 0.0).astype(BF16)
        gu = jnp.dot(xb, wgu_bf[...], preferred_element_type=F32) + bgu_ref[0]
        glu = jnp.minimum(gu[:, :f], SWIGLU_LIMIT)
        lin = jnp.clip(gu[:, f:], -SWIGLU_LIMIT, SWIGLU_LIMIT)
        act = glu * _sigmoid(SWIGLU_ALPHA * glu) * (lin + 1.0)
        y = jnp.dot(act.astype(BF16), wdn_bf[...], preferred_element_type=F32) + bdn_ref[0]
        _store_token_tiles(ys_ref, y)

    @pl.when(valid <= 0)
    def _():
        ys_ref[...] = jnp.zeros_like(ys_ref)


def _experts(block_exp, block_valid, xs, wgu, bgu, wdn, bdn):
    d = wgu.shape[1]
    n_slots = xs.shape[0] // SUBLANES
    nb = n_slots // EXPERT_BLOCK
    f2 = wgu.shape[2]
    f = wdn.shape[1]
    tile_rows = EXPERT_BLOCK * SUBLANES
    grid_spec = pltpu.PrefetchScalarGridSpec(
        num_scalar_prefetch=2,
        grid=(nb,),
        in_specs=[pl.BlockSpec((tile_rows, LANES), lambda i, be, bv: (i, 0)),
                  pl.BlockSpec((1, d, f2), lambda i, be, bv: (be[i], 0, 0)),
                  pl.BlockSpec((1, 1, f2), lambda i, be, bv: (be[i], 0, 0)),
                  pl.BlockSpec((1, f, d), lambda i, be, bv: (be[i], 0, 0)),
                  pl.BlockSpec((1, 1, d), lambda i, be, bv: (be[i], 0, 0))],
        out_specs=pl.BlockSpec((tile_rows, LANES), lambda i, be, bv: (i, 0)),
        scratch_shapes=[pltpu.VMEM((d, f2), BF16), pltpu.VMEM((f, d), BF16)],
    )
    return pl.pallas_call(
        _expert_kernel,
        grid_spec=grid_spec,
        out_shape=jax.ShapeDtypeStruct(xs.shape, F32),
        compiler_params=pltpu.CompilerParams(
            dimension_semantics=("arbitrary",), vmem_limit_bytes=V7X_VMEM_LIMIT),
        name="experts",
    )(block_exp, block_valid, xs, wgu, bgu, wdn, bdn)


def _combine_kernel(dest_hbm, h_ref, gate_ref, g_ref, ys_hbm, out_ref, dsm_s, ybuf, sem_idx, sem_rows,
                    *, tm):
    i = pl.program_id(0)
    nsteps = pl.num_programs(0)
    slot = i % 2

    def idx_copy(tile, s):
        return pltpu.make_async_copy(dest_hbm.at[tile], dsm_s.at[s], sem_idx.at[s])

    def issue_rows(s):
        def issue(r, c):
            for kk in range(TOP_K):
                src = pl.multiple_of(dsm_s[s, kk, r] * SUBLANES, SUBLANES)
                dst = pl.multiple_of(r * SUBLANES, SUBLANES)
                pltpu.make_async_copy(ys_hbm.at[pl.ds(src, SUBLANES), :],
                                      ybuf.at[s, kk, pl.ds(dst, SUBLANES), :],
                                      sem_rows.at[s]).start(priority=kk % 2)
            return c
        lax.fori_loop(0, tm, issue, 0, unroll=8)

    @pl.when(i == 0)
    def _():
        idx_copy(0, 0).start()
        idx_copy(0, 0).wait()
        issue_rows(0)

        @pl.when(nsteps > 1)
        def _():
            idx_copy(1, 1).start()

    @pl.when(i + 1 < nsteps)
    def _():
        idx_copy(i + 1, 1 - slot).wait()
        issue_rows(1 - slot)

        @pl.when(i + 2 < nsteps)
        def _():
            idx_copy(i + 2, slot).start()

    for kk in range(TOP_K):
        pltpu.make_async_copy(ys_hbm.at[pl.ds(0, tm * SUBLANES), :], ybuf.at[slot, kk],
                              sem_rows.at[slot]).wait()

    gates = jnp.transpose(gate_ref[0])
    acc = h_ref[...]
    for kk in range(TOP_K):
        acc = acc + gates[:, kk:kk + 1] * _load_token_tiles(ybuf.at[slot, kk], tm)
    out_ref[...] = acc * lax.rsqrt(jnp.mean(acc * acc, axis=-1, keepdims=True) + EPS) * g_ref[...]


def _combine(dest, h, gates, g_final, ys, tm):
    n, d = h.shape
    return pl.pallas_call(
        functools.partial(_combine_kernel, tm=tm),
        grid=(n // tm,),
        in_specs=[pl.BlockSpec(memory_space=pl.ANY),
                  pl.BlockSpec((tm, d), lambda i: (i, 0)),
                  pl.BlockSpec((1, ROUTE_ROWS, tm), lambda i: (i, 0, 0)),
                  pl.BlockSpec((1, d), lambda i: (0, 0)),
                  pl.BlockSpec(memory_space=pl.ANY)],
        out_specs=pl.BlockSpec((tm, d), lambda i: (i, 0)),
        out_shape=jax.ShapeDtypeStruct((n, d), F32),
        scratch_shapes=[pltpu.SMEM((2, ROUTE_ROWS, tm), jnp.int32),
                        pltpu.VMEM((2, TOP_K, tm * SUBLANES, LANES), F32),
                        pltpu.SemaphoreType.DMA((2,)), pltpu.SemaphoreType.DMA((2,))],
        compiler_params=pltpu.CompilerParams(
            dimension_semantics=("arbitrary",), vmem_limit_bytes=V7X_VMEM_LIMIT),
        name="combine",
    )(dest, h, gates, g_final, ys)


def kernel(x, meta_tokens, norm_mix, w_in, b_in, lam_params, subln_gain, w_attn_proj, conv_dw_w,
           conv_dw_b, conv_ln_g, conv_ln_b, w_conv_proj, w_out, norm_ffn, w_router, b_router,
           w_gate_up, b_gate_up, w_down, b_down, norm_final):
    bsz, seq, d = x.shape
    assert norm_mix.shape[0] == 1, "single-layer block"
    n = bsz * seq
    tm_proj = min(512, seq)
    tq = min(512, seq)
    tm = min(256, seq)
    lam_init = 0.8 - 0.6 * 1.0

    x2d = x.reshape(n, d)
    w_rest = jnp.concatenate([w_in[0][:, d:2 * d], w_in[0][:, 3 * d:]], axis=1).astype(BF16)
    b_rest = jnp.concatenate([b_in[0][d:2 * d], b_in[0][3 * d:]])[None]
    w_t = jnp.concatenate([w_in[0][:, :d], w_in[0][:, 2 * d:3 * d]], axis=1).T.astype(BF16)
    b_t = jnp.concatenate([b_in[0][:d], b_in[0][2 * d:3 * d]])[:, None]
    g_mix = norm_mix[0][None]
    qt, k, vt, u, sga, sgc = _inproj(x2d, g_mix, w_rest, b_rest, w_t, b_t, tm_proj)
    _, k_m, vt_m, u_m, _, _ = _inproj(meta_tokens.astype(x.dtype), g_mix, w_rest, b_rest, w_t, b_t,
                                      N_META)

    o = _attention(lam_params[0], qt, k, vt, k_m, vt_m, subln_gain[0][:, None], bsz, seq, tq,
                   lam_init)

    meta_halo = jnp.concatenate([jnp.zeros((HALO - N_META, d), BF16), u_m], axis=0)
    wr_t = w_router[0].T
    wr_hi = wr_t.astype(BF16)
    wr_split = jnp.concatenate([wr_hi, (wr_t - wr_hi.astype(F32)).astype(BF16)], axis=0)
    h, hn, eidx, rank, gates, counts = _mixer(
        x2d, o, u, sga, sgc, meta_halo,
        w_attn_proj[0].astype(BF16), w_conv_proj[0].astype(BF16), w_out[0].astype(BF16),
        conv_dw_w[0], conv_dw_b[0][None], conv_ln_g[0][None], conv_ln_b[0][None],
        norm_ffn[0][None], wr_split, b_router[0][:, None], seq, tm)

    cnt = counts[:, 0].astype(jnp.int32)
    padded = (cnt + EXPERT_BLOCK - 1) // EXPERT_BLOCK * EXPERT_BLOCK
    pad_end = jnp.cumsum(padded)
    pad_start = pad_end - padded
    n_blocks = -(-(n * TOP_K + N_EXPERTS * (EXPERT_BLOCK - 1)) // EXPERT_BLOCK)
    n_slots = n_blocks * EXPERT_BLOCK
    dest = pad_start[eidx] + rank
    block_start = jnp.arange(n_blocks, dtype=jnp.int32) * EXPERT_BLOCK
    block_exp = jnp.minimum(jnp.sum(block_start[:, None] >= pad_end[None, :], axis=1),
                            N_EXPERTS - 1).astype(jnp.int32)
    block_valid = jnp.clip(pad_start[block_exp] + cnt[block_exp] - block_start, 0,
                           EXPERT_BLOCK).astype(jnp.int32)

    xs = _dispatch(dest, hn, n_slots, tm)
    ys = _experts(block_exp, block_valid, xs, w_gate_up[0], b_gate_up[0][:, None, :],
                  w_down[0], b_down[0][:, None, :])
    out = _combine(dest, h, gates, norm_final[None], ys, tm)
    return out.reshape(bsz, seq, d)
```

```python
import functools

import jax
import jax.numpy as jnp
from jax import lax
from jax.experimental import pallas as pl
from jax.experimental.pallas import tpu as pltpu

CHUNK = 64
N_META = 16
N_HEADS = 8
HEAD_DIM = 64
V_DIM = 2 * HEAD_DIM
CONV_K = 31
N_EXPERTS = 32
TOP_K = 4
SWIGLU_ALPHA = 1.702
SWIGLU_LIMIT = 7.0
EXPERT_BLOCK = 512
EPS = 1e-5
NEG = -1e30
LOG2E = 1.4426950408889634

V7X_VMEM_LIMIT = 56 * 1024 * 1024
SUBLANES = 8
LANES = 128
META_PAD = 128
HALO = 32
ROUTE_ROWS = 8

BF16 = jnp.bfloat16
F32 = jnp.float32


def _sigmoid(x):
    return 1.0 / (1.0 + jnp.exp(-x))


def _load_token_tiles(ref, rows):
    return jnp.concatenate([ref[pl.ds(s, rows, stride=SUBLANES), :] for s in range(SUBLANES)],
                           axis=-1)


def _store_token_tiles(ref, x):
    rows = x.shape[0]
    for s in range(SUBLANES):
        ref[pl.ds(s, rows, stride=SUBLANES), :] = x[:, s * LANES:(s + 1) * LANES]


def _inproj_kernel(x_ref, g_ref, w_ref, b_ref, wt_ref, bt_ref,
                   qt_ref, k_ref, vt_ref, u_ref, sga_ref, sgc_ref):
    d = x_ref.shape[1]
    x = x_ref[...]
    xn = x * lax.rsqrt(jnp.mean(x * x, axis=-1, keepdims=True) + EPS) * g_ref[...]
    xb = xn.astype(BF16)

    def proj(c):
        return (jnp.dot(xb, w_ref[:, c * d:(c + 1) * d], preferred_element_type=F32)
                + b_ref[:, c * d:(c + 1) * d])

    kf = proj(0).astype(BF16)
    for hd in range(N_HEADS):
        k_ref[hd] = kf[:, hd * V_DIM:(hd + 1) * V_DIM]
    t = lax.dot_general(wt_ref[...], xb, (((1,), (1,)), ((), ())), preferred_element_type=F32)
    t = t + bt_ref[...]
    qt_ref[...] = (t[:d] * (HEAD_DIM ** -0.5 * LOG2E)).astype(BF16)
    vt_ref[...] = t[d:].astype(BF16)
    u_ref[...] = (proj(1) * _sigmoid(proj(2))).astype(BF16)
    sga_ref[...] = _sigmoid(proj(3)).astype(BF16)
    sgc_ref[...] = _sigmoid(proj(4)).astype(BF16)


def _inproj(x2d, g, w_bf, b, wt_bf, bt, tm):
    n, d = x2d.shape
    in_w = w_bf.shape[1]
    out = jax.ShapeDtypeStruct((n, d), BF16)
    row = pl.BlockSpec((tm, d), lambda i: (i, 0))
    col = pl.BlockSpec((d, tm), lambda i: (0, i))
    heads = pl.BlockSpec((N_HEADS, tm, V_DIM), lambda i: (0, i, 0))
    out_heads = jax.ShapeDtypeStruct((N_HEADS, n, V_DIM), BF16)
    out_t = jax.ShapeDtypeStruct((d, n), BF16)
    return pl.pallas_call(
        _inproj_kernel,
        grid=(n // tm,),
        in_specs=[row,
                  pl.BlockSpec((1, d), lambda i: (0, 0)),
                  pl.BlockSpec((d, in_w), lambda i: (0, 0)),
                  pl.BlockSpec((1, in_w), lambda i: (0, 0)),
                  pl.BlockSpec((2 * d, d), lambda i: (0, 0)),
                  pl.BlockSpec((2 * d, 1), lambda i: (0, 0))],
        out_specs=[col, heads, col, row, row, row],
        out_shape=[out_t, out_heads, out_t, out, out, out],
        compiler_params=pltpu.CompilerParams(
            dimension_semantics=("arbitrary",), vmem_limit_bytes=V7X_VMEM_LIMIT),
        name="inproj",
    )(x2d, g, w_bf, b, wt_bf, bt)


def _attn_kernel(lam_ref, qt_ref, k_ref, vt_ref, km_ref, vmt_ref, bias_ref, bmeta_ref,
                 gain_ref, o_ref, *scratch, tq, lam_init):
    def q_tile(qi, carry):
        cols = pl.ds(pl.multiple_of(qi * tq, tq), tq)
        _attn_tile(qi, lam_ref, qt_ref.at[:, cols], k_ref, vt_ref, km_ref, vmt_ref, bias_ref,
                   bmeta_ref, gain_ref, o_ref.at[0, cols, :], *scratch, tq=tq, lam_init=lam_init)
        return carry

    lax.fori_loop(0, qt_ref.shape[1] // tq, q_tile, 0)


def _attn_tile(qi, lam_ref, qt_ref, k_ref, vt_ref, km_ref, vmt_ref, bias_ref, bmeta_ref,
               gain_ref, o_ref, m_s, l_s, acc_s, s0, s1, mx0, mx1, *, tq, lam_init):
    h = pl.program_id(1)
    slope = jnp.exp2(-(h + 1).astype(F32))
    lp = lam_ref[...]
    lam = (jnp.exp(jnp.sum(lp[0:1] * lp[1:2], axis=-1, keepdims=True))
           - jnp.exp(jnp.sum(lp[2:3] * lp[3:4], axis=-1, keepdims=True)) + lam_init)

    qt = qt_ref[...]
    feat = lax.broadcasted_iota(jnp.int32, qt.shape, 0)
    zero = jnp.zeros_like(qt)
    qs = (jnp.where(feat < HEAD_DIM, qt, zero), jnp.where(feat >= HEAD_DIM, qt, zero))

    def scores_t(kk, qm):
        return jnp.dot(kk, qm, preferred_element_type=F32)

    tk = tq // 2
    n_off = 2 * qi

    def tile_shift(t):
        off = (qi * tq - t * tk).astype(F32) * LOG2E
        return jnp.where(t < n_off, slope * off, 0.0)

    def stage_a(t, s_buf, mx_buf, cols=slice(None)):
        kk = k_ref[0, pl.ds(pl.multiple_of(t * tk, tk), tk), :]
        bias = bias_ref[0, jnp.where(t < n_off, 0, t - n_off + 1), :, cols]
        for idx in range(2):
            s = scores_t(kk, qs[idx][:, cols]) + bias
            s_buf[idx, :, cols] = s
            mx_buf[idx, :, cols] = jnp.max(s, axis=0, keepdims=True)

    def stage_b(t, s_buf, mx_buf, cols=slice(None), meta=None):
        vt = vt_ref[:, pl.ds(pl.multiple_of(t * tk, tk), tk)]
        vt1 = jnp.concatenate([vt, jnp.ones((2 * SUBLANES, tk), BF16)], axis=0)
        shift = tile_shift(t)
        for idx in range(2):
            m_old = m_s[idx, :, cols]
            m_new = jnp.maximum(m_old, mx_buf[idx, :, cols] - shift)
            if meta is not None:
                s_meta, vmt1, shift_m = meta
                m_new = jnp.maximum(m_new, jnp.max(s_meta[idx], axis=0, keepdims=True) - shift_m)
            p = jnp.exp2(s_buf[idx, :, cols] - (m_new + shift))
            pv1 = jnp.dot(vt1, p.astype(BF16), preferred_element_type=F32)
            if meta is not None:
                pm = jnp.exp2(s_meta[idx] - (m_new + shift_m))
                pv1 = pv1 + jnp.dot(vmt1, pm.astype(BF16), preferred_element_type=F32)
            pv = pv1[:V_DIM]
            ps = pv1[V_DIM:V_DIM + 1]
            alpha = jnp.exp2(m_old - m_new)
            l_s[idx, :, cols] = alpha * l_s[idx, :, cols] + ps
            acc_s[idx, :, cols] = alpha * acc_s[idx, :, cols] + pv
            m_s[idx, :, cols] = m_new

    m_s[...] = jnp.full(m_s.shape, NEG, F32)
    l_s[...] = jnp.zeros_like(l_s)
    acc_s[...] = jnp.zeros_like(acc_s)
    stage_a(0, s0, mx0)

    def pair(i):
        stage_a(2 * i + 1, s1, mx1)
        stage_b(2 * i, s0, mx0)
        stage_a(2 * i + 2, s0, mx0)
        stage_b(2 * i + 1, s1, mx1)

    def body(ii, carry):
        pair(2 * ii)
        pair(2 * ii + 1)
        return carry

    lax.fori_loop(0, qi // 2, body, 0)

    @pl.when(qi % 2 == 1)
    def _():
        pair(qi - 1)

    late = slice(tq // 2, tq)
    stage_a(n_off + 1, s1, mx1, late)
    s_meta = [scores_t(km_ref[0], qs[idx]) + bmeta_ref[0] for idx in range(2)]
    vmt1 = jnp.concatenate([vmt_ref[...], jnp.ones((2 * SUBLANES, META_PAD), BF16)], axis=0)
    shift0 = slope * ((qi * tq).astype(F32) * LOG2E)
    stage_b(n_off, s0, mx0, meta=(s_meta, vmt1, shift0))
    stage_b(n_off + 1, s1, mx1, late)

    o = acc_s[0] / l_s[0] - lam * (acc_s[1] / l_s[1])
    o = o * lax.rsqrt(jnp.mean(o * o, axis=0, keepdims=True) + EPS) * gain_ref[...]
    o_ref[...] = jnp.transpose(o * (1.0 - lam_init)).astype(BF16)


def _attention(lam_params, qt, k, vt, k_meta, vt_meta, gain, bsz, seq, tq, lam_init):
    n = qt.shape[1]
    slopes = jnp.asarray([2.0 ** (-8.0 * (h + 1) / N_HEADS) for h in range(N_HEADS)], F32)
    tk = tq // 2
    r = jnp.arange(tq, dtype=jnp.int32)
    diff = r[None, :] - r[:, None]
    slopes = slopes * LOG2E
    boff = -slopes[:, None, None] * diff.astype(F32)[None, :tk]
    vis = (r[:, None] // CHUNK) <= (r[None, :] // CHUNK)
    bdiag = jnp.where(vis[None], -slopes[:, None, None] * jnp.abs(diff).astype(F32)[None], NEG)
    bias_all = jnp.stack([boff, bdiag[:, :tk], bdiag[:, tk:]], axis=1)
    mrow = jnp.arange(META_PAD, dtype=jnp.int32)
    dmeta = (N_META + r[None, :] - mrow[:, None]).astype(F32)
    bmeta = jnp.where((mrow < N_META)[None, :, None], -slopes[:, None, None] * dmeta[None], NEG)

    km = jnp.pad(k_meta, ((0, 0), (0, META_PAD - N_META), (0, 0)))
    vmt = jnp.pad(vt_meta, ((0, 0), (0, META_PAD - N_META)))

    kern = functools.partial(_attn_kernel, tq=tq, lam_init=lam_init)
    return pl.pallas_call(
        kern,
        grid=(bsz, N_HEADS),
        in_specs=[
            pl.BlockSpec((4, HEAD_DIM), lambda b, h: (0, 0)),
            pl.BlockSpec((V_DIM, seq), lambda b, h: (h, b)),
            pl.BlockSpec((1, seq, V_DIM), lambda b, h: (h, b, 0)),
            pl.BlockSpec((V_DIM, seq), lambda b, h: (h, b)),
            pl.BlockSpec((1, META_PAD, V_DIM), lambda b, h: (h, 0, 0)),
            pl.BlockSpec((V_DIM, META_PAD), lambda b, h: (h, 0)),
            pl.BlockSpec((1, 3, tk, tq), lambda b, h: (h, 0, 0, 0)),
            pl.BlockSpec((1, META_PAD, tq), lambda b, h: (h, 0, 0)),
            pl.BlockSpec((V_DIM, 1), lambda b, h: (0, 0)),
        ],
        out_specs=pl.BlockSpec((1, seq, V_DIM), lambda b, h: (h, b, 0)),
        out_shape=jax.ShapeDtypeStruct((N_HEADS, n, V_DIM), BF16),
        scratch_shapes=[pltpu.VMEM((2, 1, tq), F32), pltpu.VMEM((2, 1, tq), F32),
                        pltpu.VMEM((2, V_DIM, tq), F32),
                        pltpu.VMEM((2, tk, tq), F32), pltpu.VMEM((2, tk, tq), F32),
                        pltpu.VMEM((2, 1, tq), F32), pltpu.VMEM((2, 1, tq), F32)],
        compiler_params=pltpu.CompilerParams(
            dimension_semantics=("arbitrary", "arbitrary"),
            vmem_limit_bytes=V7X_VMEM_LIMIT),
        name="diff_attention",
    )(lam_params, qt, k, vt, km, vmt, bias_all, bmeta, gain)


def _mixer_kernel(x_ref, o_ref, u_ref, uh_ref, mh_ref, sga_ref, sgc_ref,
                  wap_ref, wpw_ref, wout_ref, dww_ref, dwb_ref, lng_ref, lnb_ref,
                  nffn_ref, wr_ref, br_ref, tri_ref,
                  h_ref, hn_ref, exp_ref, rank_ref, gate_ref, cnt_ref,
                  ext_s, y_s, cnt_s, *, tm, tiles_per_seq):
    i = pl.program_id(0)
    d = x_ref.shape[1]

    @pl.when(i == 0)
    def _():
        cnt_s[...] = jnp.zeros_like(cnt_s)

    first = (i % tiles_per_seq) == 0
    halo = jnp.where(first, mh_ref[...], uh_ref[...])
    ext_s[0, 0:HALO, :] = halo.astype(F32)
    ext_s[0, HALO:HALO + tm, :] = u_ref[...].astype(F32)
    span = tm + HALO - SUBLANES
    for b in range(1, SUBLANES):
        ext_s[b, 0:span, :] = ext_s[0, b:b + span, :]
    off = HALO - (CONV_K - 1)

    o = jnp.concatenate([o_ref[hd] for hd in range(N_HEADS)], axis=-1)
    y_attn = jnp.dot(o, wap_ref[...], preferred_element_type=F32)
    for c in range(d // 128):
        cs = slice(c * 128, (c + 1) * 128)
        for r0 in (0, tm // 2):
            acc = jnp.zeros((tm // 2, 128), F32)
            for t in range(CONV_K):
                a, b = divmod(off + t, SUBLANES)
                row = a * SUBLANES + r0
                acc = acc + dww_ref[t:t + 1, cs] * ext_s[b, row:row + tm // 2, cs]
            y_s[r0:r0 + tm // 2, cs] = acc
    y = y_s[...] + dwb_ref[...]
    mu = jnp.mean(y, axis=-1, keepdims=True)
    yc = y - mu
    var = jnp.mean(yc * yc, axis=-1, keepdims=True)
    y = yc * lax.rsqrt(var + EPS) * lng_ref[...] + lnb_ref[...]
    y = y * _sigmoid(y)
    y_conv = jnp.dot(y.astype(BF16), wpw_ref[...], preferred_element_type=F32)

    mixed = sga_ref[...].astype(F32) * y_attn + sgc_ref[...].astype(F32) * y_conv
    h = x_ref[...] + jnp.dot(mixed.astype(BF16), wout_ref[...], preferred_element_type=F32)
    h_ref[...] = h
    hn = h * lax.rsqrt(jnp.mean(h * h, axis=-1, keepdims=True) + EPS) * nffn_ref[...]
    _store_token_tiles(hn_ref, hn)

    nt_dims = (((1,), (1,)), ((), ()))
    hn_hi = hn.astype(BF16)
    hn_lo = (hn - hn_hi.astype(F32)).astype(BF16)
    both = lax.dot_general(wr_ref[...], hn_hi, nt_dims, preferred_element_type=F32)
    logits = (both[:N_EXPERTS] + both[N_EXPERTS:]
              + lax.dot_general(wr_ref[0:N_EXPERTS, :], hn_lo, nt_dims, preferred_element_type=F32)
              + br_ref[...])
    eio = lax.broadcasted_iota(jnp.int32, logits.shape, 0)
    sels, tops, idxs = [], [], []
    cur = logits
    for _ in range(TOP_K):
        m = jnp.max(cur, axis=0, keepdims=True)
        idx = jnp.min(jnp.where(cur == m, eio, N_EXPERTS), axis=0, keepdims=True)
        sel = eio == idx
        sels.append(sel)
        tops.append(m)
        idxs.append(idx)
        cur = jnp.where(sel, -jnp.inf, cur)
    exps = [jnp.exp(t - tops[0]) for t in tops]
    denom = exps[0] + exps[1] + exps[2] + exps[3]
    hot = jnp.zeros(logits.shape, F32)
    for sel in sels:
        hot = hot + jnp.where(sel, 1.0, 0.0)
    before = jnp.dot(hot.astype(BF16), tri_ref[...], preferred_element_type=F32) + cnt_s[...]
    ranks = [jnp.sum(jnp.where(sel, before, 0.0), axis=0, keepdims=True).astype(jnp.int32)
             for sel in sels]
    zi = jnp.zeros((ROUTE_ROWS - TOP_K, tm), jnp.int32)
    exp_ref[0] = jnp.concatenate(idxs + [zi], axis=0)
    rank_ref[0] = jnp.concatenate(ranks + [zi], axis=0)
    gate_ref[0] = jnp.concatenate([e / denom for e in exps] + [zi.astype(F32)], axis=0)
    cnt_s[...] = cnt_s[...] + jnp.sum(hot, axis=1, keepdims=True)
    cnt_ref[...] = jnp.broadcast_to(cnt_s[...], cnt_ref.shape)


def _mixer(x2d, o, u, sga, sgc, meta_halo, wap, wpw, wout, dww, dwb, lng, lnb, nffn, wr_t, br,
           seq, tm):
    n, d = x2d.shape
    nt = n // tm
    hb = tm // HALO
    tri = (jnp.arange(tm)[:, None] < jnp.arange(tm)[None, :]).astype(BF16)
    row = pl.BlockSpec((tm, d), lambda i: (i, 0))
    route = pl.BlockSpec((1, ROUTE_ROWS, tm), lambda i: (i, 0, 0))

    def full(shp):
        return pl.BlockSpec(shp, lambda i: tuple(0 for _ in shp))

    kern = functools.partial(_mixer_kernel, tm=tm, tiles_per_seq=seq // tm)
    return pl.pallas_call(
        kern,
        grid=(nt,),
        in_specs=[row, pl.BlockSpec((N_HEADS, tm, V_DIM), lambda i: (0, i, 0)), row,
                  pl.BlockSpec((HALO, d), lambda i: (jnp.maximum(i * hb - 1, 0), 0)),
                  full((HALO, d)), row, row,
                  full((d, d)), full((d, d)), full((d, d)),
                  full((CONV_K, d)), full((1, d)), full((1, d)), full((1, d)), full((1, d)),
                  full((2 * N_EXPERTS, d)), full((N_EXPERTS, 1)), full((tm, tm))],
        out_specs=[row, pl.BlockSpec((tm * SUBLANES, LANES), lambda i: (i, 0)),
                   route, route, route, full((N_EXPERTS, 128))],
        out_shape=[jax.ShapeDtypeStruct((n, d), F32),
                   jax.ShapeDtypeStruct((n * SUBLANES, LANES), F32),
                   jax.ShapeDtypeStruct((nt, ROUTE_ROWS, tm), jnp.int32),
                   jax.ShapeDtypeStruct((nt, ROUTE_ROWS, tm), jnp.int32),
                   jax.ShapeDtypeStruct((nt, ROUTE_ROWS, tm), F32),
                   jax.ShapeDtypeStruct((N_EXPERTS, 128), F32)],
        scratch_shapes=[pltpu.VMEM((SUBLANES, HALO + tm, d), F32), pltpu.VMEM((tm, d), F32),
                        pltpu.VMEM((N_EXPERTS, 1), F32)],
        compiler_params=pltpu.CompilerParams(
            dimension_semantics=("arbitrary",), vmem_limit_bytes=V7X_VMEM_LIMIT),
        name="mixer_router",
    )(x2d, o, u, u, meta_halo, sga, sgc, wap, wpw, wout, dww, dwb, lng, lnb, nffn, wr_t, br, tri)


def _prefetch_routes(dest_hbm, dsm_s, sem_idx):
    i = pl.program_id(0)
    slot = i % 2

    def idx_copy(tile, s):
        return pltpu.make_async_copy(dest_hbm.at[tile], dsm_s.at[s], sem_idx.at[s])

    @pl.when(i == 0)
    def _():
        idx_copy(0, 0).start()

    idx_copy(i, slot).wait()

    @pl.when(i + 1 < pl.num_programs(0))
    def _():
        idx_copy(i + 1, 1 - slot).start()

    return slot


def _dispatch_kernel(dest_hbm, hn_ref, xs_hbm, dsm_s, sem_idx, sem_rows, *, tm):
    slot = _prefetch_routes(dest_hbm, dsm_s, sem_idx)

    def issue(r, c):
        src = hn_ref.at[pl.ds(pl.multiple_of(r * SUBLANES, SUBLANES), SUBLANES), :]
        for kk in range(TOP_K):
            dst = pl.multiple_of(dsm_s[slot, kk, r] * SUBLANES, SUBLANES)
            pltpu.make_async_copy(src, xs_hbm.at[pl.ds(dst, SUBLANES), :],
                                  sem_rows).start(priority=kk % 2)
        return c
    lax.fori_loop(0, tm, issue, 0, unroll=8)

    for kk in range(TOP_K):
        pltpu.make_async_copy(hn_ref, xs_hbm.at[pl.ds(0, tm * SUBLANES), :], sem_rows).wait()


def _dispatch(dest, hn, n_slots, tm):
    n = hn.shape[0] // SUBLANES
    return pl.pallas_call(
        functools.partial(_dispatch_kernel, tm=tm),
        grid=(n // tm,),
        in_specs=[pl.BlockSpec(memory_space=pl.ANY),
                  pl.BlockSpec((tm * SUBLANES, LANES), lambda i: (i, 0))],
        out_specs=pl.BlockSpec(memory_space=pl.ANY),
        out_shape=jax.ShapeDtypeStruct((n_slots * SUBLANES, LANES), F32),
        scratch_shapes=[pltpu.SMEM((2, ROUTE_ROWS, tm), jnp.int32),
                        pltpu.SemaphoreType.DMA((2,)), pltpu.SemaphoreType.DMA(())],
        compiler_params=pltpu.CompilerParams(
            dimension_semantics=("arbitrary",), vmem_limit_bytes=V7X_VMEM_LIMIT),
        name="dispatch",
    )(dest, hn)


def _expert_kernel(bexp_ref, bvalid_ref, xs_ref, wgu_ref, bgu_ref, wdn_ref, bdn_ref, ys_ref,
                   wgu_bf, wdn_bf):
    i = pl.program_id(0)
    valid = bvalid_ref[i]
    f = wdn_ref.shape[1]

    @pl.when((i == 0) | (bexp_ref[i] != bexp_ref[jnp.maximum(i - 1, 0)]))
    def _():
        wgu_bf[...] = wgu_ref[0].astype(BF16)
        wdn_bf[...] = wdn_ref[0].astype(BF16)

    @pl.when(valid > 0)
    def _():
        x = _load_token_tiles(xs_ref, EXPERT_BLOCK)
        rows = lax.broadcasted_iota(jnp.int32, x.shape, 0)
        xb = jnp.where(rows < valid, x, 0.0).astype(BF16)
        gu = jnp.dot(xb, wgu_bf[...], preferred_element_type=F32) + bgu_ref[0]
        glu = jnp.minimum(gu[:, :f], SWIGLU_LIMIT)
        lin = jnp.clip(gu[:, f:], -SWIGLU_LIMIT, SWIGLU_LIMIT)
        act = glu * _sigmoid(SWIGLU_ALPHA * glu) * (lin + 1.0)
        y = jnp.dot(act.astype(BF16), wdn_bf[...], preferred_element_type=F32) + bdn_ref[0]
        _store_token_tiles(ys_ref, y)

    @pl.when(valid <= 0)
    def _():
        ys_ref[...] = jnp.zeros_like(ys_ref)


def _experts(block_exp, block_valid, xs, wgu, bgu, wdn, bdn):
    d = wgu.shape[1]
    n_slots = xs.shape[0] // SUBLANES
    nb = n_slots // EXPERT_BLOCK
    f2 = wgu.shape[2]
    f = wdn.shape[1]
    tile_rows = EXPERT_BLOCK * SUBLANES
    grid_spec = pltpu.PrefetchScalarGridSpec(
        num_scalar_prefetch=2,
        grid=(nb,),
        in_specs=[pl.BlockSpec((tile_rows, LANES), lambda i, be, bv: (i, 0)),
                  pl.BlockSpec((1, d, f2), lambda i, be, bv: (be[i], 0, 0)),
                  pl.BlockSpec((1, 1, f2), lambda i, be, bv: (be[i], 0, 0)),
                  pl.BlockSpec((1, f, d), lambda i, be, bv: (be[i], 0, 0)),
                  pl.BlockSpec((1, 1, d), lambda i, be, bv: (be[i], 0, 0))],
        out_specs=pl.BlockSpec((tile_rows, LANES), lambda i, be, bv: (i, 0)),
        scratch_shapes=[pltpu.VMEM((d, f2), BF16), pltpu.VMEM((f, d), BF16)],
    )
    return pl.pallas_call(
        _expert_kernel,
        grid_spec=grid_spec,
        out_shape=jax.ShapeDtypeStruct(xs.shape, F32),
        compiler_params=pltpu.CompilerParams(
            dimension_semantics=("arbitrary",), vmem_limit_bytes=V7X_VMEM_LIMIT),
        name="experts",
    )(block_exp, block_valid, xs, wgu, bgu, wdn, bdn)


def _combine_kernel(dest_hbm, h_ref, gate_ref, g_ref, ys_hbm, out_ref, dsm_s, ybuf, sem_idx, sem_rows,
                    *, tm):
    i = pl.program_id(0)
    nsteps = pl.num_programs(0)
    slot = i % 2

    def idx_copy(tile, s):
        return pltpu.make_async_copy(dest_hbm.at[tile], dsm_s.at[s], sem_idx.at[s])

    def issue_rows(s):
        def issue(r, c):
            for kk in range(TOP_K):
                src = pl.multiple_of(dsm_s[s, kk, r] * SUBLANES, SUBLANES)
                dst = pl.multiple_of(r * SUBLANES, SUBLANES)
                pltpu.make_async_copy(ys_hbm.at[pl.ds(src, SUBLANES), :],
                                      ybuf.at[s, kk, pl.ds(dst, SUBLANES), :],
                                      sem_rows.at[s]).start(priority=kk % 2)
            return c
        lax.fori_loop(0, tm, issue, 0, unroll=8)

    @pl.when(i == 0)
    def _():
        idx_copy(0, 0).start()
        idx_copy(0, 0).wait()
        issue_rows(0)

        @pl.when(nsteps > 1)
        def _():
            idx_copy(1, 1).start()

    @pl.when(i + 1 < nsteps)
    def _():
        idx_copy(i + 1, 1 - slot).wait()
        issue_rows(1 - slot)

        @pl.when(i + 2 < nsteps)
        def _():
            idx_copy(i + 2, slot).start()

    for kk in range(TOP_K):
        pltpu.make_async_copy(ys_hbm.at[pl.ds(0, tm * SUBLANES), :], ybuf.at[slot, kk],
                              sem_rows.at[slot]).wait()

    gates = jnp.transpose(gate_ref[0])
    acc = h_ref[...]
    for kk in range(TOP_K):
        acc = acc + gates[:, kk:kk + 1] * _load_token_tiles(ybuf.at[slot, kk], tm)
    out_ref[...] = acc * lax.rsqrt(jnp.mean(acc * acc, axis=-1, keepdims=True) + EPS) * g_ref[...]


def _combine(dest, h, gates, g_final, ys, tm):
    n, d = h.shape
    return pl.pallas_call(
        functools.partial(_combine_kernel, tm=tm),
        grid=(n // tm,),
        in_specs=[pl.BlockSpec(memory_space=pl.ANY),
                  pl.BlockSpec((tm, d), lambda i: (i, 0)),
                  pl.BlockSpec((1, ROUTE_ROWS, tm), lambda i: (i, 0, 0)),
                  pl.BlockSpec((1, d), lambda i: (0, 0)),
                  pl.BlockSpec(memory_space=pl.ANY)],
        out_specs=pl.BlockSpec((tm, d), lambda i: (i, 0)),
        out_shape=jax.ShapeDtypeStruct((n, d), F32),
        scratch_shapes=[pltpu.SMEM((2, ROUTE_ROWS, tm), jnp.int32),
                        pltpu.VMEM((2, TOP_K, tm * SUBLANES, LANES), F32),
                        pltpu.SemaphoreType.DMA((2,)), pltpu.SemaphoreType.DMA((2,))],
        compiler_params=pltpu.CompilerParams(
            dimension_semantics=("arbitrary",), vmem_limit_bytes=V7X_VMEM_LIMIT),
        name="combine",
    )(dest, h, gates, g_final, ys)


def kernel(x, meta_tokens, norm_mix, w_in, b_in, lam_params, subln_gain, w_attn_proj, conv_dw_w,
           conv_dw_b, conv_ln_g, conv_ln_b, w_conv_proj, w_out, norm_ffn, w_router, b_router,
           w_gate_up, b_gate_up, w_down, b_down, norm_final):
    bsz, seq, d = x.shape
    assert norm_mix.shape[0] == 1, "single-layer block"
    n = bsz * seq
    tm_proj = min(512, seq)
    tq = min(512, seq)
    tm = min(256, seq)
    lam_init = 0.8 - 0.6 * 1.0

    x2d = x.reshape(n, d)
    w_rest = jnp.concatenate([w_in[0][:, d:2 * d], w_in[0][:, 3 * d:]], axis=1).astype(BF16)
    b_rest = jnp.concatenate([b_in[0][d:2 * d], b_in[0][3 * d:]])[None]
    w_t = jnp.concatenate([w_in[0][:, :d], w_in[0][:, 2 * d:3 * d]], axis=1).T.astype(BF16)
    b_t = jnp.concatenate([b_in[0][:d], b_in[0][2 * d:3 * d]])[:, None]
    g_mix = norm_mix[0][None]
    qt, k, vt, u, sga, sgc = _inproj(x2d, g_mix, w_rest, b_rest, w_t, b_t, tm_proj)
    _, k_m, vt_m, u_m, _, _ = _inproj(meta_tokens.astype(x.dtype), g_mix, w_rest, b_rest, w_t, b_t,
                                      N_META)

    o = _attention(lam_params[0], qt, k, vt, k_m, vt_m, subln_gain[0][:, None], bsz, seq, tq,
                   lam_init)

    meta_halo = jnp.concatenate([jnp.zeros((HALO - N_META, d), BF16), u_m], axis=0)
    wr_t = w_router[0].T
    wr_hi = wr_t.astype(BF16)
    wr_split = jnp.concatenate([wr_hi, (wr_t - wr_hi.astype(F32)).astype(BF16)], axis=0)
    h, hn, eidx, rank, gates, counts = _mixer(
        x2d, o, u, sga, sgc, meta_halo,
        w_attn_proj[0].astype(BF16), w_conv_proj[0].astype(BF16), w_out[0].astype(BF16),
        conv_dw_w[0], conv_dw_b[0][None], conv_ln_g[0][None], conv_ln_b[0][None],
        norm_ffn[0][None], wr_split, b_router[0][:, None], seq, tm)

    cnt = counts[:, 0].astype(jnp.int32)
    padded = (cnt + EXPERT_BLOCK - 1) // EXPERT_BLOCK * EXPERT_BLOCK
    pad_end = jnp.cumsum(padded)
    pad_start = pad_end - padded
    n_blocks = -(-(n * TOP_K + N_EXPERTS * (EXPERT_BLOCK - 1)) // EXPERT_BLOCK)
    n_slots = n_blocks * EXPERT_BLOCK
    dest = pad_start[eidx] + rank
    block_start = jnp.arange(n_blocks, dtype=jnp.int32) * EXPERT_BLOCK
    block_exp = jnp.minimum(jnp.sum(block_start[:, None] >= pad_end[None, :], axis=1),
                            N_EXPERTS - 1).astype(jnp.int32)
    block_valid = jnp.clip(pad_start[block_exp] + cnt[block_exp] - block_start, 0,
                           EXPERT_BLOCK).astype(jnp.int32)

    xs = _dispatch(dest, hn, n_slots, tm)
    ys = _experts(block_exp, block_valid, xs, w_gate_up[0], b_gate_up[0][:, None, :],
                  w_down[0], b_down[0][:, None, :])
    out = _combine(dest, h, gates, norm_final[None], ys, tm)
    return out.reshape(bsz, seq, d)
```

```python
import functools

import jax
import jax.numpy as jnp
from jax import lax
from jax.experimental import pallas as pl
from jax.experimental.pallas import tpu as pltpu

CHUNK = 64
N_META = 16
N_HEADS = 8
HEAD_DIM = 64
V_DIM = 2 * HEAD_DIM
CONV_K = 31
N_EXPERTS = 32
TOP_K = 4
SWIGLU_ALPHA = 1.702
SWIGLU_LIMIT = 7.0
EXPERT_BLOCK = 512
EPS = 1e-5
NEG = -1e30
LOG2E = 1.4426950408889634

V7X_VMEM_LIMIT = 56 * 1024 * 1024
SUBLANES = 8
LANES = 128
META_PAD = 128
HALO = 32
ROUTE_ROWS = 8

BF16 = jnp.bfloat16
F32 = jnp.float32


def _sigmoid(x):
    return 1.0 / (1.0 + jnp.exp(-x))


def _load_token_tiles(ref, rows):
    return jnp.concatenate([ref[pl.ds(s, rows, stride=SUBLANES), :] for s in range(SUBLANES)],
                           axis=-1)


def _store_token_tiles(ref, x):
    rows = x.shape[0]
    for s in range(SUBLANES):
        ref[pl.ds(s, rows, stride=SUBLANES), :] = x[:, s * LANES:(s + 1) * LANES]


def _inproj_kernel(x_ref, g_ref, w_ref, b_ref, wt_ref, bt_ref,
                   qt_ref, k_ref, vt_ref, u_ref, sga_ref, sgc_ref):
    d = x_ref.shape[1]
    x = x_ref[...]
    xn = x * lax.rsqrt(jnp.mean(x * x, axis=-1, keepdims=True) + EPS) * g_ref[...]
    xb = xn.astype(BF16)

    def proj(c):
        return (jnp.dot(xb, w_ref[:, c * d:(c + 1) * d], preferred_element_type=F32)
                + b_ref[:, c * d:(c + 1) * d])

    kf = proj(0).astype(BF16)
    for hd in range(N_HEADS):
        k_ref[hd] = kf[:, hd * V_DIM:(hd + 1) * V_DIM]
    t = lax.dot_general(wt_ref[...], xb, (((1,), (1,)), ((), ())), preferred_element_type=F32)
    t = t + bt_ref[...]
    qt_ref[...] = (t[:d] * (HEAD_DIM ** -0.5 * LOG2E)).astype(BF16)
    vt_ref[...] = t[d:].astype(BF16)
    u_ref[...] = (proj(1) * _sigmoid(proj(2))).astype(BF16)
    sga_ref[...] = _sigmoid(proj(3)).astype(BF16)
    sgc_ref[...] = _sigmoid(proj(4)).astype(BF16)


def _inproj(x2d, g, w_bf, b, wt_bf, bt, tm):
    n, d = x2d.shape
    in_w = w_bf.shape[1]
    out = jax.ShapeDtypeStruct((n, d), BF16)
    row = pl.BlockSpec((tm, d), lambda i: (i, 0))
    col = pl.BlockSpec((d, tm), lambda i: (0, i))
    heads = pl.BlockSpec((N_HEADS, tm, V_DIM), lambda i: (0, i, 0))
    out_heads = jax.ShapeDtypeStruct((N_HEADS, n, V_DIM), BF16)
    out_t = jax.ShapeDtypeStruct((d, n), BF16)
    return pl.pallas_call(
        _inproj_kernel,
        grid=(n // tm,),
        in_specs=[row,
                  pl.BlockSpec((1, d), lambda i: (0, 0)),
                  pl.BlockSpec((d, in_w), lambda i: (0, 0)),
                  pl.BlockSpec((1, in_w), lambda i: (0, 0)),
                  pl.BlockSpec((2 * d, d), lambda i: (0, 0)),
                  pl.BlockSpec((2 * d, 1), lambda i: (0, 0))],
        out_specs=[col, heads, col, row, row, row],
        out_shape=[out_t, out_heads, out_t, out, out, out],
        compiler_params=pltpu.CompilerParams(
            dimension_semantics=("arbitrary",), vmem_limit_bytes=V7X_VMEM_LIMIT),
        name="inproj",
    )(x2d, g, w_bf, b, wt_bf, bt)


def _attn_kernel(lam_ref, qt_ref, k_ref, vt_ref, km_ref, vmt_ref, bias_ref, bmeta_ref,
                 gain_ref, o_ref, *scratch, tq, lam_init):
    def q_tile(qi, carry):
        cols = pl.ds(pl.multiple_of(qi * tq, tq), tq)
        _attn_tile(qi, lam_ref, qt_ref.at[:, cols], k_ref, vt_ref, km_ref, vmt_ref, bias_ref,
                   bmeta_ref, gain_ref, o_ref.at[0, cols, :], *scratch, tq=tq, lam_init=lam_init)
        return carry

    lax.fori_loop(0, qt_ref.shape[1] // tq, q_tile, 0)


def _attn_tile(qi, lam_ref, qt_ref, k_ref, vt_ref, km_ref, vmt_ref, bias_ref, bmeta_ref,
               gain_ref, o_ref, m_s, l_s, acc_s, s0, s1, mx0, mx1, *, tq, lam_init):
    h = pl.program_id(1)
    slope = jnp.exp2(-(h + 1).astype(F32))
    lp = lam_ref[...]
    lam = (jnp.exp(jnp.sum(lp[0:1] * lp[1:2], axis=-1, keepdims=True))
           - jnp.exp(jnp.sum(lp[2:3] * lp[3:4], axis=-1, keepdims=True)) + lam_init)

    qt = qt_ref[...]
    feat = lax.broadcasted_iota(jnp.int32, qt.shape, 0)
    zero = jnp.zeros_like(qt)
    qs = (jnp.where(feat < HEAD_DIM, qt, zero), jnp.where(feat >= HEAD_DIM, qt, zero))

    def scores_t(kk, qm):
        return jnp.dot(kk, qm, preferred_element_type=F32)

    tk = tq // 2
    n_off = 2 * qi

    def tile_shift(t):
        off = (qi * tq - t * tk).astype(F32) * LOG2E
        return jnp.where(t < n_off, slope * off, 0.0)

    def stage_a(t, s_buf, mx_buf, cols=slice(None)):
        kk = k_ref[0, pl.ds(pl.multiple_of(t * tk, tk), tk), :]
        bias = bias_ref[0, jnp.where(t < n_off, 0, t - n_off + 1), :, cols]
        for idx in range(2):
            s = scores_t(kk, qs[idx][:, cols]) + bias
            s_buf[idx, :, cols] = s
            mx_buf[idx, :, cols] = jnp.max(s, axis=0, keepdims=True)

    def stage_b(t, s_buf, mx_buf, cols=slice(None), meta=None):
        vt = vt_ref[:, pl.ds(pl.multiple_of(t * tk, tk), tk)]
        vt1 = jnp.concatenate([vt, jnp.ones((2 * SUBLANES, tk), BF16)], axis=0)
        shift = tile_shift(t)
        for idx in range(2):
            m_old = m_s[idx, :, cols]
            m_new = jnp.maximum(m_old, mx_buf[idx, :, cols] - shift)
            if meta is not None:
                s_meta, vmt1, shift_m = meta
                m_new = jnp.maximum(m_new, jnp.max(s_meta[idx], axis=0, keepdims=True) - shift_m)
            p = jnp.exp2(s_buf[idx, :, cols] - (m_new + shift))
            pv1 = jnp.dot(vt1, p.astype(BF16), preferred_element_type=F32)
            if meta is not None:
                pm = jnp.exp2(s_meta[idx] - (m_new + shift_m))
                pv1 = pv1 + jnp.dot(vmt1, pm.astype(BF16), preferred_element_type=F32)
            pv = pv1[:V_DIM]
            ps = pv1[V_DIM:V_DIM + 1]
            alpha = jnp.exp2(m_old - m_new)
            l_s[idx, :, cols] = alpha * l_s[idx, :, cols] + ps
            acc_s[idx, :, cols] = alpha * acc_s[idx, :, cols] + pv
            m_s[idx, :, cols] = m_new

    m_s[...] = jnp.full(m_s.shape, NEG, F32)
    l_s[...] = jnp.zeros_like(l_s)
    acc_s[...] = jnp.zeros_like(acc_s)
    stage_a(0, s0, mx0)

    def pair(i):
        stage_a(2 * i + 1, s1, mx1)
        stage_b(2 * i, s0, mx0)
        stage_a(2 * i + 2, s0, mx0)
        stage_b(2 * i + 1, s1, mx1)

    def body(ii, carry):
        pair(2 * ii)
        pair(2 * ii + 1)
        return carry

    lax.fori_loop(0, qi // 2, body, 0)

    @pl.when(qi % 2 == 1)
    def _():
        pair(qi - 1)

    late = slice(tq // 2, tq)
    stage_a(n_off + 1, s1, mx1, late)
    s_meta = [scores_t(km_ref[0], qs[idx]) + bmeta_ref[0] for idx in range(2)]
    vmt1 = jnp.concatenate([vmt_ref[...], jnp.ones((2 * SUBLANES, META_PAD), BF16)], axis=0)
    shift0 = slope * ((qi * tq).astype(F32) * LOG2E)
    stage_b(n_off, s0, mx0, meta=(s_meta, vmt1, shift0))
    stage_b(n_off + 1, s1, mx1, late)

    o = acc_s[0] / l_s[0] - lam * (acc_s[1] / l_s[1])
    o = o * lax.rsqrt(jnp.mean(o * o, axis=0, keepdims=True) + EPS) * gain_ref[...]
    o_ref[...] = jnp.transpose(o * (1.0 - lam_init)).astype(BF16)


def _attention(lam_params, qt, k, vt, k_meta, vt_meta, gain, bsz, seq, tq, lam_init):
    n = qt.shape[1]
    slopes = jnp.asarray([2.0 ** (-8.0 * (h + 1) / N_HEADS) for h in range(N_HEADS)], F32)
    tk = tq // 2
    r = jnp.arange(tq, dtype=jnp.int32)
    diff = r[None, :] - r[:, None]
    slopes = slopes * LOG2E
    boff = -slopes[:, None, None] * diff.astype(F32)[None, :tk]
    vis = (r[:, None] // CHUNK) <= (r[None, :] // CHUNK)
    bdiag = jnp.where(vis[None], -slopes[:, None, None] * jnp.abs(diff).astype(F32)[None], NEG)
    bias_all = jnp.stack([boff, bdiag[:, :tk], bdiag[:, tk:]], axis=1)
    mrow = jnp.arange(META_PAD, dtype=jnp.int32)
    dmeta = (N_META + r[None, :] - mrow[:, None]).astype(F32)
    bmeta = jnp.where((mrow < N_META)[None, :, None], -slopes[:, None, None] * dmeta[None], NEG)

    km = jnp.pad(k_meta, ((0, 0), (0, META_PAD - N_META), (0, 0)))
    vmt = jnp.pad(vt_meta, ((0, 0), (0, META_PAD - N_META)))

    kern = functools.partial(_attn_kernel, tq=tq, lam_init=lam_init)
    return pl.pallas_call(
        kern,
        grid=(bsz, N_HEADS),
        in_specs=[
            pl.BlockSpec((4, HEAD_DIM), lambda b, h: (0, 0)),
            pl.BlockSpec((V_DIM, seq), lambda b, h: (h, b)),
            pl.BlockSpec((1, seq, V_DIM), lambda b, h: (h, b, 0)),
            pl.BlockSpec((V_DIM, seq), lambda b, h: (h, b)),
            pl.BlockSpec((1, META_PAD, V_DIM), lambda b, h: (h, 0, 0)),
            pl.BlockSpec((V_DIM, META_PAD), lambda b, h: (h, 0)),
            pl.BlockSpec((1, 3, tk, tq), lambda b, h: (h, 0, 0, 0)),
            pl.BlockSpec((1, META_PAD, tq), lambda b, h: (h, 0, 0)),
            pl.BlockSpec((V_DIM, 1), lambda b, h: (0, 0)),
        ],
        out_specs=pl.BlockSpec((1, seq, V_DIM), lambda b, h: (h, b, 0)),
        out_shape=jax.ShapeDtypeStruct((N_HEADS, n, V_DIM), BF16),
        scratch_shapes=[pltpu.VMEM((2, 1, tq), F32), pltpu.VMEM((2, 1, tq), F32),
                        pltpu.VMEM((2, V_DIM, tq), F32),
                        pltpu.VMEM((2, tk, tq), F32), pltpu.VMEM((2, tk, tq), F32),
                        pltpu.VMEM((2, 1, tq), F32), pltpu.VMEM((2, 1, tq), F32)],
        compiler_params=pltpu.CompilerParams(
            dimension_semantics=("arbitrary", "arbitrary"),
            vmem_limit_bytes=V7X_VMEM_LIMIT),
        name="diff_attention",
    )(lam_params, qt, k, vt, km, vmt, bias_all, bmeta, gain)


def _mixer_kernel(x_ref, o_ref, u_ref, uh_ref, mh_ref, sga_ref, sgc_ref,
                  wap_ref, wpw_ref, wout_ref, dww_ref, dwb_ref, lng_ref, lnb_ref,
                  nffn_ref, wr_ref, br_ref, tri_ref,
                  h_ref, hn_ref, exp_ref, rank_ref, gate_ref, cnt_ref,
                  ext_s, y_s, cnt_s, *, tm, tiles_per_seq):
    i = pl.program_id(0)
    d = x_ref.shape[1]

    @pl.when(i == 0)
    def _():
        cnt_s[...] = jnp.zeros_like(cnt_s)

    first = (i % tiles_per_seq) == 0
    halo = jnp.where(first, mh_ref[...], uh_ref[...])
    ext_s[0, 0:HALO, :] = halo.astype(F32)
    ext_s[0, HALO:HALO + tm, :] = u_ref[...].astype(F32)
    span = tm + HALO - SUBLANES
    for b in range(1, SUBLANES):
        ext_s[b, 0:span, :] = ext_s[0, b:b + span, :]
    off = HALO - (CONV_K - 1)

    o = jnp.concatenate([o_ref[hd] for hd in range(N_HEADS)], axis=-1)
    y_attn = jnp.dot(o, wap_ref[...], preferred_element_type=F32)
    for c in range(d // 128):
        cs = slice(c * 128, (c + 1) * 128)
        for r0 in (0, tm // 2):
            acc = jnp.zeros((tm // 2, 128), F32)
            for t in range(CONV_K):
                a, b = divmod(off + t, SUBLANES)
                row = a * SUBLANES + r0
                acc = acc + dww_ref[t:t + 1, cs] * ext_s[b, row:row + tm // 2, cs]
            y_s[r0:r0 + tm // 2, cs] = acc
    y = y_s[...] + dwb_ref[...]
    mu = jnp.mean(y, axis=-1, keepdims=True)
    yc = y - mu
    var = jnp.mean(yc * yc, axis=-1, keepdims=True)
    y = yc * lax.rsqrt(var + EPS) * lng_ref[...] + lnb_ref[...]
    y = y * _sigmoid(y)
    y_conv = jnp.dot(y.astype(BF16), wpw_ref[...], preferred_element_type=F32)

    mixed = sga_ref[...].astype(F32) * y_attn + sgc_ref[...].astype(F32) * y_conv
    h = x_ref[...] + jnp.dot(mixed.astype(BF16), wout_ref[...], preferred_element_type=F32)
    h_ref[...] = h
    hn = h * lax.rsqrt(jnp.mean(h * h, axis=-1, keepdims=True) + EPS) * nffn_ref[...]
    _store_token_tiles(hn_ref, hn)

    nt_dims = (((1,), (1,)), ((), ()))
    hn_hi = hn.astype(BF16)
    hn_lo = (hn - hn_hi.astype(F32)).astype(BF16)
    both = lax.dot_general(wr_ref[...], hn_hi, nt_dims, preferred_element_type=F32)
    logits = (both[:N_EXPERTS] + both[N_EXPERTS:]
              + lax.dot_general(wr_ref[0:N_EXPERTS, :], hn_lo, nt_dims, preferred_element_type=F32)
              + br_ref[...])
    eio = lax.broadcasted_iota(jnp.int32, logits.shape, 0)
    sels, tops, idxs = [], [], []
    cur = logits
    for _ in range(TOP_K):
        m = jnp.max(cur, axis=0, keepdims=True)
        idx = jnp.min(jnp.where(cur == m, eio, N_EXPERTS), axis=0, keepdims=True)
        sel = eio == idx
        sels.append(sel)
        tops.append(m)
        idxs.append(idx)
        cur = jnp.where(sel, -jnp.inf, cur)
    exps = [jnp.exp(t - tops[0]) for t in tops]
    denom = exps[0] + exps[1] + exps[2] + exps[3]
    hot = jnp.zeros(logits.shape, F32)
    for sel in sels:
        hot = hot + jnp.where(sel, 1.0, 0.0)
    before = jnp.dot(hot.astype(BF16), tri_ref[...], preferred_element_type=F32) + cnt_s[...]
    ranks = [jnp.sum(jnp.where(sel, before, 0.0), axis=0, keepdims=True).astype(jnp.int32)
             for sel in sels]
    zi = jnp.zeros((ROUTE_ROWS - TOP_K, tm), jnp.int32)
    exp_ref[0] = jnp.concatenate(idxs + [zi], axis=0)
    rank_ref[0] = jnp.concatenate(ranks + [zi], axis=0)
    gate_ref[0] = jnp.concatenate([e / denom for e in exps] + [zi.astype(F32)], axis=0)
    cnt_s[...] = cnt_s[...] + jnp.sum(hot, axis=1, keepdims=True)
    cnt_ref[...] = jnp.broadcast_to(cnt_s[...], cnt_ref.shape)


def _mixer(x2d, o, u, sga, sgc, meta_halo, wap, wpw, wout, dww, dwb, lng, lnb, nffn, wr_t, br,
           seq, tm):
    n, d = x2d.shape
    nt = n // tm
    hb = tm // HALO
    tri = (jnp.arange(tm)[:, None] < jnp.arange(tm)[None, :]).astype(BF16)
    row = pl.BlockSpec((tm, d), lambda i: (i, 0))
    route = pl.BlockSpec((1, ROUTE_ROWS, tm), lambda i: (i, 0, 0))

    def full(shp):
        return pl.BlockSpec(shp, lambda i: tuple(0 for _ in shp))

    kern = functools.partial(_mixer_kernel, tm=tm, tiles_per_seq=seq // tm)
    return pl.pallas_call(
        kern,
        grid=(nt,),
        in_specs=[row, pl.BlockSpec((N_HEADS, tm, V_DIM), lambda i: (0, i, 0)), row,
                  pl.BlockSpec((HALO, d), lambda i: (jnp.maximum(i * hb - 1, 0), 0)),
                  full((HALO, d)), row, row,
                  full((d, d)), full((d, d)), full((d, d)),
                  full((CONV_K, d)), full((1, d)), full((1, d)), full((1, d)), full((1, d)),
                  full((2 * N_EXPERTS, d)), full((N_EXPERTS, 1)), full((tm, tm))],
        out_specs=[row, pl.BlockSpec((tm * SUBLANES, LANES), lambda i: (i, 0)),
                   route, route, route, full((N_EXPERTS, 128))],
        out_shape=[jax.ShapeDtypeStruct((n, d), F32),
                   jax.ShapeDtypeStruct((n * SUBLANES, LANES), F32),
                   jax.ShapeDtypeStruct((nt, ROUTE_ROWS, tm), jnp.int32),
                   jax.ShapeDtypeStruct((nt, ROUTE_ROWS, tm), jnp.int32),
                   jax.ShapeDtypeStruct((nt, ROUTE_ROWS, tm), F32),
                   jax.ShapeDtypeStruct((N_EXPERTS, 128), F32)],
        scratch_shapes=[pltpu.VMEM((SUBLANES, HALO + tm, d), F32), pltpu.VMEM((tm, d), F32),
                        pltpu.VMEM((N_EXPERTS, 1), F32)],
        compiler_params=pltpu.CompilerParams(
            dimension_semantics=("arbitrary",), vmem_limit_bytes=V7X_VMEM_LIMIT),
        name="mixer_router",
    )(x2d, o, u, u, meta_halo, sga, sgc, wap, wpw, wout, dww, dwb, lng, lnb, nffn, wr_t, br, tri)


def _prefetch_routes(dest_hbm, dsm_s, sem_idx):
    i = pl.program_id(0)
    slot = i % 2

    def idx_copy(tile, s):
        return pltpu.make_async_copy(dest_hbm.at[tile], dsm_s.at[s], sem_idx.at[s])

    @pl.when(i == 0)
    def _():
        idx_copy(0, 0).start()

    idx_copy(i, slot).wait()

    @pl.when(i + 1 < pl.num_programs(0))
    def _():
        idx_copy(i + 1, 1 - slot).start()

    return slot


def _dispatch_kernel(dest_hbm, hn_ref, xs_hbm, dsm_s, sem_idx, sem_rows, *, tm):
    slot = _prefetch_routes(dest_hbm, dsm_s, sem_idx)

    def issue(r, c):
        src = hn_ref.at[pl.ds(pl.multiple_of(r * SUBLANES, SUBLANES), SUBLANES), :]
        for kk in range(TOP_K):
            dst = pl.multiple_of(dsm_s[slot, kk, r] * SUBLANES, SUBLANES)
            pltpu.make_async_copy(src, xs_hbm.at[pl.ds(dst, SUBLANES), :],
                                  sem_rows).start(priority=kk % 2)
        return c
    lax.fori_loop(0, tm, issue, 0, unroll=8)

    for kk in range(TOP_K):
        pltpu.make_async_copy(hn_ref, xs_hbm.at[pl.ds(0, tm * SUBLANES), :], sem_rows).wait()


def _dispatch(dest, hn, n_slots, tm):
    n = hn.shape[0] // SUBLANES
    return pl.pallas_call(
        functools.partial(_dispatch_kernel, tm=tm),
        grid=(n // tm,),
        in_specs=[pl.BlockSpec(memory_space=pl.ANY),
                  pl.BlockSpec((tm * SUBLANES, LANES), lambda i: (i, 0))],
        out_specs=pl.BlockSpec(memory_space=pl.ANY),
        out_shape=jax.ShapeDtypeStruct((n_slots * SUBLANES, LANES), F32),
        scratch_shapes=[pltpu.SMEM((2, ROUTE_ROWS, tm), jnp.int32),
                        pltpu.SemaphoreType.DMA((2,)), pltpu.SemaphoreType.DMA(())],
        compiler_params=pltpu.CompilerParams(
            dimension_semantics=("arbitrary",), vmem_limit_bytes=V7X_VMEM_LIMIT),
        name="dispatch",
    )(dest, hn)


def _expert_kernel(bexp_ref, bvalid_ref, xs_ref, wgu_ref, bgu_ref, wdn_ref, bdn_ref, ys_ref,
                   wgu_bf, wdn_bf):
    i = pl.program_id(0)
    valid = bvalid_ref[i]
    f = wdn_ref.shape[1]

    @pl.when((i == 0) | (bexp_ref[i] != bexp_ref[jnp.maximum(i - 1, 0)]))
    def _():
        wgu_bf[...] = wgu_ref[0].astype(BF16)
        wdn_bf[...] = wdn_ref[0].astype(BF16)

    @pl.when(valid > 0)
    def _():
        x = _load_token_tiles(xs_ref, EXPERT_BLOCK)
        rows = lax.broadcasted_iota(jnp.int32, x.shape, 0)
        xb = jnp.where(rows < valid, x, 0.0).astype(BF16)
        gu = jnp.dot(xb, wgu_bf[...], preferred_element_type=F32) + bgu_ref[0]
        glu = jnp.minimum(gu[:, :f], SWIGLU_LIMIT)
        lin = jnp.clip(gu[:, f:], -SWIGLU_LIMIT, SWIGLU_LIMIT)
        act = glu * _sigmoid(SWIGLU_ALPHA * glu) * (lin + 1.0)
        y = jnp.dot(act.astype(BF16), wdn_bf[...], preferred_element_type=F32) + bdn_ref[0]
        _store_token_tiles(ys_ref, y)

    @pl.when(valid <= 0)
    def _():
        ys_ref[...] = jnp.zeros_like(ys_ref)


def _experts(block_exp, block_valid, xs, wgu, bgu, wdn, bdn):
    d = wgu.shape[1]
    n_slots = xs.shape[0] // SUBLANES
    nb = n_slots // EXPERT_BLOCK
    f2 = wgu.shape[2]
    f = wdn.shape[1]
    tile_rows = EXPERT_BLOCK * SUBLANES
    grid_spec = pltpu.PrefetchScalarGridSpec(
        num_scalar_prefetch=2,
        grid=(nb,),
        in_specs=[pl.BlockSpec((tile_rows, LANES), lambda i, be, bv: (i, 0)),
                  pl.BlockSpec((1, d, f2), lambda i, be, bv: (be[i], 0, 0)),
                  pl.BlockSpec((1, 1, f2), lambda i, be, bv: (be[i], 0, 0)),
                  pl.BlockSpec((1, f, d), lambda i, be, bv: (be[i], 0, 0)),
                  pl.BlockSpec((1, 1, d), lambda i, be, bv: (be[i], 0, 0))],
        out_specs=pl.BlockSpec((tile_rows, LANES), lambda i, be, bv: (i, 0)),
        scratch_shapes=[pltpu.VMEM((d, f2), BF16), pltpu.VMEM((f, d), BF16)],
    )
    return pl.pallas_call(
        _expert_kernel,
        grid_spec=grid_spec,
        out_shape=jax.ShapeDtypeStruct(xs.shape, F32),
        compiler_params=pltpu.CompilerParams(
            dimension_semantics=("arbitrary",), vmem_limit_bytes=V7X_VMEM_LIMIT),
        name="experts",
    )(block_exp, block_valid, xs, wgu, bgu, wdn, bdn)


def _combine_kernel(dest_hbm, h_ref, gate_ref, g_ref, ys_hbm, out_ref, dsm_s, ybuf, sem_idx, sem_rows,
                    *, tm):
    i = pl.program_id(0)
    last = pl.num_programs(0) - 1
    slot = i % 2

    def idx_copy(tile, s):
        return pltpu.make_async_copy(dest_hbm.at[jnp.minimum(tile, last)], dsm_s.at[s],
                                     sem_idx.at[s])

    def row_copy(s, kk, r):
        src = pl.multiple_of(dsm_s[s, kk, r] * SUBLANES, SUBLANES)
        return pltpu.make_async_copy(ys_hbm.at[pl.ds(src, SUBLANES), :],
                                     ybuf.at[s, kk, pl.ds(r * SUBLANES, SUBLANES), :],
                                     sem_rows.at[s])

    def wait_rows(s):
        for kk in range(TOP_K):
            pltpu.make_async_copy(ys_hbm.at[pl.ds(0, tm * SUBLANES), :], ybuf.at[s, kk],
                                  sem_rows.at[s]).wait()

    @pl.when(i == 0)
    def _():
        idx_copy(0, 0).start()
        idx_copy(0, 0).wait()

        def issue(r, c):
            for kk in range(TOP_K):
                row_copy(0, kk, r).start(priority=kk % 2)
            return c
        lax.fori_loop(0, tm, issue, 0, unroll=8)
        idx_copy(1, 1).start()

    idx_copy(i + 1, 1 - slot).wait()
    wait_rows(slot)
    for r in range(tm):
        for kk in range(TOP_K):
            row_copy(1 - slot, kk, r).start(priority=kk % 2)
    idx_copy(i + 2, slot).start()

    gates = jnp.transpose(gate_ref[0])
    acc = h_ref[...]
    for kk in range(TOP_K):
        acc = acc + gates[:, kk:kk + 1] * _load_token_tiles(ybuf.at[slot, kk], tm)
    out_ref[...] = acc * lax.rsqrt(jnp.mean(acc * acc, axis=-1, keepdims=True) + EPS) * g_ref[...]

    @pl.when(i == last)
    def _():
        wait_rows(1 - slot)
        idx_copy(i + 2, slot).wait()


def _combine(dest, h, gates, g_final, ys, tm):
    n, d = h.shape
    return pl.pallas_call(
        functools.partial(_combine_kernel, tm=tm),
        grid=(n // tm,),
        in_specs=[pl.BlockSpec(memory_space=pl.ANY),
                  pl.BlockSpec((tm, d), lambda i: (i, 0)),
                  pl.BlockSpec((1, ROUTE_ROWS, tm), lambda i: (i, 0, 0)),
                  pl.BlockSpec((1, d), lambda i: (0, 0)),
                  pl.BlockSpec(memory_space=pl.ANY)],
        out_specs=pl.BlockSpec((tm, d), lambda i: (i, 0)),
        out_shape=jax.ShapeDtypeStruct((n, d), F32),
        scratch_shapes=[pltpu.SMEM((2, ROUTE_ROWS, tm), jnp.int32),
                        pltpu.VMEM((2, TOP_K, tm * SUBLANES, LANES), F32),
                        pltpu.SemaphoreType.DMA((2,)), pltpu.SemaphoreType.DMA((2,))],
        compiler_params=pltpu.CompilerParams(
            dimension_semantics=("arbitrary",), vmem_limit_bytes=V7X_VMEM_LIMIT),
        name="combine",
    )(dest, h, gates, g_final, ys)


def kernel(x, meta_tokens, norm_mix, w_in, b_in, lam_params, subln_gain, w_attn_proj, conv_dw_w,
           conv_dw_b, conv_ln_g, conv_ln_b, w_conv_proj, w_out, norm_ffn, w_router, b_router,
           w_gate_up, b_gate_up, w_down, b_down, norm_final):
    bsz, seq, d = x.shape
    assert norm_mix.shape[0] == 1, "single-layer block"
    n = bsz * seq
    tm_proj = min(512, seq)
    tq = min(512, seq)
    tm = min(256, seq)
    lam_init = 0.8 - 0.6 * 1.0

    x2d = x.reshape(n, d)
    w_rest = jnp.concatenate([w_in[0][:, d:2 * d], w_in[0][:, 3 * d:]], axis=1).astype(BF16)
    b_rest = jnp.concatenate([b_in[0][d:2 * d], b_in[0][3 * d:]])[None]
    w_t = jnp.concatenate([w_in[0][:, :d], w_in[0][:, 2 * d:3 * d]], axis=1).T.astype(BF16)
    b_t = jnp.concatenate([b_in[0][:d], b_in[0][2 * d:3 * d]])[:, None]
    g_mix = norm_mix[0][None]
    qt, k, vt, u, sga, sgc = _inproj(x2d, g_mix, w_rest, b_rest, w_t, b_t, tm_proj)
    _, k_m, vt_m, u_m, _, _ = _inproj(meta_tokens.astype(x.dtype), g_mix, w_rest, b_rest, w_t, b_t,
                                      N_META)

    o = _attention(lam_params[0], qt, k, vt, k_m, vt_m, subln_gain[0][:, None], bsz, seq, tq,
                   lam_init)

    meta_halo = jnp.concatenate([jnp.zeros((HALO - N_META, d), BF16), u_m], axis=0)
    wr_t = w_router[0].T
    wr_hi = wr_t.astype(BF16)
    wr_split = jnp.concatenate([wr_hi, (wr_t - wr_hi.astype(F32)).astype(BF16)], axis=0)
    h, hn, eidx, rank, gates, counts = _mixer(
        x2d, o, u, sga, sgc, meta_halo,
        w_attn_proj[0].astype(BF16), w_conv_proj[0].astype(BF16), w_out[0].astype(BF16),
        conv_dw_w[0], conv_dw_b[0][None], conv_ln_g[0][None], conv_ln_b[0][None],
        norm_ffn[0][None], wr_split, b_router[0][:, None], seq, tm)

    cnt = counts[:, 0].astype(jnp.int32)
    padded = (cnt + EXPERT_BLOCK - 1) // EXPERT_BLOCK * EXPERT_BLOCK
    pad_end = jnp.cumsum(padded)
    pad_start = pad_end - padded
    n_blocks = -(-(n * TOP_K + N_EXPERTS * (EXPERT_BLOCK - 1)) // EXPERT_BLOCK)
    n_slots = n_blocks * EXPERT_BLOCK
    dest = pad_start[eidx] + rank
    block_start = jnp.arange(n_blocks, dtype=jnp.int32) * EXPERT_BLOCK
    block_exp = jnp.minimum(jnp.sum(block_start[:, None] >= pad_end[None, :], axis=1),
                            N_EXPERTS - 1).astype(jnp.int32)
    block_valid = jnp.clip(pad_start[block_exp] + cnt[block_exp] - block_start, 0,
                           EXPERT_BLOCK).astype(jnp.int32)

    xs = _dispatch(dest, hn, n_slots, tm)
    ys = _experts(block_exp, block_valid, xs, w_gate_up[0], b_gate_up[0][:, None, :],
                  w_down[0], b_down[0][:, None, :])
    out = _combine(dest, h, gates, norm_final[None], ys, tm)
    return out.reshape(bsz, seq, d)
```

```python
import functools

import jax
import jax.numpy as jnp
from jax import lax
from jax.experimental import pallas as pl
from jax.experimental.pallas import tpu as pltpu

CHUNK = 64
N_META = 16
N_HEADS = 8
HEAD_DIM = 64
V_DIM = 2 * HEAD_DIM
CONV_K = 31
N_EXPERTS = 32
TOP_K = 4
SWIGLU_ALPHA = 1.702
SWIGLU_LIMIT = 7.0
EXPERT_BLOCK = 512
INPROJ_ROWS = 512
ATTN_Q_TILE = 512
MIXER_ROWS = 256
EPS = 1e-5
NEG = -1e30
LOG2E = 1.4426950408889634

V7X_VMEM_LIMIT = 56 * 1024 * 1024
SUBLANES = 8
LANES = 128
META_PAD = 128
HALO = 32
ROUTE_ROWS = 8

BF16 = jnp.bfloat16
F32 = jnp.float32


def _sigmoid(x):
    return 1.0 / (1.0 + jnp.exp(-x))


def _load_token_tiles(ref, rows):
    return jnp.concatenate([ref[pl.ds(s, rows, stride=SUBLANES), :] for s in range(SUBLANES)],
                           axis=-1)


def _store_token_tiles(ref, x):
    rows = x.shape[0]
    for s in range(SUBLANES):
        ref[pl.ds(s, rows, stride=SUBLANES), :] = x[:, s * LANES:(s + 1) * LANES]


def _inproj_kernel(x_ref, g_ref, w_ref, b_ref, wt_ref, bt_ref,
                   qt_ref, k_ref, vt_ref, u_ref, sga_ref, sgc_ref):
    d = x_ref.shape[1]
    x = x_ref[...]
    xn = x * lax.rsqrt(jnp.mean(x * x, axis=-1, keepdims=True) + EPS) * g_ref[...]
    xb = xn.astype(BF16)

    def proj(c):
        return (jnp.dot(xb, w_ref[:, c * d:(c + 1) * d], preferred_element_type=F32)
                + b_ref[:, c * d:(c + 1) * d])

    kf = proj(0).astype(BF16)
    for hd in range(N_HEADS):
        k_ref[hd] = kf[:, hd * V_DIM:(hd + 1) * V_DIM]
    t = lax.dot_general(wt_ref[...], xb, (((1,), (1,)), ((), ())), preferred_element_type=F32)
    t = t + bt_ref[...]
    qt_ref[...] = (t[:d] * (HEAD_DIM ** -0.5 * LOG2E)).astype(BF16)
    vt_ref[...] = t[d:].astype(BF16)
    u_ref[...] = (proj(1) * _sigmoid(proj(2))).astype(BF16)
    sga_ref[...] = _sigmoid(proj(3)).astype(BF16)
    sgc_ref[...] = _sigmoid(proj(4)).astype(BF16)


def _inproj(x2d, g, w_bf, b, wt_bf, bt, tm):
    n, d = x2d.shape
    in_w = w_bf.shape[1]
    out = jax.ShapeDtypeStruct((n, d), BF16)
    row = pl.BlockSpec((tm, d), lambda i: (i, 0))
    col = pl.BlockSpec((d, tm), lambda i: (0, i))
    heads = pl.BlockSpec((N_HEADS, tm, V_DIM), lambda i: (0, i, 0))
    out_heads = jax.ShapeDtypeStruct((N_HEADS, n, V_DIM), BF16)
    out_t = jax.ShapeDtypeStruct((d, n), BF16)
    return pl.pallas_call(
        _inproj_kernel,
        grid=(n // tm,),
        in_specs=[row,
                  pl.BlockSpec((1, d), lambda i: (0, 0)),
                  pl.BlockSpec((d, in_w), lambda i: (0, 0)),
                  pl.BlockSpec((1, in_w), lambda i: (0, 0)),
                  pl.BlockSpec((2 * d, d), lambda i: (0, 0)),
                  pl.BlockSpec((2 * d, 1), lambda i: (0, 0))],
        out_specs=[col, heads, col, row, row, row],
        out_shape=[out_t, out_heads, out_t, out, out, out],
        compiler_params=pltpu.CompilerParams(
            dimension_semantics=("arbitrary",), vmem_limit_bytes=V7X_VMEM_LIMIT),
        name="inproj",
    )(x2d, g, w_bf, b, wt_bf, bt)


def _attn_kernel(lam_ref, qt_ref, k_ref, vt_ref, km_ref, vmt_ref, bias_ref, bmeta_ref,
                 gain_ref, o_ref, *scratch, tq, lam_init):
    def q_tile(qi, carry):
        cols = pl.ds(pl.multiple_of(qi * tq, tq), tq)
        _attn_tile(qi, lam_ref, qt_ref.at[:, cols], k_ref, vt_ref, km_ref, vmt_ref, bias_ref,
                   bmeta_ref, gain_ref, o_ref.at[0, cols, :], *scratch, tq=tq, lam_init=lam_init)
        return carry

    lax.fori_loop(0, qt_ref.shape[1] // tq, q_tile, 0)


def _attn_tile(qi, lam_ref, qt_ref, k_ref, vt_ref, km_ref, vmt_ref, bias_ref, bmeta_ref,
               gain_ref, o_ref, m_s, l_s, acc_s, s0, s1, mx0, mx1, *, tq, lam_init):
    h = pl.program_id(1)
    slope = jnp.exp2(-(h + 1).astype(F32))
    lp = lam_ref[...]
    lam = (jnp.exp(jnp.sum(lp[0:1] * lp[1:2], axis=-1, keepdims=True))
           - jnp.exp(jnp.sum(lp[2:3] * lp[3:4], axis=-1, keepdims=True)) + lam_init)

    qt = qt_ref[...]
    feat = lax.broadcasted_iota(jnp.int32, qt.shape, 0)
    zero = jnp.zeros_like(qt)
    qs = (jnp.where(feat < HEAD_DIM, qt, zero), jnp.where(feat >= HEAD_DIM, qt, zero))

    def scores_t(kk, qm):
        return jnp.dot(kk, qm, preferred_element_type=F32)

    tk = tq // 2
    n_off = 2 * qi

    def tile_shift(t):
        off = (qi * tq - t * tk).astype(F32) * LOG2E
        return jnp.where(t < n_off, slope * off, 0.0)

    def stage_a(t, s_buf, mx_buf, cols=slice(None)):
        kk = k_ref[0, pl.ds(pl.multiple_of(t * tk, tk), tk), :]
        bias = bias_ref[0, jnp.where(t < n_off, 0, t - n_off + 1), :, cols]
        for idx in range(2):
            s = scores_t(kk, qs[idx][:, cols]) + bias
            s_buf[idx, :, cols] = s
            mx_buf[idx, :, cols] = jnp.max(s, axis=0, keepdims=True)

    def stage_b(t, s_buf, mx_buf, cols=slice(None), meta=None):
        vt = vt_ref[:, pl.ds(pl.multiple_of(t * tk, tk), tk)]
        vt1 = jnp.concatenate([vt, jnp.ones((2 * SUBLANES, tk), BF16)], axis=0)
        shift = tile_shift(t)
        for idx in range(2):
            m_old = m_s[idx, :, cols]
            m_new = jnp.maximum(m_old, mx_buf[idx, :, cols] - shift)
            if meta is not None:
                s_meta, vmt1, shift_m = meta
                m_new = jnp.maximum(m_new, jnp.max(s_meta[idx], axis=0, keepdims=True) - shift_m)
            p = jnp.exp2(s_buf[idx, :, cols] - (m_new + shift))
            pv1 = jnp.dot(vt1, p.astype(BF16), preferred_element_type=F32)
            if meta is not None:
                pm = jnp.exp2(s_meta[idx] - (m_new + shift_m))
                pv1 = pv1 + jnp.dot(vmt1, pm.astype(BF16), preferred_element_type=F32)
            pv = pv1[:V_DIM]
            ps = pv1[V_DIM:V_DIM + 1]
            alpha = jnp.exp2(m_old - m_new)
            l_s[idx, :, cols] = alpha * l_s[idx, :, cols] + ps
            acc_s[idx, :, cols] = alpha * acc_s[idx, :, cols] + pv
            m_s[idx, :, cols] = m_new

    m_s[...] = jnp.full(m_s.shape, NEG, F32)
    l_s[...] = jnp.zeros_like(l_s)
    acc_s[...] = jnp.zeros_like(acc_s)
    stage_a(0, s0, mx0)

    def pair(i):
        stage_a(2 * i + 1, s1, mx1)
        stage_b(2 * i, s0, mx0)
        stage_a(2 * i + 2, s0, mx0)
        stage_b(2 * i + 1, s1, mx1)

    def body(ii, carry):
        pair(2 * ii)
        pair(2 * ii + 1)
        return carry

    lax.fori_loop(0, qi // 2, body, 0)

    @pl.when(qi % 2 == 1)
    def _():
        pair(qi - 1)

    late = slice(tq // 2, tq)
    stage_a(n_off + 1, s1, mx1, late)
    s_meta = [scores_t(km_ref[0], qs[idx]) + bmeta_ref[0] for idx in range(2)]
    vmt1 = jnp.concatenate([vmt_ref[...], jnp.ones((2 * SUBLANES, META_PAD), BF16)], axis=0)
    shift0 = slope * ((qi * tq).astype(F32) * LOG2E)
    stage_b(n_off, s0, mx0, meta=(s_meta, vmt1, shift0))
    stage_b(n_off + 1, s1, mx1, late)

    o = acc_s[0] / l_s[0] - lam * (acc_s[1] / l_s[1])
    o = o * lax.rsqrt(jnp.mean(o * o, axis=0, keepdims=True) + EPS) * gain_ref[...]
    o_ref[...] = jnp.transpose(o * (1.0 - lam_init)).astype(BF16)


def _attention(lam_params, qt, k, vt, k_meta, vt_meta, gain, bsz, seq, tq, lam_init):
    n = qt.shape[1]
    slopes = jnp.asarray([2.0 ** (-8.0 * (h + 1) / N_HEADS) for h in range(N_HEADS)], F32)
    tk = tq // 2
    r = jnp.arange(tq, dtype=jnp.int32)
    diff = r[None, :] - r[:, None]
    slopes = slopes * LOG2E
    boff = -slopes[:, None, None] * diff.astype(F32)[None, :tk]
    vis = (r[:, None] // CHUNK) <= (r[None, :] // CHUNK)
    bdiag = jnp.where(vis[None], -slopes[:, None, None] * jnp.abs(diff).astype(F32)[None], NEG)
    bias_all = jnp.stack([boff, bdiag[:, :tk], bdiag[:, tk:]], axis=1)
    mrow = jnp.arange(META_PAD, dtype=jnp.int32)
    dmeta = (N_META + r[None, :] - mrow[:, None]).astype(F32)
    bmeta = jnp.where((mrow < N_META)[None, :, None], -slopes[:, None, None] * dmeta[None], NEG)

    km = jnp.pad(k_meta, ((0, 0), (0, META_PAD - N_META), (0, 0)))
    vmt = jnp.pad(vt_meta, ((0, 0), (0, META_PAD - N_META)))

    kern = functools.partial(_attn_kernel, tq=tq, lam_init=lam_init)
    return pl.pallas_call(
        kern,
        grid=(bsz, N_HEADS),
        in_specs=[
            pl.BlockSpec((4, HEAD_DIM), lambda b, h: (0, 0)),
            pl.BlockSpec((V_DIM, seq), lambda b, h: (h, b)),
            pl.BlockSpec((1, seq, V_DIM), lambda b, h: (h, b, 0)),
            pl.BlockSpec((V_DIM, seq), lambda b, h: (h, b)),
            pl.BlockSpec((1, META_PAD, V_DIM), lambda b, h: (h, 0, 0)),
            pl.BlockSpec((V_DIM, META_PAD), lambda b, h: (h, 0)),
            pl.BlockSpec((1, 3, tk, tq), lambda b, h: (h, 0, 0, 0)),
            pl.BlockSpec((1, META_PAD, tq), lambda b, h: (h, 0, 0)),
            pl.BlockSpec((V_DIM, 1), lambda b, h: (0, 0)),
        ],
        out_specs=pl.BlockSpec((1, seq, V_DIM), lambda b, h: (h, b, 0)),
        out_shape=jax.ShapeDtypeStruct((N_HEADS, n, V_DIM), BF16),
        scratch_shapes=[pltpu.VMEM((2, 1, tq), F32), pltpu.VMEM((2, 1, tq), F32),
                        pltpu.VMEM((2, V_DIM, tq), F32),
                        pltpu.VMEM((2, tk, tq), F32), pltpu.VMEM((2, tk, tq), F32),
                        pltpu.VMEM((2, 1, tq), F32), pltpu.VMEM((2, 1, tq), F32)],
        compiler_params=pltpu.CompilerParams(
            dimension_semantics=("arbitrary", "arbitrary"),
            vmem_limit_bytes=V7X_VMEM_LIMIT),
        name="diff_attention",
    )(lam_params, qt, k, vt, km, vmt, bias_all, bmeta, gain)


def _mixer_kernel(x_ref, o_ref, u_ref, uh_ref, mh_ref, sga_ref, sgc_ref,
                  wap_ref, wpw_ref, wout_ref, dww_ref, dwb_ref, lng_ref, lnb_ref,
                  nffn_ref, wr_ref, br_ref, tri_ref,
                  h_ref, hn_ref, exp_ref, rank_ref, gate_ref, cnt_ref,
                  ext_s, y_s, cnt_s, *, tm, tiles_per_seq):
    i = pl.program_id(0)
    d = x_ref.shape[1]

    @pl.when(i == 0)
    def _():
        cnt_s[...] = jnp.zeros_like(cnt_s)

    first = (i % tiles_per_seq) == 0
    halo = jnp.where(first, mh_ref[...], uh_ref[...])
    ext_s[0, 0:HALO, :] = halo.astype(F32)
    ext_s[0, HALO:HALO + tm, :] = u_ref[...].astype(F32)
    span = tm + HALO - SUBLANES
    for b in range(1, SUBLANES):
        ext_s[b, 0:span, :] = ext_s[0, b:b + span, :]
    off = HALO - (CONV_K - 1)

    o = jnp.concatenate([o_ref[hd] for hd in range(N_HEADS)], axis=-1)
    y_attn = jnp.dot(o, wap_ref[...], preferred_element_type=F32)
    for c in range(d // LANES):
        cs = slice(c * LANES, (c + 1) * LANES)
        for r0 in (0, tm // 2):
            acc = jnp.zeros((tm // 2, LANES), F32)
            for t in range(CONV_K):
                a, b = divmod(off + t, SUBLANES)
                row = a * SUBLANES + r0
                acc = acc + dww_ref[t:t + 1, cs] * ext_s[b, row:row + tm // 2, cs]
            y_s[r0:r0 + tm // 2, cs] = acc
    y = y_s[...] + dwb_ref[...]
    mu = jnp.mean(y, axis=-1, keepdims=True)
    yc = y - mu
    var = jnp.mean(yc * yc, axis=-1, keepdims=True)
    y = yc * lax.rsqrt(var + EPS) * lng_ref[...] + lnb_ref[...]
    y = y * _sigmoid(y)
    y_conv = jnp.dot(y.astype(BF16), wpw_ref[...], preferred_element_type=F32)

    mixed = sga_ref[...].astype(F32) * y_attn + sgc_ref[...].astype(F32) * y_conv
    h = x_ref[...] + jnp.dot(mixed.astype(BF16), wout_ref[...], preferred_element_type=F32)
    h_ref[...] = h
    hn = h * lax.rsqrt(jnp.mean(h * h, axis=-1, keepdims=True) + EPS) * nffn_ref[...]
    _store_token_tiles(hn_ref, hn)

    nt_dims = (((1,), (1,)), ((), ()))
    hn_hi = hn.astype(BF16)
    hn_lo = (hn - hn_hi.astype(F32)).astype(BF16)
    both = lax.dot_general(wr_ref[...], hn_hi, nt_dims, preferred_element_type=F32)
    logits = (both[:N_EXPERTS] + both[N_EXPERTS:]
              + lax.dot_general(wr_ref[0:N_EXPERTS, :], hn_lo, nt_dims, preferred_element_type=F32)
              + br_ref[...])
    eio = lax.broadcasted_iota(jnp.int32, logits.shape, 0)
    sels, tops, idxs = [], [], []
    cur = logits
    for _ in range(TOP_K):
        m = jnp.max(cur, axis=0, keepdims=True)
        idx = jnp.min(jnp.where(cur == m, eio, N_EXPERTS), axis=0, keepdims=True)
        sel = eio == idx
        sels.append(sel)
        tops.append(m)
        idxs.append(idx)
        cur = jnp.where(sel, -jnp.inf, cur)
    exps = [jnp.exp(t - tops[0]) for t in tops]
    denom = exps[0] + exps[1] + exps[2] + exps[3]
    hot = jnp.zeros(logits.shape, F32)
    for sel in sels:
        hot = hot + jnp.where(sel, 1.0, 0.0)
    before = jnp.dot(hot.astype(BF16), tri_ref[...], preferred_element_type=F32) + cnt_s[...]
    ranks = [jnp.sum(jnp.where(sel, before, 0.0), axis=0, keepdims=True).astype(jnp.int32)
             for sel in sels]
    zi = jnp.zeros((ROUTE_ROWS - TOP_K, tm), jnp.int32)
    exp_ref[0] = jnp.concatenate(idxs + [zi], axis=0)
    rank_ref[0] = jnp.concatenate(ranks + [zi], axis=0)
    gate_ref[0] = jnp.concatenate([e / denom for e in exps] + [zi.astype(F32)], axis=0)
    cnt_s[...] = cnt_s[...] + jnp.sum(hot, axis=1, keepdims=True)
    cnt_ref[...] = jnp.broadcast_to(cnt_s[...], cnt_ref.shape)


def _mixer(x2d, o, u, sga, sgc, meta_halo, wap, wpw, wout, dww, dwb, lng, lnb, nffn, wr_t, br,
           seq, tm):
    n, d = x2d.shape
    nt = n // tm
    hb = tm // HALO
    tri = (jnp.arange(tm)[:, None] < jnp.arange(tm)[None, :]).astype(BF16)
    row = pl.BlockSpec((tm, d), lambda i: (i, 0))
    route = pl.BlockSpec((1, ROUTE_ROWS, tm), lambda i: (i, 0, 0))

    def full(shp):
        return pl.BlockSpec(shp, lambda i: tuple(0 for _ in shp))

    kern = functools.partial(_mixer_kernel, tm=tm, tiles_per_seq=seq // tm)
    return pl.pallas_call(
        kern,
        grid=(nt,),
        in_specs=[row, pl.BlockSpec((N_HEADS, tm, V_DIM), lambda i: (0, i, 0)), row,
                  pl.BlockSpec((HALO, d), lambda i: (jnp.maximum(i * hb - 1, 0), 0)),
                  full((HALO, d)), row, row,
                  full((d, d)), full((d, d)), full((d, d)),
                  full((CONV_K, d)), full((1, d)), full((1, d)), full((1, d)), full((1, d)),
                  full((2 * N_EXPERTS, d)), full((N_EXPERTS, 1)), full((tm, tm))],
        out_specs=[row, pl.BlockSpec((tm * SUBLANES, LANES), lambda i: (i, 0)),
                   route, route, route, full((N_EXPERTS, LANES))],
        out_shape=[jax.ShapeDtypeStruct((n, d), F32),
                   jax.ShapeDtypeStruct((n * SUBLANES, LANES), F32),
                   jax.ShapeDtypeStruct((nt, ROUTE_ROWS, tm), jnp.int32),
                   jax.ShapeDtypeStruct((nt, ROUTE_ROWS, tm), jnp.int32),
                   jax.ShapeDtypeStruct((nt, ROUTE_ROWS, tm), F32),
                   jax.ShapeDtypeStruct((N_EXPERTS, LANES), F32)],
        scratch_shapes=[pltpu.VMEM((SUBLANES, HALO + tm, d), F32), pltpu.VMEM((tm, d), F32),
                        pltpu.VMEM((N_EXPERTS, 1), F32)],
        compiler_params=pltpu.CompilerParams(
            dimension_semantics=("arbitrary",), vmem_limit_bytes=V7X_VMEM_LIMIT),
        name="mixer_router",
    )(x2d, o, u, u, meta_halo, sga, sgc, wap, wpw, wout, dww, dwb, lng, lnb, nffn, wr_t, br, tri)


def _prefetch_routes(dest_hbm, dsm_s, sem_idx):
    i = pl.program_id(0)
    slot = i % 2

    def idx_copy(tile, s):
        return pltpu.make_async_copy(dest_hbm.at[tile], dsm_s.at[s], sem_idx.at[s])

    @pl.when(i == 0)
    def _():
        idx_copy(0, 0).start()

    idx_copy(i, slot).wait()

    @pl.when(i + 1 < pl.num_programs(0))
    def _():
        idx_copy(i + 1, 1 - slot).start()

    return slot


def _dispatch_kernel(dest_hbm, hn_ref, xs_hbm, dsm_s, sem_idx, sem_rows, *, tm):
    slot = _prefetch_routes(dest_hbm, dsm_s, sem_idx)

    def issue(r, c):
        src = hn_ref.at[pl.ds(pl.multiple_of(r * SUBLANES, SUBLANES), SUBLANES), :]
        for kk in range(TOP_K):
            dst = pl.multiple_of(dsm_s[slot, kk, r] * SUBLANES, SUBLANES)
            pltpu.make_async_copy(src, xs_hbm.at[pl.ds(dst, SUBLANES), :],
                                  sem_rows).start(priority=kk % 2)
        return c
    lax.fori_loop(0, tm, issue, 0, unroll=8)

    for kk in range(TOP_K):
        pltpu.make_async_copy(hn_ref, xs_hbm.at[pl.ds(0, tm * SUBLANES), :], sem_rows).wait()


def _dispatch(dest, hn, n_slots, tm):
    n = hn.shape[0] // SUBLANES
    return pl.pallas_call(
        functools.partial(_dispatch_kernel, tm=tm),
        grid=(n // tm,),
        in_specs=[pl.BlockSpec(memory_space=pl.ANY),
                  pl.BlockSpec((tm * SUBLANES, LANES), lambda i: (i, 0))],
        out_specs=pl.BlockSpec(memory_space=pl.ANY),
        out_shape=jax.ShapeDtypeStruct((n_slots * SUBLANES, LANES), F32),
        scratch_shapes=[pltpu.SMEM((2, ROUTE_ROWS, tm), jnp.int32),
                        pltpu.SemaphoreType.DMA((2,)), pltpu.SemaphoreType.DMA(())],
        compiler_params=pltpu.CompilerParams(
            dimension_semantics=("arbitrary",), vmem_limit_bytes=V7X_VMEM_LIMIT),
        name="dispatch",
    )(dest, hn)


def _expert_kernel(bexp_ref, bvalid_ref, xs_ref, wgu_ref, bgu_ref, wdn_ref, bdn_ref, ys_ref,
                   wgu_bf, wdn_bf):
    i = pl.program_id(0)
    valid = bvalid_ref[i]
    f = wdn_ref.shape[1]

    @pl.when((i == 0) | (bexp_ref[i] != bexp_ref[jnp.maximum(i - 1, 0)]))
    def _():
        wgu_bf[...] = wgu_ref[0].astype(BF16)
        wdn_bf[...] = wdn_ref[0].astype(BF16)

    @pl.when(valid > 0)
    def _():
        x = _load_token_tiles(xs_ref, EXPERT_BLOCK)
        rows = lax.broadcasted_iota(jnp.int32, x.shape, 0)
        xb = jnp.where(rows < valid, x, 0.0).astype(BF16)
        gu = jnp.dot(xb, wgu_bf[...], preferred_element_type=F32) + bgu_ref[0]
        glu = jnp.minimum(gu[:, :f], SWIGLU_LIMIT)
        lin = jnp.clip(gu[:, f:], -SWIGLU_LIMIT, SWIGLU_LIMIT)
        act = glu * _sigmoid(SWIGLU_ALPHA * glu) * (lin + 1.0)
        y = jnp.dot(act.astype(BF16), wdn_bf[...], preferred_element_type=F32) + bdn_ref[0]
        _store_token_tiles(ys_ref, y)

    @pl.when(valid <= 0)
    def _():
        ys_ref[...] = jnp.zeros_like(ys_ref)


def _experts(block_exp, block_valid, xs, wgu, bgu, wdn, bdn):
    d = wgu.shape[1]
    n_slots = xs.shape[0] // SUBLANES
    nb = n_slots // EXPERT_BLOCK
    f2 = wgu.shape[2]
    f = wdn.shape[1]
    tile_rows = EXPERT_BLOCK * SUBLANES
    grid_spec = pltpu.PrefetchScalarGridSpec(
        num_scalar_prefetch=2,
        grid=(nb,),
        in_specs=[pl.BlockSpec((tile_rows, LANES), lambda i, be, bv: (i, 0)),
                  pl.BlockSpec((1, d, f2), lambda i, be, bv: (be[i], 0, 0)),
                  pl.BlockSpec((1, 1, f2), lambda i, be, bv: (be[i], 0, 0)),
                  pl.BlockSpec((1, f, d), lambda i, be, bv: (be[i], 0, 0)),
                  pl.BlockSpec((1, 1, d), lambda i, be, bv: (be[i], 0, 0))],
        out_specs=pl.BlockSpec((tile_rows, LANES), lambda i, be, bv: (i, 0)),
        scratch_shapes=[pltpu.VMEM((d, f2), BF16), pltpu.VMEM((f, d), BF16)],
    )
    return pl.pallas_call(
        _expert_kernel,
        grid_spec=grid_spec,
        out_shape=jax.ShapeDtypeStruct(xs.shape, F32),
        compiler_params=pltpu.CompilerParams(
            dimension_semantics=("arbitrary",), vmem_limit_bytes=V7X_VMEM_LIMIT),
        name="experts",
    )(block_exp, block_valid, xs, wgu, bgu, wdn, bdn)


def _combine_kernel(dest_hbm, h_ref, gate_ref, g_ref, ys_hbm, out_ref, dsm_s, ybuf, sem_idx, sem_rows,
                    *, tm):
    i = pl.program_id(0)
    last = pl.num_programs(0) - 1
    slot = i % 2

    def idx_copy(tile, s):
        return pltpu.make_async_copy(dest_hbm.at[jnp.minimum(tile, last)], dsm_s.at[s],
                                     sem_idx.at[s])

    def row_copy(s, kk, r):
        src = pl.multiple_of(dsm_s[s, kk, r] * SUBLANES, SUBLANES)
        return pltpu.make_async_copy(ys_hbm.at[pl.ds(src, SUBLANES), :],
                                     ybuf.at[s, kk, pl.ds(r * SUBLANES, SUBLANES), :],
                                     sem_rows.at[s])

    def wait_rows(s):
        for kk in range(TOP_K):
            pltpu.make_async_copy(ys_hbm.at[pl.ds(0, tm * SUBLANES), :], ybuf.at[s, kk],
                                  sem_rows.at[s]).wait()

    @pl.when(i == 0)
    def _():
        idx_copy(0, 0).start()
        idx_copy(0, 0).wait()

        def issue(r, c):
            for kk in range(TOP_K):
                row_copy(0, kk, r).start(priority=kk % 2)
            return c
        lax.fori_loop(0, tm, issue, 0, unroll=8)
        idx_copy(1, 1).start()

    idx_copy(i + 1, 1 - slot).wait()
    wait_rows(slot)
    for r in range(tm):
        for kk in range(TOP_K):
            row_copy(1 - slot, kk, r).start(priority=kk % 2)
    idx_copy(i + 2, slot).start()

    gates = jnp.transpose(gate_ref[0])
    acc = h_ref[...]
    for kk in range(TOP_K):
        acc = acc + gates[:, kk:kk + 1] * _load_token_tiles(ybuf.at[slot, kk], tm)
    out_ref[...] = acc * lax.rsqrt(jnp.mean(acc * acc, axis=-1, keepdims=True) + EPS) * g_ref[...]

    @pl.when(i == last)
    def _():
        wait_rows(1 - slot)
        idx_copy(i + 2, slot).wait()


def _combine(dest, h, gates, g_final, ys, tm):
    n, d = h.shape
    return pl.pallas_call(
        functools.partial(_combine_kernel, tm=tm),
        grid=(n // tm,),
        in_specs=[pl.BlockSpec(memory_space=pl.ANY),
                  pl.BlockSpec((tm, d), lambda i: (i, 0)),
                  pl.BlockSpec((1, ROUTE_ROWS, tm), lambda i: (i, 0, 0)),
                  pl.BlockSpec((1, d), lambda i: (0, 0)),
                  pl.BlockSpec(memory_space=pl.ANY)],
        out_specs=pl.BlockSpec((tm, d), lambda i: (i, 0)),
        out_shape=jax.ShapeDtypeStruct((n, d), F32),
        scratch_shapes=[pltpu.SMEM((2, ROUTE_ROWS, tm), jnp.int32),
                        pltpu.VMEM((2, TOP_K, tm * SUBLANES, LANES), F32),
                        pltpu.SemaphoreType.DMA((2,)), pltpu.SemaphoreType.DMA((2,))],
        compiler_params=pltpu.CompilerParams(
            dimension_semantics=("arbitrary",), vmem_limit_bytes=V7X_VMEM_LIMIT),
        name="combine",
    )(dest, h, gates, g_final, ys)


def kernel(x, meta_tokens, norm_mix, w_in, b_in, lam_params, subln_gain, w_attn_proj, conv_dw_w,
           conv_dw_b, conv_ln_g, conv_ln_b, w_conv_proj, w_out, norm_ffn, w_router, b_router,
           w_gate_up, b_gate_up, w_down, b_down, norm_final):
    bsz, seq, d = x.shape
    assert norm_mix.shape[0] == 1, "single-layer block"
    n = bsz * seq
    tm_proj = min(INPROJ_ROWS, seq)
    tq = min(ATTN_Q_TILE, seq)
    tm = min(MIXER_ROWS, seq)
    lam_init = 0.8 - 0.6 * 1.0

    x2d = x.reshape(n, d)
    w_rest = jnp.concatenate([w_in[0][:, d:2 * d], w_in[0][:, 3 * d:]], axis=1).astype(BF16)
    b_rest = jnp.concatenate([b_in[0][d:2 * d], b_in[0][3 * d:]])[None]
    w_t = jnp.concatenate([w_in[0][:, :d], w_in[0][:, 2 * d:3 * d]], axis=1).T.astype(BF16)
    b_t = jnp.concatenate([b_in[0][:d], b_in[0][2 * d:3 * d]])[:, None]
    g_mix = norm_mix[0][None]
    qt, k, vt, u, sga, sgc = _inproj(x2d, g_mix, w_rest, b_rest, w_t, b_t, tm_proj)
    _, k_m, vt_m, u_m, _, _ = _inproj(meta_tokens.astype(x.dtype), g_mix, w_rest, b_rest, w_t, b_t,
                                      N_META)

    o = _attention(lam_params[0], qt, k, vt, k_m, vt_m, subln_gain[0][:, None], bsz, seq, tq,
                   lam_init)

    meta_halo = jnp.concatenate([jnp.zeros((HALO - N_META, d), BF16), u_m], axis=0)
    wr_t = w_router[0].T
    wr_hi = wr_t.astype(BF16)
    wr_split = jnp.concatenate([wr_hi, (wr_t - wr_hi.astype(F32)).astype(BF16)], axis=0)
    h, hn, eidx, rank, gates, counts = _mixer(
        x2d, o, u, sga, sgc, meta_halo,
        w_attn_proj[0].astype(BF16), w_conv_proj[0].astype(BF16), w_out[0].astype(BF16),
        conv_dw_w[0], conv_dw_b[0][None], conv_ln_g[0][None], conv_ln_b[0][None],
        norm_ffn[0][None], wr_split, b_router[0][:, None], seq, tm)

    cnt = counts[:, 0].astype(jnp.int32)
    padded = (cnt + EXPERT_BLOCK - 1) // EXPERT_BLOCK * EXPERT_BLOCK
    pad_end = jnp.cumsum(padded)
    pad_start = pad_end - padded
    n_blocks = -(-(n * TOP_K + N_EXPERTS * (EXPERT_BLOCK - 1)) // EXPERT_BLOCK)
    n_slots = n_blocks * EXPERT_BLOCK
    dest = pad_start[eidx] + rank
    block_start = jnp.arange(n_blocks, dtype=jnp.int32) * EXPERT_BLOCK
    block_exp = jnp.minimum(jnp.sum(block_start[:, None] >= pad_end[None, :], axis=1),
                            N_EXPERTS - 1).astype(jnp.int32)
    block_valid = jnp.clip(pad_start[block_exp] + cnt[block_exp] - block_start, 0,
                           EXPERT_BLOCK).astype(jnp.int32)

    xs = _dispatch(dest, hn, n_slots, tm)
    ys = _experts(block_exp, block_valid, xs, w_gate_up[0], b_gate_up[0][:, None, :],
                  w_down[0], b_down[0][:, None, :])
    out = _combine(dest, h, gates, norm_final[None], ys, tm)
    return out.reshape(bsz, seq, d)
```

```python
import functools

import jax
import jax.numpy as jnp
from jax import lax
from jax.experimental import pallas as pl
from jax.experimental.pallas import tpu as pltpu

CHUNK = 64
N_META = 16
N_HEADS = 8
HEAD_DIM = 64
V_DIM = 2 * HEAD_DIM
CONV_K = 31
N_EXPERTS = 32
TOP_K = 4
SWIGLU_ALPHA = 1.702
SWIGLU_LIMIT = 7.0
EXPERT_BLOCK = 1024
INPROJ_ROWS = 512
ATTN_Q_TILE = 512
MIXER_ROWS = 256
EPS = 1e-5
NEG = -1e30
LOG2E = 1.4426950408889634

V7X_VMEM_LIMIT = 56 * 1024 * 1024
SUBLANES = 8
LANES = 128
META_PAD = 128
HALO = 32
ROUTE_ROWS = 8

BF16 = jnp.bfloat16
F32 = jnp.float32


def _sigmoid(x):
    return 1.0 / (1.0 + jnp.exp(-x))


def _load_token_tiles(ref, rows):
    return jnp.concatenate([ref[pl.ds(s, rows, stride=SUBLANES), :] for s in range(SUBLANES)],
                           axis=-1)


def _store_token_tiles(ref, x):
    rows = x.shape[0]
    for s in range(SUBLANES):
        ref[pl.ds(s, rows, stride=SUBLANES), :] = x[:, s * LANES:(s + 1) * LANES]


def _inproj_kernel(x_ref, g_ref, w_ref, b_ref, wt_ref, bt_ref,
                   qt_ref, k_ref, vt_ref, u_ref, sga_ref, sgc_ref):
    d = x_ref.shape[1]
    x = x_ref[...]
    xn = x * lax.rsqrt(jnp.mean(x * x, axis=-1, keepdims=True) + EPS) * g_ref[...]
    xb = xn.astype(BF16)

    def proj(c):
        return (jnp.dot(xb, w_ref[:, c * d:(c + 1) * d], preferred_element_type=F32)
                + b_ref[:, c * d:(c + 1) * d])

    kf = proj(0).astype(BF16)
    for hd in range(N_HEADS):
        k_ref[hd] = kf[:, hd * V_DIM:(hd + 1) * V_DIM]
    t = lax.dot_general(wt_ref[...], xb, (((1,), (1,)), ((), ())), preferred_element_type=F32)
    t = t + bt_ref[...]
    qt_ref[...] = (t[:d] * (HEAD_DIM ** -0.5 * LOG2E)).astype(BF16)
    vt_ref[...] = t[d:].astype(BF16)
    u_ref[...] = (proj(1) * _sigmoid(proj(2))).astype(BF16)
    sga_ref[...] = _sigmoid(proj(3)).astype(BF16)
    sgc_ref[...] = _sigmoid(proj(4)).astype(BF16)


def _inproj(x2d, g, w_bf, b, wt_bf, bt, tm):
    n, d = x2d.shape
    in_w = w_bf.shape[1]
    out = jax.ShapeDtypeStruct((n, d), BF16)
    row = pl.BlockSpec((tm, d), lambda i: (i, 0))
    col = pl.BlockSpec((d, tm), lambda i: (0, i))
    heads = pl.BlockSpec((N_HEADS, tm, V_DIM), lambda i: (0, i, 0))
    out_heads = jax.ShapeDtypeStruct((N_HEADS, n, V_DIM), BF16)
    out_t = jax.ShapeDtypeStruct((d, n), BF16)
    return pl.pallas_call(
        _inproj_kernel,
        grid=(n // tm,),
        in_specs=[row,
                  pl.BlockSpec((1, d), lambda i: (0, 0)),
                  pl.BlockSpec((d, in_w), lambda i: (0, 0)),
                  pl.BlockSpec((1, in_w), lambda i: (0, 0)),
                  pl.BlockSpec((2 * d, d), lambda i: (0, 0)),
                  pl.BlockSpec((2 * d, 1), lambda i: (0, 0))],
        out_specs=[col, heads, col, row, row, row],
        out_shape=[out_t, out_heads, out_t, out, out, out],
        compiler_params=pltpu.CompilerParams(
            dimension_semantics=("arbitrary",), vmem_limit_bytes=V7X_VMEM_LIMIT),
        name="inproj",
    )(x2d, g, w_bf, b, wt_bf, bt)


def _attn_kernel(lam_ref, qt_ref, k_ref, vt_ref, km_ref, vmt_ref, bias_ref, bmeta_ref,
                 gain_ref, o_ref, *scratch, tq, lam_init):
    def q_tile(qi, carry):
        cols = pl.ds(pl.multiple_of(qi * tq, tq), tq)
        _attn_tile(qi, lam_ref, qt_ref.at[:, cols], k_ref, vt_ref, km_ref, vmt_ref, bias_ref,
                   bmeta_ref, gain_ref, o_ref.at[0, cols, :], *scratch, tq=tq, lam_init=lam_init)
        return carry

    lax.fori_loop(0, qt_ref.shape[1] // tq, q_tile, 0)


def _attn_tile(qi, lam_ref, qt_ref, k_ref, vt_ref, km_ref, vmt_ref, bias_ref, bmeta_ref,
               gain_ref, o_ref, m_s, l_s, acc_s, s0, s1, mx0, mx1, *, tq, lam_init):
    h = pl.program_id(1)
    slope = jnp.exp2(-(h + 1).astype(F32))
    lp = lam_ref[...]
    lam = (jnp.exp(jnp.sum(lp[0:1] * lp[1:2], axis=-1, keepdims=True))
           - jnp.exp(jnp.sum(lp[2:3] * lp[3:4], axis=-1, keepdims=True)) + lam_init)

    qt = qt_ref[...]
    feat = lax.broadcasted_iota(jnp.int32, qt.shape, 0)
    zero = jnp.zeros_like(qt)
    qs = (jnp.where(feat < HEAD_DIM, qt, zero), jnp.where(feat >= HEAD_DIM, qt, zero))

    def scores_t(kk, qm):
        return jnp.dot(kk, qm, preferred_element_type=F32)

    tk = tq // 2
    n_off = 2 * qi

    def tile_shift(t):
        off = (qi * tq - t * tk).astype(F32) * LOG2E
        return jnp.where(t < n_off, slope * off, 0.0)

    def stage_a(t, s_buf, mx_buf, cols=slice(None)):
        kk = k_ref[0, pl.ds(pl.multiple_of(t * tk, tk), tk), :]
        bias = bias_ref[0, jnp.where(t < n_off, 0, t - n_off + 1), :, cols]
        for idx in range(2):
            s = scores_t(kk, qs[idx][:, cols]) + bias
            s_buf[idx, :, cols] = s
            mx_buf[idx, :, cols] = jnp.max(s, axis=0, keepdims=True)

    def stage_b(t, s_buf, mx_buf, cols=slice(None), meta=None):
        vt = vt_ref[:, pl.ds(pl.multiple_of(t * tk, tk), tk)]
        vt1 = jnp.concatenate([vt, jnp.ones((2 * SUBLANES, tk), BF16)], axis=0)
        shift = tile_shift(t)
        for idx in range(2):
            m_old = m_s[idx, :, cols]
            m_new = jnp.maximum(m_old, mx_buf[idx, :, cols] - shift)
            if meta is not None:
                s_meta, vmt1, shift_m = meta
                m_new = jnp.maximum(m_new, jnp.max(s_meta[idx], axis=0, keepdims=True) - shift_m)
            p = jnp.exp2(s_buf[idx, :, cols] - (m_new + shift))
            pv1 = jnp.dot(vt1, p.astype(BF16), preferred_element_type=F32)
            if meta is not None:
                pm = jnp.exp2(s_meta[idx] - (m_new + shift_m))
                pv1 = pv1 + jnp.dot(vmt1, pm.astype(BF16), preferred_element_type=F32)
            pv = pv1[:V_DIM]
            ps = pv1[V_DIM:V_DIM + 1]
            alpha = jnp.exp2(m_old - m_new)
            l_s[idx, :, cols] = alpha * l_s[idx, :, cols] + ps
            acc_s[idx, :, cols] = alpha * acc_s[idx, :, cols] + pv
            m_s[idx, :, cols] = m_new

    m_s[...] = jnp.full(m_s.shape, NEG, F32)
    l_s[...] = jnp.zeros_like(l_s)
    acc_s[...] = jnp.zeros_like(acc_s)
    stage_a(0, s0, mx0)

    def pair(i):
        stage_a(2 * i + 1, s1, mx1)
        stage_b(2 * i, s0, mx0)
        stage_a(2 * i + 2, s0, mx0)
        stage_b(2 * i + 1, s1, mx1)

    def body(ii, carry):
        pair(2 * ii)
        pair(2 * ii + 1)
        return carry

    lax.fori_loop(0, qi // 2, body, 0)

    @pl.when(qi % 2 == 1)
    def _():
        pair(qi - 1)

    late = slice(tq // 2, tq)
    stage_a(n_off + 1, s1, mx1, late)
    s_meta = [scores_t(km_ref[0], qs[idx]) + bmeta_ref[0] for idx in range(2)]
    vmt1 = jnp.concatenate([vmt_ref[...], jnp.ones((2 * SUBLANES, META_PAD), BF16)], axis=0)
    shift0 = slope * ((qi * tq).astype(F32) * LOG2E)
    stage_b(n_off, s0, mx0, meta=(s_meta, vmt1, shift0))
    stage_b(n_off + 1, s1, mx1, late)

    o = acc_s[0] / l_s[0] - lam * (acc_s[1] / l_s[1])
    o = o * lax.rsqrt(jnp.mean(o * o, axis=0, keepdims=True) + EPS) * gain_ref[...]
    o_ref[...] = jnp.transpose(o * (1.0 - lam_init)).astype(BF16)


def _attention(lam_params, qt, k, vt, k_meta, vt_meta, gain, bsz, seq, tq, lam_init):
    n = qt.shape[1]
    slopes = jnp.asarray([2.0 ** (-8.0 * (h + 1) / N_HEADS) for h in range(N_HEADS)], F32)
    tk = tq // 2
    r = jnp.arange(tq, dtype=jnp.int32)
    diff = r[None, :] - r[:, None]
    slopes = slopes * LOG2E
    boff = -slopes[:, None, None] * diff.astype(F32)[None, :tk]
    vis = (r[:, None] // CHUNK) <= (r[None, :] // CHUNK)
    bdiag = jnp.where(vis[None], -slopes[:, None, None] * jnp.abs(diff).astype(F32)[None], NEG)
    bias_all = jnp.stack([boff, bdiag[:, :tk], bdiag[:, tk:]], axis=1)
    mrow = jnp.arange(META_PAD, dtype=jnp.int32)
    dmeta = (N_META + r[None, :] - mrow[:, None]).astype(F32)
    bmeta = jnp.where((mrow < N_META)[None, :, None], -slopes[:, None, None] * dmeta[None], NEG)

    km = jnp.pad(k_meta, ((0, 0), (0, META_PAD - N_META), (0, 0)))
    vmt = jnp.pad(vt_meta, ((0, 0), (0, META_PAD - N_META)))

    kern = functools.partial(_attn_kernel, tq=tq, lam_init=lam_init)
    return pl.pallas_call(
        kern,
        grid=(bsz, N_HEADS),
        in_specs=[
            pl.BlockSpec((4, HEAD_DIM), lambda b, h: (0, 0)),
            pl.BlockSpec((V_DIM, seq), lambda b, h: (h, b)),
            pl.BlockSpec((1, seq, V_DIM), lambda b, h: (h, b, 0)),
            pl.BlockSpec((V_DIM, seq), lambda b, h: (h, b)),
            pl.BlockSpec((1, META_PAD, V_DIM), lambda b, h: (h, 0, 0)),
            pl.BlockSpec((V_DIM, META_PAD), lambda b, h: (h, 0)),
            pl.BlockSpec((1, 3, tk, tq), lambda b, h: (h, 0, 0, 0)),
            pl.BlockSpec((1, META_PAD, tq), lambda b, h: (h, 0, 0)),
            pl.BlockSpec((V_DIM, 1), lambda b, h: (0, 0)),
        ],
        out_specs=pl.BlockSpec((1, seq, V_DIM), lambda b, h: (h, b, 0)),
        out_shape=jax.ShapeDtypeStruct((N_HEADS, n, V_DIM), BF16),
        scratch_shapes=[pltpu.VMEM((2, 1, tq), F32), pltpu.VMEM((2, 1, tq), F32),
                        pltpu.VMEM((2, V_DIM, tq), F32),
                        pltpu.VMEM((2, tk, tq), F32), pltpu.VMEM((2, tk, tq), F32),
                        pltpu.VMEM((2, 1, tq), F32), pltpu.VMEM((2, 1, tq), F32)],
        compiler_params=pltpu.CompilerParams(
            dimension_semantics=("arbitrary", "arbitrary"),
            vmem_limit_bytes=V7X_VMEM_LIMIT),
        name="diff_attention",
    )(lam_params, qt, k, vt, km, vmt, bias_all, bmeta, gain)


def _mixer_kernel(x_ref, o_ref, u_ref, uh_ref, mh_ref, sga_ref, sgc_ref,
                  wap_ref, wpw_ref, wout_ref, dww_ref, dwb_ref, lng_ref, lnb_ref,
                  nffn_ref, wr_ref, br_ref, tri_ref,
                  h_ref, hn_ref, exp_ref, rank_ref, gate_ref, cnt_ref,
                  ext_s, y_s, cnt_s, *, tm, tiles_per_seq):
    i = pl.program_id(0)
    d = x_ref.shape[1]

    @pl.when(i == 0)
    def _():
        cnt_s[...] = jnp.zeros_like(cnt_s)

    first = (i % tiles_per_seq) == 0
    halo = jnp.where(first, mh_ref[...], uh_ref[...])
    ext_s[0, 0:HALO, :] = halo.astype(F32)
    ext_s[0, HALO:HALO + tm, :] = u_ref[...].astype(F32)
    span = tm + HALO - SUBLANES
    for b in range(1, SUBLANES):
        ext_s[b, 0:span, :] = ext_s[0, b:b + span, :]
    off = HALO - (CONV_K - 1)

    o = jnp.concatenate([o_ref[hd] for hd in range(N_HEADS)], axis=-1)
    y_attn = jnp.dot(o, wap_ref[...], preferred_element_type=F32)
    for c in range(d // LANES):
        cs = slice(c * LANES, (c + 1) * LANES)
        for r0 in (0, tm // 2):
            acc = jnp.zeros((tm // 2, LANES), F32)
            for t in range(CONV_K):
                a, b = divmod(off + t, SUBLANES)
                row = a * SUBLANES + r0
                acc = acc + dww_ref[t:t + 1, cs] * ext_s[b, row:row + tm // 2, cs]
            y_s[r0:r0 + tm // 2, cs] = acc
    y = y_s[...] + dwb_ref[...]
    mu = jnp.mean(y, axis=-1, keepdims=True)
    yc = y - mu
    var = jnp.mean(yc * yc, axis=-1, keepdims=True)
    y = yc * lax.rsqrt(var + EPS) * lng_ref[...] + lnb_ref[...]
    y = y * _sigmoid(y)
    y_conv = jnp.dot(y.astype(BF16), wpw_ref[...], preferred_element_type=F32)

    mixed = sga_ref[...].astype(F32) * y_attn + sgc_ref[...].astype(F32) * y_conv
    h = x_ref[...] + jnp.dot(mixed.astype(BF16), wout_ref[...], preferred_element_type=F32)
    h_ref[...] = h
    hn = h * lax.rsqrt(jnp.mean(h * h, axis=-1, keepdims=True) + EPS) * nffn_ref[...]
    _store_token_tiles(hn_ref, hn)

    nt_dims = (((1,), (1,)), ((), ()))
    hn_hi = hn.astype(BF16)
    hn_lo = (hn - hn_hi.astype(F32)).astype(BF16)
    both = lax.dot_general(wr_ref[...], hn_hi, nt_dims, preferred_element_type=F32)
    logits = (both[:N_EXPERTS] + both[N_EXPERTS:]
              + lax.dot_general(wr_ref[0:N_EXPERTS, :], hn_lo, nt_dims, preferred_element_type=F32)
              + br_ref[...])
    eio = lax.broadcasted_iota(jnp.int32, logits.shape, 0)
    sels, tops, idxs = [], [], []
    cur = logits
    for _ in range(TOP_K):
        m = jnp.max(cur, axis=0, keepdims=True)
        idx = jnp.min(jnp.where(cur == m, eio, N_EXPERTS), axis=0, keepdims=True)
        sel = eio == idx
        sels.append(sel)
        tops.append(m)
        idxs.append(idx)
        cur = jnp.where(sel, -jnp.inf, cur)
    exps = [jnp.exp(t - tops[0]) for t in tops]
    denom = exps[0] + exps[1] + exps[2] + exps[3]
    hot = jnp.zeros(logits.shape, F32)
    for sel in sels:
        hot = hot + jnp.where(sel, 1.0, 0.0)
    before = jnp.dot(hot.astype(BF16), tri_ref[...], preferred_element_type=F32) + cnt_s[...]
    ranks = [jnp.sum(jnp.where(sel, before, 0.0), axis=0, keepdims=True).astype(jnp.int32)
             for sel in sels]
    zi = jnp.zeros((ROUTE_ROWS - TOP_K, tm), jnp.int32)
    exp_ref[0] = jnp.concatenate(idxs + [zi], axis=0)
    rank_ref[0] = jnp.concatenate(ranks + [zi], axis=0)
    gate_ref[0] = jnp.concatenate([e / denom for e in exps] + [zi.astype(F32)], axis=0)
    cnt_s[...] = cnt_s[...] + jnp.sum(hot, axis=1, keepdims=True)
    cnt_ref[...] = jnp.broadcast_to(cnt_s[...], cnt_ref.shape)


def _mixer(x2d, o, u, sga, sgc, meta_halo, wap, wpw, wout, dww, dwb, lng, lnb, nffn, wr_t, br,
           seq, tm):
    n, d = x2d.shape
    nt = n // tm
    hb = tm // HALO
    tri = (jnp.arange(tm)[:, None] < jnp.arange(tm)[None, :]).astype(BF16)
    row = pl.BlockSpec((tm, d), lambda i: (i, 0))
    route = pl.BlockSpec((1, ROUTE_ROWS, tm), lambda i: (i, 0, 0))

    def full(shp):
        return pl.BlockSpec(shp, lambda i: tuple(0 for _ in shp))

    kern = functools.partial(_mixer_kernel, tm=tm, tiles_per_seq=seq // tm)
    return pl.pallas_call(
        kern,
        grid=(nt,),
        in_specs=[row, pl.BlockSpec((N_HEADS, tm, V_DIM), lambda i: (0, i, 0)), row,
                  pl.BlockSpec((HALO, d), lambda i: (jnp.maximum(i * hb - 1, 0), 0)),
                  full((HALO, d)), row, row,
                  full((d, d)), full((d, d)), full((d, d)),
                  full((CONV_K, d)), full((1, d)), full((1, d)), full((1, d)), full((1, d)),
                  full((2 * N_EXPERTS, d)), full((N_EXPERTS, 1)), full((tm, tm))],
        out_specs=[row, pl.BlockSpec((tm * SUBLANES, LANES), lambda i: (i, 0)),
                   route, route, route, full((N_EXPERTS, LANES))],
        out_shape=[jax.ShapeDtypeStruct((n, d), F32),
                   jax.ShapeDtypeStruct((n * SUBLANES, LANES), F32),
                   jax.ShapeDtypeStruct((nt, ROUTE_ROWS, tm), jnp.int32),
                   jax.ShapeDtypeStruct((nt, ROUTE_ROWS, tm), jnp.int32),
                   jax.ShapeDtypeStruct((nt, ROUTE_ROWS, tm), F32),
                   jax.ShapeDtypeStruct((N_EXPERTS, LANES), F32)],
        scratch_shapes=[pltpu.VMEM((SUBLANES, HALO + tm, d), F32), pltpu.VMEM((tm, d), F32),
                        pltpu.VMEM((N_EXPERTS, 1), F32)],
        compiler_params=pltpu.CompilerParams(
            dimension_semantics=("arbitrary",), vmem_limit_bytes=V7X_VMEM_LIMIT),
        name="mixer_router",
    )(x2d, o, u, u, meta_halo, sga, sgc, wap, wpw, wout, dww, dwb, lng, lnb, nffn, wr_t, br, tri)


def _prefetch_routes(dest_hbm, dsm_s, sem_idx):
    i = pl.program_id(0)
    slot = i % 2

    def idx_copy(tile, s):
        return pltpu.make_async_copy(dest_hbm.at[tile], dsm_s.at[s], sem_idx.at[s])

    @pl.when(i == 0)
    def _():
        idx_copy(0, 0).start()

    idx_copy(i, slot).wait()

    @pl.when(i + 1 < pl.num_programs(0))
    def _():
        idx_copy(i + 1, 1 - slot).start()

    return slot


def _dispatch_kernel(dest_hbm, hn_ref, xs_hbm, dsm_s, sem_idx, sem_rows, *, tm):
    slot = _prefetch_routes(dest_hbm, dsm_s, sem_idx)

    def issue(r, c):
        src = hn_ref.at[pl.ds(pl.multiple_of(r * SUBLANES, SUBLANES), SUBLANES), :]
        for kk in range(TOP_K):
            dst = pl.multiple_of(dsm_s[slot, kk, r] * SUBLANES, SUBLANES)
            pltpu.make_async_copy(src, xs_hbm.at[pl.ds(dst, SUBLANES), :],
                                  sem_rows).start(priority=kk % 2)
        return c
    lax.fori_loop(0, tm, issue, 0, unroll=8)

    for kk in range(TOP_K):
        pltpu.make_async_copy(hn_ref, xs_hbm.at[pl.ds(0, tm * SUBLANES), :], sem_rows).wait()


def _dispatch(dest, hn, n_slots, tm):
    n = hn.shape[0] // SUBLANES
    return pl.pallas_call(
        functools.partial(_dispatch_kernel, tm=tm),
        grid=(n // tm,),
        in_specs=[pl.BlockSpec(memory_space=pl.ANY),
                  pl.BlockSpec((tm * SUBLANES, LANES), lambda i: (i, 0))],
        out_specs=pl.BlockSpec(memory_space=pl.ANY),
        out_shape=jax.ShapeDtypeStruct((n_slots * SUBLANES, LANES), F32),
        scratch_shapes=[pltpu.SMEM((2, ROUTE_ROWS, tm), jnp.int32),
                        pltpu.SemaphoreType.DMA((2,)), pltpu.SemaphoreType.DMA(())],
        compiler_params=pltpu.CompilerParams(
            dimension_semantics=("arbitrary",), vmem_limit_bytes=V7X_VMEM_LIMIT),
        name="dispatch",
    )(dest, hn)


def _expert_kernel(bexp_ref, bvalid_ref, xs_ref, wgu_ref, bgu_ref, wdn_ref, bdn_ref, ys_ref,
                   wgu_bf, wdn_bf):
    i = pl.program_id(0)
    valid = bvalid_ref[i]
    f = wdn_ref.shape[1]

    @pl.when((i == 0) | (bexp_ref[i] != bexp_ref[jnp.maximum(i - 1, 0)]))
    def _():
        wgu_bf[...] = wgu_ref[0].astype(BF16)
        wdn_bf[...] = wdn_ref[0].astype(BF16)

    @pl.when(valid > 0)
    def _():
        x = _load_token_tiles(xs_ref, EXPERT_BLOCK)
        rows = lax.broadcasted_iota(jnp.int32, x.shape, 0)
        xb = jnp.where(rows < valid, x, 0.0).astype(BF16)
        gu = jnp.dot(xb, wgu_bf[...], preferred_element_type=F32) + bgu_ref[0]
        glu = jnp.minimum(gu[:, :f], SWIGLU_LIMIT)
        lin = jnp.clip(gu[:, f:], -SWIGLU_LIMIT, SWIGLU_LIMIT)
        act = glu * _sigmoid(SWIGLU_ALPHA * glu) * (lin + 1.0)
        y = jnp.dot(act.astype(BF16), wdn_bf[...], preferred_element_type=F32) + bdn_ref[0]
        _store_token_tiles(ys_ref, y)

    @pl.when(valid <= 0)
    def _():
        ys_ref[...] = jnp.zeros_like(ys_ref)


def _experts(block_exp, block_valid, xs, wgu, bgu, wdn, bdn):
    d = wgu.shape[1]
    n_slots = xs.shape[0] // SUBLANES
    nb = n_slots // EXPERT_BLOCK
    f2 = wgu.shape[2]
    f = wdn.shape[1]
    tile_rows = EXPERT_BLOCK * SUBLANES
    grid_spec = pltpu.PrefetchScalarGridSpec(
        num_scalar_prefetch=2,
        grid=(nb,),
        in_specs=[pl.BlockSpec((tile_rows, LANES), lambda i, be, bv: (i, 0)),
                  pl.BlockSpec((1, d, f2), lambda i, be, bv: (be[i], 0, 0)),
                  pl.BlockSpec((1, 1, f2), lambda i, be, bv: (be[i], 0, 0)),
                  pl.BlockSpec((1, f, d), lambda i, be, bv: (be[i], 0, 0)),
                  pl.BlockSpec((1, 1, d), lambda i, be, bv: (be[i], 0, 0))],
        out_specs=pl.BlockSpec((tile_rows, LANES), lambda i, be, bv: (i, 0)),
        scratch_shapes=[pltpu.VMEM((d, f2), BF16), pltpu.VMEM((f, d), BF16)],
    )
    return pl.pallas_call(
        _expert_kernel,
        grid_spec=grid_spec,
        out_shape=jax.ShapeDtypeStruct(xs.shape, F32),
        compiler_params=pltpu.CompilerParams(
            dimension_semantics=("arbitrary",), vmem_limit_bytes=V7X_VMEM_LIMIT),
        name="experts",
    )(block_exp, block_valid, xs, wgu, bgu, wdn, bdn)


def _combine_kernel(dest_hbm, h_ref, gate_ref, g_ref, ys_hbm, out_ref, dsm_s, ybuf, sem_idx, sem_rows,
                    *, tm):
    i = pl.program_id(0)
    last = pl.num_programs(0) - 1
    slot = i % 2

    def idx_copy(tile, s):
        return pltpu.make_async_copy(dest_hbm.at[jnp.minimum(tile, last)], dsm_s.at[s],
                                     sem_idx.at[s])

    def row_copy(s, kk, r):
        src = pl.multiple_of(dsm_s[s, kk, r] * SUBLANES, SUBLANES)
        return pltpu.make_async_copy(ys_hbm.at[pl.ds(src, SUBLANES), :],
                                     ybuf.at[s, kk, pl.ds(r * SUBLANES, SUBLANES), :],
                                     sem_rows.at[s])

    def wait_rows(s):
        for kk in range(TOP_K):
            pltpu.make_async_copy(ys_hbm.at[pl.ds(0, tm * SUBLANES), :], ybuf.at[s, kk],
                                  sem_rows.at[s]).wait()

    @pl.when(i == 0)
    def _():
        idx_copy(0, 0).start()
        idx_copy(0, 0).wait()

        def issue(r, c):
            for kk in range(TOP_K):
                row_copy(0, kk, r).start(priority=kk % 2)
            return c
        lax.fori_loop(0, tm, issue, 0, unroll=8)
        idx_copy(1, 1).start()

    idx_copy(i + 1, 1 - slot).wait()
    wait_rows(slot)
    for r in range(tm):
        for kk in range(TOP_K):
            row_copy(1 - slot, kk, r).start(priority=kk % 2)
    idx_copy(i + 2, slot).start()

    gates = jnp.transpose(gate_ref[0])
    acc = h_ref[...]
    for kk in range(TOP_K):
        acc = acc + gates[:, kk:kk + 1] * _load_token_tiles(ybuf.at[slot, kk], tm)
    out_ref[...] = acc * lax.rsqrt(jnp.mean(acc * acc, axis=-1, keepdims=True) + EPS) * g_ref[...]

    @pl.when(i == last)
    def _():
        wait_rows(1 - slot)
        idx_copy(i + 2, slot).wait()


def _combine(dest, h, gates, g_final, ys, tm):
    n, d = h.shape
    return pl.pallas_call(
        functools.partial(_combine_kernel, tm=tm),
        grid=(n // tm,),
        in_specs=[pl.BlockSpec(memory_space=pl.ANY),
                  pl.BlockSpec((tm, d), lambda i: (i, 0)),
                  pl.BlockSpec((1, ROUTE_ROWS, tm), lambda i: (i, 0, 0)),
                  pl.BlockSpec((1, d), lambda i: (0, 0)),
                  pl.BlockSpec(memory_space=pl.ANY)],
        out_specs=pl.BlockSpec((tm, d), lambda i: (i, 0)),
        out_shape=jax.ShapeDtypeStruct((n, d), F32),
        scratch_shapes=[pltpu.SMEM((2, ROUTE_ROWS, tm), jnp.int32),
                        pltpu.VMEM((2, TOP_K, tm * SUBLANES, LANES), F32),
                        pltpu.SemaphoreType.DMA((2,)), pltpu.SemaphoreType.DMA((2,))],
        compiler_params=pltpu.CompilerParams(
            dimension_semantics=("arbitrary",), vmem_limit_bytes=V7X_VMEM_LIMIT),
        name="combine",
    )(dest, h, gates, g_final, ys)


def kernel(x, meta_tokens, norm_mix, w_in, b_in, lam_params, subln_gain, w_attn_proj, conv_dw_w,
           conv_dw_b, conv_ln_g, conv_ln_b, w_conv_proj, w_out, norm_ffn, w_router, b_router,
           w_gate_up, b_gate_up, w_down, b_down, norm_final):
    bsz, seq, d = x.shape
    assert norm_mix.shape[0] == 1, "single-layer block"
    n = bsz * seq
    tm_proj = min(INPROJ_ROWS, seq)
    tq = min(ATTN_Q_TILE, seq)
    tm = min(MIXER_ROWS, seq)
    lam_init = 0.8 - 0.6 * 1.0

    x2d = x.reshape(n, d)
    w_rest = jnp.concatenate([w_in[0][:, d:2 * d], w_in[0][:, 3 * d:]], axis=1).astype(BF16)
    b_rest = jnp.concatenate([b_in[0][d:2 * d], b_in[0][3 * d:]])[None]
    w_t = jnp.concatenate([w_in[0][:, :d], w_in[0][:, 2 * d:3 * d]], axis=1).T.astype(BF16)
    b_t = jnp.concatenate([b_in[0][:d], b_in[0][2 * d:3 * d]])[:, None]
    g_mix = norm_mix[0][None]
    qt, k, vt, u, sga, sgc = _inproj(x2d, g_mix, w_rest, b_rest, w_t, b_t, tm_proj)
    _, k_m, vt_m, u_m, _, _ = _inproj(meta_tokens.astype(x.dtype), g_mix, w_rest, b_rest, w_t, b_t,
                                      N_META)

    o = _attention(lam_params[0], qt, k, vt, k_m, vt_m, subln_gain[0][:, None], bsz, seq, tq,
                   lam_init)

    meta_halo = jnp.concatenate([jnp.zeros((HALO - N_META, d), BF16), u_m], axis=0)
    wr_t = w_router[0].T
    wr_hi = wr_t.astype(BF16)
    wr_split = jnp.concatenate([wr_hi, (wr_t - wr_hi.astype(F32)).astype(BF16)], axis=0)
    h, hn, eidx, rank, gates, counts = _mixer(
        x2d, o, u, sga, sgc, meta_halo,
        w_attn_proj[0].astype(BF16), w_conv_proj[0].astype(BF16), w_out[0].astype(BF16),
        conv_dw_w[0], conv_dw_b[0][None], conv_ln_g[0][None], conv_ln_b[0][None],
        norm_ffn[0][None], wr_split, b_router[0][:, None], seq, tm)

    cnt = counts[:, 0].astype(jnp.int32)
    padded = (cnt + EXPERT_BLOCK - 1) // EXPERT_BLOCK * EXPERT_BLOCK
    pad_end = jnp.cumsum(padded)
    pad_start = pad_end - padded
    n_blocks = -(-(n * TOP_K + N_EXPERTS * (EXPERT_BLOCK - 1)) // EXPERT_BLOCK)
    n_slots = n_blocks * EXPERT_BLOCK
    dest = pad_start[eidx] + rank
    block_start = jnp.arange(n_blocks, dtype=jnp.int32) * EXPERT_BLOCK
    block_exp = jnp.minimum(jnp.sum(block_start[:, None] >= pad_end[None, :], axis=1),
                            N_EXPERTS - 1).astype(jnp.int32)
    block_valid = jnp.clip(pad_start[block_exp] + cnt[block_exp] - block_start, 0,
                           EXPERT_BLOCK).astype(jnp.int32)

    xs = _dispatch(dest, hn, n_slots, tm)
    ys = _experts(block_exp, block_valid, xs, w_gate_up[0], b_gate_up[0][:, None, :],
                  w_down[0], b_down[0][:, None, :])
    out = _combine(dest, h, gates, norm_final[None], ys, tm)
    return out.reshape(bsz, seq, d)
```

```python
import functools

import jax
import jax.numpy as jnp
from jax import lax
from jax.experimental import pallas as pl
from jax.experimental.pallas import tpu as pltpu

CHUNK = 64
N_META = 16
N_HEADS = 8
HEAD_DIM = 64
V_DIM = 2 * HEAD_DIM
CONV_K = 31
N_EXPERTS = 32
TOP_K = 4
SWIGLU_ALPHA = 1.702
SWIGLU_LIMIT = 7.0
EXPERT_BLOCK = 1024
INPROJ_ROWS = 512
ATTN_Q_TILE = 512
MIXER_ROWS = 512
EPS = 1e-5
NEG = -1e30
LOG2E = 1.4426950408889634

V7X_VMEM_LIMIT = 56 * 1024 * 1024
SUBLANES = 8
LANES = 128
META_PAD = 128
HALO = 32
ROUTE_ROWS = 8

BF16 = jnp.bfloat16
F32 = jnp.float32


def _sigmoid(x):
    return 1.0 / (1.0 + jnp.exp(-x))


def _load_token_tiles(ref, rows):
    return jnp.concatenate([ref[pl.ds(s, rows, stride=SUBLANES), :] for s in range(SUBLANES)],
                           axis=-1)


def _store_token_tiles(ref, x):
    rows = x.shape[0]
    for s in range(SUBLANES):
        ref[pl.ds(s, rows, stride=SUBLANES), :] = x[:, s * LANES:(s + 1) * LANES]


def _inproj_kernel(x_ref, g_ref, w_ref, b_ref, wt_ref, bt_ref,
                   qt_ref, k_ref, vt_ref, u_ref, sga_ref, sgc_ref):
    d = x_ref.shape[1]
    x = x_ref[...]
    xn = x * lax.rsqrt(jnp.mean(x * x, axis=-1, keepdims=True) + EPS) * g_ref[...]
    xb = xn.astype(BF16)

    def proj(c):
        return (jnp.dot(xb, w_ref[:, c * d:(c + 1) * d], preferred_element_type=F32)
                + b_ref[:, c * d:(c + 1) * d])

    kf = proj(0).astype(BF16)
    for hd in range(N_HEADS):
        k_ref[hd] = kf[:, hd * V_DIM:(hd + 1) * V_DIM]
    t = lax.dot_general(wt_ref[...], xb, (((1,), (1,)), ((), ())), preferred_element_type=F32)
    t = t + bt_ref[...]
    qt_ref[...] = (t[:d] * (HEAD_DIM ** -0.5 * LOG2E)).astype(BF16)
    vt_ref[...] = t[d:].astype(BF16)
    u_ref[...] = (proj(1) * _sigmoid(proj(2))).astype(BF16)
    sga_ref[...] = _sigmoid(proj(3)).astype(BF16)
    sgc_ref[...] = _sigmoid(proj(4)).astype(BF16)


def _inproj(x2d, g, w_bf, b, wt_bf, bt, tm):
    n, d = x2d.shape
    in_w = w_bf.shape[1]
    out = jax.ShapeDtypeStruct((n, d), BF16)
    row = pl.BlockSpec((tm, d), lambda i: (i, 0))
    col = pl.BlockSpec((d, tm), lambda i: (0, i))
    heads = pl.BlockSpec((N_HEADS, tm, V_DIM), lambda i: (0, i, 0))
    out_heads = jax.ShapeDtypeStruct((N_HEADS, n, V_DIM), BF16)
    out_t = jax.ShapeDtypeStruct((d, n), BF16)
    return pl.pallas_call(
        _inproj_kernel,
        grid=(n // tm,),
        in_specs=[row,
                  pl.BlockSpec((1, d), lambda i: (0, 0)),
                  pl.BlockSpec((d, in_w), lambda i: (0, 0)),
                  pl.BlockSpec((1, in_w), lambda i: (0, 0)),
                  pl.BlockSpec((2 * d, d), lambda i: (0, 0)),
                  pl.BlockSpec((2 * d, 1), lambda i: (0, 0))],
        out_specs=[col, heads, col, row, row, row],
        out_shape=[out_t, out_heads, out_t, out, out, out],
        compiler_params=pltpu.CompilerParams(
            dimension_semantics=("arbitrary",), vmem_limit_bytes=V7X_VMEM_LIMIT),
        name="inproj",
    )(x2d, g, w_bf, b, wt_bf, bt)


def _attn_kernel(lam_ref, qt_ref, k_ref, vt_ref, km_ref, vmt_ref, bias_ref, bmeta_ref,
                 gain_ref, o_ref, *scratch, tq, lam_init):
    def q_tile(qi, carry):
        cols = pl.ds(pl.multiple_of(qi * tq, tq), tq)
        _attn_tile(qi, lam_ref, qt_ref.at[:, cols], k_ref, vt_ref, km_ref, vmt_ref, bias_ref,
                   bmeta_ref, gain_ref, o_ref.at[0, cols, :], *scratch, tq=tq, lam_init=lam_init)
        return carry

    lax.fori_loop(0, qt_ref.shape[1] // tq, q_tile, 0)


def _attn_tile(qi, lam_ref, qt_ref, k_ref, vt_ref, km_ref, vmt_ref, bias_ref, bmeta_ref,
               gain_ref, o_ref, m_s, l_s, acc_s, s0, s1, mx0, mx1, *, tq, lam_init):
    h = pl.program_id(1)
    slope = jnp.exp2(-(h + 1).astype(F32))
    lp = lam_ref[...]
    lam = (jnp.exp(jnp.sum(lp[0:1] * lp[1:2], axis=-1, keepdims=True))
           - jnp.exp(jnp.sum(lp[2:3] * lp[3:4], axis=-1, keepdims=True)) + lam_init)

    qt = qt_ref[...]
    feat = lax.broadcasted_iota(jnp.int32, qt.shape, 0)
    zero = jnp.zeros_like(qt)
    qs = (jnp.where(feat < HEAD_DIM, qt, zero), jnp.where(feat >= HEAD_DIM, qt, zero))

    def scores_t(kk, qm):
        return jnp.dot(kk, qm, preferred_element_type=F32)

    tk = tq // 2
    n_off = 2 * qi

    def tile_shift(t):
        off = (qi * tq - t * tk).astype(F32) * LOG2E
        return jnp.where(t < n_off, slope * off, 0.0)

    def stage_a(t, s_buf, mx_buf, cols=slice(None)):
        kk = k_ref[0, pl.ds(pl.multiple_of(t * tk, tk), tk), :]
        bias = bias_ref[0, jnp.where(t < n_off, 0, t - n_off + 1), :, cols]
        for idx in range(2):
            s = scores_t(kk, qs[idx][:, cols]) + bias
            s_buf[idx, :, cols] = s
            mx_buf[idx, :, cols] = jnp.max(s, axis=0, keepdims=True)

    def stage_b(t, s_buf, mx_buf, cols=slice(None), meta=None):
        vt = vt_ref[:, pl.ds(pl.multiple_of(t * tk, tk), tk)]
        vt1 = jnp.concatenate([vt, jnp.ones((2 * SUBLANES, tk), BF16)], axis=0)
        shift = tile_shift(t)
        for idx in range(2):
            m_old = m_s[idx, :, cols]
            m_new = jnp.maximum(m_old, mx_buf[idx, :, cols] - shift)
            if meta is not None:
                s_meta, vmt1, shift_m = meta
                m_new = jnp.maximum(m_new, jnp.max(s_meta[idx], axis=0, keepdims=True) - shift_m)
            p = jnp.exp2(s_buf[idx, :, cols] - (m_new + shift))
            pv1 = jnp.dot(vt1, p.astype(BF16), preferred_element_type=F32)
            if meta is not None:
                pm = jnp.exp2(s_meta[idx] - (m_new + shift_m))
                pv1 = pv1 + jnp.dot(vmt1, pm.astype(BF16), preferred_element_type=F32)
            pv = pv1[:V_DIM]
            ps = pv1[V_DIM:V_DIM + 1]
            alpha = jnp.exp2(m_old - m_new)
            l_s[idx, :, cols] = alpha * l_s[idx, :, cols] + ps
            acc_s[idx, :, cols] = alpha * acc_s[idx, :, cols] + pv
            m_s[idx, :, cols] = m_new

    m_s[...] = jnp.full(m_s.shape, NEG, F32)
    l_s[...] = jnp.zeros_like(l_s)
    acc_s[...] = jnp.zeros_like(acc_s)
    stage_a(0, s0, mx0)

    def pair(i):
        stage_a(2 * i + 1, s1, mx1)
        stage_b(2 * i, s0, mx0)
        stage_a(2 * i + 2, s0, mx0)
        stage_b(2 * i + 1, s1, mx1)

    def body(ii, carry):
        pair(2 * ii)
        pair(2 * ii + 1)
        return carry

    lax.fori_loop(0, qi // 2, body, 0)

    @pl.when(qi % 2 == 1)
    def _():
        pair(qi - 1)

    late = slice(tq // 2, tq)
    stage_a(n_off + 1, s1, mx1, late)
    s_meta = [scores_t(km_ref[0], qs[idx]) + bmeta_ref[0] for idx in range(2)]
    vmt1 = jnp.concatenate([vmt_ref[...], jnp.ones((2 * SUBLANES, META_PAD), BF16)], axis=0)
    shift0 = slope * ((qi * tq).astype(F32) * LOG2E)
    stage_b(n_off, s0, mx0, meta=(s_meta, vmt1, shift0))
    stage_b(n_off + 1, s1, mx1, late)

    o = acc_s[0] / l_s[0] - lam * (acc_s[1] / l_s[1])
    o = o * lax.rsqrt(jnp.mean(o * o, axis=0, keepdims=True) + EPS) * gain_ref[...]
    o_ref[...] = jnp.transpose(o * (1.0 - lam_init)).astype(BF16)


def _attention(lam_params, qt, k, vt, k_meta, vt_meta, gain, bsz, seq, tq, lam_init):
    n = qt.shape[1]
    slopes = jnp.asarray([2.0 ** (-8.0 * (h + 1) / N_HEADS) for h in range(N_HEADS)], F32)
    tk = tq // 2
    r = jnp.arange(tq, dtype=jnp.int32)
    diff = r[None, :] - r[:, None]
    slopes = slopes * LOG2E
    boff = -slopes[:, None, None] * diff.astype(F32)[None, :tk]
    vis = (r[:, None] // CHUNK) <= (r[None, :] // CHUNK)
    bdiag = jnp.where(vis[None], -slopes[:, None, None] * jnp.abs(diff).astype(F32)[None], NEG)
    bias_all = jnp.stack([boff, bdiag[:, :tk], bdiag[:, tk:]], axis=1)
    mrow = jnp.arange(META_PAD, dtype=jnp.int32)
    dmeta = (N_META + r[None, :] - mrow[:, None]).astype(F32)
    bmeta = jnp.where((mrow < N_META)[None, :, None], -slopes[:, None, None] * dmeta[None], NEG)

    km = jnp.pad(k_meta, ((0, 0), (0, META_PAD - N_META), (0, 0)))
    vmt = jnp.pad(vt_meta, ((0, 0), (0, META_PAD - N_META)))

    kern = functools.partial(_attn_kernel, tq=tq, lam_init=lam_init)
    return pl.pallas_call(
        kern,
        grid=(bsz, N_HEADS),
        in_specs=[
            pl.BlockSpec((4, HEAD_DIM), lambda b, h: (0, 0)),
            pl.BlockSpec((V_DIM, seq), lambda b, h: (h, b)),
            pl.BlockSpec((1, seq, V_DIM), lambda b, h: (h, b, 0)),
            pl.BlockSpec((V_DIM, seq), lambda b, h: (h, b)),
            pl.BlockSpec((1, META_PAD, V_DIM), lambda b, h: (h, 0, 0)),
            pl.BlockSpec((V_DIM, META_PAD), lambda b, h: (h, 0)),
            pl.BlockSpec((1, 3, tk, tq), lambda b, h: (h, 0, 0, 0)),
            pl.BlockSpec((1, META_PAD, tq), lambda b, h: (h, 0, 0)),
            pl.BlockSpec((V_DIM, 1), lambda b, h: (0, 0)),
        ],
        out_specs=pl.BlockSpec((1, seq, V_DIM), lambda b, h: (h, b, 0)),
        out_shape=jax.ShapeDtypeStruct((N_HEADS, n, V_DIM), BF16),
        scratch_shapes=[pltpu.VMEM((2, 1, tq), F32), pltpu.VMEM((2, 1, tq), F32),
                        pltpu.VMEM((2, V_DIM, tq), F32),
                        pltpu.VMEM((2, tk, tq), F32), pltpu.VMEM((2, tk, tq), F32),
                        pltpu.VMEM((2, 1, tq), F32), pltpu.VMEM((2, 1, tq), F32)],
        compiler_params=pltpu.CompilerParams(
            dimension_semantics=("arbitrary", "arbitrary"),
            vmem_limit_bytes=V7X_VMEM_LIMIT),
        name="diff_attention",
    )(lam_params, qt, k, vt, km, vmt, bias_all, bmeta, gain)


def _mixer_kernel(x_ref, o_ref, u_ref, uh_ref, mh_ref, sga_ref, sgc_ref,
                  wap_ref, wpw_ref, wout_ref, dww_ref, dwb_ref, lng_ref, lnb_ref,
                  nffn_ref, wr_ref, br_ref, tri_ref,
                  h_ref, hn_ref, exp_ref, rank_ref, gate_ref, cnt_ref,
                  ext_s, y_s, cnt_s, *, tm, tiles_per_seq):
    i = pl.program_id(0)
    d = x_ref.shape[1]

    @pl.when(i == 0)
    def _():
        cnt_s[...] = jnp.zeros_like(cnt_s)

    first = (i % tiles_per_seq) == 0
    off = HALO - (CONV_K - 1)
    span = tm + HALO - SUBLANES
    acc_rows = 128

    o = jnp.concatenate([o_ref[hd] for hd in range(N_HEADS)], axis=-1)
    y_attn = jnp.dot(o, wap_ref[...], preferred_element_type=F32)
    for c in range(d // LANES):
        cs = slice(c * LANES, (c + 1) * LANES)
        halo = jnp.where(first, mh_ref[:, cs], uh_ref[:, cs])
        ext_s[0, 0:HALO, :] = halo.astype(F32)
        ext_s[0, HALO:HALO + tm, :] = u_ref[:, cs].astype(F32)
        for b in range(1, SUBLANES):
            ext_s[b, 0:span, :] = ext_s[0, b:b + span, :]
        for r0 in range(0, tm, acc_rows):
            acc = jnp.zeros((acc_rows, LANES), F32)
            for t in range(CONV_K):
                a, b = divmod(off + t, SUBLANES)
                row = a * SUBLANES + r0
                acc = acc + dww_ref[t:t + 1, cs] * ext_s[b, row:row + acc_rows, :]
            y_s[r0:r0 + acc_rows, cs] = acc
    y = y_s[...] + dwb_ref[...]
    mu = jnp.mean(y, axis=-1, keepdims=True)
    yc = y - mu
    var = jnp.mean(yc * yc, axis=-1, keepdims=True)
    y = yc * lax.rsqrt(var + EPS) * lng_ref[...] + lnb_ref[...]
    y = y * _sigmoid(y)
    y_conv = jnp.dot(y.astype(BF16), wpw_ref[...], preferred_element_type=F32)

    mixed = sga_ref[...].astype(F32) * y_attn + sgc_ref[...].astype(F32) * y_conv
    h = x_ref[...] + jnp.dot(mixed.astype(BF16), wout_ref[...], preferred_element_type=F32)
    h_ref[...] = h
    hn = h * lax.rsqrt(jnp.mean(h * h, axis=-1, keepdims=True) + EPS) * nffn_ref[...]
    _store_token_tiles(hn_ref, hn)

    nt_dims = (((1,), (1,)), ((), ()))
    hn_hi = hn.astype(BF16)
    hn_lo = (hn - hn_hi.astype(F32)).astype(BF16)
    both = lax.dot_general(wr_ref[...], hn_hi, nt_dims, preferred_element_type=F32)
    logits = (both[:N_EXPERTS] + both[N_EXPERTS:]
              + lax.dot_general(wr_ref[0:N_EXPERTS, :], hn_lo, nt_dims, preferred_element_type=F32)
              + br_ref[...])
    eio = lax.broadcasted_iota(jnp.int32, logits.shape, 0)
    sels, tops, idxs = [], [], []
    cur = logits
    for _ in range(TOP_K):
        m = jnp.max(cur, axis=0, keepdims=True)
        idx = jnp.min(jnp.where(cur == m, eio, N_EXPERTS), axis=0, keepdims=True)
        sel = eio == idx
        sels.append(sel)
        tops.append(m)
        idxs.append(idx)
        cur = jnp.where(sel, -jnp.inf, cur)
    exps = [jnp.exp(t - tops[0]) for t in tops]
    denom = exps[0] + exps[1] + exps[2] + exps[3]
    hot = jnp.zeros(logits.shape, F32)
    for sel in sels:
        hot = hot + jnp.where(sel, 1.0, 0.0)
    before = jnp.dot(hot.astype(BF16), tri_ref[...], preferred_element_type=F32) + cnt_s[...]
    ranks = [jnp.sum(jnp.where(sel, before, 0.0), axis=0, keepdims=True).astype(jnp.int32)
             for sel in sels]
    zi = jnp.zeros((ROUTE_ROWS - TOP_K, tm), jnp.int32)
    exp_ref[0] = jnp.concatenate(idxs + [zi], axis=0)
    rank_ref[0] = jnp.concatenate(ranks + [zi], axis=0)
    gate_ref[0] = jnp.concatenate([e / denom for e in exps] + [zi.astype(F32)], axis=0)
    cnt_s[...] = cnt_s[...] + jnp.sum(hot, axis=1, keepdims=True)
    cnt_ref[...] = jnp.broadcast_to(cnt_s[...], cnt_ref.shape)


def _mixer(x2d, o, u, sga, sgc, meta_halo, wap, wpw, wout, dww, dwb, lng, lnb, nffn, wr_t, br,
           seq, tm):
    n, d = x2d.shape
    nt = n // tm
    hb = tm // HALO
    tri = (jnp.arange(tm)[:, None] < jnp.arange(tm)[None, :]).astype(BF16)
    row = pl.BlockSpec((tm, d), lambda i: (i, 0))
    route = pl.BlockSpec((1, ROUTE_ROWS, tm), lambda i: (i, 0, 0))

    def full(shp):
        return pl.BlockSpec(shp, lambda i: tuple(0 for _ in shp))

    kern = functools.partial(_mixer_kernel, tm=tm, tiles_per_seq=seq // tm)
    return pl.pallas_call(
        kern,
        grid=(nt,),
        in_specs=[row, pl.BlockSpec((N_HEADS, tm, V_DIM), lambda i: (0, i, 0)), row,
                  pl.BlockSpec((HALO, d), lambda i: (jnp.maximum(i * hb - 1, 0), 0)),
                  full((HALO, d)), row, row,
                  full((d, d)), full((d, d)), full((d, d)),
                  full((CONV_K, d)), full((1, d)), full((1, d)), full((1, d)), full((1, d)),
                  full((2 * N_EXPERTS, d)), full((N_EXPERTS, 1)), full((tm, tm))],
        out_specs=[row, pl.BlockSpec((tm * SUBLANES, LANES), lambda i: (i, 0)),
                   route, route, route, full((N_EXPERTS, LANES))],
        out_shape=[jax.ShapeDtypeStruct((n, d), F32),
                   jax.ShapeDtypeStruct((n * SUBLANES, LANES), F32),
                   jax.ShapeDtypeStruct((nt, ROUTE_ROWS, tm), jnp.int32),
                   jax.ShapeDtypeStruct((nt, ROUTE_ROWS, tm), jnp.int32),
                   jax.ShapeDtypeStruct((nt, ROUTE_ROWS, tm), F32),
                   jax.ShapeDtypeStruct((N_EXPERTS, LANES), F32)],
        scratch_shapes=[pltpu.VMEM((SUBLANES, HALO + tm, LANES), F32), pltpu.VMEM((tm, d), F32),
                        pltpu.VMEM((N_EXPERTS, 1), F32)],
        compiler_params=pltpu.CompilerParams(
            dimension_semantics=("arbitrary",), vmem_limit_bytes=V7X_VMEM_LIMIT),
        name="mixer_router",
    )(x2d, o, u, u, meta_halo, sga, sgc, wap, wpw, wout, dww, dwb, lng, lnb, nffn, wr_t, br, tri)


def _prefetch_routes(dest_hbm, dsm_s, sem_idx):
    i = pl.program_id(0)
    slot = i % 2

    def idx_copy(tile, s):
        return pltpu.make_async_copy(dest_hbm.at[tile], dsm_s.at[s], sem_idx.at[s])

    @pl.when(i == 0)
    def _():
        idx_copy(0, 0).start()

    idx_copy(i, slot).wait()

    @pl.when(i + 1 < pl.num_programs(0))
    def _():
        idx_copy(i + 1, 1 - slot).start()

    return slot


def _dispatch_kernel(dest_hbm, hn_ref, xs_hbm, dsm_s, sem_idx, sem_rows, *, tm):
    slot = _prefetch_routes(dest_hbm, dsm_s, sem_idx)

    def issue(r, c):
        src = hn_ref.at[pl.ds(pl.multiple_of(r * SUBLANES, SUBLANES), SUBLANES), :]
        for kk in range(TOP_K):
            dst = pl.multiple_of(dsm_s[slot, kk, r] * SUBLANES, SUBLANES)
            pltpu.make_async_copy(src, xs_hbm.at[pl.ds(dst, SUBLANES), :],
                                  sem_rows).start(priority=kk % 2)
        return c
    lax.fori_loop(0, tm, issue, 0, unroll=8)

    for kk in range(TOP_K):
        pltpu.make_async_copy(hn_ref, xs_hbm.at[pl.ds(0, tm * SUBLANES), :], sem_rows).wait()


def _dispatch(dest, hn, n_slots, tm):
    n = hn.shape[0] // SUBLANES
    return pl.pallas_call(
        functools.partial(_dispatch_kernel, tm=tm),
        grid=(n // tm,),
        in_specs=[pl.BlockSpec(memory_space=pl.ANY),
                  pl.BlockSpec((tm * SUBLANES, LANES), lambda i: (i, 0))],
        out_specs=pl.BlockSpec(memory_space=pl.ANY),
        out_shape=jax.ShapeDtypeStruct((n_slots * SUBLANES, LANES), F32),
        scratch_shapes=[pltpu.SMEM((2, ROUTE_ROWS, tm), jnp.int32),
                        pltpu.SemaphoreType.DMA((2,)), pltpu.SemaphoreType.DMA(())],
        compiler_params=pltpu.CompilerParams(
            dimension_semantics=("arbitrary",), vmem_limit_bytes=V7X_VMEM_LIMIT),
        name="dispatch",
    )(dest, hn)


def _expert_kernel(bexp_ref, bvalid_ref, xs_ref, wgu_ref, bgu_ref, wdn_ref, bdn_ref, ys_ref,
                   wgu_bf, wdn_bf):
    i = pl.program_id(0)
    valid = bvalid_ref[i]
    f = wdn_ref.shape[1]

    @pl.when((i == 0) | (bexp_ref[i] != bexp_ref[jnp.maximum(i - 1, 0)]))
    def _():
        wgu_bf[...] = wgu_ref[0].astype(BF16)
        wdn_bf[...] = wdn_ref[0].astype(BF16)

    @pl.when(valid > 0)
    def _():
        x = _load_token_tiles(xs_ref, EXPERT_BLOCK)
        rows = lax.broadcasted_iota(jnp.int32, x.shape, 0)
        xb = jnp.where(rows < valid, x, 0.0).astype(BF16)
        gu = jnp.dot(xb, wgu_bf[...], preferred_element_type=F32) + bgu_ref[0]
        glu = jnp.minimum(gu[:, :f], SWIGLU_LIMIT)
        lin = jnp.clip(gu[:, f:], -SWIGLU_LIMIT, SWIGLU_LIMIT)
        act = glu * _sigmoid(SWIGLU_ALPHA * glu) * (lin + 1.0)
        y = jnp.dot(act.astype(BF16), wdn_bf[...], preferred_element_type=F32) + bdn_ref[0]
        _store_token_tiles(ys_ref, y)

    @pl.when(valid <= 0)
    def _():
        ys_ref[...] = jnp.zeros_like(ys_ref)


def _experts(block_exp, block_valid, xs, wgu, bgu, wdn, bdn):
    d = wgu.shape[1]
    n_slots = xs.shape[0] // SUBLANES
    nb = n_slots // EXPERT_BLOCK
    f2 = wgu.shape[2]
    f = wdn.shape[1]
    tile_rows = EXPERT_BLOCK * SUBLANES
    grid_spec = pltpu.PrefetchScalarGridSpec(
        num_scalar_prefetch=2,
        grid=(nb,),
        in_specs=[pl.BlockSpec((tile_rows, LANES), lambda i, be, bv: (i, 0)),
                  pl.BlockSpec((1, d, f2), lambda i, be, bv: (be[i], 0, 0)),
                  pl.BlockSpec((1, 1, f2), lambda i, be, bv: (be[i], 0, 0)),
                  pl.BlockSpec((1, f, d), lambda i, be, bv: (be[i], 0, 0)),
                  pl.BlockSpec((1, 1, d), lambda i, be, bv: (be[i], 0, 0))],
        out_specs=pl.BlockSpec((tile_rows, LANES), lambda i, be, bv: (i, 0)),
        scratch_shapes=[pltpu.VMEM((d, f2), BF16), pltpu.VMEM((f, d), BF16)],
    )
    return pl.pallas_call(
        _expert_kernel,
        grid_spec=grid_spec,
        out_shape=jax.ShapeDtypeStruct(xs.shape, F32),
        compiler_params=pltpu.CompilerParams(
            dimension_semantics=("arbitrary",), vmem_limit_bytes=V7X_VMEM_LIMIT),
        name="experts",
    )(block_exp, block_valid, xs, wgu, bgu, wdn, bdn)


def _combine_kernel(dest_hbm, h_ref, gate_ref, g_ref, ys_hbm, out_ref, dsm_s, ybuf, sem_idx, sem_rows,
                    *, tm):
    i = pl.program_id(0)
    last = pl.num_programs(0) - 1
    slot = i % 2

    def idx_copy(tile, s):
        return pltpu.make_async_copy(dest_hbm.at[jnp.minimum(tile, last)], dsm_s.at[s],
                                     sem_idx.at[s])

    def row_copy(s, kk, r):
        src = pl.multiple_of(dsm_s[s, kk, r] * SUBLANES, SUBLANES)
        return pltpu.make_async_copy(ys_hbm.at[pl.ds(src, SUBLANES), :],
                                     ybuf.at[s, kk, pl.ds(r * SUBLANES, SUBLANES), :],
                                     sem_rows.at[s])

    def wait_rows(s):
        for kk in range(TOP_K):
            pltpu.make_async_copy(ys_hbm.at[pl.ds(0, tm * SUBLANES), :], ybuf.at[s, kk],
                                  sem_rows.at[s]).wait()

    @pl.when(i == 0)
    def _():
        idx_copy(0, 0).start()
        idx_copy(0, 0).wait()

        def issue(r, c):
            for kk in range(TOP_K):
                row_copy(0, kk, r).start(priority=kk % 2)
            return c
        lax.fori_loop(0, tm, issue, 0, unroll=8)
        idx_copy(1, 1).start()

    idx_copy(i + 1, 1 - slot).wait()
    wait_rows(slot)
    for r in range(tm):
        for kk in range(TOP_K):
            row_copy(1 - slot, kk, r).start(priority=kk % 2)
    idx_copy(i + 2, slot).start()

    gates = jnp.transpose(gate_ref[0])
    acc = h_ref[...]
    for kk in range(TOP_K):
        acc = acc + gates[:, kk:kk + 1] * _load_token_tiles(ybuf.at[slot, kk], tm)
    out_ref[...] = acc * lax.rsqrt(jnp.mean(acc * acc, axis=-1, keepdims=True) + EPS) * g_ref[...]

    @pl.when(i == last)
    def _():
        wait_rows(1 - slot)
        idx_copy(i + 2, slot).wait()


def _combine(dest, h, gates, g_final, ys, tm):
    n, d = h.shape
    return pl.pallas_call(
        functools.partial(_combine_kernel, tm=tm),
        grid=(n // tm,),
        in_specs=[pl.BlockSpec(memory_space=pl.ANY),
                  pl.BlockSpec((tm, d), lambda i: (i, 0)),
                  pl.BlockSpec((1, ROUTE_ROWS, tm), lambda i: (i, 0, 0)),
                  pl.BlockSpec((1, d), lambda i: (0, 0)),
                  pl.BlockSpec(memory_space=pl.ANY)],
        out_specs=pl.BlockSpec((tm, d), lambda i: (i, 0)),
        out_shape=jax.ShapeDtypeStruct((n, d), F32),
        scratch_shapes=[pltpu.SMEM((2, ROUTE_ROWS, tm), jnp.int32),
                        pltpu.VMEM((2, TOP_K, tm * SUBLANES, LANES), F32),
                        pltpu.SemaphoreType.DMA((2,)), pltpu.SemaphoreType.DMA((2,))],
        compiler_params=pltpu.CompilerParams(
            dimension_semantics=("arbitrary",), vmem_limit_bytes=V7X_VMEM_LIMIT),
        name="combine",
    )(dest, h, gates, g_final, ys)


def kernel(x, meta_tokens, norm_mix, w_in, b_in, lam_params, subln_gain, w_attn_proj, conv_dw_w,
           conv_dw_b, conv_ln_g, conv_ln_b, w_conv_proj, w_out, norm_ffn, w_router, b_router,
           w_gate_up, b_gate_up, w_down, b_down, norm_final):
    bsz, seq, d = x.shape
    assert norm_mix.shape[0] == 1, "single-layer block"
    n = bsz * seq
    tm_proj = min(INPROJ_ROWS, seq)
    tq = min(ATTN_Q_TILE, seq)
    tm = min(MIXER_ROWS, seq)
    lam_init = 0.8 - 0.6 * 1.0

    x2d = x.reshape(n, d)
    w_rest = jnp.concatenate([w_in[0][:, d:2 * d], w_in[0][:, 3 * d:]], axis=1).astype(BF16)
    b_rest = jnp.concatenate([b_in[0][d:2 * d], b_in[0][3 * d:]])[None]
    w_t = jnp.concatenate([w_in[0][:, :d], w_in[0][:, 2 * d:3 * d]], axis=1).T.astype(BF16)
    b_t = jnp.concatenate([b_in[0][:d], b_in[0][2 * d:3 * d]])[:, None]
    g_mix = norm_mix[0][None]
    qt, k, vt, u, sga, sgc = _inproj(x2d, g_mix, w_rest, b_rest, w_t, b_t, tm_proj)
    _, k_m, vt_m, u_m, _, _ = _inproj(meta_tokens.astype(x.dtype), g_mix, w_rest, b_rest, w_t, b_t,
                                      N_META)

    o = _attention(lam_params[0], qt, k, vt, k_m, vt_m, subln_gain[0][:, None], bsz, seq, tq,
                   lam_init)

    meta_halo = jnp.concatenate([jnp.zeros((HALO - N_META, d), BF16), u_m], axis=0)
    wr_t = w_router[0].T
    wr_hi = wr_t.astype(BF16)
    wr_split = jnp.concatenate([wr_hi, (wr_t - wr_hi.astype(F32)).astype(BF16)], axis=0)
    h, hn, eidx, rank, gates, counts = _mixer(
        x2d, o, u, sga, sgc, meta_halo,
        w_attn_proj[0].astype(BF16), w_conv_proj[0].astype(BF16), w_out[0].astype(BF16),
        conv_dw_w[0], conv_dw_b[0][None], conv_ln_g[0][None], conv_ln_b[0][None],
        norm_ffn[0][None], wr_split, b_router[0][:, None], seq, tm)

    cnt = counts[:, 0].astype(jnp.int32)
    padded = (cnt + EXPERT_BLOCK - 1) // EXPERT_BLOCK * EXPERT_BLOCK
    pad_end = jnp.cumsum(padded)
    pad_start = pad_end - padded
    n_blocks = -(-(n * TOP_K + N_EXPERTS * (EXPERT_BLOCK - 1)) // EXPERT_BLOCK)
    n_slots = n_blocks * EXPERT_BLOCK
    experts = jnp.arange(N_EXPERTS, dtype=jnp.int32)
    dest = rank + jnp.sum(jnp.where(eidx[..., None] == experts, pad_start, 0), axis=-1)
    block_start = jnp.arange(n_blocks, dtype=jnp.int32) * EXPERT_BLOCK
    block_exp = jnp.minimum(jnp.sum(block_start[:, None] >= pad_end[None, :], axis=1),
                            N_EXPERTS - 1).astype(jnp.int32)
    block_valid = jnp.clip(pad_start[block_exp] + cnt[block_exp] - block_start, 0,
                           EXPERT_BLOCK).astype(jnp.int32)

    xs = _dispatch(dest, hn, n_slots, tm)
    ys = _experts(block_exp, block_valid, xs, w_gate_up[0], b_gate_up[0][:, None, :],
                  w_down[0], b_down[0][:, None, :])
    out = _combine(dest, h, gates, norm_final[None], ys, tm)
    return out.reshape(bsz, seq, d)
```

```python
import functools

import jax
import jax.numpy as jnp
from jax import lax
from jax.experimental import pallas as pl
from jax.experimental.pallas import tpu as pltpu

CHUNK = 64
N_META = 16
N_HEADS = 8
HEAD_DIM = 64
V_DIM = 2 * HEAD_DIM
CONV_K = 31
N_EXPERTS = 32
TOP_K = 4
SWIGLU_ALPHA = 1.702
SWIGLU_LIMIT = 7.0
EXPERT_BLOCK = 1024
INPROJ_ROWS = 512
ATTN_Q_TILE = 1024
MIXER_ROWS = 512
EPS = 1e-5
NEG = -1e30
LOG2E = 1.4426950408889634

V7X_VMEM_LIMIT = 56 * 1024 * 1024
SUBLANES = 8
LANES = 128
META_PAD = 128
HALO = 32
ROUTE_ROWS = 8

BF16 = jnp.bfloat16
F32 = jnp.float32


def _sigmoid(x):
    return 1.0 / (1.0 + jnp.exp(-x))


def _load_token_tiles(ref, rows):
    return jnp.concatenate([ref[pl.ds(s, rows, stride=SUBLANES), :] for s in range(SUBLANES)],
                           axis=-1)


def _store_token_tiles(ref, x):
    rows = x.shape[0]
    for s in range(SUBLANES):
        ref[pl.ds(s, rows, stride=SUBLANES), :] = x[:, s * LANES:(s + 1) * LANES]


def _inproj_kernel(x_ref, g_ref, w_ref, b_ref, wt_ref, bt_ref,
                   qt_ref, k_ref, vt_ref, u_ref, sga_ref, sgc_ref):
    d = x_ref.shape[1]
    x = x_ref[...]
    xn = x * lax.rsqrt(jnp.mean(x * x, axis=-1, keepdims=True) + EPS) * g_ref[...]
    xb = xn.astype(BF16)

    def proj(c):
        return (jnp.dot(xb, w_ref[:, c * d:(c + 1) * d], preferred_element_type=F32)
                + b_ref[:, c * d:(c + 1) * d])

    kf = proj(0).astype(BF16)
    for hd in range(N_HEADS):
        k_ref[hd] = kf[:, hd * V_DIM:(hd + 1) * V_DIM]
    t = lax.dot_general(wt_ref[...], xb, (((1,), (1,)), ((), ())), preferred_element_type=F32)
    t = t + bt_ref[...]
    qt_ref[...] = (t[:d] * (HEAD_DIM ** -0.5 * LOG2E)).astype(BF16)
    vt_ref[...] = t[d:].astype(BF16)
    u_ref[...] = (proj(1) * _sigmoid(proj(2))).astype(BF16)
    sga_ref[...] = _sigmoid(proj(3)).astype(BF16)
    sgc_ref[...] = _sigmoid(proj(4)).astype(BF16)


def _inproj(x2d, g, w_bf, b, wt_bf, bt, tm):
    n, d = x2d.shape
    in_w = w_bf.shape[1]
    out = jax.ShapeDtypeStruct((n, d), BF16)
    row = pl.BlockSpec((tm, d), lambda i: (i, 0))
    col = pl.BlockSpec((d, tm), lambda i: (0, i))
    heads = pl.BlockSpec((N_HEADS, tm, V_DIM), lambda i: (0, i, 0))
    out_heads = jax.ShapeDtypeStruct((N_HEADS, n, V_DIM), BF16)
    out_t = jax.ShapeDtypeStruct((d, n), BF16)
    return pl.pallas_call(
        _inproj_kernel,
        grid=(n // tm,),
        in_specs=[row,
                  pl.BlockSpec((1, d), lambda i: (0, 0)),
                  pl.BlockSpec((d, in_w), lambda i: (0, 0)),
                  pl.BlockSpec((1, in_w), lambda i: (0, 0)),
                  pl.BlockSpec((2 * d, d), lambda i: (0, 0)),
                  pl.BlockSpec((2 * d, 1), lambda i: (0, 0))],
        out_specs=[col, heads, col, row, row, row],
        out_shape=[out_t, out_heads, out_t, out, out, out],
        compiler_params=pltpu.CompilerParams(
            dimension_semantics=("arbitrary",), vmem_limit_bytes=V7X_VMEM_LIMIT),
        name="inproj",
    )(x2d, g, w_bf, b, wt_bf, bt)


def _attn_kernel(lam_ref, qt_ref, k_ref, vt_ref, km_ref, vmt_ref, bias_ref, bmeta_ref,
                 gain_ref, o_ref, *scratch, tq, lam_init):
    def q_tile(qi, carry):
        cols = pl.ds(pl.multiple_of(qi * tq, tq), tq)
        _attn_tile(qi, lam_ref, qt_ref.at[:, cols], k_ref, vt_ref, km_ref, vmt_ref, bias_ref,
                   bmeta_ref, gain_ref, o_ref.at[0, cols, :], *scratch, tq=tq, lam_init=lam_init)
        return carry

    lax.fori_loop(0, qt_ref.shape[1] // tq, q_tile, 0)


def _attn_tile(qi, lam_ref, qt_ref, k_ref, vt_ref, km_ref, vmt_ref, bias_ref, bmeta_ref,
               gain_ref, o_ref, m_s, l_s, acc_s, s0, s1, mx0, mx1, *, tq, lam_init):
    h = pl.program_id(1)
    slope = jnp.exp2(-(h + 1).astype(F32))
    lp = lam_ref[...]
    lam = (jnp.exp(jnp.sum(lp[0:1] * lp[1:2], axis=-1, keepdims=True))
           - jnp.exp(jnp.sum(lp[2:3] * lp[3:4], axis=-1, keepdims=True)) + lam_init)

    qt = qt_ref[...]
    feat = lax.broadcasted_iota(jnp.int32, qt.shape, 0)
    zero = jnp.zeros_like(qt)
    qs = (jnp.where(feat < HEAD_DIM, qt, zero), jnp.where(feat >= HEAD_DIM, qt, zero))

    def scores_t(kk, qm):
        return jnp.dot(kk, qm, preferred_element_type=F32)

    tk = tq // 2
    n_off = 2 * qi

    def tile_shift(t):
        off = (qi * tq - t * tk).astype(F32) * LOG2E
        return jnp.where(t < n_off, slope * off, 0.0)

    def stage_a(t, s_buf, mx_buf, cols=slice(None)):
        kk = k_ref[0, pl.ds(pl.multiple_of(t * tk, tk), tk), :]
        bias = bias_ref[0, jnp.where(t < n_off, 0, t - n_off + 1), :, cols]
        for idx in range(2):
            s = scores_t(kk, qs[idx][:, cols]) + bias
            s_buf[idx, :, cols] = s
            mx_buf[idx, :, cols] = jnp.max(s, axis=0, keepdims=True)

    def stage_b(t, s_buf, mx_buf, cols=slice(None), meta=None):
        vt = vt_ref[:, pl.ds(pl.multiple_of(t * tk, tk), tk)]
        vt1 = jnp.concatenate([vt, jnp.ones((2 * SUBLANES, tk), BF16)], axis=0)
        shift = tile_shift(t)
        for idx in range(2):
            m_old = m_s[idx, :, cols]
            m_new = jnp.maximum(m_old, mx_buf[idx, :, cols] - shift)
            if meta is not None:
                s_meta, vmt1, shift_m = meta
                m_new = jnp.maximum(m_new, jnp.max(s_meta[idx], axis=0, keepdims=True) - shift_m)
            p = jnp.exp2(s_buf[idx, :, cols] - (m_new + shift))
            pv1 = jnp.dot(vt1, p.astype(BF16), preferred_element_type=F32)
            if meta is not None:
                pm = jnp.exp2(s_meta[idx] - (m_new + shift_m))
                pv1 = pv1 + jnp.dot(vmt1, pm.astype(BF16), preferred_element_type=F32)
            pv = pv1[:V_DIM]
            ps = pv1[V_DIM:V_DIM + 1]
            alpha = jnp.exp2(m_old - m_new)
            l_s[idx, :, cols] = alpha * l_s[idx, :, cols] + ps
            acc_s[idx, :, cols] = alpha * acc_s[idx, :, cols] + pv
            m_s[idx, :, cols] = m_new

    m_s[...] = jnp.full(m_s.shape, NEG, F32)
    l_s[...] = jnp.zeros_like(l_s)
    acc_s[...] = jnp.zeros_like(acc_s)
    stage_a(0, s0, mx0)

    def pair(i):
        stage_a(2 * i + 1, s1, mx1)
        stage_b(2 * i, s0, mx0)
        stage_a(2 * i + 2, s0, mx0)
        stage_b(2 * i + 1, s1, mx1)

    def body(ii, carry):
        pair(2 * ii)
        pair(2 * ii + 1)
        return carry

    lax.fori_loop(0, qi // 2, body, 0)

    @pl.when(qi % 2 == 1)
    def _():
        pair(qi - 1)

    late = slice(tq // 2, tq)
    stage_a(n_off + 1, s1, mx1, late)
    s_meta = [scores_t(km_ref[0], qs[idx]) + bmeta_ref[0] for idx in range(2)]
    vmt1 = jnp.concatenate([vmt_ref[...], jnp.ones((2 * SUBLANES, META_PAD), BF16)], axis=0)
    shift0 = slope * ((qi * tq).astype(F32) * LOG2E)
    stage_b(n_off, s0, mx0, meta=(s_meta, vmt1, shift0))
    stage_b(n_off + 1, s1, mx1, late)

    o = acc_s[0] / l_s[0] - lam * (acc_s[1] / l_s[1])
    o = o * lax.rsqrt(jnp.mean(o * o, axis=0, keepdims=True) + EPS) * gain_ref[...]
    o_ref[...] = jnp.transpose(o * (1.0 - lam_init)).astype(BF16)


def _attention(lam_params, qt, k, vt, k_meta, vt_meta, gain, bsz, seq, tq, lam_init):
    n = qt.shape[1]
    slopes = jnp.asarray([2.0 ** (-8.0 * (h + 1) / N_HEADS) for h in range(N_HEADS)], F32)
    tk = tq // 2
    r = jnp.arange(tq, dtype=jnp.int32)
    diff = r[None, :] - r[:, None]
    slopes = slopes * LOG2E
    boff = -slopes[:, None, None] * diff.astype(F32)[None, :tk]
    vis = (r[:, None] // CHUNK) <= (r[None, :] // CHUNK)
    bdiag = jnp.where(vis[None], -slopes[:, None, None] * jnp.abs(diff).astype(F32)[None], NEG)
    bias_all = jnp.stack([boff, bdiag[:, :tk], bdiag[:, tk:]], axis=1)
    mrow = jnp.arange(META_PAD, dtype=jnp.int32)
    dmeta = (N_META + r[None, :] - mrow[:, None]).astype(F32)
    bmeta = jnp.where((mrow < N_META)[None, :, None], -slopes[:, None, None] * dmeta[None], NEG)

    km = jnp.pad(k_meta, ((0, 0), (0, META_PAD - N_META), (0, 0)))
    vmt = jnp.pad(vt_meta, ((0, 0), (0, META_PAD - N_META)))

    kern = functools.partial(_attn_kernel, tq=tq, lam_init=lam_init)
    return pl.pallas_call(
        kern,
        grid=(bsz, N_HEADS),
        in_specs=[
            pl.BlockSpec((4, HEAD_DIM), lambda b, h: (0, 0)),
            pl.BlockSpec((V_DIM, seq), lambda b, h: (h, b)),
            pl.BlockSpec((1, seq, V_DIM), lambda b, h: (h, b, 0)),
            pl.BlockSpec((V_DIM, seq), lambda b, h: (h, b)),
            pl.BlockSpec((1, META_PAD, V_DIM), lambda b, h: (h, 0, 0)),
            pl.BlockSpec((V_DIM, META_PAD), lambda b, h: (h, 0)),
            pl.BlockSpec((1, 3, tk, tq), lambda b, h: (h, 0, 0, 0)),
            pl.BlockSpec((1, META_PAD, tq), lambda b, h: (h, 0, 0)),
            pl.BlockSpec((V_DIM, 1), lambda b, h: (0, 0)),
        ],
        out_specs=pl.BlockSpec((1, seq, V_DIM), lambda b, h: (h, b, 0)),
        out_shape=jax.ShapeDtypeStruct((N_HEADS, n, V_DIM), BF16),
        scratch_shapes=[pltpu.VMEM((2, 1, tq), F32), pltpu.VMEM((2, 1, tq), F32),
                        pltpu.VMEM((2, V_DIM, tq), F32),
                        pltpu.VMEM((2, tk, tq), F32), pltpu.VMEM((2, tk, tq), F32),
                        pltpu.VMEM((2, 1, tq), F32), pltpu.VMEM((2, 1, tq), F32)],
        compiler_params=pltpu.CompilerParams(
            dimension_semantics=("arbitrary", "arbitrary"),
            vmem_limit_bytes=V7X_VMEM_LIMIT),
        name="diff_attention",
    )(lam_params, qt, k, vt, km, vmt, bias_all, bmeta, gain)


def _mixer_kernel(x_ref, o_ref, u_ref, uh_ref, mh_ref, sga_ref, sgc_ref,
                  wap_ref, wpw_ref, wout_ref, dww_ref, dwb_ref, lng_ref, lnb_ref,
                  nffn_ref, wr_ref, br_ref, tri_ref,
                  h_ref, hn_ref, exp_ref, rank_ref, gate_ref, cnt_ref,
                  ext_s, y_s, cnt_s, *, tm, tiles_per_seq):
    i = pl.program_id(0)
    d = x_ref.shape[1]

    @pl.when(i == 0)
    def _():
        cnt_s[...] = jnp.zeros_like(cnt_s)

    first = (i % tiles_per_seq) == 0
    off = HALO - (CONV_K - 1)
    span = tm + HALO - SUBLANES
    acc_rows = 128

    o = jnp.concatenate([o_ref[hd] for hd in range(N_HEADS)], axis=-1)
    y_attn = jnp.dot(o, wap_ref[...], preferred_element_type=F32)
    for c in range(d // LANES):
        cs = slice(c * LANES, (c + 1) * LANES)
        halo = jnp.where(first, mh_ref[:, cs], uh_ref[:, cs])
        ext_s[0, 0:HALO, :] = halo.astype(F32)
        ext_s[0, HALO:HALO + tm, :] = u_ref[:, cs].astype(F32)
        for b in range(1, SUBLANES):
            ext_s[b, 0:span, :] = ext_s[0, b:b + span, :]
        for r0 in range(0, tm, acc_rows):
            acc = jnp.zeros((acc_rows, LANES), F32)
            for t in range(CONV_K):
                a, b = divmod(off + t, SUBLANES)
                row = a * SUBLANES + r0
                acc = acc + dww_ref[t:t + 1, cs] * ext_s[b, row:row + acc_rows, :]
            y_s[r0:r0 + acc_rows, cs] = acc
    y = y_s[...] + dwb_ref[...]
    mu = jnp.mean(y, axis=-1, keepdims=True)
    yc = y - mu
    var = jnp.mean(yc * yc, axis=-1, keepdims=True)
    y = yc * lax.rsqrt(var + EPS) * lng_ref[...] + lnb_ref[...]
    y = y * _sigmoid(y)
    y_conv = jnp.dot(y.astype(BF16), wpw_ref[...], preferred_element_type=F32)

    mixed = sga_ref[...].astype(F32) * y_attn + sgc_ref[...].astype(F32) * y_conv
    h = x_ref[...] + jnp.dot(mixed.astype(BF16), wout_ref[...], preferred_element_type=F32)
    h_ref[...] = h
    hn = h * lax.rsqrt(jnp.mean(h * h, axis=-1, keepdims=True) + EPS) * nffn_ref[...]
    _store_token_tiles(hn_ref, hn)

    nt_dims = (((1,), (1,)), ((), ()))
    hn_hi = hn.astype(BF16)
    hn_lo = (hn - hn_hi.astype(F32)).astype(BF16)
    both = lax.dot_general(wr_ref[...], hn_hi, nt_dims, preferred_element_type=F32)
    logits = (both[:N_EXPERTS] + both[N_EXPERTS:]
              + lax.dot_general(wr_ref[0:N_EXPERTS, :], hn_lo, nt_dims, preferred_element_type=F32)
              + br_ref[...])
    eio = lax.broadcasted_iota(jnp.int32, logits.shape, 0)
    sels, tops, idxs = [], [], []
    cur = logits
    for _ in range(TOP_K):
        m = jnp.max(cur, axis=0, keepdims=True)
        idx = jnp.min(jnp.where(cur == m, eio, N_EXPERTS), axis=0, keepdims=True)
        sel = eio == idx
        sels.append(sel)
        tops.append(m)
        idxs.append(idx)
        cur = jnp.where(sel, -jnp.inf, cur)
    exps = [jnp.exp(t - tops[0]) for t in tops]
    denom = exps[0] + exps[1] + exps[2] + exps[3]
    hot = jnp.zeros(logits.shape, F32)
    for sel in sels:
        hot = hot + jnp.where(sel, 1.0, 0.0)
    before = jnp.dot(hot.astype(BF16), tri_ref[...], preferred_element_type=F32) + cnt_s[...]
    ranks = [jnp.sum(jnp.where(sel, before, 0.0), axis=0, keepdims=True).astype(jnp.int32)
             for sel in sels]
    zi = jnp.zeros((ROUTE_ROWS - TOP_K, tm), jnp.int32)
    exp_ref[0] = jnp.concatenate(idxs + [zi], axis=0)
    rank_ref[0] = jnp.concatenate(ranks + [zi], axis=0)
    gate_ref[0] = jnp.concatenate([e / denom for e in exps] + [zi.astype(F32)], axis=0)
    cnt_s[...] = cnt_s[...] + jnp.sum(hot, axis=1, keepdims=True)
    cnt_ref[...] = jnp.broadcast_to(cnt_s[...], cnt_ref.shape)


def _mixer(x2d, o, u, sga, sgc, meta_halo, wap, wpw, wout, dww, dwb, lng, lnb, nffn, wr_t, br,
           seq, tm):
    n, d = x2d.shape
    nt = n // tm
    hb = tm // HALO
    tri = (jnp.arange(tm)[:, None] < jnp.arange(tm)[None, :]).astype(BF16)
    row = pl.BlockSpec((tm, d), lambda i: (i, 0))
    route = pl.BlockSpec((1, ROUTE_ROWS, tm), lambda i: (i, 0, 0))

    def full(shp):
        return pl.BlockSpec(shp, lambda i: tuple(0 for _ in shp))

    kern = functools.partial(_mixer_kernel, tm=tm, tiles_per_seq=seq // tm)
    return pl.pallas_call(
        kern,
        grid=(nt,),
        in_specs=[row, pl.BlockSpec((N_HEADS, tm, V_DIM), lambda i: (0, i, 0)), row,
                  pl.BlockSpec((HALO, d), lambda i: (jnp.maximum(i * hb - 1, 0), 0)),
                  full((HALO, d)), row, row,
                  full((d, d)), full((d, d)), full((d, d)),
                  full((CONV_K, d)), full((1, d)), full((1, d)), full((1, d)), full((1, d)),
                  full((2 * N_EXPERTS, d)), full((N_EXPERTS, 1)), full((tm, tm))],
        out_specs=[row, pl.BlockSpec((tm * SUBLANES, LANES), lambda i: (i, 0)),
                   route, route, route, full((N_EXPERTS, LANES))],
        out_shape=[jax.ShapeDtypeStruct((n, d), F32),
                   jax.ShapeDtypeStruct((n * SUBLANES, LANES), F32),
                   jax.ShapeDtypeStruct((nt, ROUTE_ROWS, tm), jnp.int32),
                   jax.ShapeDtypeStruct((nt, ROUTE_ROWS, tm), jnp.int32),
                   jax.ShapeDtypeStruct((nt, ROUTE_ROWS, tm), F32),
                   jax.ShapeDtypeStruct((N_EXPERTS, LANES), F32)],
        scratch_shapes=[pltpu.VMEM((SUBLANES, HALO + tm, LANES), F32), pltpu.VMEM((tm, d), F32),
                        pltpu.VMEM((N_EXPERTS, 1), F32)],
        compiler_params=pltpu.CompilerParams(
            dimension_semantics=("arbitrary",), vmem_limit_bytes=V7X_VMEM_LIMIT),
        name="mixer_router",
    )(x2d, o, u, u, meta_halo, sga, sgc, wap, wpw, wout, dww, dwb, lng, lnb, nffn, wr_t, br, tri)


def _prefetch_routes(dest_hbm, dsm_s, sem_idx):
    i = pl.program_id(0)
    slot = i % 2

    def idx_copy(tile, s):
        return pltpu.make_async_copy(dest_hbm.at[tile], dsm_s.at[s], sem_idx.at[s])

    @pl.when(i == 0)
    def _():
        idx_copy(0, 0).start()

    idx_copy(i, slot).wait()

    @pl.when(i + 1 < pl.num_programs(0))
    def _():
        idx_copy(i + 1, 1 - slot).start()

    return slot


def _dispatch_kernel(dest_hbm, hn_ref, xs_hbm, dsm_s, sem_idx, sem_rows, *, tm):
    slot = _prefetch_routes(dest_hbm, dsm_s, sem_idx)

    def issue(r, c):
        src = hn_ref.at[pl.ds(pl.multiple_of(r * SUBLANES, SUBLANES), SUBLANES), :]
        for kk in range(TOP_K):
            dst = pl.multiple_of(dsm_s[slot, kk, r] * SUBLANES, SUBLANES)
            pltpu.make_async_copy(src, xs_hbm.at[pl.ds(dst, SUBLANES), :],
                                  sem_rows).start(priority=kk % 2)
        return c
    lax.fori_loop(0, tm, issue, 0, unroll=8)

    for kk in range(TOP_K):
        pltpu.make_async_copy(hn_ref, xs_hbm.at[pl.ds(0, tm * SUBLANES), :], sem_rows).wait()


def _dispatch(dest, hn, n_slots, tm):
    n = hn.shape[0] // SUBLANES
    return pl.pallas_call(
        functools.partial(_dispatch_kernel, tm=tm),
        grid=(n // tm,),
        in_specs=[pl.BlockSpec(memory_space=pl.ANY),
                  pl.BlockSpec((tm * SUBLANES, LANES), lambda i: (i, 0))],
        out_specs=pl.BlockSpec(memory_space=pl.ANY),
        out_shape=jax.ShapeDtypeStruct((n_slots * SUBLANES, LANES), F32),
        scratch_shapes=[pltpu.SMEM((2, ROUTE_ROWS, tm), jnp.int32),
                        pltpu.SemaphoreType.DMA((2,)), pltpu.SemaphoreType.DMA(())],
        compiler_params=pltpu.CompilerParams(
            dimension_semantics=("arbitrary",), vmem_limit_bytes=V7X_VMEM_LIMIT),
        name="dispatch",
    )(dest, hn)


def _expert_kernel(bexp_ref, bvalid_ref, xs_ref, wgu_ref, bgu_ref, wdn_ref, bdn_ref, ys_ref,
                   wgu_bf, wdn_bf):
    i = pl.program_id(0)
    valid = bvalid_ref[i]
    f = wdn_ref.shape[1]

    @pl.when((i == 0) | (bexp_ref[i] != bexp_ref[jnp.maximum(i - 1, 0)]))
    def _():
        wgu_bf[...] = wgu_ref[0].astype(BF16)
        wdn_bf[...] = wdn_ref[0].astype(BF16)

    @pl.when(valid > 0)
    def _():
        x = _load_token_tiles(xs_ref, EXPERT_BLOCK)
        rows = lax.broadcasted_iota(jnp.int32, x.shape, 0)
        xb = jnp.where(rows < valid, x, 0.0).astype(BF16)
        gu = jnp.dot(xb, wgu_bf[...], preferred_element_type=F32) + bgu_ref[0]
        glu = jnp.minimum(gu[:, :f], SWIGLU_LIMIT)
        lin = jnp.clip(gu[:, f:], -SWIGLU_LIMIT, SWIGLU_LIMIT)
        act = glu * _sigmoid(SWIGLU_ALPHA * glu) * (lin + 1.0)
        y = jnp.dot(act.astype(BF16), wdn_bf[...], preferred_element_type=F32) + bdn_ref[0]
        _store_token_tiles(ys_ref, y)

    @pl.when(valid <= 0)
    def _():
        ys_ref[...] = jnp.zeros_like(ys_ref)


def _experts(block_exp, block_valid, xs, wgu, bgu, wdn, bdn):
    d = wgu.shape[1]
    n_slots = xs.shape[0] // SUBLANES
    nb = n_slots // EXPERT_BLOCK
    f2 = wgu.shape[2]
    f = wdn.shape[1]
    tile_rows = EXPERT_BLOCK * SUBLANES
    grid_spec = pltpu.PrefetchScalarGridSpec(
        num_scalar_prefetch=2,
        grid=(nb,),
        in_specs=[pl.BlockSpec((tile_rows, LANES), lambda i, be, bv: (i, 0)),
                  pl.BlockSpec((1, d, f2), lambda i, be, bv: (be[i], 0, 0)),
                  pl.BlockSpec((1, 1, f2), lambda i, be, bv: (be[i], 0, 0)),
                  pl.BlockSpec((1, f, d), lambda i, be, bv: (be[i], 0, 0)),
                  pl.BlockSpec((1, 1, d), lambda i, be, bv: (be[i], 0, 0))],
        out_specs=pl.BlockSpec((tile_rows, LANES), lambda i, be, bv: (i, 0)),
        scratch_shapes=[pltpu.VMEM((d, f2), BF16), pltpu.VMEM((f, d), BF16)],
    )
    return pl.pallas_call(
        _expert_kernel,
        grid_spec=grid_spec,
        out_shape=jax.ShapeDtypeStruct(xs.shape, F32),
        compiler_params=pltpu.CompilerParams(
            dimension_semantics=("arbitrary",), vmem_limit_bytes=V7X_VMEM_LIMIT),
        name="experts",
    )(block_exp, block_valid, xs, wgu, bgu, wdn, bdn)


def _combine_kernel(dest_hbm, h_ref, gate_ref, g_ref, ys_hbm, out_ref, dsm_s, ybuf, sem_idx, sem_rows,
                    *, tm):
    i = pl.program_id(0)
    last = pl.num_programs(0) - 1
    slot = i % 2

    def idx_copy(tile, s):
        return pltpu.make_async_copy(dest_hbm.at[jnp.minimum(tile, last)], dsm_s.at[s],
                                     sem_idx.at[s])

    def row_copy(s, kk, r):
        src = pl.multiple_of(dsm_s[s, kk, r] * SUBLANES, SUBLANES)
        return pltpu.make_async_copy(ys_hbm.at[pl.ds(src, SUBLANES), :],
                                     ybuf.at[s, kk, pl.ds(r * SUBLANES, SUBLANES), :],
                                     sem_rows.at[s])

    def wait_rows(s):
        for kk in range(TOP_K):
            pltpu.make_async_copy(ys_hbm.at[pl.ds(0, tm * SUBLANES), :], ybuf.at[s, kk],
                                  sem_rows.at[s]).wait()

    @pl.when(i == 0)
    def _():
        idx_copy(0, 0).start()
        idx_copy(0, 0).wait()

        def issue(r, c):
            for kk in range(TOP_K):
                row_copy(0, kk, r).start(priority=kk % 2)
            return c
        lax.fori_loop(0, tm, issue, 0, unroll=8)
        idx_copy(1, 1).start()

    idx_copy(i + 1, 1 - slot).wait()
    wait_rows(slot)
    for r in range(tm):
        for kk in range(TOP_K):
            row_copy(1 - slot, kk, r).start(priority=kk % 2)
    idx_copy(i + 2, slot).start()

    gates = jnp.transpose(gate_ref[0])
    acc = h_ref[...]
    for kk in range(TOP_K):
        acc = acc + gates[:, kk:kk + 1] * _load_token_tiles(ybuf.at[slot, kk], tm)
    out_ref[...] = acc * lax.rsqrt(jnp.mean(acc * acc, axis=-1, keepdims=True) + EPS) * g_ref[...]

    @pl.when(i == last)
    def _():
        wait_rows(1 - slot)
        idx_copy(i + 2, slot).wait()


def _combine(dest, h, gates, g_final, ys, tm):
    n, d = h.shape
    return pl.pallas_call(
        functools.partial(_combine_kernel, tm=tm),
        grid=(n // tm,),
        in_specs=[pl.BlockSpec(memory_space=pl.ANY),
                  pl.BlockSpec((tm, d), lambda i: (i, 0)),
                  pl.BlockSpec((1, ROUTE_ROWS, tm), lambda i: (i, 0, 0)),
                  pl.BlockSpec((1, d), lambda i: (0, 0)),
                  pl.BlockSpec(memory_space=pl.ANY)],
        out_specs=pl.BlockSpec((tm, d), lambda i: (i, 0)),
        out_shape=jax.ShapeDtypeStruct((n, d), F32),
        scratch_shapes=[pltpu.SMEM((2, ROUTE_ROWS, tm), jnp.int32),
                        pltpu.VMEM((2, TOP_K, tm * SUBLANES, LANES), F32),
                        pltpu.SemaphoreType.DMA((2,)), pltpu.SemaphoreType.DMA((2,))],
        compiler_params=pltpu.CompilerParams(
            dimension_semantics=("arbitrary",), vmem_limit_bytes=V7X_VMEM_LIMIT),
        name="combine",
    )(dest, h, gates, g_final, ys)


def kernel(x, meta_tokens, norm_mix, w_in, b_in, lam_params, subln_gain, w_attn_proj, conv_dw_w,
           conv_dw_b, conv_ln_g, conv_ln_b, w_conv_proj, w_out, norm_ffn, w_router, b_router,
           w_gate_up, b_gate_up, w_down, b_down, norm_final):
    bsz, seq, d = x.shape
    assert norm_mix.shape[0] == 1, "single-layer block"
    n = bsz * seq
    tm_proj = min(INPROJ_ROWS, seq)
    tq = min(ATTN_Q_TILE, seq)
    tm = min(MIXER_ROWS, seq)
    lam_init = 0.8 - 0.6 * 1.0

    x2d = x.reshape(n, d)
    w_rest = jnp.concatenate([w_in[0][:, d:2 * d], w_in[0][:, 3 * d:]], axis=1).astype(BF16)
    b_rest = jnp.concatenate([b_in[0][d:2 * d], b_in[0][3 * d:]])[None]
    w_t = jnp.concatenate([w_in[0][:, :d], w_in[0][:, 2 * d:3 * d]], axis=1).T.astype(BF16)
    b_t = jnp.concatenate([b_in[0][:d], b_in[0][2 * d:3 * d]])[:, None]
    g_mix = norm_mix[0][None]
    qt, k, vt, u, sga, sgc = _inproj(x2d, g_mix, w_rest, b_rest, w_t, b_t, tm_proj)
    _, k_m, vt_m, u_m, _, _ = _inproj(meta_tokens.astype(x.dtype), g_mix, w_rest, b_rest, w_t, b_t,
                                      N_META)

    o = _attention(lam_params[0], qt, k, vt, k_m, vt_m, subln_gain[0][:, None], bsz, seq, tq,
                   lam_init)

    meta_halo = jnp.concatenate([jnp.zeros((HALO - N_META, d), BF16), u_m], axis=0)
    wr_t = w_router[0].T
    wr_hi = wr_t.astype(BF16)
    wr_split = jnp.concatenate([wr_hi, (wr_t - wr_hi.astype(F32)).astype(BF16)], axis=0)
    h, hn, eidx, rank, gates, counts = _mixer(
        x2d, o, u, sga, sgc, meta_halo,
        w_attn_proj[0].astype(BF16), w_conv_proj[0].astype(BF16), w_out[0].astype(BF16),
        conv_dw_w[0], conv_dw_b[0][None], conv_ln_g[0][None], conv_ln_b[0][None],
        norm_ffn[0][None], wr_split, b_router[0][:, None], seq, tm)

    cnt = counts[:, 0].astype(jnp.int32)
    padded = (cnt + EXPERT_BLOCK - 1) // EXPERT_BLOCK * EXPERT_BLOCK
    pad_end = jnp.cumsum(padded)
    pad_start = pad_end - padded
    n_blocks = -(-(n * TOP_K + N_EXPERTS * (EXPERT_BLOCK - 1)) // EXPERT_BLOCK)
    n_slots = n_blocks * EXPERT_BLOCK
    experts = jnp.arange(N_EXPERTS, dtype=jnp.int32)
    dest = rank + jnp.sum(jnp.where(eidx[..., None] == experts, pad_start, 0), axis=-1)
    block_start = jnp.arange(n_blocks, dtype=jnp.int32) * EXPERT_BLOCK
    block_exp = jnp.minimum(jnp.sum(block_start[:, None] >= pad_end[None, :], axis=1),
                            N_EXPERTS - 1).astype(jnp.int32)
    block_valid = jnp.clip(pad_start[block_exp] + cnt[block_exp] - block_start, 0,
                           EXPERT_BLOCK).astype(jnp.int32)

    xs = _dispatch(dest, hn, n_slots, tm)
    ys = _experts(block_exp, block_valid, xs, w_gate_up[0], b_gate_up[0][:, None, :],
                  w_down[0], b_down[0][:, None, :])
    out = _combine(dest, h, gates, norm_final[None], ys, tm)
    return out.reshape(bsz, seq, d)
```

```python
import functools

import jax
import jax.numpy as jnp
from jax import lax
from jax.experimental import pallas as pl
from jax.experimental.pallas import tpu as pltpu

CHUNK = 64
N_META = 16
N_HEADS = 8
HEAD_DIM = 64
V_DIM = 2 * HEAD_DIM
CONV_K = 31
N_EXPERTS = 32
TOP_K = 4
SWIGLU_ALPHA = 1.702
SWIGLU_LIMIT = 7.0
EXPERT_BLOCK = 1024
EXPERT_F_CHUNKS = 2
INPROJ_ROWS = 512
ATTN_Q_TILE = 1024
MIXER_ROWS = 512
EPS = 1e-5
NEG = -1e30
LOG2E = 1.4426950408889634

V7X_VMEM_LIMIT = 56 * 1024 * 1024
SUBLANES = 8
LANES = 128
META_PAD = 128
HALO = 32
ROUTE_ROWS = 8

BF16 = jnp.bfloat16
F32 = jnp.float32


def _sigmoid(x):
    return 1.0 / (1.0 + jnp.exp(-x))


def _load_token_tiles(ref, rows):
    return jnp.concatenate([ref[pl.ds(s, rows, stride=SUBLANES), :] for s in range(SUBLANES)],
                           axis=-1)


def _store_token_tiles(ref, x):
    rows = x.shape[0]
    for s in range(SUBLANES):
        ref[pl.ds(s, rows, stride=SUBLANES), :] = x[:, s * LANES:(s + 1) * LANES]


def _inproj_kernel(x_ref, g_ref, w_ref, b_ref, wt_ref, bt_ref,
                   qt_ref, k_ref, vt_ref, u_ref, sga_ref, sgc_ref):
    d = x_ref.shape[1]
    x = x_ref[...]
    xn = x * lax.rsqrt(jnp.mean(x * x, axis=-1, keepdims=True) + EPS) * g_ref[...]
    xb = xn.astype(BF16)

    def proj(c):
        return (jnp.dot(xb, w_ref[:, c * d:(c + 1) * d], preferred_element_type=F32)
                + b_ref[:, c * d:(c + 1) * d])

    kf = proj(0).astype(BF16)
    for hd in range(N_HEADS):
        k_ref[hd] = kf[:, hd * V_DIM:(hd + 1) * V_DIM]
    t = lax.dot_general(wt_ref[...], xb, (((1,), (1,)), ((), ())), preferred_element_type=F32)
    t = t + bt_ref[...]
    qt_ref[...] = (t[:d] * (HEAD_DIM ** -0.5 * LOG2E)).astype(BF16)
    vt_ref[...] = t[d:].astype(BF16)
    u_ref[...] = (proj(1) * _sigmoid(proj(2))).astype(BF16)
    sga_ref[...] = _sigmoid(proj(3)).astype(BF16)
    sgc_ref[...] = _sigmoid(proj(4)).astype(BF16)


def _inproj(x2d, g, w_bf, b, wt_bf, bt, tm):
    n, d = x2d.shape
    in_w = w_bf.shape[1]
    out = jax.ShapeDtypeStruct((n, d), BF16)
    row = pl.BlockSpec((tm, d), lambda i: (i, 0))
    col = pl.BlockSpec((d, tm), lambda i: (0, i))
    heads = pl.BlockSpec((N_HEADS, tm, V_DIM), lambda i: (0, i, 0))
    out_heads = jax.ShapeDtypeStruct((N_HEADS, n, V_DIM), BF16)
    out_t = jax.ShapeDtypeStruct((d, n), BF16)
    return pl.pallas_call(
        _inproj_kernel,
        grid=(n // tm,),
        in_specs=[row,
                  pl.BlockSpec((1, d), lambda i: (0, 0)),
                  pl.BlockSpec((d, in_w), lambda i: (0, 0)),
                  pl.BlockSpec((1, in_w), lambda i: (0, 0)),
                  pl.BlockSpec((2 * d, d), lambda i: (0, 0)),
                  pl.BlockSpec((2 * d, 1), lambda i: (0, 0))],
        out_specs=[col, heads, col, row, row, row],
        out_shape=[out_t, out_heads, out_t, out, out, out],
        compiler_params=pltpu.CompilerParams(
            dimension_semantics=("arbitrary",), vmem_limit_bytes=V7X_VMEM_LIMIT),
        name="inproj",
    )(x2d, g, w_bf, b, wt_bf, bt)


def _attn_kernel(lam_ref, qt_ref, k_ref, vt_ref, km_ref, vmt_ref, bias_ref, bmeta_ref,
                 gain_ref, o_ref, *scratch, tq, lam_init):
    def q_tile(qi, carry):
        cols = pl.ds(pl.multiple_of(qi * tq, tq), tq)
        _attn_tile(qi, lam_ref, qt_ref.at[:, cols], k_ref, vt_ref, km_ref, vmt_ref, bias_ref,
                   bmeta_ref, gain_ref, o_ref.at[0, cols, :], *scratch, tq=tq, lam_init=lam_init)
        return carry

    lax.fori_loop(0, qt_ref.shape[1] // tq, q_tile, 0)


def _attn_tile(qi, lam_ref, qt_ref, k_ref, vt_ref, km_ref, vmt_ref, bias_ref, bmeta_ref,
               gain_ref, o_ref, m_s, l_s, acc_s, s0, s1, mx0, mx1, *, tq, lam_init):
    h = pl.program_id(1)
    slope = jnp.exp2(-(h + 1).astype(F32))
    lp = lam_ref[...]
    lam = (jnp.exp(jnp.sum(lp[0:1] * lp[1:2], axis=-1, keepdims=True))
           - jnp.exp(jnp.sum(lp[2:3] * lp[3:4], axis=-1, keepdims=True)) + lam_init)

    qt = qt_ref[...]
    feat = lax.broadcasted_iota(jnp.int32, qt.shape, 0)
    zero = jnp.zeros_like(qt)
    qs = (jnp.where(feat < HEAD_DIM, qt, zero), jnp.where(feat >= HEAD_DIM, qt, zero))

    def scores_t(kk, qm):
        return jnp.dot(kk, qm, preferred_element_type=F32)

    tk = tq // 2
    n_off = 2 * qi

    def tile_shift(t):
        off = (qi * tq - t * tk).astype(F32) * LOG2E
        return jnp.where(t < n_off, slope * off, 0.0)

    def stage_a(t, s_buf, mx_buf, cols=slice(None)):
        kk = k_ref[0, pl.ds(pl.multiple_of(t * tk, tk), tk), :]
        bias = bias_ref[0, jnp.where(t < n_off, 0, t - n_off + 1), :, cols]
        for idx in range(2):
            s = scores_t(kk, qs[idx][:, cols]) + bias
            s_buf[idx, :, cols] = s
            mx_buf[idx, :, cols] = jnp.max(s, axis=0, keepdims=True)

    def stage_b(t, s_buf, mx_buf, cols=slice(None), meta=None):
        vt = vt_ref[:, pl.ds(pl.multiple_of(t * tk, tk), tk)]
        vt1 = jnp.concatenate([vt, jnp.ones((2 * SUBLANES, tk), BF16)], axis=0)
        shift = tile_shift(t)
        for idx in range(2):
            m_old = m_s[idx, :, cols]
            m_new = jnp.maximum(m_old, mx_buf[idx, :, cols] - shift)
            if meta is not None:
                s_meta, vmt1, shift_m = meta
                m_new = jnp.maximum(m_new, jnp.max(s_meta[idx], axis=0, keepdims=True) - shift_m)
            p = jnp.exp2(s_buf[idx, :, cols] - (m_new + shift))
            pv1 = jnp.dot(vt1, p.astype(BF16), preferred_element_type=F32)
            if meta is not None:
                pm = jnp.exp2(s_meta[idx] - (m_new + shift_m))
                pv1 = pv1 + jnp.dot(vmt1, pm.astype(BF16), preferred_element_type=F32)
            pv = pv1[:V_DIM]
            ps = pv1[V_DIM:V_DIM + 1]
            alpha = jnp.exp2(m_old - m_new)
            l_s[idx, :, cols] = alpha * l_s[idx, :, cols] + ps
            acc_s[idx, :, cols] = alpha * acc_s[idx, :, cols] + pv
            m_s[idx, :, cols] = m_new

    m_s[...] = jnp.full(m_s.shape, NEG, F32)
    l_s[...] = jnp.zeros_like(l_s)
    acc_s[...] = jnp.zeros_like(acc_s)
    stage_a(0, s0, mx0)

    def pair(i):
        stage_a(2 * i + 1, s1, mx1)
        stage_b(2 * i, s0, mx0)
        stage_a(2 * i + 2, s0, mx0)
        stage_b(2 * i + 1, s1, mx1)

    def body(ii, carry):
        pair(2 * ii)
        pair(2 * ii + 1)
        return carry

    lax.fori_loop(0, qi // 2, body, 0)

    @pl.when(qi % 2 == 1)
    def _():
        pair(qi - 1)

    late = slice(tq // 2, tq)
    stage_a(n_off + 1, s1, mx1, late)
    s_meta = [scores_t(km_ref[0], qs[idx]) + bmeta_ref[0] for idx in range(2)]
    vmt1 = jnp.concatenate([vmt_ref[...], jnp.ones((2 * SUBLANES, META_PAD), BF16)], axis=0)
    shift0 = slope * ((qi * tq).astype(F32) * LOG2E)
    stage_b(n_off, s0, mx0, meta=(s_meta, vmt1, shift0))
    stage_b(n_off + 1, s1, mx1, late)

    o = acc_s[0] / l_s[0] - lam * (acc_s[1] / l_s[1])
    o = o * lax.rsqrt(jnp.mean(o * o, axis=0, keepdims=True) + EPS) * gain_ref[...]
    o_ref[...] = jnp.transpose(o * (1.0 - lam_init)).astype(BF16)


def _attention(lam_params, qt, k, vt, k_meta, vt_meta, gain, bsz, seq, tq, lam_init):
    n = qt.shape[1]
    slopes = jnp.asarray([2.0 ** (-8.0 * (h + 1) / N_HEADS) for h in range(N_HEADS)], F32)
    tk = tq // 2
    r = jnp.arange(tq, dtype=jnp.int32)
    diff = r[None, :] - r[:, None]
    slopes = slopes * LOG2E
    boff = -slopes[:, None, None] * diff.astype(F32)[None, :tk]
    vis = (r[:, None] // CHUNK) <= (r[None, :] // CHUNK)
    bdiag = jnp.where(vis[None], -slopes[:, None, None] * jnp.abs(diff).astype(F32)[None], NEG)
    bias_all = jnp.stack([boff, bdiag[:, :tk], bdiag[:, tk:]], axis=1)
    mrow = jnp.arange(META_PAD, dtype=jnp.int32)
    dmeta = (N_META + r[None, :] - mrow[:, None]).astype(F32)
    bmeta = jnp.where((mrow < N_META)[None, :, None], -slopes[:, None, None] * dmeta[None], NEG)

    km = jnp.pad(k_meta, ((0, 0), (0, META_PAD - N_META), (0, 0)))
    vmt = jnp.pad(vt_meta, ((0, 0), (0, META_PAD - N_META)))

    kern = functools.partial(_attn_kernel, tq=tq, lam_init=lam_init)
    return pl.pallas_call(
        kern,
        grid=(bsz, N_HEADS),
        in_specs=[
            pl.BlockSpec((4, HEAD_DIM), lambda b, h: (0, 0)),
            pl.BlockSpec((V_DIM, seq), lambda b, h: (h, b)),
            pl.BlockSpec((1, seq, V_DIM), lambda b, h: (h, b, 0)),
            pl.BlockSpec((V_DIM, seq), lambda b, h: (h, b)),
            pl.BlockSpec((1, META_PAD, V_DIM), lambda b, h: (h, 0, 0)),
            pl.BlockSpec((V_DIM, META_PAD), lambda b, h: (h, 0)),
            pl.BlockSpec((1, 3, tk, tq), lambda b, h: (h, 0, 0, 0)),
            pl.BlockSpec((1, META_PAD, tq), lambda b, h: (h, 0, 0)),
            pl.BlockSpec((V_DIM, 1), lambda b, h: (0, 0)),
        ],
        out_specs=pl.BlockSpec((1, seq, V_DIM), lambda b, h: (h, b, 0)),
        out_shape=jax.ShapeDtypeStruct((N_HEADS, n, V_DIM), BF16),
        scratch_shapes=[pltpu.VMEM((2, 1, tq), F32), pltpu.VMEM((2, 1, tq), F32),
                        pltpu.VMEM((2, V_DIM, tq), F32),
                        pltpu.VMEM((2, tk, tq), F32), pltpu.VMEM((2, tk, tq), F32),
                        pltpu.VMEM((2, 1, tq), F32), pltpu.VMEM((2, 1, tq), F32)],
        compiler_params=pltpu.CompilerParams(
            dimension_semantics=("arbitrary", "arbitrary"),
            vmem_limit_bytes=V7X_VMEM_LIMIT),
        name="diff_attention",
    )(lam_params, qt, k, vt, km, vmt, bias_all, bmeta, gain)


def _mixer_kernel(x_ref, o_ref, u_ref, uh_ref, mh_ref, sga_ref, sgc_ref,
                  wap_ref, wpw_ref, wout_ref, dww_ref, dwb_ref, lng_ref, lnb_ref,
                  nffn_ref, wr_ref, br_ref, tri_ref,
                  h_ref, hn_ref, exp_ref, rank_ref, gate_ref, cnt_ref,
                  ext_s, y_s, cnt_s, *, tm, tiles_per_seq):
    i = pl.program_id(0)
    d = x_ref.shape[1]

    @pl.when(i == 0)
    def _():
        cnt_s[...] = jnp.zeros_like(cnt_s)

    first = (i % tiles_per_seq) == 0
    off = HALO - (CONV_K - 1)
    span = tm + HALO - SUBLANES
    acc_rows = 128

    o = jnp.concatenate([o_ref[hd] for hd in range(N_HEADS)], axis=-1)
    y_attn = jnp.dot(o, wap_ref[...], preferred_element_type=F32)
    for c in range(d // LANES):
        cs = slice(c * LANES, (c + 1) * LANES)
        halo = jnp.where(first, mh_ref[:, cs], uh_ref[:, cs])
        ext_s[0, 0:HALO, :] = halo.astype(F32)
        ext_s[0, HALO:HALO + tm, :] = u_ref[:, cs].astype(F32)
        for b in range(1, SUBLANES):
            ext_s[b, 0:span, :] = ext_s[0, b:b + span, :]
        for r0 in range(0, tm, acc_rows):
            acc = jnp.zeros((acc_rows, LANES), F32)
            for t in range(CONV_K):
                a, b = divmod(off + t, SUBLANES)
                row = a * SUBLANES + r0
                acc = acc + dww_ref[t:t + 1, cs] * ext_s[b, row:row + acc_rows, :]
            y_s[r0:r0 + acc_rows, cs] = acc
    y = y_s[...] + dwb_ref[...]
    mu = jnp.mean(y, axis=-1, keepdims=True)
    yc = y - mu
    var = jnp.mean(yc * yc, axis=-1, keepdims=True)
    y = yc * lax.rsqrt(var + EPS) * lng_ref[...] + lnb_ref[...]
    y = y * _sigmoid(y)
    y_conv = jnp.dot(y.astype(BF16), wpw_ref[...], preferred_element_type=F32)

    mixed = sga_ref[...].astype(F32) * y_attn + sgc_ref[...].astype(F32) * y_conv
    h = x_ref[...] + jnp.dot(mixed.astype(BF16), wout_ref[...], preferred_element_type=F32)
    h_ref[...] = h
    hn = h * lax.rsqrt(jnp.mean(h * h, axis=-1, keepdims=True) + EPS) * nffn_ref[...]
    _store_token_tiles(hn_ref, hn)

    nt_dims = (((1,), (1,)), ((), ()))
    hn_hi = hn.astype(BF16)
    hn_lo = (hn - hn_hi.astype(F32)).astype(BF16)
    both = lax.dot_general(wr_ref[...], hn_hi, nt_dims, preferred_element_type=F32)
    logits = (both[:N_EXPERTS] + both[N_EXPERTS:]
              + lax.dot_general(wr_ref[0:N_EXPERTS, :], hn_lo, nt_dims, preferred_element_type=F32)
              + br_ref[...])
    eio = lax.broadcasted_iota(jnp.int32, logits.shape, 0)
    sels, tops, idxs = [], [], []
    cur = logits
    for _ in range(TOP_K):
        m = jnp.max(cur, axis=0, keepdims=True)
        idx = jnp.min(jnp.where(cur == m, eio, N_EXPERTS), axis=0, keepdims=True)
        sel = eio == idx
        sels.append(sel)
        tops.append(m)
        idxs.append(idx)
        cur = jnp.where(sel, -jnp.inf, cur)
    exps = [jnp.exp(t - tops[0]) for t in tops]
    denom = exps[0] + exps[1] + exps[2] + exps[3]
    hot = jnp.zeros(logits.shape, F32)
    for sel in sels:
        hot = hot + jnp.where(sel, 1.0, 0.0)
    before = jnp.dot(hot.astype(BF16), tri_ref[...], preferred_element_type=F32) + cnt_s[...]
    ranks = [jnp.sum(jnp.where(sel, before, 0.0), axis=0, keepdims=True).astype(jnp.int32)
             for sel in sels]
    zi = jnp.zeros((ROUTE_ROWS - TOP_K, tm), jnp.int32)
    exp_ref[0] = jnp.concatenate(idxs + [zi], axis=0)
    rank_ref[0] = jnp.concatenate(ranks + [zi], axis=0)
    gate_ref[0] = jnp.concatenate([e / denom for e in exps] + [zi.astype(F32)], axis=0)
    cnt_s[...] = cnt_s[...] + jnp.sum(hot, axis=1, keepdims=True)
    cnt_ref[...] = jnp.broadcast_to(cnt_s[...], cnt_ref.shape)


def _mixer(x2d, o, u, sga, sgc, meta_halo, wap, wpw, wout, dww, dwb, lng, lnb, nffn, wr_t, br,
           seq, tm):
    n, d = x2d.shape
    nt = n // tm
    hb = tm // HALO
    tri = (jnp.arange(tm)[:, None] < jnp.arange(tm)[None, :]).astype(BF16)
    row = pl.BlockSpec((tm, d), lambda i: (i, 0))
    route = pl.BlockSpec((1, ROUTE_ROWS, tm), lambda i: (i, 0, 0))

    def full(shp):
        return pl.BlockSpec(shp, lambda i: tuple(0 for _ in shp))

    kern = functools.partial(_mixer_kernel, tm=tm, tiles_per_seq=seq // tm)
    return pl.pallas_call(
        kern,
        grid=(nt,),
        in_specs=[row, pl.BlockSpec((N_HEADS, tm, V_DIM), lambda i: (0, i, 0)), row,
                  pl.BlockSpec((HALO, d), lambda i: (jnp.maximum(i * hb - 1, 0), 0)),
                  full((HALO, d)), row, row,
                  full((d, d)), full((d, d)), full((d, d)),
                  full((CONV_K, d)), full((1, d)), full((1, d)), full((1, d)), full((1, d)),
                  full((2 * N_EXPERTS, d)), full((N_EXPERTS, 1)), full((tm, tm))],
        out_specs=[row, pl.BlockSpec((tm * SUBLANES, LANES), lambda i: (i, 0)),
                   route, route, route, full((N_EXPERTS, LANES))],
        out_shape=[jax.ShapeDtypeStruct((n, d), F32),
                   jax.ShapeDtypeStruct((n * SUBLANES, LANES), F32),
                   jax.ShapeDtypeStruct((nt, ROUTE_ROWS, tm), jnp.int32),
                   jax.ShapeDtypeStruct((nt, ROUTE_ROWS, tm), jnp.int32),
                   jax.ShapeDtypeStruct((nt, ROUTE_ROWS, tm), F32),
                   jax.ShapeDtypeStruct((N_EXPERTS, LANES), F32)],
        scratch_shapes=[pltpu.VMEM((SUBLANES, HALO + tm, LANES), F32), pltpu.VMEM((tm, d), F32),
                        pltpu.VMEM((N_EXPERTS, 1), F32)],
        compiler_params=pltpu.CompilerParams(
            dimension_semantics=("arbitrary",), vmem_limit_bytes=V7X_VMEM_LIMIT),
        name="mixer_router",
    )(x2d, o, u, u, meta_halo, sga, sgc, wap, wpw, wout, dww, dwb, lng, lnb, nffn, wr_t, br, tri)


def _prefetch_routes(dest_hbm, dsm_s, sem_idx):
    i = pl.program_id(0)
    slot = i % 2

    def idx_copy(tile, s):
        return pltpu.make_async_copy(dest_hbm.at[tile], dsm_s.at[s], sem_idx.at[s])

    @pl.when(i == 0)
    def _():
        idx_copy(0, 0).start()

    idx_copy(i, slot).wait()

    @pl.when(i + 1 < pl.num_programs(0))
    def _():
        idx_copy(i + 1, 1 - slot).start()

    return slot


def _dispatch_kernel(dest_hbm, hn_ref, xs_hbm, dsm_s, sem_idx, sem_rows, *, tm):
    slot = _prefetch_routes(dest_hbm, dsm_s, sem_idx)

    def issue(r, c):
        src = hn_ref.at[pl.ds(pl.multiple_of(r * SUBLANES, SUBLANES), SUBLANES), :]
        for kk in range(TOP_K):
            dst = pl.multiple_of(dsm_s[slot, kk, r] * SUBLANES, SUBLANES)
            pltpu.make_async_copy(src, xs_hbm.at[pl.ds(dst, SUBLANES), :],
                                  sem_rows).start(priority=kk % 2)
        return c
    lax.fori_loop(0, tm, issue, 0, unroll=8)

    for kk in range(TOP_K):
        pltpu.make_async_copy(hn_ref, xs_hbm.at[pl.ds(0, tm * SUBLANES), :], sem_rows).wait()


def _dispatch(dest, hn, n_slots, tm):
    n = hn.shape[0] // SUBLANES
    return pl.pallas_call(
        functools.partial(_dispatch_kernel, tm=tm),
        grid=(n // tm,),
        in_specs=[pl.BlockSpec(memory_space=pl.ANY),
                  pl.BlockSpec((tm * SUBLANES, LANES), lambda i: (i, 0))],
        out_specs=pl.BlockSpec(memory_space=pl.ANY),
        out_shape=jax.ShapeDtypeStruct((n_slots * SUBLANES, LANES), F32),
        scratch_shapes=[pltpu.SMEM((2, ROUTE_ROWS, tm), jnp.int32),
                        pltpu.SemaphoreType.DMA((2,)), pltpu.SemaphoreType.DMA(())],
        compiler_params=pltpu.CompilerParams(
            dimension_semantics=("arbitrary",), vmem_limit_bytes=V7X_VMEM_LIMIT),
        name="dispatch",
    )(dest, hn)


def _expert_kernel(bexp_ref, bvalid_ref, xs_ref, wgu_ref, bgu_ref, wdn_ref, bdn_ref, ys_ref,
                   wgu_bf, wdn_bf):
    i = pl.program_id(0)
    valid = bvalid_ref[i]
    f = wdn_ref.shape[1]

    @pl.when((i == 0) | (bexp_ref[i] != bexp_ref[jnp.maximum(i - 1, 0)]))
    def _():
        wgu_bf[...] = wgu_ref[0].astype(BF16)
        wdn_bf[...] = wdn_ref[0].astype(BF16)

    @pl.when(valid > 0)
    def _():
        x = _load_token_tiles(xs_ref, EXPERT_BLOCK)
        rows = lax.broadcasted_iota(jnp.int32, x.shape, 0)
        xb = jnp.where(rows < valid, x, 0.0).astype(BF16)
        fc = f // EXPERT_F_CHUNKS
        y = bdn_ref[0]
        for c in range(EXPERT_F_CHUNKS):
            g_cols = slice(c * fc, (c + 1) * fc)
            l_cols = slice(f + c * fc, f + (c + 1) * fc)
            g = jnp.dot(xb, wgu_bf[:, g_cols], preferred_element_type=F32) + bgu_ref[0, :, g_cols]
            lin = jnp.dot(xb, wgu_bf[:, l_cols], preferred_element_type=F32) + bgu_ref[0, :, l_cols]
            glu = jnp.minimum(g, SWIGLU_LIMIT)
            lin = jnp.clip(lin, -SWIGLU_LIMIT, SWIGLU_LIMIT)
            act = glu * _sigmoid(SWIGLU_ALPHA * glu) * (lin + 1.0)
            y = y + jnp.dot(act.astype(BF16), wdn_bf[g_cols, :], preferred_element_type=F32)
        _store_token_tiles(ys_ref, y)

    @pl.when(valid <= 0)
    def _():
        ys_ref[...] = jnp.zeros_like(ys_ref)


def _experts(block_exp, block_valid, xs, wgu, bgu, wdn, bdn):
    d = wgu.shape[1]
    n_slots = xs.shape[0] // SUBLANES
    nb = n_slots // EXPERT_BLOCK
    f2 = wgu.shape[2]
    f = wdn.shape[1]
    tile_rows = EXPERT_BLOCK * SUBLANES
    grid_spec = pltpu.PrefetchScalarGridSpec(
        num_scalar_prefetch=2,
        grid=(nb,),
        in_specs=[pl.BlockSpec((tile_rows, LANES), lambda i, be, bv: (i, 0)),
                  pl.BlockSpec((1, d, f2), lambda i, be, bv: (be[i], 0, 0)),
                  pl.BlockSpec((1, 1, f2), lambda i, be, bv: (be[i], 0, 0)),
                  pl.BlockSpec((1, f, d), lambda i, be, bv: (be[i], 0, 0)),
                  pl.BlockSpec((1, 1, d), lambda i, be, bv: (be[i], 0, 0))],
        out_specs=pl.BlockSpec((tile_rows, LANES), lambda i, be, bv: (i, 0)),
        scratch_shapes=[pltpu.VMEM((d, f2), BF16), pltpu.VMEM((f, d), BF16)],
    )
    return pl.pallas_call(
        _expert_kernel,
        grid_spec=grid_spec,
        out_shape=jax.ShapeDtypeStruct(xs.shape, F32),
        compiler_params=pltpu.CompilerParams(
            dimension_semantics=("arbitrary",), vmem_limit_bytes=V7X_VMEM_LIMIT),
        name="experts",
    )(block_exp, block_valid, xs, wgu, bgu, wdn, bdn)


def _combine_kernel(dest_hbm, h_ref, gate_ref, g_ref, ys_hbm, out_ref, dsm_s, ybuf, sem_idx, sem_rows,
                    *, tm):
    i = pl.program_id(0)
    last = pl.num_programs(0) - 1
    slot = i % 2

    def idx_copy(tile, s):
        return pltpu.make_async_copy(dest_hbm.at[jnp.minimum(tile, last)], dsm_s.at[s],
                                     sem_idx.at[s])

    def row_copy(s, kk, r):
        src = pl.multiple_of(dsm_s[s, kk, r] * SUBLANES, SUBLANES)
        return pltpu.make_async_copy(ys_hbm.at[pl.ds(src, SUBLANES), :],
                                     ybuf.at[s, kk, pl.ds(r * SUBLANES, SUBLANES), :],
                                     sem_rows.at[s])

    def wait_rows(s):
        for kk in range(TOP_K):
            pltpu.make_async_copy(ys_hbm.at[pl.ds(0, tm * SUBLANES), :], ybuf.at[s, kk],
                                  sem_rows.at[s]).wait()

    @pl.when(i == 0)
    def _():
        idx_copy(0, 0).start()
        idx_copy(0, 0).wait()

        def issue(r, c):
            for kk in range(TOP_K):
                row_copy(0, kk, r).start(priority=kk % 2)
            return c
        lax.fori_loop(0, tm, issue, 0, unroll=8)
        idx_copy(1, 1).start()

    idx_copy(i + 1, 1 - slot).wait()
    wait_rows(slot)
    for r in range(tm):
        for kk in range(TOP_K):
            row_copy(1 - slot, kk, r).start(priority=kk % 2)
    idx_copy(i + 2, slot).start()

    gates = jnp.transpose(gate_ref[0])
    acc = h_ref[...]
    for kk in range(TOP_K):
        acc = acc + gates[:, kk:kk + 1] * _load_token_tiles(ybuf.at[slot, kk], tm)
    out_ref[...] = acc * lax.rsqrt(jnp.mean(acc * acc, axis=-1, keepdims=True) + EPS) * g_ref[...]

    @pl.when(i == last)
    def _():
        wait_rows(1 - slot)
        idx_copy(i + 2, slot).wait()


def _combine(dest, h, gates, g_final, ys, tm):
    n, d = h.shape
    return pl.pallas_call(
        functools.partial(_combine_kernel, tm=tm),
        grid=(n // tm,),
        in_specs=[pl.BlockSpec(memory_space=pl.ANY),
                  pl.BlockSpec((tm, d), lambda i: (i, 0)),
                  pl.BlockSpec((1, ROUTE_ROWS, tm), lambda i: (i, 0, 0)),
                  pl.BlockSpec((1, d), lambda i: (0, 0)),
                  pl.BlockSpec(memory_space=pl.ANY)],
        out_specs=pl.BlockSpec((tm, d), lambda i: (i, 0)),
        out_shape=jax.ShapeDtypeStruct((n, d), F32),
        scratch_shapes=[pltpu.SMEM((2, ROUTE_ROWS, tm), jnp.int32),
                        pltpu.VMEM((2, TOP_K, tm * SUBLANES, LANES), F32),
                        pltpu.SemaphoreType.DMA((2,)), pltpu.SemaphoreType.DMA((2,))],
        compiler_params=pltpu.CompilerParams(
            dimension_semantics=("arbitrary",), vmem_limit_bytes=V7X_VMEM_LIMIT),
        name="combine",
    )(dest, h, gates, g_final, ys)


def kernel(x, meta_tokens, norm_mix, w_in, b_in, lam_params, subln_gain, w_attn_proj, conv_dw_w,
           conv_dw_b, conv_ln_g, conv_ln_b, w_conv_proj, w_out, norm_ffn, w_router, b_router,
           w_gate_up, b_gate_up, w_down, b_down, norm_final):
    bsz, seq, d = x.shape
    assert norm_mix.shape[0] == 1, "single-layer block"
    n = bsz * seq
    tm_proj = min(INPROJ_ROWS, seq)
    tq = min(ATTN_Q_TILE, seq)
    tm = min(MIXER_ROWS, seq)
    lam_init = 0.8 - 0.6 * 1.0

    x2d = x.reshape(n, d)
    w_rest = jnp.concatenate([w_in[0][:, d:2 * d], w_in[0][:, 3 * d:]], axis=1).astype(BF16)
    b_rest = jnp.concatenate([b_in[0][d:2 * d], b_in[0][3 * d:]])[None]
    w_t = jnp.concatenate([w_in[0][:, :d], w_in[0][:, 2 * d:3 * d]], axis=1).T.astype(BF16)
    b_t = jnp.concatenate([b_in[0][:d], b_in[0][2 * d:3 * d]])[:, None]
    g_mix = norm_mix[0][None]
    qt, k, vt, u, sga, sgc = _inproj(x2d, g_mix, w_rest, b_rest, w_t, b_t, tm_proj)
    _, k_m, vt_m, u_m, _, _ = _inproj(meta_tokens.astype(x.dtype), g_mix, w_rest, b_rest, w_t, b_t,
                                      N_META)

    o = _attention(lam_params[0], qt, k, vt, k_m, vt_m, subln_gain[0][:, None], bsz, seq, tq,
                   lam_init)

    meta_halo = jnp.concatenate([jnp.zeros((HALO - N_META, d), BF16), u_m], axis=0)
    wr_t = w_router[0].T
    wr_hi = wr_t.astype(BF16)
    wr_split = jnp.concatenate([wr_hi, (wr_t - wr_hi.astype(F32)).astype(BF16)], axis=0)
    h, hn, eidx, rank, gates, counts = _mixer(
        x2d, o, u, sga, sgc, meta_halo,
        w_attn_proj[0].astype(BF16), w_conv_proj[0].astype(BF16), w_out[0].astype(BF16),
        conv_dw_w[0], conv_dw_b[0][None], conv_ln_g[0][None], conv_ln_b[0][None],
        norm_ffn[0][None], wr_split, b_router[0][:, None], seq, tm)

    cnt = counts[:, 0].astype(jnp.int32)
    padded = (cnt + EXPERT_BLOCK - 1) // EXPERT_BLOCK * EXPERT_BLOCK
    pad_end = jnp.cumsum(padded)
    pad_start = pad_end - padded
    n_blocks = -(-(n * TOP_K + N_EXPERTS * (EXPERT_BLOCK - 1)) // EXPERT_BLOCK)
    n_slots = n_blocks * EXPERT_BLOCK
    experts = jnp.arange(N_EXPERTS, dtype=jnp.int32)
    dest = rank + jnp.sum(jnp.where(eidx[..., None] == experts, pad_start, 0), axis=-1)
    block_start = jnp.arange(n_blocks, dtype=jnp.int32) * EXPERT_BLOCK
    block_exp = jnp.minimum(jnp.sum(block_start[:, None] >= pad_end[None, :], axis=1),
                            N_EXPERTS - 1).astype(jnp.int32)
    block_valid = jnp.clip(pad_start[block_exp] + cnt[block_exp] - block_start, 0,
                           EXPERT_BLOCK).astype(jnp.int32)

    xs = _dispatch(dest, hn, n_slots, tm)
    ys = _experts(block_exp, block_valid, xs, w_gate_up[0], b_gate_up[0][:, None, :],
                  w_down[0], b_down[0][:, None, :])
    out = _combine(dest, h, gates, norm_final[None], ys, tm)
    return out.reshape(bsz, seq, d)
```

```python
import functools

import jax
import jax.numpy as jnp
from jax import lax
from jax.experimental import pallas as pl
from jax.experimental.pallas import tpu as pltpu

CHUNK = 64
N_META = 16
N_HEADS = 8
HEAD_DIM = 64
V_DIM = 2 * HEAD_DIM
CONV_K = 31
N_EXPERTS = 32
TOP_K = 4
SWIGLU_ALPHA = 1.702
SWIGLU_LIMIT = 7.0
EXPERT_BLOCK = 1024
INPROJ_ROWS = 512
ATTN_Q_TILE = 1024
MIXER_ROWS = 512
EPS = 1e-5
NEG = -1e30
LOG2E = 1.4426950408889634

V7X_VMEM_LIMIT = 56 * 1024 * 1024
SUBLANES = 8
LANES = 128
META_PAD = 128
HALO = 32
ROUTE_ROWS = 8

BF16 = jnp.bfloat16
F32 = jnp.float32


def _sigmoid(x):
    return 1.0 / (1.0 + jnp.exp(-x))


def _load_token_tiles(ref, rows):
    return jnp.concatenate([ref[pl.ds(s, rows, stride=SUBLANES), :] for s in range(SUBLANES)],
                           axis=-1)


def _store_token_tiles(ref, x):
    rows = x.shape[0]
    for s in range(SUBLANES):
        ref[pl.ds(s, rows, stride=SUBLANES), :] = x[:, s * LANES:(s + 1) * LANES]


def _inproj_kernel(x_ref, g_ref, w_ref, b_ref, wt_ref, bt_ref,
                   qt_ref, k_ref, vt_ref, u_ref, sga_ref, sgc_ref):
    d = x_ref.shape[1]
    x = x_ref[...]
    xn = x * lax.rsqrt(jnp.mean(x * x, axis=-1, keepdims=True) + EPS) * g_ref[...]
    xb = xn.astype(BF16)

    def proj(c):
        return (jnp.dot(xb, w_ref[:, c * d:(c + 1) * d], preferred_element_type=F32)
                + b_ref[:, c * d:(c + 1) * d])

    kf = proj(0).astype(BF16)
    for hd in range(N_HEADS):
        k_ref[hd] = kf[:, hd * V_DIM:(hd + 1) * V_DIM]
    t = lax.dot_general(wt_ref[...], xb, (((1,), (1,)), ((), ())), preferred_element_type=F32)
    t = t + bt_ref[...]
    qt_ref[...] = (t[:d] * (HEAD_DIM ** -0.5 * LOG2E)).astype(BF16)
    vt_ref[...] = t[d:].astype(BF16)
    u_ref[...] = (proj(1) * _sigmoid(proj(2))).astype(BF16)
    sga_ref[...] = _sigmoid(proj(3)).astype(BF16)
    sgc_ref[...] = _sigmoid(proj(4)).astype(BF16)


def _inproj(x2d, g, w_bf, b, wt_bf, bt, tm):
    n, d = x2d.shape
    in_w = w_bf.shape[1]
    out = jax.ShapeDtypeStruct((n, d), BF16)
    row = pl.BlockSpec((tm, d), lambda i: (i, 0))
    col = pl.BlockSpec((d, tm), lambda i: (0, i))
    heads = pl.BlockSpec((N_HEADS, tm, V_DIM), lambda i: (0, i, 0))
    out_heads = jax.ShapeDtypeStruct((N_HEADS, n, V_DIM), BF16)
    out_t = jax.ShapeDtypeStruct((d, n), BF16)
    return pl.pallas_call(
        _inproj_kernel,
        grid=(n // tm,),
        in_specs=[row,
                  pl.BlockSpec((1, d), lambda i: (0, 0)),
                  pl.BlockSpec((d, in_w), lambda i: (0, 0)),
                  pl.BlockSpec((1, in_w), lambda i: (0, 0)),
                  pl.BlockSpec((2 * d, d), lambda i: (0, 0)),
                  pl.BlockSpec((2 * d, 1), lambda i: (0, 0))],
        out_specs=[col, heads, col, row, row, row],
        out_shape=[out_t, out_heads, out_t, out, out, out],
        compiler_params=pltpu.CompilerParams(
            dimension_semantics=("arbitrary",), vmem_limit_bytes=V7X_VMEM_LIMIT),
        name="inproj",
    )(x2d, g, w_bf, b, wt_bf, bt)


def _attn_kernel(lam_ref, qt_ref, k_ref, vt_ref, km_ref, vmt_ref, bias_ref, bmeta_ref,
                 gain_ref, o_ref, *scratch, tq, lam_init):
    def q_tile(qi, carry):
        cols = pl.ds(pl.multiple_of(qi * tq, tq), tq)
        _attn_tile(qi, lam_ref, qt_ref.at[:, cols], k_ref, vt_ref, km_ref, vmt_ref, bias_ref,
                   bmeta_ref, gain_ref, o_ref.at[0, cols, :], *scratch, tq=tq, lam_init=lam_init)
        return carry

    lax.fori_loop(0, qt_ref.shape[1] // tq, q_tile, 0)


def _attn_tile(qi, lam_ref, qt_ref, k_ref, vt_ref, km_ref, vmt_ref, bias_ref, bmeta_ref,
               gain_ref, o_ref, m_s, l_s, acc_s, s0, s1, mx0, mx1, *, tq, lam_init):
    h = pl.program_id(1)
    slope = jnp.exp2(-(h + 1).astype(F32))
    lp = lam_ref[...]
    lam = (jnp.exp(jnp.sum(lp[0:1] * lp[1:2], axis=-1, keepdims=True))
           - jnp.exp(jnp.sum(lp[2:3] * lp[3:4], axis=-1, keepdims=True)) + lam_init)

    qt = qt_ref[...]
    feat = lax.broadcasted_iota(jnp.int32, qt.shape, 0)
    zero = jnp.zeros_like(qt)
    qs = (jnp.where(feat < HEAD_DIM, qt, zero), jnp.where(feat >= HEAD_DIM, qt, zero))

    def scores_t(kk, qm):
        return jnp.dot(kk, qm, preferred_element_type=F32)

    tk = tq // 2
    n_off = 2 * qi

    def tile_shift(t):
        off = (qi * tq - t * tk).astype(F32) * LOG2E
        return jnp.where(t < n_off, slope * off, 0.0)

    def stage_a(t, s_buf, mx_buf, cols=slice(None)):
        kk = k_ref[0, pl.ds(pl.multiple_of(t * tk, tk), tk), :]
        bias = bias_ref[0, jnp.where(t < n_off, 0, t - n_off + 1), :, cols]
        for idx in range(2):
            s = scores_t(kk, qs[idx][:, cols]) + bias
            s_buf[idx, :, cols] = s
            mx_buf[idx, :, cols] = jnp.max(s, axis=0, keepdims=True)

    def stage_b(t, s_buf, mx_buf, cols=slice(None), meta=None):
        vt = vt_ref[:, pl.ds(pl.multiple_of(t * tk, tk), tk)]
        vt1 = jnp.concatenate([vt, jnp.ones((2 * SUBLANES, tk), BF16)], axis=0)
        shift = tile_shift(t)
        for idx in range(2):
            m_old = m_s[idx, :, cols]
            m_new = jnp.maximum(m_old, mx_buf[idx, :, cols] - shift)
            if meta is not None:
                s_meta, vmt1, shift_m = meta
                m_new = jnp.maximum(m_new, jnp.max(s_meta[idx], axis=0, keepdims=True) - shift_m)
            p = jnp.exp2(s_buf[idx, :, cols] - (m_new + shift))
            pv1 = jnp.dot(vt1, p.astype(BF16), preferred_element_type=F32)
            if meta is not None:
                pm = jnp.exp2(s_meta[idx] - (m_new + shift_m))
                pv1 = pv1 + jnp.dot(vmt1, pm.astype(BF16), preferred_element_type=F32)
            pv = pv1[:V_DIM]
            ps = pv1[V_DIM:V_DIM + 1]
            alpha = jnp.exp2(m_old - m_new)
            l_s[idx, :, cols] = alpha * l_s[idx, :, cols] + ps
            acc_s[idx, :, cols] = alpha * acc_s[idx, :, cols] + pv
            m_s[idx, :, cols] = m_new

    m_s[...] = jnp.full(m_s.shape, NEG, F32)
    l_s[...] = jnp.zeros_like(l_s)
    acc_s[...] = jnp.zeros_like(acc_s)
    stage_a(0, s0, mx0)

    def pair(i):
        stage_a(2 * i + 1, s1, mx1)
        stage_b(2 * i, s0, mx0)
        stage_a(2 * i + 2, s0, mx0)
        stage_b(2 * i + 1, s1, mx1)

    def body(ii, carry):
        pair(2 * ii)
        pair(2 * ii + 1)
        return carry

    lax.fori_loop(0, qi // 2, body, 0)

    @pl.when(qi % 2 == 1)
    def _():
        pair(qi - 1)

    late = slice(tq // 2, tq)
    stage_a(n_off + 1, s1, mx1, late)
    s_meta = [scores_t(km_ref[0], qs[idx]) + bmeta_ref[0] for idx in range(2)]
    vmt1 = jnp.concatenate([vmt_ref[...], jnp.ones((2 * SUBLANES, META_PAD), BF16)], axis=0)
    shift0 = slope * ((qi * tq).astype(F32) * LOG2E)
    stage_b(n_off, s0, mx0, meta=(s_meta, vmt1, shift0))
    stage_b(n_off + 1, s1, mx1, late)

    o = acc_s[0] / l_s[0] - lam * (acc_s[1] / l_s[1])
    o = o * lax.rsqrt(jnp.mean(o * o, axis=0, keepdims=True) + EPS) * gain_ref[...]
    o_ref[...] = jnp.transpose(o * (1.0 - lam_init)).astype(BF16)


def _attention(lam_params, qt, k, vt, k_meta, vt_meta, gain, bsz, seq, tq, lam_init):
    n = qt.shape[1]
    slopes = jnp.asarray([2.0 ** (-8.0 * (h + 1) / N_HEADS) for h in range(N_HEADS)], F32)
    tk = tq // 2
    r = jnp.arange(tq, dtype=jnp.int32)
    diff = r[None, :] - r[:, None]
    slopes = slopes * LOG2E
    boff = -slopes[:, None, None] * diff.astype(F32)[None, :tk]
    vis = (r[:, None] // CHUNK) <= (r[None, :] // CHUNK)
    bdiag = jnp.where(vis[None], -slopes[:, None, None] * jnp.abs(diff).astype(F32)[None], NEG)
    bias_all = jnp.stack([boff, bdiag[:, :tk], bdiag[:, tk:]], axis=1)
    mrow = jnp.arange(META_PAD, dtype=jnp.int32)
    dmeta = (N_META + r[None, :] - mrow[:, None]).astype(F32)
    bmeta = jnp.where((mrow < N_META)[None, :, None], -slopes[:, None, None] * dmeta[None], NEG)

    km = jnp.pad(k_meta, ((0, 0), (0, META_PAD - N_META), (0, 0)))
    vmt = jnp.pad(vt_meta, ((0, 0), (0, META_PAD - N_META)))

    kern = functools.partial(_attn_kernel, tq=tq, lam_init=lam_init)
    return pl.pallas_call(
        kern,
        grid=(bsz, N_HEADS),
        in_specs=[
            pl.BlockSpec((4, HEAD_DIM), lambda b, h: (0, 0)),
            pl.BlockSpec((V_DIM, seq), lambda b, h: (h, b)),
            pl.BlockSpec((1, seq, V_DIM), lambda b, h: (h, b, 0)),
            pl.BlockSpec((V_DIM, seq), lambda b, h: (h, b)),
            pl.BlockSpec((1, META_PAD, V_DIM), lambda b, h: (h, 0, 0)),
            pl.BlockSpec((V_DIM, META_PAD), lambda b, h: (h, 0)),
            pl.BlockSpec((1, 3, tk, tq), lambda b, h: (h, 0, 0, 0)),
            pl.BlockSpec((1, META_PAD, tq), lambda b, h: (h, 0, 0)),
            pl.BlockSpec((V_DIM, 1), lambda b, h: (0, 0)),
        ],
        out_specs=pl.BlockSpec((1, seq, V_DIM), lambda b, h: (h, b, 0)),
        out_shape=jax.ShapeDtypeStruct((N_HEADS, n, V_DIM), BF16),
        scratch_shapes=[pltpu.VMEM((2, 1, tq), F32), pltpu.VMEM((2, 1, tq), F32),
                        pltpu.VMEM((2, V_DIM, tq), F32),
                        pltpu.VMEM((2, tk, tq), F32), pltpu.VMEM((2, tk, tq), F32),
                        pltpu.VMEM((2, 1, tq), F32), pltpu.VMEM((2, 1, tq), F32)],
        compiler_params=pltpu.CompilerParams(
            dimension_semantics=("arbitrary", "arbitrary"),
            vmem_limit_bytes=V7X_VMEM_LIMIT),
        name="diff_attention",
    )(lam_params, qt, k, vt, km, vmt, bias_all, bmeta, gain)


def _mixer_kernel(x_ref, o_ref, u_ref, uh_ref, mh_ref, sga_ref, sgc_ref,
                  wap_ref, wpw_ref, wout_ref, dww_ref, dwb_ref, lng_ref, lnb_ref,
                  nffn_ref, wr_ref, br_ref, tri_ref,
                  h_ref, hn_ref, exp_ref, rank_ref, gate_ref, cnt_ref,
                  ext_s, y_s, cnt_s, *, tm, tiles_per_seq):
    i = pl.program_id(0)
    d = x_ref.shape[1]

    @pl.when(i == 0)
    def _():
        cnt_s[...] = jnp.zeros_like(cnt_s)

    first = (i % tiles_per_seq) == 0
    off = HALO - (CONV_K - 1)
    span = tm + HALO - SUBLANES
    acc_rows = 128

    o = jnp.concatenate([o_ref[hd] for hd in range(N_HEADS)], axis=-1)
    y_attn = jnp.dot(o, wap_ref[...], preferred_element_type=F32)
    for c in range(d // LANES):
        cs = slice(c * LANES, (c + 1) * LANES)
        halo = jnp.where(first, mh_ref[:, cs], uh_ref[:, cs])
        ext_s[0, 0:HALO, :] = halo.astype(F32)
        ext_s[0, HALO:HALO + tm, :] = u_ref[:, cs].astype(F32)
        for b in range(1, SUBLANES):
            ext_s[b, 0:span, :] = ext_s[0, b:b + span, :]
        for r0 in range(0, tm, acc_rows):
            acc = jnp.zeros((acc_rows, LANES), F32)
            for t in range(CONV_K):
                a, b = divmod(off + t, SUBLANES)
                row = a * SUBLANES + r0
                acc = acc + dww_ref[t:t + 1, cs] * ext_s[b, row:row + acc_rows, :]
            y_s[r0:r0 + acc_rows, cs] = acc
    y = y_s[...] + dwb_ref[...]
    mu = jnp.mean(y, axis=-1, keepdims=True)
    yc = y - mu
    var = jnp.mean(yc * yc, axis=-1, keepdims=True)
    y = yc * lax.rsqrt(var + EPS) * lng_ref[...] + lnb_ref[...]
    y = y * _sigmoid(y)
    y_conv = jnp.dot(y.astype(BF16), wpw_ref[...], preferred_element_type=F32)

    mixed = sga_ref[...].astype(F32) * y_attn + sgc_ref[...].astype(F32) * y_conv
    h = x_ref[...] + jnp.dot(mixed.astype(BF16), wout_ref[...], preferred_element_type=F32)
    h_ref[...] = h
    hn = h * lax.rsqrt(jnp.mean(h * h, axis=-1, keepdims=True) + EPS) * nffn_ref[...]
    _store_token_tiles(hn_ref, hn)

    nt_dims = (((1,), (1,)), ((), ()))
    hn_hi = hn.astype(BF16)
    hn_lo = (hn - hn_hi.astype(F32)).astype(BF16)
    both = lax.dot_general(wr_ref[...], hn_hi, nt_dims, preferred_element_type=F32)
    logits = (both[:N_EXPERTS] + both[N_EXPERTS:]
              + lax.dot_general(wr_ref[0:N_EXPERTS, :], hn_lo, nt_dims, preferred_element_type=F32)
              + br_ref[...])
    eio = lax.broadcasted_iota(jnp.int32, logits.shape, 0)
    sels, tops, idxs = [], [], []
    cur = logits
    for _ in range(TOP_K):
        m = jnp.max(cur, axis=0, keepdims=True)
        idx = jnp.min(jnp.where(cur == m, eio, N_EXPERTS), axis=0, keepdims=True)
        sel = eio == idx
        sels.append(sel)
        tops.append(m)
        idxs.append(idx)
        cur = jnp.where(sel, -jnp.inf, cur)
    exps = [jnp.exp(t - tops[0]) for t in tops]
    denom = exps[0] + exps[1] + exps[2] + exps[3]
    hot = jnp.zeros(logits.shape, F32)
    for sel in sels:
        hot = hot + jnp.where(sel, 1.0, 0.0)
    before = jnp.dot(hot.astype(BF16), tri_ref[...], preferred_element_type=F32) + cnt_s[...]
    ranks = [jnp.sum(jnp.where(sel, before, 0.0), axis=0, keepdims=True).astype(jnp.int32)
             for sel in sels]
    zi = jnp.zeros((ROUTE_ROWS - TOP_K, tm), jnp.int32)
    exp_ref[0] = jnp.concatenate(idxs + [zi], axis=0)
    rank_ref[0] = jnp.concatenate(ranks + [zi], axis=0)
    gate_ref[0] = jnp.concatenate([e / denom for e in exps] + [zi.astype(F32)], axis=0)
    cnt_s[...] = cnt_s[...] + jnp.sum(hot, axis=1, keepdims=True)
    cnt_ref[...] = jnp.broadcast_to(cnt_s[...], cnt_ref.shape)


def _mixer(x2d, o, u, sga, sgc, meta_halo, wap, wpw, wout, dww, dwb, lng, lnb, nffn, wr_t, br,
           seq, tm):
    n, d = x2d.shape
    nt = n // tm
    hb = tm // HALO
    tri = (jnp.arange(tm)[:, None] < jnp.arange(tm)[None, :]).astype(BF16)
    row = pl.BlockSpec((tm, d), lambda i: (i, 0))
    route = pl.BlockSpec((1, ROUTE_ROWS, tm), lambda i: (i, 0, 0))

    def full(shp):
        return pl.BlockSpec(shp, lambda i: tuple(0 for _ in shp))

    kern = functools.partial(_mixer_kernel, tm=tm, tiles_per_seq=seq // tm)
    return pl.pallas_call(
        kern,
        grid=(nt,),
        in_specs=[row, pl.BlockSpec((N_HEADS, tm, V_DIM), lambda i: (0, i, 0)), row,
                  pl.BlockSpec((HALO, d), lambda i: (jnp.maximum(i * hb - 1, 0), 0)),
                  full((HALO, d)), row, row,
                  full((d, d)), full((d, d)), full((d, d)),
                  full((CONV_K, d)), full((1, d)), full((1, d)), full((1, d)), full((1, d)),
                  full((2 * N_EXPERTS, d)), full((N_EXPERTS, 1)), full((tm, tm))],
        out_specs=[row, pl.BlockSpec((tm * SUBLANES, LANES), lambda i: (i, 0)),
                   route, route, route, full((N_EXPERTS, LANES))],
        out_shape=[jax.ShapeDtypeStruct((n, d), F32),
                   jax.ShapeDtypeStruct((n * SUBLANES, LANES), F32),
                   jax.ShapeDtypeStruct((nt, ROUTE_ROWS, tm), jnp.int32),
                   jax.ShapeDtypeStruct((nt, ROUTE_ROWS, tm), jnp.int32),
                   jax.ShapeDtypeStruct((nt, ROUTE_ROWS, tm), F32),
                   jax.ShapeDtypeStruct((N_EXPERTS, LANES), F32)],
        scratch_shapes=[pltpu.VMEM((SUBLANES, HALO + tm, LANES), F32), pltpu.VMEM((tm, d), F32),
                        pltpu.VMEM((N_EXPERTS, 1), F32)],
        compiler_params=pltpu.CompilerParams(
            dimension_semantics=("arbitrary",), vmem_limit_bytes=V7X_VMEM_LIMIT),
        name="mixer_router",
    )(x2d, o, u, u, meta_halo, sga, sgc, wap, wpw, wout, dww, dwb, lng, lnb, nffn, wr_t, br, tri)


def _prefetch_routes(dest_hbm, dsm_s, sem_idx):
    i = pl.program_id(0)
    slot = i % 2

    def idx_copy(tile, s):
        return pltpu.make_async_copy(dest_hbm.at[tile], dsm_s.at[s], sem_idx.at[s])

    @pl.when(i == 0)
    def _():
        idx_copy(0, 0).start()

    idx_copy(i, slot).wait()

    @pl.when(i + 1 < pl.num_programs(0))
    def _():
        idx_copy(i + 1, 1 - slot).start()

    return slot


def _dispatch_kernel(dest_hbm, hn_ref, xs_hbm, dsm_s, sem_idx, sem_rows, *, tm):
    slot = _prefetch_routes(dest_hbm, dsm_s, sem_idx)

    def issue(r, c):
        src = hn_ref.at[pl.ds(pl.multiple_of(r * SUBLANES, SUBLANES), SUBLANES), :]
        for kk in range(TOP_K):
            dst = pl.multiple_of(dsm_s[slot, kk, r], SUBLANES)
            pltpu.make_async_copy(src, xs_hbm.at[pl.ds(dst, SUBLANES), :],
                                  sem_rows).start(priority=kk % 2)
        return c
    lax.fori_loop(0, tm, issue, 0, unroll=8)

    for kk in range(TOP_K):
        pltpu.make_async_copy(hn_ref, xs_hbm.at[pl.ds(0, tm * SUBLANES), :], sem_rows).wait()


def _dispatch(dest, hn, n_slots, tm):
    n = hn.shape[0] // SUBLANES
    return pl.pallas_call(
        functools.partial(_dispatch_kernel, tm=tm),
        grid=(n // tm,),
        in_specs=[pl.BlockSpec(memory_space=pl.ANY),
                  pl.BlockSpec((tm * SUBLANES, LANES), lambda i: (i, 0))],
        out_specs=pl.BlockSpec(memory_space=pl.ANY),
        out_shape=jax.ShapeDtypeStruct((n_slots * SUBLANES, LANES), F32),
        scratch_shapes=[pltpu.SMEM((2, ROUTE_ROWS, tm), jnp.int32),
                        pltpu.SemaphoreType.DMA((2,)), pltpu.SemaphoreType.DMA(())],
        compiler_params=pltpu.CompilerParams(
            dimension_semantics=("arbitrary",), vmem_limit_bytes=V7X_VMEM_LIMIT),
        name="dispatch",
    )(dest, hn)


def _expert_kernel(bexp_ref, bvalid_ref, xs_ref, wgu_ref, bgu_ref, wdn_ref, bdn_ref, ys_ref,
                   wgu_bf, wdn_bf):
    i = pl.program_id(0)
    valid = bvalid_ref[i]
    f = wdn_ref.shape[1]

    @pl.when((i == 0) | (bexp_ref[i] != bexp_ref[jnp.maximum(i - 1, 0)]))
    def _():
        wgu_bf[...] = wgu_ref[0].astype(BF16)
        wdn_bf[...] = wdn_ref[0].astype(BF16)

    @pl.when(valid > 0)
    def _():
        x = _load_token_tiles(xs_ref, EXPERT_BLOCK)
        rows = lax.broadcasted_iota(jnp.int32, x.shape, 0)
        xb = jnp.where(rows < valid, x, 0.0).astype(BF16)
        gu = jnp.dot(xb, wgu_bf[...], preferred_element_type=F32) + bgu_ref[0]
        glu = jnp.minimum(gu[:, :f], SWIGLU_LIMIT)
        lin = jnp.clip(gu[:, f:], -SWIGLU_LIMIT, SWIGLU_LIMIT)
        act = glu * _sigmoid(SWIGLU_ALPHA * glu) * (lin + 1.0)
        y = jnp.dot(act.astype(BF16), wdn_bf[...], preferred_element_type=F32) + bdn_ref[0]
        _store_token_tiles(ys_ref, y)

    @pl.when(valid <= 0)
    def _():
        ys_ref[...] = jnp.zeros_like(ys_ref)


def _experts(block_exp, block_valid, xs, wgu, bgu, wdn, bdn):
    d = wgu.shape[1]
    n_slots = xs.shape[0] // SUBLANES
    nb = n_slots // EXPERT_BLOCK
    f2 = wgu.shape[2]
    f = wdn.shape[1]
    tile_rows = EXPERT_BLOCK * SUBLANES
    grid_spec = pltpu.PrefetchScalarGridSpec(
        num_scalar_prefetch=2,
        grid=(nb,),
        in_specs=[pl.BlockSpec((tile_rows, LANES), lambda i, be, bv: (i, 0)),
                  pl.BlockSpec((1, d, f2), lambda i, be, bv: (be[i], 0, 0)),
                  pl.BlockSpec((1, 1, f2), lambda i, be, bv: (be[i], 0, 0)),
                  pl.BlockSpec((1, f, d), lambda i, be, bv: (be[i], 0, 0)),
                  pl.BlockSpec((1, 1, d), lambda i, be, bv: (be[i], 0, 0))],
        out_specs=pl.BlockSpec((tile_rows, LANES), lambda i, be, bv: (i, 0)),
        scratch_shapes=[pltpu.VMEM((d, f2), BF16), pltpu.VMEM((f, d), BF16)],
    )
    return pl.pallas_call(
        _expert_kernel,
        grid_spec=grid_spec,
        out_shape=jax.ShapeDtypeStruct(xs.shape, F32),
        compiler_params=pltpu.CompilerParams(
            dimension_semantics=("arbitrary",), vmem_limit_bytes=V7X_VMEM_LIMIT),
        name="experts",
    )(block_exp, block_valid, xs, wgu, bgu, wdn, bdn)


def _combine_kernel(dest_hbm, h_ref, gate_ref, g_ref, ys_hbm, out_ref, dsm_s, ybuf, sem_idx, sem_rows,
                    *, tm):
    i = pl.program_id(0)
    last = pl.num_programs(0) - 1
    slot = i % 2

    def idx_copy(tile, s):
        return pltpu.make_async_copy(dest_hbm.at[jnp.minimum(tile, last)], dsm_s.at[s],
                                     sem_idx.at[s])

    def row_copy(s, kk, r):
        src = pl.multiple_of(dsm_s[s, kk, r], SUBLANES)
        return pltpu.make_async_copy(ys_hbm.at[pl.ds(src, SUBLANES), :],
                                     ybuf.at[s, kk, pl.ds(r * SUBLANES, SUBLANES), :],
                                     sem_rows.at[s])

    def wait_rows(s):
        for kk in range(TOP_K):
            pltpu.make_async_copy(ys_hbm.at[pl.ds(0, tm * SUBLANES), :], ybuf.at[s, kk],
                                  sem_rows.at[s]).wait()

    @pl.when(i == 0)
    def _():
        idx_copy(0, 0).start()
        idx_copy(0, 0).wait()

        def issue(r, c):
            for kk in range(TOP_K):
                row_copy(0, kk, r).start(priority=kk % 2)
            return c
        lax.fori_loop(0, tm, issue, 0, unroll=8)
        idx_copy(1, 1).start()

    idx_copy(i + 1, 1 - slot).wait()
    wait_rows(slot)
    for r in range(tm):
        for kk in range(TOP_K):
            row_copy(1 - slot, kk, r).start(priority=kk % 2)
    idx_copy(i + 2, slot).start()

    gates = jnp.transpose(gate_ref[0])
    acc = h_ref[...]
    for kk in range(TOP_K):
        acc = acc + gates[:, kk:kk + 1] * _load_token_tiles(ybuf.at[slot, kk], tm)
    out_ref[...] = acc * lax.rsqrt(jnp.mean(acc * acc, axis=-1, keepdims=True) + EPS) * g_ref[...]

    @pl.when(i == last)
    def _():
        wait_rows(1 - slot)
        idx_copy(i + 2, slot).wait()


def _combine(dest, h, gates, g_final, ys, tm):
    n, d = h.shape
    return pl.pallas_call(
        functools.partial(_combine_kernel, tm=tm),
        grid=(n // tm,),
        in_specs=[pl.BlockSpec(memory_space=pl.ANY),
                  pl.BlockSpec((tm, d), lambda i: (i, 0)),
                  pl.BlockSpec((1, ROUTE_ROWS, tm), lambda i: (i, 0, 0)),
                  pl.BlockSpec((1, d), lambda i: (0, 0)),
                  pl.BlockSpec(memory_space=pl.ANY)],
        out_specs=pl.BlockSpec((tm, d), lambda i: (i, 0)),
        out_shape=jax.ShapeDtypeStruct((n, d), F32),
        scratch_shapes=[pltpu.SMEM((2, ROUTE_ROWS, tm), jnp.int32),
                        pltpu.VMEM((2, TOP_K, tm * SUBLANES, LANES), F32),
                        pltpu.SemaphoreType.DMA((2,)), pltpu.SemaphoreType.DMA((2,))],
        compiler_params=pltpu.CompilerParams(
            dimension_semantics=("arbitrary",), vmem_limit_bytes=V7X_VMEM_LIMIT),
        name="combine",
    )(dest, h, gates, g_final, ys)


def kernel(x, meta_tokens, norm_mix, w_in, b_in, lam_params, subln_gain, w_attn_proj, conv_dw_w,
           conv_dw_b, conv_ln_g, conv_ln_b, w_conv_proj, w_out, norm_ffn, w_router, b_router,
           w_gate_up, b_gate_up, w_down, b_down, norm_final):
    bsz, seq, d = x.shape
    assert norm_mix.shape[0] == 1, "single-layer block"
    n = bsz * seq
    tm_proj = min(INPROJ_ROWS, seq)
    tq = min(ATTN_Q_TILE, seq)
    tm = min(MIXER_ROWS, seq)
    lam_init = 0.8 - 0.6 * 1.0

    x2d = x.reshape(n, d)
    w_rest = jnp.concatenate([w_in[0][:, d:2 * d], w_in[0][:, 3 * d:]], axis=1).astype(BF16)
    b_rest = jnp.concatenate([b_in[0][d:2 * d], b_in[0][3 * d:]])[None]
    w_t = jnp.concatenate([w_in[0][:, :d], w_in[0][:, 2 * d:3 * d]], axis=1).T.astype(BF16)
    b_t = jnp.concatenate([b_in[0][:d], b_in[0][2 * d:3 * d]])[:, None]
    g_mix = norm_mix[0][None]
    qt, k, vt, u, sga, sgc = _inproj(x2d, g_mix, w_rest, b_rest, w_t, b_t, tm_proj)
    _, k_m, vt_m, u_m, _, _ = _inproj(meta_tokens.astype(x.dtype), g_mix, w_rest, b_rest, w_t, b_t,
                                      N_META)

    o = _attention(lam_params[0], qt, k, vt, k_m, vt_m, subln_gain[0][:, None], bsz, seq, tq,
                   lam_init)

    meta_halo = jnp.concatenate([jnp.zeros((HALO - N_META, d), BF16), u_m], axis=0)
    wr_t = w_router[0].T
    wr_hi = wr_t.astype(BF16)
    wr_split = jnp.concatenate([wr_hi, (wr_t - wr_hi.astype(F32)).astype(BF16)], axis=0)
    h, hn, eidx, rank, gates, counts = _mixer(
        x2d, o, u, sga, sgc, meta_halo,
        w_attn_proj[0].astype(BF16), w_conv_proj[0].astype(BF16), w_out[0].astype(BF16),
        conv_dw_w[0], conv_dw_b[0][None], conv_ln_g[0][None], conv_ln_b[0][None],
        norm_ffn[0][None], wr_split, b_router[0][:, None], seq, tm)

    cnt = counts[:, 0].astype(jnp.int32)
    padded = (cnt + EXPERT_BLOCK - 1) // EXPERT_BLOCK * EXPERT_BLOCK
    pad_end = jnp.cumsum(padded)
    pad_start = pad_end - padded
    n_blocks = -(-(n * TOP_K + N_EXPERTS * (EXPERT_BLOCK - 1)) // EXPERT_BLOCK)
    n_slots = n_blocks * EXPERT_BLOCK
    experts = jnp.arange(N_EXPERTS, dtype=jnp.int32)
    dest = rank + jnp.sum(jnp.where(eidx[..., None] == experts, pad_start, 0), axis=-1)
    block_start = jnp.arange(n_blocks, dtype=jnp.int32) * EXPERT_BLOCK
    block_exp = jnp.minimum(jnp.sum(block_start[:, None] >= pad_end[None, :], axis=1),
                            N_EXPERTS - 1).astype(jnp.int32)
    block_valid = jnp.clip(pad_start[block_exp] + cnt[block_exp] - block_start, 0,
                           EXPERT_BLOCK).astype(jnp.int32)

    dest = dest * SUBLANES
    xs = _dispatch(dest, hn, n_slots, tm)
    ys = _experts(block_exp, block_valid, xs, w_gate_up[0], b_gate_up[0][:, None, :],
                  w_down[0], b_down[0][:, None, :])
    out = _combine(dest, h, gates, norm_final[None], ys, tm)
    return out.reshape(bsz, seq, d)
```
